```python
import math
import jax, jax.numpy as jnp
from jax import lax
import numpy as np

D_MODEL = 1024
BATCH = 4
SEQ = 4096
DEPTH = 2
DEC_BATCH = 32
DEC_SEQ = 4
PAST_LEN = 16384
PAGE_SIZE = 128

HEAD_DIM = 64
A_HEADS = 4
B_HEADS = 6
C_HEADS = 6
A_W = A_HEADS * HEAD_DIM
B_W = B_HEADS * HEAD_DIM
C_W = C_HEADS * HEAD_DIM
MIX_W = A_W + B_W + C_W
CHUNK = 128
PATTERNS = ((128, 1), (512, 4), (2048, 16))
MAX_WINDOW = 2048
QBLK = 128
ROT_DIM = HEAD_DIM // 4
ROPE_THETA = 500000.0
CONV_W = 4
DELTA_CHUNK = 64
N_GROUPS = 4
EXP_PER_GROUP = 8
N_EXPERTS = N_GROUPS * EXP_PER_GROUP
TOP_K = 2
D_EXPERT = 256
MOE_TOKEN_BLOCK = 1024
EPS = 1e-6
PROJ_SIZES = (A_W, A_W, B_W, B_W, B_W, C_W, C_W, C_W, C_W, C_HEADS, C_HEADS)
PROJ_W = 2 * A_W + 3 * B_W + 4 * C_W + 2 * C_HEADS

kernel_name = 'hymba_gmlp_dilated_gdn_hmoe_step'


def rmsnorm(x, g):
    xf = x.astype(jnp.float32)
    y = xf * lax.rsqrt(jnp.mean(xf * xf, axis=-1, keepdims=True) + EPS)
    return (y * g.astype(jnp.float32)).astype(x.dtype)


def layernorm(x, g):
    xf = x.astype(jnp.float32)
    xc = xf - jnp.mean(xf, axis=-1, keepdims=True)
    y = xc * lax.rsqrt(jnp.mean(xc * xc, axis=-1, keepdims=True) + EPS)
    return (y * g.astype(jnp.float32)).astype(x.dtype)


def l2norm(x):
    return x * lax.rsqrt(jnp.sum(x * x, axis=-1, keepdims=True) + EPS)


def split_cols(p):
    parts, start = [], 0
    for size in PROJ_SIZES:
        parts.append(p[..., start:start + size])
        start += size
    return parts


def partial_rope(x, pos):
    half = ROT_DIM // 2
    inv_freq = jnp.power(ROPE_THETA, -jnp.arange(0, ROT_DIM, 2, dtype=jnp.float32) / ROT_DIM)
    ang = pos.astype(jnp.float32)[:, None] * inv_freq[None, :]
    cos = jnp.cos(ang)[:, None, :].astype(x.dtype)
    sin = jnp.sin(ang)[:, None, :].astype(x.dtype)
    x1, x2 = x[..., :half], x[..., half:ROT_DIM]
    return jnp.concatenate([x1 * cos - x2 * sin, x2 * cos + x1 * sin, x[..., ROT_DIM:]], axis=-1)


def chunk_mlp(u, v, gain, ws, bs):
    bx, t, _ = v.shape
    vn = layernorm(v, gain)
    length = CHUNK if t % CHUNK == 0 else t
    n = t // length
    mask = jnp.tril(jnp.ones((length, length), dtype=bool))
    w = jnp.where(mask[None], ws[:, :length, :length], 0).astype(v.dtype)
    vc = vn.reshape(bx, n, length, A_HEADS, HEAD_DIM)
    s = jnp.einsum('hpq,bcqhd->bcphd', w, vc) + bs[:, :length].T.astype(v.dtype)[None, None, :, :, None]
    return u * s.reshape(bx, t, A_W), vn


def dilated_attention(q, k, v, q_idx):
    scale = HEAD_DIM ** -0.5
    outs, lses = [], []
    for window, dil in PATTERNS:
        offs = jnp.arange(window // dil + 1, dtype=jnp.int32) * dil
        idx = q_idx[:, None] - offs[None, :]
        valid = idx >= 0
        idx = jnp.maximum(idx, 0)
        kg = k[:, idx]
        vg = v[:, idx]
        s = jnp.einsum('bqhd,bqmhd->bhqm', q, kg).astype(jnp.float32) * scale
        s = jnp.where(valid[None, None], s, -jnp.inf)
        m = jnp.max(s, axis=-1, keepdims=True)
        e = jnp.exp(s - m)
        den = jnp.sum(e, axis=-1, keepdims=True)
        outs.append(jnp.einsum('bhqm,bqmhd->bqhd', (e / den).astype(v.dtype), vg))
        lses.append((m + jnp.log(den))[..., 0])
    wts = jax.nn.softmax(jnp.stack(lses, axis=0), axis=0)
    wts = jnp.transpose(wts, (0, 1, 3, 2))[..., None].astype(v.dtype)
    return jnp.sum(wts * jnp.stack(outs, axis=0), axis=0)


def dilated_attention_blocked(q, k, v, q_idx):
    bx, tq, h, hd = q.shape
    blk = QBLK if tq % QBLK == 0 else tq
    nb = tq // blk
    qb = jnp.transpose(q.reshape(bx, nb, blk, h, hd), (1, 0, 2, 3, 4))
    ib = q_idx.reshape(nb, blk)
    ob = lax.map(lambda a: dilated_attention(a[0], k, v, a[1]), (qb, ib))
    return jnp.transpose(ob, (1, 0, 2, 3, 4)).reshape(bx, tq, h, hd)


def short_conv(x_new, buf, w):
    t = x_new.shape[1]
    xp = jnp.concatenate([buf.astype(x_new.dtype), x_new], axis=1)
    out = xp[:, 0:t] * w[0]
    for j in range(1, CONV_W):
        out = out + xp[:, j:j + t] * w[j]
    return jax.nn.silu(out), xp[:, -(CONV_W - 1):]


def gated_delta(q, k, v, g, beta, s0, chunk):
    bx, t, h, dk = q.shape
    dv = v.shape[-1]
    n = t // chunk

    def blocks(a):
        a = a.reshape((bx, n, chunk) + a.shape[2:])
        return jnp.moveaxis(jnp.moveaxis(a, 1, 0), 2, 3)

    qb, kb, vb, gb, bb = blocks(q), blocks(k), blocks(v), blocks(g), blocks(beta)
    cg = jnp.cumsum(gb, axis=-1)
    pos = jnp.arange(chunk)
    incl = pos[:, None] >= pos[None, :]
    strict = pos[:, None] > pos[None, :]
    decay = jnp.exp(jnp.where(incl, cg[..., :, None] - cg[..., None, :], -jnp.inf))
    a_mat = jnp.where(strict, bb[..., :, None] * jnp.einsum('...id,...jd->...ij', kb, kb) * decay, 0.0)
    rhs = jnp.concatenate([bb[..., None] * vb, (bb * jnp.exp(cg))[..., None] * kb], axis=-1)
    sol = lax.linalg.triangular_solve(a_mat + jnp.eye(chunk, dtype=jnp.float32), rhs,
                                      left_side=True, lower=True, unit_diagonal=True)
    u_c, w_c = sol[..., :dv], sol[..., dv:]
    attn = jnp.einsum('...id,...jd->...ij', qb, kb) * decay
    q_dec = qb * jnp.exp(cg)[..., None]
    k_dec = kb * jnp.exp(cg[..., -1:] - cg)[..., None]
    g_tot = jnp.exp(cg[..., -1])

    def step(s, inp):
        uu, ww, aa, qd, kd, gt = inp
        u_p = uu - jnp.einsum('bhcd,bhde->bhce', ww, s)
        o = jnp.einsum('bhcd,bhde->bhce', qd, s) + jnp.einsum('bhij,bhje->bhie', aa, u_p)
        s = gt[..., None, None] * s + jnp.einsum('bhcd,bhce->bhde', kd, u_p)
        return s, o

    s_fin, o = lax.scan(step, s0, (u_c, w_c, attn, q_dec, k_dec, g_tot))
    o = jnp.moveaxis(jnp.moveaxis(o, 3, 2), 0, 1).reshape(bx, t, h, dv)
    return o, s_fin


def expert_mix(t, gates, w1, w3, w2):
    hid = jax.nn.silu(jnp.einsum('nd,edf->nef', t, w1)) * jnp.einsum('nd,edf->nef', t, w3)
    return jnp.einsum('nef,efd->nd', hid * gates[..., None], w2)


def hier_moe(x, w_group, w_router, w1, w3, w2):
    shp = x.shape
    t = x.reshape(-1, shp[-1])
    n = t.shape[0]
    g_prob = jax.nn.softmax((t @ w_group).astype(jnp.float32), axis=-1)
    g_w, g_idx = lax.top_k(g_prob, 1)
    e_logits = (t @ w_router).astype(jnp.float32).reshape(n, N_GROUPS, EXP_PER_GROUP)
    e_logits = jnp.take_along_axis(e_logits, g_idx[:, :, None], axis=1)[:, 0]
    top_p, top_i = lax.top_k(jax.nn.softmax(e_logits, axis=-1), TOP_K)
    top_p = top_p / jnp.sum(top_p, axis=-1, keepdims=True)
    gates = g_w * top_p
    eidx = g_idx * EXP_PER_GROUP + top_i
    dense = jnp.sum(jax.nn.one_hot(eidx, N_EXPERTS, dtype=jnp.float32) * gates[..., None], axis=1)
    dense = dense.astype(t.dtype)
    blk = MOE_TOKEN_BLOCK if n % MOE_TOKEN_BLOCK == 0 else n
    y = lax.map(lambda a: expert_mix(a[0], a[1], w1, w3, w2),
                (t.reshape(n // blk, blk, shp[-1]), dense.reshape(n // blk, blk, N_EXPERTS)))
    return y.reshape(shp)


def decoder_layer(x, pos0, kbuf, vbuf, conv_buf, s0, norm1_g, w_in, a_vnorm_g, a_ws, a_bs,
                  c_conv_w, c_a_log, c_dt_bias, c_norm_g, w_out, norm2_g,
                  w_group, w_router, w1, w3, w2):
    bx, t, _ = x.shape
    h = rmsnorm(x, norm1_g)
    au, av, bq, bk, bv, cq, ck, cv, cz, cb, ca = split_cols(h @ w_in)

    out_a, a_rows = chunk_mlp(au, av, a_vnorm_g, a_ws, a_bs)

    pos = pos0 + jnp.arange(t, dtype=jnp.int32)
    q = partial_rope(bq.reshape(bx, t, B_HEADS, HEAD_DIM), pos)
    k_new = partial_rope(bk.reshape(bx, t, B_HEADS, HEAD_DIM), pos)
    v_new = bv.reshape(bx, t, B_HEADS, HEAD_DIM)
    k_all = jnp.concatenate([kbuf.astype(x.dtype), k_new], axis=1)
    v_all = jnp.concatenate([vbuf.astype(x.dtype), v_new], axis=1)
    q_idx = kbuf.shape[1] + jnp.arange(t, dtype=jnp.int32)
    out_b = dilated_attention_blocked(q, k_all, v_all, q_idx).reshape(bx, t, B_W)
    keep = min(MAX_WINDOW, t)

    qkv, conv_state = short_conv(jnp.concatenate([cq, ck, cv], axis=-1), conv_buf, c_conv_w)
    qkv = qkv.astype(jnp.float32).reshape(bx, t, 3, C_HEADS, HEAD_DIM)
    qc = l2norm(qkv[:, :, 0]) * (HEAD_DIM ** -0.5)
    kc = l2norm(qkv[:, :, 1])
    vc = qkv[:, :, 2]
    beta = jax.nn.sigmoid(cb.astype(jnp.float32))
    g = -jnp.exp(c_a_log.astype(jnp.float32)) * jax.nn.softplus(ca.astype(jnp.float32) + c_dt_bias.astype(jnp.float32))
    chunk = DELTA_CHUNK if t % DELTA_CHUNK == 0 else t
    o, s_new = gated_delta(qc, kc, vc, g, beta, s0.astype(jnp.float32), chunk)
    o = o * lax.rsqrt(jnp.mean(o * o, axis=-1, keepdims=True) + EPS) * c_norm_g.astype(jnp.float32)
    o = o * jax.nn.silu(cz.astype(jnp.float32).reshape(bx, t, C_HEADS, HEAD_DIM))
    out_c = o.reshape(bx, t, C_W).astype(x.dtype)

    x = x + jnp.concatenate([out_a, out_b, out_c], axis=-1) @ w_out
    x = x + hier_moe(rmsnorm(x, norm2_g), w_group, w_router, w1, w3, w2)
    return x, (k_new[:, -keep:], v_new[:, -keep:], a_rows, conv_state, s_new.astype(s0.dtype))


def setup_inputs(seed: int = 0) -> dict:
    key = jax.random.key(seed)
    ks = jax.random.split(key, 24)
    f32 = jnp.float32
    win_rows = min(MAX_WINDOW, PAST_LEN)

    def nrm(k, shape, scale):
        return jax.random.normal(k, shape, f32) * scale

    dt = jnp.exp(jax.random.uniform(ks[13], (DEPTH, C_HEADS), f32, math.log(1e-3), math.log(1e-1)))
    return {
        'x_prompt': nrm(ks[0], (BATCH, SEQ, D_MODEL), 1.0),
        'x_sample': nrm(ks[1], (DEC_BATCH, DEC_SEQ, D_MODEL), 1.0),
        'cache_win_k': nrm(ks[2], (DEPTH, DEC_BATCH, win_rows, B_HEADS, HEAD_DIM), 1.0),
        'cache_win_v': nrm(ks[3], (DEPTH, DEC_BATCH, win_rows, B_HEADS, HEAD_DIM), 1.0),
        'state_conv': nrm(ks[4], (DEPTH, DEC_BATCH, CONV_W - 1, 3 * C_W), 1.0),
        'state_delta': nrm(ks[5], (DEPTH, DEC_BATCH, C_HEADS, HEAD_DIM, HEAD_DIM), 0.1),
        'norm1_g': 1.0 + nrm(ks[6], (DEPTH, D_MODEL), 0.1),
        'w_in': nrm(ks[7], (DEPTH, D_MODEL, PROJ_W), D_MODEL ** -0.5),
        'a_vnorm_g': 1.0 + nrm(ks[8], (DEPTH, A_W), 0.1),
        'a_ws': nrm(ks[9], (DEPTH, A_HEADS, CHUNK, CHUNK), CHUNK ** -0.5),
        'a_bs': 1.0 + nrm(ks[10], (DEPTH, A_HEADS, CHUNK), 0.1),
        'c_conv_w': nrm(ks[11], (DEPTH, CONV_W, 3 * C_W), 0.5),
        'c_a_log': jnp.log(jax.random.uniform(ks[12], (DEPTH, C_HEADS), f32, 1.0, 16.0)),
        'c_dt_bias': dt + jnp.log(-jnp.expm1(-dt)),
        'c_norm_g': 1.0 + nrm(ks[14], (DEPTH, HEAD_DIM), 0.1),
        'w_out': nrm(ks[15], (DEPTH, MIX_W, D_MODEL), MIX_W ** -0.5),
        'norm2_g': 1.0 + nrm(ks[16], (DEPTH, D_MODEL), 0.1),
        'w_group': nrm(ks[17], (DEPTH, D_MODEL, N_GROUPS), D_MODEL ** -0.5),
        'w_router': nrm(ks[18], (DEPTH, D_MODEL, N_EXPERTS), D_MODEL ** -0.5),
        'w1': nrm(ks[19], (DEPTH, N_EXPERTS, D_MODEL, D_EXPERT), D_MODEL ** -0.5),
        'w3': nrm(ks[20], (DEPTH, N_EXPERTS, D_MODEL, D_EXPERT), D_MODEL ** -0.5),
        'w2': nrm(ks[21], (DEPTH, N_EXPERTS, D_EXPERT, D_MODEL), D_EXPERT ** -0.5),
        'final_g': 1.0 + nrm(ks[22], (D_MODEL,), 0.1),
    }


def reference(x_prompt, x_sample, cache_win_k, cache_win_v, state_conv, state_delta,
              norm1_g, w_in, a_vnorm_g, a_ws, a_bs, c_conv_w, c_a_log, c_dt_bias, c_norm_g,
              w_out, norm2_g, w_group, w_router, w1, w3, w2, final_g):
    dtype = x_prompt.dtype
    bp = x_prompt.shape[0]
    kbuf_p = jnp.zeros((bp, 0, B_HEADS, HEAD_DIM), dtype)
    conv_p = jnp.zeros((bp, CONV_W - 1, 3 * C_W), dtype)
    s_p = jnp.zeros((bp, C_HEADS, HEAD_DIM, HEAD_DIM), jnp.float32)
    xp, xs = x_prompt, x_sample
    kp_l, vp_l, cp_l, dp_l = [], [], [], []
    ks_l, vs_l, as_l, cs_l, ds_l = [], [], [], [], []
    for l in range(DEPTH):
        lw = (norm1_g[l], w_in[l], a_vnorm_g[l], a_ws[l], a_bs[l], c_conv_w[l], c_a_log[l],
              c_dt_bias[l], c_norm_g[l], w_out[l], norm2_g[l], w_group[l], w_router[l],
              w1[l], w3[l], w2[l])
        xp, (kr, vr, _, cs, sn) = decoder_layer(xp, 0, kbuf_p, kbuf_p, conv_p, s_p, *lw)
        kp_l.append(kr); vp_l.append(vr); cp_l.append(cs); dp_l.append(sn)
        xs, (kr, vr, ar, cs, sn) = decoder_layer(xs, PAST_LEN, cache_win_k[l], cache_win_v[l],
                                                 state_conv[l], state_delta[l], *lw)
        ks_l.append(kr); vs_l.append(vr); as_l.append(ar); cs_l.append(cs); ds_l.append(sn)
    y_prompt = rmsnorm(xp, final_g)
    y_sample = rmsnorm(xs, final_g)
    win_k_prompt = jnp.stack(kp_l)
    win_v_prompt = jnp.stack(vp_l)
    conv_prompt = jnp.stack(cp_l)
    delta_prompt = jnp.stack(dp_l)
    win_k_sample = jnp.stack(ks_l)
    win_v_sample = jnp.stack(vs_l)
    chunk_v_sample = jnp.stack(as_l)
    conv_sample = jnp.stack(cs_l)
    delta_sample = jnp.stack(ds_l)
    return (y_prompt, y_sample, win_k_prompt, win_v_prompt, conv_prompt, delta_prompt,
            win_k_sample, win_v_sample, chunk_v_sample, conv_sample, delta_sample)
```

```python
import functools
import math

import jax
import jax.numpy as jnp
from jax import lax
from jax.experimental import pallas as pl
from jax.experimental.pallas import tpu as pltpu

F32 = jnp.float32
BF16 = jnp.bfloat16
HI = lax.Precision.HIGHEST

D_MODEL = 1024
HEAD_DIM = 64
A_HEADS = 4
B_HEADS = 6
C_HEADS = 6
A_W = A_HEADS * HEAD_DIM
B_W = B_HEADS * HEAD_DIM
C_W = C_HEADS * HEAD_DIM
CHUNK = 128
PATTERNS = ((128, 1), (512, 4), (2048, 16))
MAX_WINDOW = 2048
ROT_DIM = HEAD_DIM // 4
ROPE_THETA = 500000.0
CONV_W = 4
DELTA_CHUNK = 64
N_GROUPS = 4
EXP_PER_GROUP = 8
N_EXPERTS = N_GROUPS * EXP_PER_GROUP
D_EXPERT = 256
EPS = 1e-6
PAST_LEN = 16384

LANES = 128
SUBLANES = 8
WBLK = 128
PROJ_MAIN = 2 * A_W + 3 * B_W + 4 * C_W
NEG = -1e30
ROUTE_OFF = N_GROUPS
VMEM_LIMIT = 56 * 1024 * 1024


def _cparams(sem):
    return pltpu.CompilerParams(dimension_semantics=sem, vmem_limit_bytes=VMEM_LIMIT)


def _sigmoid(x):
    return 1.0 / (1.0 + jnp.exp(-x))


def _silu(x):
    return x * _sigmoid(x)


def _dot(a, b, precision=None):
    return jnp.dot(a, b, preferred_element_type=F32, precision=precision)


def _dot_nt(a, b, precision=None):
    return lax.dot_general(a, b, (((1,), (1,)), ((), ())), preferred_element_type=F32, precision=precision)


def _proj_kernel(x_ref, g_ref, w_ref, wsm_ref, c_ref, s1_ref, s2_ref,
                 au_ref, av_ref, q_ref, k_ref, v_ref, cqkv_ref, cz_ref, cba_ref):
    x = x_ref[...]
    h = x * lax.rsqrt(jnp.mean(x * x, axis=-1, keepdims=True) + EPS) * g_ref[...]
    hb = h.astype(BF16)

    def seg(a, b):
        return _dot(hb, w_ref[:, a:b])

    au_ref[...] = seg(0, A_W)
    av_ref[...] = seg(A_W, 2 * A_W)
    c, s1, s2 = c_ref[...], s1_ref[...], s2_ref[...]
    q0 = 2 * A_W
    k0 = q0 + B_W
    for j in range(B_W // LANES):
        for base, ref, scale in ((q0, q_ref, HEAD_DIM ** -0.5), (k0, k_ref, None)):
            xc = seg(base + LANES * j, base + LANES * (j + 1))
            r = xc * c + pltpu.roll(xc, ROT_DIM // 2, 1) * s1 + pltpu.roll(xc, LANES - ROT_DIM // 2, 1) * s2
            if scale is not None:
                r = r * scale
            ref[:, LANES * j:LANES * (j + 1)] = r
    v0 = k0 + B_W
    v_ref[...] = seg(v0, v0 + B_W)
    c0 = v0 + B_W
    cqkv_ref[...] = seg(c0, c0 + 3 * C_W)
    cz_ref[...] = seg(c0 + 3 * C_W, c0 + 4 * C_W)
    cba_ref[...] = _dot(hb, wsm_ref[...])


def _proj(x, g, w_main, w_small, tabs, tm):
    n = x.shape[0]
    nt = n // tm
    ntab = tabs[0].shape[0] // tm
    row = lambda i: (i, 0)
    fixed = lambda i: (0, 0)
    tab = lambda i: (i % ntab, 0)
    widths = (A_W, A_W, B_W, B_W, B_W, 3 * C_W, C_W, LANES)
    return pl.pallas_call(
        _proj_kernel,
        grid=(nt,),
        in_specs=[pl.BlockSpec((tm, D_MODEL), row), pl.BlockSpec((1, D_MODEL), fixed),
                  pl.BlockSpec((D_MODEL, PROJ_MAIN), fixed), pl.BlockSpec((D_MODEL, LANES), fixed),
                  pl.BlockSpec((tm, LANES), tab), pl.BlockSpec((tm, LANES), tab), pl.BlockSpec((tm, LANES), tab)],
        out_specs=[pl.BlockSpec((tm, w), row) for w in widths],
        out_shape=[jax.ShapeDtypeStruct((n, w), F32) for w in widths],
        compiler_params=_cparams(("parallel",)),
        name="proj",
    )(x, g, w_main, w_small, *tabs)


def _rope_tables(pos):
    half = ROT_DIM // 2
    inv_freq = jnp.power(ROPE_THETA, -jnp.arange(0, ROT_DIM, 2, dtype=F32) / ROT_DIM)
    ang = pos.astype(F32)[:, None] * inv_freq[None, :]
    cos, sin = jnp.cos(ang), jnp.sin(ang)
    p = pos.shape[0]
    z8 = jnp.zeros((p, half), F32)
    rest0 = jnp.zeros((p, HEAD_DIM - ROT_DIM), F32)
    c64 = jnp.concatenate([cos, cos, jnp.ones((p, HEAD_DIM - ROT_DIM), F32)], axis=-1)
    s1 = jnp.concatenate([z8, sin, rest0], axis=-1)
    s2 = jnp.concatenate([-sin, z8, rest0], axis=-1)
    two = lambda a: jnp.concatenate([a, a], axis=-1)
    return two(c64), two(s1), two(s2)


def _chunk_mlp_kernel(u_ref, v_ref, g_ref, w_ref, b_ref, o_ref, vn_ref):
    v = v_ref[...]
    xc = v - jnp.mean(v, axis=-1, keepdims=True)
    vn = xc * lax.rsqrt(jnp.mean(xc * xc, axis=-1, keepdims=True) + EPS) * g_ref[...]
    vn_ref[...] = vn
    vb = vn.astype(BF16)
    rows = lax.broadcasted_iota(jnp.int32, (CHUNK, CHUNK), 0)
    cols = lax.broadcasted_iota(jnp.int32, (CHUNK, CHUNK), 1)
    tril = rows >= cols
    parts = []
    for h in range(A_HEADS):
        w = jnp.where(tril, w_ref[h], 0.0).astype(BF16)
        parts.append(_dot(w, vb[:, h * HEAD_DIM:(h + 1) * HEAD_DIM]))
    s = jnp.concatenate(parts, axis=-1) + b_ref[...]
    o_ref[...] = u_ref[...] * s


def _chunk_mlp(u, v, gain, ws, bias_tile):
    n = u.shape[0]
    row = lambda i: (i, 0)
    return pl.pallas_call(
        _chunk_mlp_kernel,
        grid=(n // CHUNK,),
        in_specs=[pl.BlockSpec((CHUNK, A_W), row), pl.BlockSpec((CHUNK, A_W), row),
                  pl.BlockSpec((1, A_W), lambda i: (0, 0)),
                  pl.BlockSpec((A_HEADS, CHUNK, CHUNK), lambda i: (0, 0, 0)),
                  pl.BlockSpec((CHUNK, A_W), lambda i: (0, 0))],
        out_specs=[pl.BlockSpec((CHUNK, A_W), row), pl.BlockSpec((CHUNK, A_W), row)],
        out_shape=[jax.ShapeDtypeStruct((n, A_W), F32)] * 2,
        compiler_params=_cparams(("parallel",)),
        name="chunk_mlp",
    )(u, v, gain, ws, bias_tile)


def _win_attn_kernel(q_ref, kp_ref, kc_ref, vp_ref, vc_ref, o_ref, lse_ref, *, seg_blocks):
    p = pl.program_id(0)
    s = pl.program_id(1)
    nb = jnp.where(p == 0, seg_blocks[0], jnp.where(p == 1, seg_blocks[1], seg_blocks[2]))
    lo = jnp.where(s % nb == 0, WBLK, 0)
    rows = lax.broadcasted_iota(jnp.int32, (WBLK, 2 * WBLK), 0)
    cols = lax.broadcasted_iota(jnp.int32, (WBLK, 2 * WBLK), 1)
    dist = rows + WBLK - cols
    valid = (dist >= 0) & (dist <= WBLK) & (cols >= lo)
    bias = jnp.where(valid, 0.0, NEG)
    lane = lax.broadcasted_iota(jnp.int32, (WBLK, LANES), 1)
    k2 = jnp.concatenate([kp_ref[...], kc_ref[...]], axis=0).astype(BF16)
    v2 = jnp.concatenate([vp_ref[...], vc_ref[...]], axis=0).astype(BF16)
    q = q_ref[...]
    lse_tile = jnp.zeros((WBLK, LANES), F32)
    for hp in range(B_W // LANES):
        sl = slice(hp * LANES, (hp + 1) * LANES)
        qp, kp, vp = q[:, sl], k2[:, sl], v2[:, sl]
        acc = None
        for half in range(2):
            hm = (lane < HEAD_DIM) if half == 0 else (lane >= HEAD_DIM)
            qh = jnp.where(hm, qp, 0.0).astype(BF16)
            sc = _dot_nt(qh, kp) + bias
            m = jnp.max(sc, axis=-1, keepdims=True)
            e = jnp.exp(sc - m)
            den = jnp.sum(e, axis=-1, keepdims=True)
            pm = (e * (1.0 / den)).astype(BF16)
            o = _dot(pm, vp)
            acc = o if acc is None else jnp.where(hm, o, acc)
            lse_tile = jnp.where(lane == 2 * hp + half, m + jnp.log(den), lse_tile)
        o_ref[:, sl] = acc
    lse_ref[...] = lse_tile


def _win_attn(qd, kd, vd, seg_blocks):
    npat, n, _ = qd.shape
    nblk = n // WBLK
    cur = lambda p, s: (p, s, 0)
    prev = lambda p, s: (p, jnp.maximum(s - 1, 0), 0)
    blk = lambda w, im: pl.BlockSpec((None, WBLK, w), im)
    return pl.pallas_call(
        functools.partial(_win_attn_kernel, seg_blocks=seg_blocks),
        grid=(npat, nblk),
        in_specs=[blk(B_W, cur), blk(B_W, prev), blk(B_W, cur), blk(B_W, prev), blk(B_W, cur)],
        out_specs=[blk(B_W, cur), blk(LANES, cur)],
        out_shape=[jax.ShapeDtypeStruct((npat, n, B_W), F32), jax.ShapeDtypeStruct((npat, n, LANES), F32)],
        compiler_params=_cparams(("parallel", "parallel")),
        name="win_attn",
    )(qd, kd, kd, vd, vd)


def _dec_attn_kernel(q_ref, kc_ref, vc_ref, kn_ref, vn_ref, o_ref, *, t_new, cache_len):
    rows_c = lax.broadcasted_iota(jnp.int32, (SUBLANES, cache_len), 0)
    cols_c = lax.broadcasted_iota(jnp.int32, (SUBLANES, cache_len), 1)
    dist_c = cache_len + rows_c % t_new - cols_c
    rows_n = lax.broadcasted_iota(jnp.int32, (SUBLANES, SUBLANES), 0)
    cols_n = lax.broadcasted_iota(jnp.int32, (SUBLANES, SUBLANES), 1)
    dist_n = rows_n % t_new - cols_n
    biases = []
    for window, dil in PATTERNS:
        vc_ok = (dist_c <= window) & ((dist_c & (dil - 1)) == 0)
        vn_ok = (dist_n >= 0) & ((dist_n & (dil - 1)) == 0)
        biases.append((jnp.where(vc_ok, 0.0, NEG), jnp.where(vn_ok, 0.0, NEG)))
    row8 = lax.broadcasted_iota(jnp.int32, (SUBLANES, LANES), 0)
    lane8 = lax.broadcasted_iota(jnp.int32, (SUBLANES, LANES), 1)
    own = (lane8 < HEAD_DIM) == (row8 < t_new)
    q = q_ref[...]
    outs = []
    for hp in range(B_W // LANES):
        sl = slice(hp * LANES, (hp + 1) * LANES)
        qq = jnp.concatenate([q[0:t_new, sl], q[0:t_new, sl]], axis=0)
        q8 = jnp.where(own, qq, 0.0).astype(BF16)
        kc, vc = kc_ref[:, sl].astype(BF16), vc_ref[:, sl].astype(BF16)
        kn, vn = kn_ref[:, sl].astype(BF16), vn_ref[:, sl].astype(BF16)
        sc_c = _dot_nt(q8, kc)
        sc_n = _dot_nt(q8, kn)
        os_, lses = [], []
        for bc, bn in biases:
            a_c, a_n = sc_c + bc, sc_n + bn
            m = jnp.maximum(jnp.max(a_c, axis=-1, keepdims=True), jnp.max(a_n, axis=-1, keepdims=True))
            e_c, e_n = jnp.exp(a_c - m), jnp.exp(a_n - m)
            den = jnp.sum(e_c, axis=-1, keepdims=True) + jnp.sum(e_n, axis=-1, keepdims=True)
            inv = 1.0 / den
            os_.append(_dot((e_c * inv).astype(BF16), vc) + _dot((e_n * inv).astype(BF16), vn))
            lses.append(m + jnp.log(den))
        mx = jnp.maximum(jnp.maximum(lses[0], lses[1]), lses[2])
        ws = [jnp.exp(l - mx) for l in lses]
        tot = ws[0] + ws[1] + ws[2]
        o8 = (ws[0] / tot) * os_[0] + (ws[1] / tot) * os_[1] + (ws[2] / tot) * os_[2]
        lane4 = lane8[0:t_new]
        outs.append(jnp.where(lane4 < HEAD_DIM, o8[0:t_new], o8[t_new:2 * t_new]))
    o_ref[...] = jnp.zeros((SUBLANES, B_W), F32)
    o_ref[0:t_new, :] = jnp.concatenate(outs, axis=-1)


def _dec_attn(q8, kc, vc, kn8, vn8, t_new):
    b, cache_len, _ = kc.shape
    assert 2 * t_new == SUBLANES
    small = pl.BlockSpec((None, SUBLANES, B_W), lambda i: (i, 0, 0))
    big = pl.BlockSpec((None, cache_len, B_W), lambda i: (i, 0, 0))
    return pl.pallas_call(
        functools.partial(_dec_attn_kernel, t_new=t_new, cache_len=cache_len),
        grid=(b,),
        in_specs=[small, big, big, small, small],
        out_specs=small,
        out_shape=jax.ShapeDtypeStruct((b, SUBLANES, B_W), F32),
        compiler_params=_cparams(("parallel",)),
        name="dec_attn",
    )(q8, kc, vc, kn8, vn8)


def _unit_lower_inverse(a):
    n = DELTA_CHUNK
    rows = lax.broadcasted_iota(jnp.int32, (n, n), 0)
    cols = lax.broadcasted_iota(jnp.int32, (n, n), 1)
    b16r, b16c = rows // 16, cols // 16
    eye = jnp.where(rows == cols, 1.0, 0.0)
    d = jnp.where(b16r == b16c, a, 0.0)
    e1 = jnp.where((b16r // 2 == b16c // 2) & (b16r != b16c), a, 0.0)
    e2 = jnp.where(b16r // 2 != b16c // 2, a, 0.0)
    mm = lambda x, y: _dot(x, y, HI)
    d2 = mm(d, d)
    d4 = mm(d2, d2)
    d8 = mm(d4, d4)
    t = eye - d
    t = t + mm(t, d2)
    t = t + mm(t, d4)
    t = t + mm(t, d8)
    t = t - mm(mm(t, e1), t)
    t = t - mm(mm(t, e2), t)
    return t


def _gdn_prep_kernel(x_ref, cba_ref, buf_ref, cw_ref, alog_ref, dtb_ref, gmat_ref,
                     u_ref, w_ref, qd_ref, kd_ref, attn_ref, gt_ref,
                     xp_scr, q_scr, k_scr, v_scr, gb_scr, *, tm, valid_len):
    ti = pl.program_id(1)

    @pl.when(ti == 0)
    def _():
        xp_scr[0:SUBLANES, :] = buf_ref[...]

    x = x_ref[...]
    xp_scr[SUBLANES:SUBLANES + tm, :] = x
    off = SUBLANES - (CONV_W - 1)
    acc = xp_scr[off:off + tm, :] * cw_ref[0:1, :]
    for j in range(1, CONV_W):
        acc = acc + xp_scr[off + j:off + j + tm, :] * cw_ref[j:j + 1, :]
    xp_scr[0:SUBLANES, :] = x[tm - SUBLANES:tm, :]
    y = _silu(acc)
    q, k = y[:, 0:C_W], y[:, C_W:2 * C_W]
    gmat = gmat_ref[...]
    q_scr[...] = q * lax.rsqrt(_dot(q * q, gmat, HI) + EPS) * (HEAD_DIM ** -0.5)
    k_scr[...] = k * lax.rsqrt(_dot(k * k, gmat, HI) + EPS)
    v_scr[...] = y[:, 2 * C_W:3 * C_W]

    cba = cba_ref[...]
    lane = lax.broadcasted_iota(jnp.int32, (tm, LANES), 1)
    tpos = ti * tm + lax.broadcasted_iota(jnp.int32, (tm, LANES), 0)
    live = tpos < valid_len
    beta = _sigmoid(cba)
    z = cba + dtb_ref[...]
    softplus = jnp.maximum(z, 0.0) + jnp.log(1.0 + jnp.exp(-jnp.abs(z)))
    g = -jnp.exp(alog_ref[...]) * softplus
    is_g = (lane >= C_HEADS) & (lane < 2 * C_HEADS)
    gb_scr[...] = jnp.where(live, jnp.where(is_g, g, jnp.where(lane < C_HEADS, beta, 0.0)), 0.0)

    n = DELTA_CHUNK
    rows = lax.broadcasted_iota(jnp.int32, (n, n), 0)
    cols = lax.broadcasted_iota(jnp.int32, (n, n), 1)
    incl = rows >= cols
    strict = rows > cols
    ltri = jnp.where(incl, 1.0, 0.0)

    def chunk(c, carry):
        r0 = pl.multiple_of(c * n, n)
        rs = pl.ds(r0, n)
        gb = gb_scr[rs, :]
        lane_c = lax.broadcasted_iota(jnp.int32, (n, LANES), 1)
        cg = _dot(ltri, jnp.where((lane_c >= C_HEADS) & (lane_c < 2 * C_HEADS), gb, 0.0), HI)
        cgt = cg.T
        qn, kn, vv = q_scr[rs, :], k_scr[rs, :], v_scr[rs, :]
        us, ws, qds, kds, ats = [], [], [], [], []
        for h in range(C_HEADS):
            sl = slice(h * HEAD_DIM, (h + 1) * HEAD_DIM)
            qh, kh, vh = qn[:, sl], kn[:, sl], vv[:, sl]
            cgc = cg[:, C_HEADS + h:C_HEADS + h + 1]
            cgr = cgt[C_HEADS + h:C_HEADS + h + 1, :]
            decay = jnp.where(incl, jnp.exp(jnp.minimum(cgc - cgr, 0.0)), 0.0)
            bcol = gb[:, h:h + 1]
            kq = jnp.concatenate([kh, qh], axis=0).astype(BF16)
            kkqk = _dot_nt(kq, kh.astype(BF16))
            a = jnp.where(strict, bcol * kkqk[0:n] * decay, 0.0)
            ats.append(kkqk[n:2 * n] * decay)
            ecg = jnp.exp(cgc)
            rhs = jnp.concatenate([bcol * vh, (bcol * ecg) * kh], axis=1)
            sol = _dot(_unit_lower_inverse(a), rhs, HI)
            us.append(sol[:, 0:HEAD_DIM])
            ws.append(sol[:, HEAD_DIM:2 * HEAD_DIM])
            qds.append(qh * ecg)
            kds.append(kh * jnp.exp(cg[n - 1:n, C_HEADS + h:C_HEADS + h + 1] - cgc))
        u_ref[rs, :] = jnp.concatenate(us, axis=1)
        w_ref[rs, :] = jnp.concatenate(ws, axis=1)
        qd_ref[rs, :] = jnp.concatenate(qds, axis=1)
        kd_ref[rs, :] = jnp.concatenate(kds, axis=1)
        attn_ref[rs, :] = jnp.concatenate(ats, axis=1)
        g0 = pl.multiple_of(c * SUBLANES, SUBLANES)
        gt_ref[pl.ds(g0, SUBLANES), :] = jnp.broadcast_to(jnp.exp(cg[n - 1:n, :]), (SUBLANES, LANES))
        return carry

    lax.fori_loop(0, tm // n, chunk, 0)


def _gdn_prep(cqkv, cba, buf8, conv_w, alog_row, dtb_row, gmat, batch, t, tm, valid_len):
    n = batch * t
    nt = t // tm
    row = lambda b, i: (b * nt + i, 0)
    fixed = lambda b, i: (0, 0)
    nch = tm // DELTA_CHUNK
    outs = [jax.ShapeDtypeStruct((n, C_W), F32)] * 5 + [jax.ShapeDtypeStruct((n // DELTA_CHUNK * SUBLANES, LANES), F32)]
    return pl.pallas_call(
        functools.partial(_gdn_prep_kernel, tm=tm, valid_len=valid_len),
        grid=(batch, nt),
        in_specs=[pl.BlockSpec((tm, 3 * C_W), row), pl.BlockSpec((tm, LANES), row),
                  pl.BlockSpec((None, SUBLANES, 3 * C_W), lambda b, i: (b, 0, 0)),
                  pl.BlockSpec((CONV_W, 3 * C_W), fixed), pl.BlockSpec((1, LANES), fixed),
                  pl.BlockSpec((1, LANES), fixed), pl.BlockSpec((C_W, C_W), fixed)],
        out_specs=[pl.BlockSpec((tm, C_W), row)] * 5 + [pl.BlockSpec((nch * SUBLANES, LANES), row)],
        out_shape=outs,
        scratch_shapes=[pltpu.VMEM((tm + SUBLANES, 3 * C_W), F32), pltpu.VMEM((tm, C_W), F32),
                        pltpu.VMEM((tm, C_W), F32), pltpu.VMEM((tm, C_W), F32), pltpu.VMEM((tm, LANES), F32)],
        compiler_params=_cparams(("parallel", "arbitrary")),
        name="gdn_prep",
    )(cqkv, cba, buf8, conv_w, alog_row, dtb_row, gmat)


def _gdn_scan_kernel(u_ref, w_ref, qd_ref, kd_ref, attn_ref, gt_ref, z_ref, s0_ref, gn_ref,
                     o_ref, sfin_ref, s_scr, *, bg):
    c = pl.program_id(1)

    @pl.when(c == 0)
    def _():
        s_scr[...] = s0_ref[...]

    gn = gn_ref[...]
    for b in range(bg):
        outs = []
        for h in range(C_HEADS):
            sl = slice(h * HEAD_DIM, (h + 1) * HEAD_DIM)
            st = s_scr[b, h]
            up = u_ref[b, :, sl] - _dot(w_ref[b, :, sl], st, HI)
            o = _dot(qd_ref[b, :, sl], st, HI) + _dot(attn_ref[b, :, sl], up, HI)
            gt = gt_ref[b, 0:1, C_HEADS + h:C_HEADS + h + 1]
            kd = kd_ref[b, :, sl]
            s_scr[b, h] = gt * st + lax.dot_general(kd, up, (((0,), (0,)), ((), ())),
                                                    preferred_element_type=F32, precision=HI)
            o = o * lax.rsqrt(jnp.mean(o * o, axis=-1, keepdims=True) + EPS) * gn
            outs.append(o * _silu(z_ref[b, :, sl]))
        o_ref[b] = jnp.concatenate(outs, axis=1)

    @pl.when(c == pl.num_programs(1) - 1)
    def _():
        sfin_ref[...] = s_scr[...]


def _gdn_scan(u, w, qd, kd, attn, gt, z, s0, gnorm, batch, t, bg):
    n = DELTA_CHUNK
    nc = t // n
    v3 = lambda a: a.reshape(batch, t, C_W)
    tok = pl.BlockSpec((bg, n, C_W), lambda b, c: (b, c, 0))
    st = pl.BlockSpec((bg, C_HEADS, HEAD_DIM, HEAD_DIM), lambda b, c: (b, 0, 0, 0))
    return pl.pallas_call(
        functools.partial(_gdn_scan_kernel, bg=bg),
        grid=(batch // bg, nc),
        in_specs=[tok, tok, tok, tok, tok,
                  pl.BlockSpec((bg, SUBLANES, LANES), lambda b, c: (b, c, 0)),
                  tok, st, pl.BlockSpec((1, HEAD_DIM), lambda b, c: (0, 0))],
        out_specs=[tok, st],
        out_shape=[jax.ShapeDtypeStruct((batch, t, C_W), F32),
                   jax.ShapeDtypeStruct((batch, C_HEADS, HEAD_DIM, HEAD_DIM), F32)],
        scratch_shapes=[pltpu.VMEM((bg, C_HEADS, HEAD_DIM, HEAD_DIM), F32)],
        compiler_params=_cparams(("parallel", "arbitrary")),
        name="gdn_scan",
    )(v3(u), v3(w), v3(qd), v3(kd), v3(attn), gt.reshape(batch, nc * SUBLANES, LANES), v3(z), s0, gnorm)


def _head_expand(wt):
    tm = wt.shape[0]
    return jnp.concatenate([jnp.broadcast_to(wt[:, h:h + 1], (tm, HEAD_DIM)) for h in range(B_HEADS)], axis=-1)


def _out_proj_kernel(*refs, combine):
    if combine:
        a_ref, o0, o1, o2, l0, l1, l2, c_ref, x_ref, w_ref, y_ref = refs
        ls = [l0[...], l1[...], l2[...]]
        mx = jnp.maximum(jnp.maximum(ls[0], ls[1]), ls[2])
        es = [jnp.exp(l - mx) for l in ls]
        tot = es[0] + es[1] + es[2]
        ob = (_head_expand(es[0] / tot) * o0[...] + _head_expand(es[1] / tot) * o1[...]
              + _head_expand(es[2] / tot) * o2[...])
    else:
        a_ref, b_ref, c_ref, x_ref, w_ref, y_ref = refs
        ob = b_ref[...]
    cat = jnp.concatenate([a_ref[...], ob, c_ref[...]], axis=-1).astype(BF16)
    y_ref[...] = x_ref[...] + _dot(cat, w_ref[...])


def _out_proj(out_a, out_b, lses, out_c, x, w_out, tm):
    n = x.shape[0]
    row = lambda i: (i, 0)
    spec = lambda w: pl.BlockSpec((tm, w), row)
    combine = lses is not None
    if combine:
        ins = [out_a, *out_b, *lses, out_c, x, w_out]
        specs = [spec(A_W)] + [spec(B_W)] * 3 + [spec(LANES)] * 3 + [spec(C_W), spec(D_MODEL)]
    else:
        ins = [out_a, out_b, out_c, x, w_out]
        specs = [spec(A_W), spec(B_W), spec(C_W), spec(D_MODEL)]
    specs.append(pl.BlockSpec((D_MODEL, D_MODEL), lambda i: (0, 0)))
    return pl.pallas_call(
        functools.partial(_out_proj_kernel, combine=combine),
        grid=(n // tm,),
        in_specs=specs,
        out_specs=spec(D_MODEL),
        out_shape=jax.ShapeDtypeStruct((n, D_MODEL), F32),
        compiler_params=_cparams(("parallel",)),
        name="out_proj",
    )(*ins)


def _router_kernel(x_ref, g_ref, w_ref, dense_ref):
    x = x_ref[...]
    t = x * lax.rsqrt(jnp.mean(x * x, axis=-1, keepdims=True) + EPS) * g_ref[...]
    lg = _dot(t, w_ref[...], HI)
    tm = lg.shape[0]
    lane = lax.broadcasted_iota(jnp.int32, (tm, LANES), 1).astype(F32)
    big = float(LANES)
    is_grp = lane < N_GROUPS
    gl = jnp.where(is_grp, lg, NEG)
    gmax = jnp.max(gl, axis=-1, keepdims=True)
    gsum = jnp.sum(jnp.where(is_grp, jnp.exp(gl - gmax), 0.0), axis=-1, keepdims=True)
    g_w = 1.0 / gsum
    g_idx = jnp.min(jnp.where(is_grp & (gl == gmax), lane, big), axis=-1, keepdims=True)
    lo = ROUTE_OFF + EXP_PER_GROUP * g_idx
    sel = (lane >= lo) & (lane < lo + EXP_PER_GROUP)
    el = jnp.where(sel, lg, NEG)
    m1 = jnp.max(el, axis=-1, keepdims=True)
    esum = jnp.sum(jnp.where(sel, jnp.exp(el - m1), 0.0), axis=-1, keepdims=True)
    i1 = jnp.min(jnp.where(sel & (el == m1), lane, big), axis=-1, keepdims=True)
    el2 = jnp.where(lane == i1, NEG, el)
    m2 = jnp.max(el2, axis=-1, keepdims=True)
    i2 = jnp.min(jnp.where(sel & (lane != i1) & (el2 == m2), lane, big), axis=-1, keepdims=True)
    p1 = 1.0 / esum
    p2 = jnp.exp(m2 - m1) / esum
    tot = p1 + p2
    dense_ref[...] = jnp.where(lane == i1, g_w * (p1 / tot), 0.0) + jnp.where(lane == i2, g_w * (p2 / tot), 0.0)


def _router(x, g, w_route, tm):
    n = x.shape[0]
    row = lambda i: (i, 0)
    return pl.pallas_call(
        _router_kernel,
        grid=(n // tm,),
        in_specs=[pl.BlockSpec((tm, D_MODEL), row), pl.BlockSpec((1, D_MODEL), lambda i: (0, 0)),
                  pl.BlockSpec((D_MODEL, LANES), lambda i: (0, 0))],
        out_specs=pl.BlockSpec((tm, LANES), row),
        out_shape=jax.ShapeDtypeStruct((n, LANES), F32),
        compiler_params=_cparams(("parallel",)),
        name="router",
    )(x, g, w_route)


def _moe_kernel(x_ref, g_ref, dense_ref, w1_ref, w3_ref, w2_ref, y_ref, t_scr):
    e = pl.program_id(1)

    @pl.when(e == 0)
    def _():
        x = x_ref[...]
        t_scr[...] = (x * lax.rsqrt(jnp.mean(x * x, axis=-1, keepdims=True) + EPS) * g_ref[...]).astype(BF16)
        y_ref[...] = x

    tb = t_scr[...]
    dense = dense_ref[...]
    lane = lax.broadcasted_iota(jnp.int32, dense.shape, 1)
    gate = jnp.sum(jnp.where(lane == e + ROUTE_OFF, dense, 0.0), axis=-1, keepdims=True)
    hid = _silu(_dot(tb, w1_ref[...])) * _dot(tb, w3_ref[...])
    y_ref[...] += _dot((hid * gate).astype(BF16), w2_ref[...])


def _moe(x, g, dense, w1, w3, w2, tm):
    n = x.shape[0]
    row = lambda i, e: (i, 0)
    return pl.pallas_call(
        _moe_kernel,
        grid=(n // tm, N_EXPERTS),
        in_specs=[pl.BlockSpec((tm, D_MODEL), row), pl.BlockSpec((1, D_MODEL), lambda i, e: (0, 0)),
                  pl.BlockSpec((tm, LANES), row),
                  pl.BlockSpec((None, D_MODEL, D_EXPERT), lambda i, e: (e, 0, 0)),
                  pl.BlockSpec((None, D_MODEL, D_EXPERT), lambda i, e: (e, 0, 0)),
                  pl.BlockSpec((None, D_EXPERT, D_MODEL), lambda i, e: (e, 0, 0))],
        out_specs=pl.BlockSpec((tm, D_MODEL), row),
        out_shape=jax.ShapeDtypeStruct((n, D_MODEL), F32),
        scratch_shapes=[pltpu.VMEM((tm, D_MODEL), BF16)],
        compiler_params=_cparams(("parallel", "arbitrary")),
        name="moe",
    )(x, g, dense, w1, w3, w2)


def _final_norm_kernel(x_ref, g_ref, y_ref):
    x = x_ref[...]
    y_ref[...] = x * lax.rsqrt(jnp.mean(x * x, axis=-1, keepdims=True) + EPS) * g_ref[...]


def _final_norm(x, g, tm):
    n = x.shape[0]
    return pl.pallas_call(
        _final_norm_kernel,
        grid=(n // tm,),
        in_specs=[pl.BlockSpec((tm, D_MODEL), lambda i: (i, 0)), pl.BlockSpec((1, D_MODEL), lambda i: (0, 0))],
        out_specs=pl.BlockSpec((tm, D_MODEL), lambda i: (i, 0)),
        out_shape=jax.ShapeDtypeStruct((n, D_MODEL), F32),
        compiler_params=_cparams(("parallel",)),
        name="final_norm",
    )(x, g)


def _deinterleave(x, batch, t, d):
    w = x.shape[-1]
    return x.reshape(batch, t // d, d, w).transpose(0, 2, 1, 3).reshape(batch * t, w)


def _interleave(x, batch, t, d):
    w = x.shape[-1]
    return x.reshape(batch, d, t // d, w).transpose(0, 2, 1, 3).reshape(batch * t, w)


def _tile_rows(n, cap):
    tm = min(n, cap)
    assert n % tm == 0
    return tm


def _layer_weights(l, norm1_g, w_in, a_vnorm_g, a_ws, a_bs, c_conv_w, c_a_log, c_dt_bias, c_norm_g,
                   w_out, norm2_g, w_group, w_router, w1, w3, w2):
    wi = w_in[l]
    pad_l = lambda a, left: jnp.pad(a, ((0, 0), (left, LANES - left - a.shape[-1])))
    return dict(
        norm1_g=norm1_g[l][None, :],
        w_main=wi[:, :PROJ_MAIN].astype(BF16),
        w_small=pad_l(wi[:, PROJ_MAIN:], 0).astype(BF16),
        a_gain=a_vnorm_g[l][None, :],
        a_ws=a_ws[l],
        a_bs=a_bs[l],
        conv_w=c_conv_w[l],
        alog_row=pad_l(c_a_log[l][None, :], C_HEADS),
        dtb_row=pad_l(c_dt_bias[l][None, :], C_HEADS),
        gnorm=c_norm_g[l][None, :],
        w_out=w_out[l].astype(BF16),
        norm2_g=norm2_g[l][None, :],
        w_route=pad_l(jnp.concatenate([w_group[l], w_router[l]], axis=-1), 0),
        w1=w1[l].astype(BF16), w3=w3[l].astype(BF16), w2=w2[l].astype(BF16),
    )


def _mixer_c(lw, cqkv, cba, cz, conv_buf, s0, gmat, batch, t, tm, valid_len, bg):
    buf8 = jnp.pad(conv_buf, ((0, 0), (SUBLANES - (CONV_W - 1), 0), (0, 0)))
    u, w, qd, kd, attn, gt = _gdn_prep(cqkv, cba, buf8, lw["conv_w"], lw["alog_row"], lw["dtb_row"], gmat,
                                       batch, t, tm, valid_len)
    return _gdn_scan(u, w, qd, kd, attn, gt, cz, s0, lw["gnorm"], batch, t, bg)


def _ffn(lw, x):
    n = x.shape[0]
    dense = _router(x, lw["norm2_g"], lw["w_route"], _tile_rows(n, 512))
    return _moe(x, lw["norm2_g"], dense, lw["w1"], lw["w3"], lw["w2"], _tile_rows(n, 1024))


def _prompt_layer(lw, x, batch, t, tabs, gmat):
    n = batch * t
    au, av, q, k, v, cqkv, cz, cba = _proj(x, lw["norm1_g"], lw["w_main"], lw["w_small"], tabs, _tile_rows(n, 512))

    bias_tile = jnp.repeat(lw["a_bs"].T, HEAD_DIM, axis=1)
    out_a, _ = _chunk_mlp(au, av, lw["a_gain"], lw["a_ws"], bias_tile)

    dils = [d for _, d in PATTERNS]
    stack = lambda a: jnp.stack([_deinterleave(a, batch, t, d) for d in dils])
    seg_blocks = tuple(t // d // WBLK for d in dils)
    o_d, lse_d = _win_attn(stack(q), stack(k), stack(v), seg_blocks)
    outs = [_interleave(o_d[i], batch, t, d) for i, d in enumerate(dils)]
    lses = [_interleave(lse_d[i], batch, t, d) for i, d in enumerate(dils)]

    zeros_buf = jnp.zeros((batch, CONV_W - 1, 3 * C_W), F32)
    zeros_s = jnp.zeros((batch, C_HEADS, HEAD_DIM, HEAD_DIM), F32)
    out_c, s_new = _mixer_c(lw, cqkv, cba, cz, zeros_buf, zeros_s, gmat, batch, t, 256, t, batch)
    out_c = out_c.reshape(n, C_W)

    x = _out_proj(out_a, outs, lses, out_c, x, lw["w_out"], _tile_rows(n, 512))
    x = _ffn(lw, x)

    keep = min(MAX_WINDOW, t)
    heads = lambda a: a.reshape(batch, t, B_HEADS, HEAD_DIM)[:, t - keep:]
    conv_state = cqkv.reshape(batch, t, 3 * C_W)[:, t - (CONV_W - 1):]
    return x, (heads(k), heads(v), conv_state, s_new)


def _sample_layer(lw, x, batch, t, tabs, gmat, kbuf, vbuf, conv_buf, s0):
    n = batch * t
    au, av, q, k, v, cqkv, cz, cba = _proj(x, lw["norm1_g"], lw["w_main"], lw["w_small"], tabs, n)

    eye = jnp.eye(batch, dtype=F32)
    ws_bd = jnp.stack([jnp.kron(eye, lw["a_ws"][h, :t, :t]) for h in range(A_HEADS)])
    bias_tile = jnp.tile(jnp.repeat(lw["a_bs"][:, :t].T, HEAD_DIM, axis=1), (batch, 1))
    out_a, a_rows = _chunk_mlp(au, av, lw["a_gain"], ws_bd, bias_tile)

    pad8 = lambda a: jnp.pad(a.reshape(batch, t, B_W), ((0, 0), (0, SUBLANES - t), (0, 0)))
    cache_len = kbuf.shape[1]
    out_b = _dec_attn(pad8(q), kbuf.reshape(batch, cache_len, B_W), vbuf.reshape(batch, cache_len, B_W),
                      pad8(k), pad8(v), t)[:, :t].reshape(n, B_W)

    tp = DELTA_CHUNK
    padt = lambda a: jnp.pad(a.reshape(batch, t, -1), ((0, 0), (0, tp - t), (0, 0))).reshape(batch * tp, -1)
    out_c, s_new = _mixer_c(lw, padt(cqkv), padt(cba), padt(cz), conv_buf, s0, gmat, batch, tp, tp, t, 4)
    out_c = out_c[:, :t].reshape(n, C_W)

    x = _out_proj(out_a, out_b, None, out_c, x, lw["w_out"], n)
    x = _ffn(lw, x)

    heads = lambda a: a.reshape(batch, t, B_HEADS, HEAD_DIM)
    conv_state = jnp.concatenate([conv_buf, cqkv.reshape(batch, t, 3 * C_W)], axis=1)[:, -(CONV_W - 1):]
    return x, (heads(k), heads(v), a_rows.reshape(batch, t, A_W), conv_state, s_new)


def kernel(x_prompt, x_sample, cache_win_k, cache_win_v, state_conv, state_delta, norm1_g, w_in, a_vnorm_g, a_ws, a_bs, c_conv_w, c_a_log, c_dt_bias, c_norm_g, w_out, norm2_g, w_group, w_router, w1, w3, w2, final_g):
    bp, tp, _ = x_prompt.shape
    bs, ts, _ = x_sample.shape
    depth = w_in.shape[0]
    assert tp % (PATTERNS[-1][1] * WBLK) == 0 and tp % 512 == 0 and bs * ts == CHUNK

    tabs_p = _rope_tables(jnp.arange(tp, dtype=jnp.int32))
    tabs_s = tuple(jnp.tile(a, (bs, 1)) for a in _rope_tables(PAST_LEN + jnp.arange(ts, dtype=jnp.int32)))
    hid = jnp.arange(C_W, dtype=jnp.int32) // HEAD_DIM
    gmat = (hid[:, None] == hid[None, :]).astype(F32)

    xp = x_prompt.reshape(bp * tp, D_MODEL)
    xs = x_sample.reshape(bs * ts, D_MODEL)
    p_out = [[] for _ in range(4)]
    s_out = [[] for _ in range(5)]
    for l in range(depth):
        lw = _layer_weights(l, norm1_g, w_in, a_vnorm_g, a_ws, a_bs, c_conv_w, c_a_log, c_dt_bias, c_norm_g,
                            w_out, norm2_g, w_group, w_router, w1, w3, w2)
        xp, st = _prompt_layer(lw, xp, bp, tp, tabs_p, gmat)
        for acc, a in zip(p_out, st):
            acc.append(a)
        xs, st = _sample_layer(lw, xs, bs, ts, tabs_s, gmat, cache_win_k[l], cache_win_v[l],
                               state_conv[l], state_delta[l])
        for acc, a in zip(s_out, st):
            acc.append(a)
    fg = final_g[None, :]
    y_prompt = _final_norm(xp, fg, _tile_rows(bp * tp, 512)).reshape(bp, tp, D_MODEL)
    y_sample = _final_norm(xs, fg, bs * ts).reshape(bs, ts, D_MODEL)
    return (y_prompt, y_sample, *[jnp.stack(a) for a in p_out], *[jnp.stack(a) for a in s_out])
```

```python
import functools
import math

import jax
import jax.numpy as jnp
from jax import lax
from jax.experimental import pallas as pl
from jax.experimental.pallas import tpu as pltpu

F32 = jnp.float32
BF16 = jnp.bfloat16
HI = lax.Precision.HIGHEST

D_MODEL = 1024
HEAD_DIM = 64
A_HEADS = 4
B_HEADS = 6
C_HEADS = 6
A_W = A_HEADS * HEAD_DIM
B_W = B_HEADS * HEAD_DIM
C_W = C_HEADS * HEAD_DIM
CHUNK = 128
PATTERNS = ((128, 1), (512, 4), (2048, 16))
MAX_WINDOW = 2048
ROT_DIM = HEAD_DIM // 4
ROPE_THETA = 500000.0
CONV_W = 4
DELTA_CHUNK = 64
N_GROUPS = 4
EXP_PER_GROUP = 8
N_EXPERTS = N_GROUPS * EXP_PER_GROUP
D_EXPERT = 256
EPS = 1e-6
PAST_LEN = 16384

LANES = 128
SUBLANES = 8
WBLK = 128
PROJ_MAIN = 2 * A_W + 3 * B_W + 4 * C_W
NEG = -1e30
ROUTE_OFF = N_GROUPS
VMEM_LIMIT = 56 * 1024 * 1024
GDN_GROUP = 1


def _cparams(sem):
    return pltpu.CompilerParams(dimension_semantics=sem, vmem_limit_bytes=VMEM_LIMIT)


def _sigmoid(x):
    return 1.0 / (1.0 + jnp.exp(-x))


def _silu(x):
    return x * _sigmoid(x)


def _dot(a, b, precision=None):
    return jnp.dot(a, b, preferred_element_type=F32, precision=precision)


def _dot_nt(a, b, precision=None):
    return lax.dot_general(a, b, (((1,), (1,)), ((), ())), preferred_element_type=F32, precision=precision)


def _proj_kernel(x_ref, g_ref, w_ref, wsm_ref, c_ref, s1_ref, s2_ref,
                 au_ref, av_ref, q_ref, k_ref, v_ref, cqkv_ref, cz_ref, cba_ref):
    x = x_ref[...]
    h = x * lax.rsqrt(jnp.mean(x * x, axis=-1, keepdims=True) + EPS) * g_ref[...]
    hb = h.astype(BF16)

    def seg(a, b):
        return _dot(hb, w_ref[:, a:b])

    au_ref[...] = seg(0, A_W)
    av_ref[...] = seg(A_W, 2 * A_W)
    c, s1, s2 = c_ref[...], s1_ref[...], s2_ref[...]
    q0 = 2 * A_W
    k0 = q0 + B_W
    for j in range(B_W // LANES):
        for base, ref, scale in ((q0, q_ref, HEAD_DIM ** -0.5), (k0, k_ref, None)):
            xc = seg(base + LANES * j, base + LANES * (j + 1))
            r = xc * c + pltpu.roll(xc, ROT_DIM // 2, 1) * s1 + pltpu.roll(xc, LANES - ROT_DIM // 2, 1) * s2
            if scale is not None:
                r = r * scale
            ref[:, LANES * j:LANES * (j + 1)] = r
    v0 = k0 + B_W
    v_ref[...] = seg(v0, v0 + B_W)
    c0 = v0 + B_W
    cqkv_ref[...] = seg(c0, c0 + 3 * C_W)
    cz_ref[...] = seg(c0 + 3 * C_W, c0 + 4 * C_W)
    cba_ref[...] = _dot(hb, wsm_ref[...])


def _proj(x, g, w_main, w_small, tabs, tm):
    n = x.shape[0]
    nt = n // tm
    ntab = tabs[0].shape[0] // tm
    row = lambda i: (i, 0)
    fixed = lambda i: (0, 0)
    tab = lambda i: (i % ntab, 0)
    widths = (A_W, A_W, B_W, B_W, B_W, 3 * C_W, C_W, LANES)
    return pl.pallas_call(
        _proj_kernel,
        grid=(nt,),
        in_specs=[pl.BlockSpec((tm, D_MODEL), row), pl.BlockSpec((1, D_MODEL), fixed),
                  pl.BlockSpec((D_MODEL, PROJ_MAIN), fixed), pl.BlockSpec((D_MODEL, LANES), fixed),
                  pl.BlockSpec((tm, LANES), tab), pl.BlockSpec((tm, LANES), tab), pl.BlockSpec((tm, LANES), tab)],
        out_specs=[pl.BlockSpec((tm, w), row) for w in widths],
        out_shape=[jax.ShapeDtypeStruct((n, w), F32) for w in widths],
        compiler_params=_cparams(("parallel",)),
        name="proj",
    )(x, g, w_main, w_small, *tabs)


def _rope_tables(pos):
    half = ROT_DIM // 2
    inv_freq = jnp.power(ROPE_THETA, -jnp.arange(0, ROT_DIM, 2, dtype=F32) / ROT_DIM)
    ang = pos.astype(F32)[:, None] * inv_freq[None, :]
    cos, sin = jnp.cos(ang), jnp.sin(ang)
    p = pos.shape[0]
    z8 = jnp.zeros((p, half), F32)
    rest0 = jnp.zeros((p, HEAD_DIM - ROT_DIM), F32)
    c64 = jnp.concatenate([cos, cos, jnp.ones((p, HEAD_DIM - ROT_DIM), F32)], axis=-1)
    s1 = jnp.concatenate([z8, sin, rest0], axis=-1)
    s2 = jnp.concatenate([-sin, z8, rest0], axis=-1)
    two = lambda a: jnp.concatenate([a, a], axis=-1)
    return two(c64), two(s1), two(s2)


def _chunk_mlp_kernel(u_ref, v_ref, g_ref, w_ref, b_ref, o_ref, vn_ref):
    v = v_ref[...]
    xc = v - jnp.mean(v, axis=-1, keepdims=True)
    vn = xc * lax.rsqrt(jnp.mean(xc * xc, axis=-1, keepdims=True) + EPS) * g_ref[...]
    vn_ref[...] = vn
    vb = vn.astype(BF16)
    rows = lax.broadcasted_iota(jnp.int32, (CHUNK, CHUNK), 0)
    cols = lax.broadcasted_iota(jnp.int32, (CHUNK, CHUNK), 1)
    tril = rows >= cols
    parts = []
    for h in range(A_HEADS):
        w = jnp.where(tril, w_ref[h], 0.0).astype(BF16)
        parts.append(_dot(w, vb[:, h * HEAD_DIM:(h + 1) * HEAD_DIM]))
    s = jnp.concatenate(parts, axis=-1) + b_ref[...]
    o_ref[...] = u_ref[...] * s


def _chunk_mlp(u, v, gain, ws, bias_tile):
    n = u.shape[0]
    row = lambda i: (i, 0)
    return pl.pallas_call(
        _chunk_mlp_kernel,
        grid=(n // CHUNK,),
        in_specs=[pl.BlockSpec((CHUNK, A_W), row), pl.BlockSpec((CHUNK, A_W), row),
                  pl.BlockSpec((1, A_W), lambda i: (0, 0)),
                  pl.BlockSpec((A_HEADS, CHUNK, CHUNK), lambda i: (0, 0, 0)),
                  pl.BlockSpec((CHUNK, A_W), lambda i: (0, 0))],
        out_specs=[pl.BlockSpec((CHUNK, A_W), row), pl.BlockSpec((CHUNK, A_W), row)],
        out_shape=[jax.ShapeDtypeStruct((n, A_W), F32)] * 2,
        compiler_params=_cparams(("parallel",)),
        name="chunk_mlp",
    )(u, v, gain, ws, bias_tile)


def _win_attn_kernel(q_ref, kp_ref, kc_ref, vp_ref, vc_ref, o_ref, lse_ref, *, seg_blocks):
    p = pl.program_id(0)
    s = pl.program_id(1)
    nb = jnp.where(p == 0, seg_blocks[0], jnp.where(p == 1, seg_blocks[1], seg_blocks[2]))
    lo = jnp.where(s % nb == 0, WBLK, 0)
    rows = lax.broadcasted_iota(jnp.int32, (WBLK, 2 * WBLK), 0)
    cols = lax.broadcasted_iota(jnp.int32, (WBLK, 2 * WBLK), 1)
    dist = rows + WBLK - cols
    valid = (dist >= 0) & (dist <= WBLK) & (cols >= lo)
    bias = jnp.where(valid, 0.0, NEG)
    lane = lax.broadcasted_iota(jnp.int32, (WBLK, LANES), 1)
    k2 = jnp.concatenate([kp_ref[...], kc_ref[...]], axis=0).astype(BF16)
    v2 = jnp.concatenate([vp_ref[...], vc_ref[...]], axis=0).astype(BF16)
    q = q_ref[...]
    lse_tile = jnp.zeros((WBLK, LANES), F32)
    for hp in range(B_W // LANES):
        sl = slice(hp * LANES, (hp + 1) * LANES)
        qp, kp, vp = q[:, sl], k2[:, sl], v2[:, sl]
        acc = None
        for half in range(2):
            hm = (lane < HEAD_DIM) if half == 0 else (lane >= HEAD_DIM)
            qh = jnp.where(hm, qp, 0.0).astype(BF16)
            sc = _dot_nt(qh, kp) + bias
            m = jnp.max(sc, axis=-1, keepdims=True)
            e = jnp.exp(sc - m)
            den = jnp.sum(e, axis=-1, keepdims=True)
            pm = (e * (1.0 / den)).astype(BF16)
            o = _dot(pm, vp)
            acc = o if acc is None else jnp.where(hm, o, acc)
            lse_tile = jnp.where(lane == 2 * hp + half, m + jnp.log(den), lse_tile)
        o_ref[:, sl] = acc
    lse_ref[...] = lse_tile


def _win_attn(qd, kd, vd, seg_blocks):
    npat, n, _ = qd.shape
    nblk = n // WBLK
    cur = lambda p, s: (p, s, 0)
    prev = lambda p, s: (p, jnp.maximum(s - 1, 0), 0)
    blk = lambda w, im: pl.BlockSpec((None, WBLK, w), im)
    return pl.pallas_call(
        functools.partial(_win_attn_kernel, seg_blocks=seg_blocks),
        grid=(npat, nblk),
        in_specs=[blk(B_W, cur), blk(B_W, prev), blk(B_W, cur), blk(B_W, prev), blk(B_W, cur)],
        out_specs=[blk(B_W, cur), blk(LANES, cur)],
        out_shape=[jax.ShapeDtypeStruct((npat, n, B_W), F32), jax.ShapeDtypeStruct((npat, n, LANES), F32)],
        compiler_params=_cparams(("parallel", "parallel")),
        name="win_attn",
    )(qd, kd, kd, vd, vd)


def _dec_attn_kernel(q_ref, kc_ref, vc_ref, kn_ref, vn_ref, o_ref, *, t_new, cache_len):
    rows_c = lax.broadcasted_iota(jnp.int32, (SUBLANES, cache_len), 0)
    cols_c = lax.broadcasted_iota(jnp.int32, (SUBLANES, cache_len), 1)
    dist_c = cache_len + rows_c % t_new - cols_c
    rows_n = lax.broadcasted_iota(jnp.int32, (SUBLANES, SUBLANES), 0)
    cols_n = lax.broadcasted_iota(jnp.int32, (SUBLANES, SUBLANES), 1)
    dist_n = rows_n % t_new - cols_n
    biases = []
    for window, dil in PATTERNS:
        vc_ok = (dist_c <= window) & ((dist_c & (dil - 1)) == 0)
        vn_ok = (dist_n >= 0) & ((dist_n & (dil - 1)) == 0)
        biases.append((jnp.where(vc_ok, 0.0, NEG), jnp.where(vn_ok, 0.0, NEG)))
    row8 = lax.broadcasted_iota(jnp.int32, (SUBLANES, LANES), 0)
    lane8 = lax.broadcasted_iota(jnp.int32, (SUBLANES, LANES), 1)
    own = (lane8 < HEAD_DIM) == (row8 < t_new)
    q = q_ref[...]
    outs = []
    for hp in range(B_W // LANES):
        sl = slice(hp * LANES, (hp + 1) * LANES)
        qq = jnp.concatenate([q[0:t_new, sl], q[0:t_new, sl]], axis=0)
        q8 = jnp.where(own, qq, 0.0).astype(BF16)
        kc, vc = kc_ref[:, sl].astype(BF16), vc_ref[:, sl].astype(BF16)
        kn, vn = kn_ref[:, sl].astype(BF16), vn_ref[:, sl].astype(BF16)
        sc_c = _dot_nt(q8, kc)
        sc_n = _dot_nt(q8, kn)
        os_, lses = [], []
        for bc, bn in biases:
            a_c, a_n = sc_c + bc, sc_n + bn
            m = jnp.maximum(jnp.max(a_c, axis=-1, keepdims=True), jnp.max(a_n, axis=-1, keepdims=True))
            e_c, e_n = jnp.exp(a_c - m), jnp.exp(a_n - m)
            den = jnp.sum(e_c, axis=-1, keepdims=True) + jnp.sum(e_n, axis=-1, keepdims=True)
            inv = 1.0 / den
            os_.append(_dot((e_c * inv).astype(BF16), vc) + _dot((e_n * inv).astype(BF16), vn))
            lses.append(m + jnp.log(den))
        mx = jnp.maximum(jnp.maximum(lses[0], lses[1]), lses[2])
        ws = [jnp.exp(l - mx) for l in lses]
        tot = ws[0] + ws[1] + ws[2]
        o8 = (ws[0] / tot) * os_[0] + (ws[1] / tot) * os_[1] + (ws[2] / tot) * os_[2]
        lane4 = lane8[0:t_new]
        outs.append(jnp.where(lane4 < HEAD_DIM, o8[0:t_new], o8[t_new:2 * t_new]))
    o_ref[...] = jnp.zeros((SUBLANES, B_W), F32)
    o_ref[0:t_new, :] = jnp.concatenate(outs, axis=-1)


def _dec_attn(q8, kc, vc, kn8, vn8, t_new):
    b, cache_len, _ = kc.shape
    assert 2 * t_new == SUBLANES
    small = pl.BlockSpec((None, SUBLANES, B_W), lambda i: (i, 0, 0))
    big = pl.BlockSpec((None, cache_len, B_W), lambda i: (i, 0, 0))
    return pl.pallas_call(
        functools.partial(_dec_attn_kernel, t_new=t_new, cache_len=cache_len),
        grid=(b,),
        in_specs=[small, big, big, small, small],
        out_specs=small,
        out_shape=jax.ShapeDtypeStruct((b, SUBLANES, B_W), F32),
        compiler_params=_cparams(("parallel",)),
        name="dec_attn",
    )(q8, kc, vc, kn8, vn8)


def _split3(x):
    hi = x.astype(BF16)
    r1 = x - hi.astype(F32)
    mid = r1.astype(BF16)
    return hi, mid, (r1 - mid.astype(F32)).astype(BF16)


def _pair_dup(xx):
    lane = lax.broadcasted_iota(jnp.int32, xx.shape, 1)
    hi = xx.astype(BF16).astype(F32)
    return jnp.where(lane < HEAD_DIM, xx, xx - hi).astype(BF16)


def _pair(x):
    return _pair_dup(jnp.concatenate([x, x], axis=1))


def _lhs4(pair):
    return jnp.concatenate([pair, pair], axis=1)


def _rhs4(y):
    hi = y.astype(BF16)
    lo = (y - hi.astype(F32)).astype(BF16)
    return jnp.concatenate([hi, hi, lo, lo], axis=0)


def _unit_lower_solves(mats, rhss):
    n = DELTA_CHUNK
    w2 = 2 * HEAD_DIM
    rows = lax.broadcasted_iota(jnp.int32, (n, w2), 0)
    cols = lax.broadcasted_iota(jnp.int32, (n, w2), 1) % HEAD_DIM
    in16 = rows // 16 == cols // 16
    eye = jnp.where(rows == cols, 1.0, 0.0)
    ds = [jnp.where(in16, a, 0.0) for a in mats]
    es = [jnp.where(in16, 0.0, a) for a in mats]
    left = lambda xx: _lhs4(_pair_dup(xx))
    pw = [_dot(left(d), _rhs4(d)) for d in ds]
    ts = [eye - d for d in ds]
    for _ in range(2):
        outs = [_dot(jnp.concatenate([left(t), left(p)], axis=0), _rhs4(p)) for t, p in zip(ts, pw)]
        ts = [t + o[0:n] for t, o in zip(ts, outs)]
        pw = [o[n:2 * n] for o in outs]
    ts = [t + _dot(left(t), _rhs4(p)) for t, p in zip(ts, pw)]
    o5 = [_dot(left(t), _rhs4(jnp.concatenate([r, e], axis=1))) for t, r, e in zip(ts, rhss, es)]
    x0 = [o[:, 0:w2] for o in o5]
    nm = [o[:, w2:2 * w2] for o in o5]
    o6 = [_dot(left(m), _rhs4(jnp.concatenate([x, m], axis=1))) for m, x in zip(nm, x0)]
    ys = [x - o[:, 0:w2] for x, o in zip(x0, o6)]
    return [y + _dot(left(o[:, w2:2 * w2]), _rhs4(y)) for y, o in zip(ys, o6)]


def _gdn_prep_kernel(x_ref, cba_ref, buf_ref, cw_ref, alog_ref, dtb_ref, gmat_ref,
                     u_ref, wq_ref, ak_ref, gt_ref,
                     xp_scr, q_scr, k_scr, v_scr, gb_scr, *, tm, valid_len, group):
    ti = pl.program_id(1)

    @pl.when(ti == 0)
    def _():
        xp_scr[0:SUBLANES, :] = buf_ref[...]

    x = x_ref[...]
    xp_scr[SUBLANES:SUBLANES + tm, :] = x
    off = SUBLANES - (CONV_W - 1)
    acc = xp_scr[off:off + tm, :] * cw_ref[0:1, :]
    for j in range(1, CONV_W):
        acc = acc + xp_scr[off + j:off + j + tm, :] * cw_ref[j:j + 1, :]
    xp_scr[0:SUBLANES, :] = x[tm - SUBLANES:tm, :]
    y = _silu(acc)
    q, k = y[:, 0:C_W], y[:, C_W:2 * C_W]
    gmat = gmat_ref[...]
    head_sum = lambda a: sum(_dot(p, gmat) for p in _split3(a))
    q_scr[...] = q * lax.rsqrt(head_sum(q * q) + EPS) * (HEAD_DIM ** -0.5)
    k_scr[...] = k * lax.rsqrt(head_sum(k * k) + EPS)
    v_scr[...] = y[:, 2 * C_W:3 * C_W]

    cba = cba_ref[...]
    lane = lax.broadcasted_iota(jnp.int32, (tm, LANES), 1)
    tpos = ti * tm + lax.broadcasted_iota(jnp.int32, (tm, LANES), 0)
    live = tpos < valid_len
    beta = _sigmoid(cba)
    z = cba + dtb_ref[...]
    softplus = jnp.maximum(z, 0.0) + jnp.log(1.0 + jnp.exp(-jnp.abs(z)))
    g = -jnp.exp(alog_ref[...]) * softplus
    is_g = (lane >= C_HEADS) & (lane < 2 * C_HEADS)
    gb_scr[...] = jnp.where(live, jnp.where(is_g, g, jnp.where(lane < C_HEADS, beta, 0.0)), 0.0)

    n = DELTA_CHUNK
    rows = lax.broadcasted_iota(jnp.int32, (n, 2 * n), 0)
    cols = lax.broadcasted_iota(jnp.int32, (n, 2 * n), 1) % n
    incl = rows >= cols
    strict = rows > cols
    ltri = jnp.where(incl[:, 0:n], 1.0, 0.0).astype(BF16)
    heads = range(C_HEADS)
    hsl = [slice(h * HEAD_DIM, (h + 1) * HEAD_DIM) for h in heads]
    gcol = [slice(C_HEADS + h, C_HEADS + h + 1) for h in heads]

    lane_c = lax.broadcasted_iota(jnp.int32, (n, LANES), 1)
    lane2 = lax.broadcasted_iota(jnp.int32, (n, 2 * HEAD_DIM), 1)

    def setup(c):
        rs = pl.ds(pl.multiple_of(c * n, n), n)
        gb = gb_scr[rs, :]
        g_only = jnp.where((lane_c >= C_HEADS) & (lane_c < 2 * C_HEADS), gb, 0.0)
        cg = sum(_dot(ltri, p) for p in _split3(g_only))
        cgt = cg.T
        cgl = cg[n - 1:n, :]
        qn, kn, vv = q_scr[rs, :], k_scr[rs, :], v_scr[rs, :]
        cgc = [cg[:, gcol[h]] for h in heads]
        cgr = [jnp.concatenate([cgt[gcol[h], :], cgt[gcol[h], :]], axis=1) for h in heads]
        decay = [jnp.where(incl, jnp.exp(jnp.minimum(cgc[h] - cgr[h], 0.0)), 0.0) for h in heads]
        bcol = [gb[:, h:h + 1] for h in heads]
        kb = [kn[:, s].astype(BF16) for s in hsl]
        kkqk = [_dot_nt(jnp.concatenate([kb[h], qn[:, hsl[h]].astype(BF16)], axis=0),
                        jnp.concatenate([kb[h], kb[h]], axis=0)) for h in heads]
        amat = [jnp.where(strict, bcol[h] * kkqk[h][0:n] * decay[h], 0.0) for h in heads]
        at_pair = [_pair_dup(kkqk[h][n:2 * n] * decay[h]) for h in heads]
        ecg = [jnp.exp(x) for x in cgc]
        rhs = [jnp.concatenate([bcol[h] * vv[:, hsl[h]], (bcol[h] * ecg[h]) * kn[:, hsl[h]]], axis=1)
               for h in heads]
        qd_pair = [_pair(qn[:, hsl[h]] * ecg[h]) for h in heads]
        kdt_pair = [_pair((kn[:, hsl[h]] * jnp.exp(cgl[:, gcol[h]] - cgc[h])).T) for h in heads]
        gt_ref[pl.ds(pl.multiple_of(c * SUBLANES, SUBLANES), SUBLANES), :] = jnp.broadcast_to(
            jnp.exp(cgl), (SUBLANES, LANES))
        return amat, rhs, at_pair, qd_pair, kdt_pair

    def finish(c, sol, at_pair, qd_pair, kdt_pair):
        u_ref[pl.ds(pl.multiple_of(c * n, n), n), :] = jnp.concatenate([s[:, 0:HEAD_DIM] for s in sol], axis=1)
        w_pair = [_pair_dup(jnp.where(lane2 < HEAD_DIM, pltpu.roll(s, HEAD_DIM, 1), s)) for s in sol]
        r2 = pl.ds(pl.multiple_of(c * 2 * n, 2 * n), 2 * n)
        wq_ref[r2, :] = jnp.concatenate(
            [jnp.concatenate([w_pair[h], qd_pair[h]], axis=0) for h in heads], axis=1)
        ak_ref[r2, :] = jnp.concatenate(
            [jnp.concatenate([at_pair[h], kdt_pair[h]], axis=0) for h in heads], axis=1)

    def chunks(i, carry):
        parts = [setup(i * group + j) for j in range(group)]
        sol = _unit_lower_solves([a for p in parts for a in p[0]], [r for p in parts for r in p[1]])
        for j, p in enumerate(parts):
            finish(i * group + j, sol[j * C_HEADS:(j + 1) * C_HEADS], *p[2:])
        return carry

    lax.fori_loop(0, tm // n // group, chunks, 0)


def _gdn_prep(cqkv, cba, buf8, conv_w, alog_row, dtb_row, gmat, batch, t, tm, valid_len):
    n = batch * t
    nt = t // tm
    row = lambda b, i: (b * nt + i, 0)
    fixed = lambda b, i: (0, 0)
    nch = tm // DELTA_CHUNK
    pair_w = C_HEADS * 2 * HEAD_DIM
    outs = [jax.ShapeDtypeStruct((n, C_W), F32), jax.ShapeDtypeStruct((2 * n, pair_w), BF16),
            jax.ShapeDtypeStruct((2 * n, pair_w), BF16),
            jax.ShapeDtypeStruct((n // DELTA_CHUNK * SUBLANES, LANES), F32)]
    return pl.pallas_call(
        functools.partial(_gdn_prep_kernel, tm=tm, valid_len=valid_len, group=min(GDN_GROUP, nch)),
        grid=(batch, nt),
        in_specs=[pl.BlockSpec((tm, 3 * C_W), row), pl.BlockSpec((tm, LANES), row),
                  pl.BlockSpec((None, SUBLANES, 3 * C_W), lambda b, i: (b, 0, 0)),
                  pl.BlockSpec((CONV_W, 3 * C_W), fixed), pl.BlockSpec((1, LANES), fixed),
                  pl.BlockSpec((1, LANES), fixed), pl.BlockSpec((C_W, C_W), fixed)],
        out_specs=[pl.BlockSpec((tm, C_W), row), pl.BlockSpec((2 * tm, pair_w), row),
                   pl.BlockSpec((2 * tm, pair_w), row), pl.BlockSpec((nch * SUBLANES, LANES), row)],
        out_shape=outs,
        scratch_shapes=[pltpu.VMEM((tm + SUBLANES, 3 * C_W), F32), pltpu.VMEM((tm, C_W), F32),
                        pltpu.VMEM((tm, C_W), F32), pltpu.VMEM((tm, C_W), F32), pltpu.VMEM((tm, LANES), F32)],
        compiler_params=_cparams(("parallel", "arbitrary")),
        name="gdn_prep",
    )(cqkv, cba, buf8, conv_w, alog_row, dtb_row, gmat)


def _gdn_scan_kernel(u_ref, wq_ref, ak_ref, gt_ref, z_ref, s0_ref, gn_ref,
                     o_ref, sfin_ref, s_scr, *, bg):
    c = pl.program_id(1)

    @pl.when(c == 0)
    def _():
        s_scr[...] = s0_ref[...]

    gn = gn_ref[...]
    n = DELTA_CHUNK
    chains = [(b, h, slice(h * HEAD_DIM, (h + 1) * HEAD_DIM), slice(h * 2 * HEAD_DIM, (h + 1) * 2 * HEAD_DIM))
              for b in range(bg) for h in range(C_HEADS)]
    st = [s_scr[b, h] for b, h, _, _ in chains]
    r1 = [_dot(_lhs4(wq_ref[b, :, ps]), _rhs4(s)) for (b, _, _, ps), s in zip(chains, st)]
    up = [u_ref[b, :, sl] - r[0:n] for (b, _, sl, _), r in zip(chains, r1)]
    r2 = [_dot(_lhs4(ak_ref[b, :, ps]), _rhs4(x)) for (b, _, _, ps), x in zip(chains, up)]
    for (b, h, _, _), s, r in zip(chains, st, r2):
        s_scr[b, h] = gt_ref[b, 0:1, C_HEADS + h:C_HEADS + h + 1] * s + r[n:2 * n]
    os_ = [a[n:2 * n] + r[0:n] for a, r in zip(r1, r2)]
    outs = [o * lax.rsqrt(jnp.mean(o * o, axis=-1, keepdims=True) + EPS) * gn * _silu(z_ref[b, :, sl])
            for (b, _, sl, _), o in zip(chains, os_)]
    for b in range(bg):
        o_ref[b] = jnp.concatenate(outs[b * C_HEADS:(b + 1) * C_HEADS], axis=1)

    @pl.when(c == pl.num_programs(1) - 1)
    def _():
        sfin_ref[...] = s_scr[...]


def _gdn_scan(u, wq, ak, gt, z, s0, gnorm, batch, t, bg):
    n = DELTA_CHUNK
    nc = t // n
    pair_w = C_HEADS * 2 * HEAD_DIM
    v3 = lambda a: a.reshape(batch, t, C_W)
    p3 = lambda a: a.reshape(batch, 2 * t, pair_w)
    tok = pl.BlockSpec((bg, n, C_W), lambda b, c: (b, c, 0))
    pair = pl.BlockSpec((bg, 2 * n, pair_w), lambda b, c: (b, c, 0))
    st = pl.BlockSpec((bg, C_HEADS, HEAD_DIM, HEAD_DIM), lambda b, c: (b, 0, 0, 0))
    return pl.pallas_call(
        functools.partial(_gdn_scan_kernel, bg=bg),
        grid=(batch // bg, nc),
        in_specs=[tok, pair, pair,
                  pl.BlockSpec((bg, SUBLANES, LANES), lambda b, c: (b, c, 0)),
                  tok, st, pl.BlockSpec((1, HEAD_DIM), lambda b, c: (0, 0))],
        out_specs=[tok, st],
        out_shape=[jax.ShapeDtypeStruct((batch, t, C_W), F32),
                   jax.ShapeDtypeStruct((batch, C_HEADS, HEAD_DIM, HEAD_DIM), F32)],
        scratch_shapes=[pltpu.VMEM((bg, C_HEADS, HEAD_DIM, HEAD_DIM), F32)],
        compiler_params=_cparams(("parallel", "arbitrary")),
        name="gdn_scan",
    )(v3(u), p3(wq), p3(ak), gt.reshape(batch, nc * SUBLANES, LANES), v3(z), s0, gnorm)


def _head_expand(wt):
    tm = wt.shape[0]
    return jnp.concatenate([jnp.broadcast_to(wt[:, h:h + 1], (tm, HEAD_DIM)) for h in range(B_HEADS)], axis=-1)


def _out_proj_kernel(*refs, combine):
    if combine:
        a_ref, o0, o1, o2, l0, l1, l2, c_ref, x_ref, w_ref, y_ref = refs
        ls = [l0[...], l1[...], l2[...]]
        mx = jnp.maximum(jnp.maximum(ls[0], ls[1]), ls[2])
        es = [jnp.exp(l - mx) for l in ls]
        tot = es[0] + es[1] + es[2]
        ob = (_head_expand(es[0] / tot) * o0[...] + _head_expand(es[1] / tot) * o1[...]
              + _head_expand(es[2] / tot) * o2[...])
    else:
        a_ref, b_ref, c_ref, x_ref, w_ref, y_ref = refs
        ob = b_ref[...]
    cat = jnp.concatenate([a_ref[...], ob, c_ref[...]], axis=-1).astype(BF16)
    y_ref[...] = x_ref[...] + _dot(cat, w_ref[...])


def _out_proj(out_a, out_b, lses, out_c, x, w_out, tm):
    n = x.shape[0]
    row = lambda i: (i, 0)
    spec = lambda w: pl.BlockSpec((tm, w), row)
    combine = lses is not None
    if combine:
        ins = [out_a, *out_b, *lses, out_c, x, w_out]
        specs = [spec(A_W)] + [spec(B_W)] * 3 + [spec(LANES)] * 3 + [spec(C_W), spec(D_MODEL)]
    else:
        ins = [out_a, out_b, out_c, x, w_out]
        specs = [spec(A_W), spec(B_W), spec(C_W), spec(D_MODEL)]
    specs.append(pl.BlockSpec((D_MODEL, D_MODEL), lambda i: (0, 0)))
    return pl.pallas_call(
        functools.partial(_out_proj_kernel, combine=combine),
        grid=(n // tm,),
        in_specs=specs,
        out_specs=spec(D_MODEL),
        out_shape=jax.ShapeDtypeStruct((n, D_MODEL), F32),
        compiler_params=_cparams(("parallel",)),
        name="out_proj",
    )(*ins)


def _router_kernel(x_ref, g_ref, w_ref, dense_ref):
    x = x_ref[...]
    t = x * lax.rsqrt(jnp.mean(x * x, axis=-1, keepdims=True) + EPS) * g_ref[...]
    lg = _dot(t, w_ref[...], HI)
    tm = lg.shape[0]
    lane = lax.broadcasted_iota(jnp.int32, (tm, LANES), 1).astype(F32)
    big = float(LANES)
    is_grp = lane < N_GROUPS
    gl = jnp.where(is_grp, lg, NEG)
    gmax = jnp.max(gl, axis=-1, keepdims=True)
    gsum = jnp.sum(jnp.where(is_grp, jnp.exp(gl - gmax), 0.0), axis=-1, keepdims=True)
    g_w = 1.0 / gsum
    g_idx = jnp.min(jnp.where(is_grp & (gl == gmax), lane, big), axis=-1, keepdims=True)
    lo = ROUTE_OFF + EXP_PER_GROUP * g_idx
    sel = (lane >= lo) & (lane < lo + EXP_PER_GROUP)
    el = jnp.where(sel, lg, NEG)
    m1 = jnp.max(el, axis=-1, keepdims=True)
    esum = jnp.sum(jnp.where(sel, jnp.exp(el - m1), 0.0), axis=-1, keepdims=True)
    i1 = jnp.min(jnp.where(sel & (el == m1), lane, big), axis=-1, keepdims=True)
    el2 = jnp.where(lane == i1, NEG, el)
    m2 = jnp.max(el2, axis=-1, keepdims=True)
    i2 = jnp.min(jnp.where(sel & (lane != i1) & (el2 == m2), lane, big), axis=-1, keepdims=True)
    p1 = 1.0 / esum
    p2 = jnp.exp(m2 - m1) / esum
    tot = p1 + p2
    dense_ref[...] = jnp.where(lane == i1, g_w * (p1 / tot), 0.0) + jnp.where(lane == i2, g_w * (p2 / tot), 0.0)


def _router(x, g, w_route, tm):
    n = x.shape[0]
    row = lambda i: (i, 0)
    return pl.pallas_call(
        _router_kernel,
        grid=(n // tm,),
        in_specs=[pl.BlockSpec((tm, D_MODEL), row), pl.BlockSpec((1, D_MODEL), lambda i: (0, 0)),
                  pl.BlockSpec((D_MODEL, LANES), lambda i: (0, 0))],
        out_specs=pl.BlockSpec((tm, LANES), row),
        out_shape=jax.ShapeDtypeStruct((n, LANES), F32),
        compiler_params=_cparams(("parallel",)),
        name="router",
    )(x, g, w_route)


def _moe_kernel(x_ref, g_ref, dense_ref, w1_ref, w3_ref, w2_ref, y_ref, t_scr):
    e = pl.program_id(1)

    @pl.when(e == 0)
    def _():
        x = x_ref[...]
        t_scr[...] = (x * lax.rsqrt(jnp.mean(x * x, axis=-1, keepdims=True) + EPS) * g_ref[...]).astype(BF16)
        y_ref[...] = x

    tb = t_scr[...]
    dense = dense_ref[...]
    lane = lax.broadcasted_iota(jnp.int32, dense.shape, 1)
    gate = jnp.sum(jnp.where(lane == e + ROUTE_OFF, dense, 0.0), axis=-1, keepdims=True)
    hid = _silu(_dot(tb, w1_ref[...])) * _dot(tb, w3_ref[...])
    y_ref[...] += _dot((hid * gate).astype(BF16), w2_ref[...])


def _moe(x, g, dense, w1, w3, w2, tm):
    n = x.shape[0]
    row = lambda i, e: (i, 0)
    return pl.pallas_call(
        _moe_kernel,
        grid=(n // tm, N_EXPERTS),
        in_specs=[pl.BlockSpec((tm, D_MODEL), row), pl.BlockSpec((1, D_MODEL), lambda i, e: (0, 0)),
                  pl.BlockSpec((tm, LANES), row),
                  pl.BlockSpec((None, D_MODEL, D_EXPERT), lambda i, e: (e, 0, 0)),
                  pl.BlockSpec((None, D_MODEL, D_EXPERT), lambda i, e: (e, 0, 0)),
                  pl.BlockSpec((None, D_EXPERT, D_MODEL), lambda i, e: (e, 0, 0))],
        out_specs=pl.BlockSpec((tm, D_MODEL), row),
        out_shape=jax.ShapeDtypeStruct((n, D_MODEL), F32),
        scratch_shapes=[pltpu.VMEM((tm, D_MODEL), BF16)],
        compiler_params=_cparams(("parallel", "arbitrary")),
        name="moe",
    )(x, g, dense, w1, w3, w2)


def _final_norm_kernel(x_ref, g_ref, y_ref):
    x = x_ref[...]
    y_ref[...] = x * lax.rsqrt(jnp.mean(x * x, axis=-1, keepdims=True) + EPS) * g_ref[...]


def _final_norm(x, g, tm):
    n = x.shape[0]
    return pl.pallas_call(
        _final_norm_kernel,
        grid=(n // tm,),
        in_specs=[pl.BlockSpec((tm, D_MODEL), lambda i: (i, 0)), pl.BlockSpec((1, D_MODEL), lambda i: (0, 0))],
        out_specs=pl.BlockSpec((tm, D_MODEL), lambda i: (i, 0)),
        out_shape=jax.ShapeDtypeStruct((n, D_MODEL), F32),
        compiler_params=_cparams(("parallel",)),
        name="final_norm",
    )(x, g)


def _deinterleave(x, batch, t, d):
    w = x.shape[-1]
    return x.reshape(batch, t // d, d, w).transpose(0, 2, 1, 3).reshape(batch * t, w)


def _interleave(x, batch, t, d):
    w = x.shape[-1]
    return x.reshape(batch, d, t // d, w).transpose(0, 2, 1, 3).reshape(batch * t, w)


def _tile_rows(n, cap):
    tm = min(n, cap)
    assert n % tm == 0
    return tm


def _layer_weights(l, norm1_g, w_in, a_vnorm_g, a_ws, a_bs, c_conv_w, c_a_log, c_dt_bias, c_norm_g,
                   w_out, norm2_g, w_group, w_router, w1, w3, w2):
    wi = w_in[l]
    pad_l = lambda a, left: jnp.pad(a, ((0, 0), (left, LANES - left - a.shape[-1])))
    return dict(
        norm1_g=norm1_g[l][None, :],
        w_main=wi[:, :PROJ_MAIN].astype(BF16),
        w_small=pad_l(wi[:, PROJ_MAIN:], 0).astype(BF16),
        a_gain=a_vnorm_g[l][None, :],
        a_ws=a_ws[l],
        a_bs=a_bs[l],
        conv_w=c_conv_w[l],
        alog_row=pad_l(c_a_log[l][None, :], C_HEADS),
        dtb_row=pad_l(c_dt_bias[l][None, :], C_HEADS),
        gnorm=c_norm_g[l][None, :],
        w_out=w_out[l].astype(BF16),
        norm2_g=norm2_g[l][None, :],
        w_route=pad_l(jnp.concatenate([w_group[l], w_router[l]], axis=-1), 0),
        w1=w1[l].astype(BF16), w3=w3[l].astype(BF16), w2=w2[l].astype(BF16),
    )


def _mixer_c(lw, cqkv, cba, cz, conv_buf, s0, gmat, batch, t, tm, valid_len, bg):
    buf8 = jnp.pad(conv_buf, ((0, 0), (SUBLANES - (CONV_W - 1), 0), (0, 0)))
    u, wq, ak, gt = _gdn_prep(cqkv, cba, buf8, lw["conv_w"], lw["alog_row"], lw["dtb_row"], gmat,
                              batch, t, tm, valid_len)
    return _gdn_scan(u, wq, ak, gt, cz, s0, lw["gnorm"], batch, t, bg)


def _ffn(lw, x):
    n = x.shape[0]
    dense = _router(x, lw["norm2_g"], lw["w_route"], _tile_rows(n, 512))
    return _moe(x, lw["norm2_g"], dense, lw["w1"], lw["w3"], lw["w2"], _tile_rows(n, 1024))


def _prompt_layer(lw, x, batch, t, tabs, gmat):
    n = batch * t
    au, av, q, k, v, cqkv, cz, cba = _proj(x, lw["norm1_g"], lw["w_main"], lw["w_small"], tabs, _tile_rows(n, 512))

    bias_tile = jnp.repeat(lw["a_bs"].T, HEAD_DIM, axis=1)
    out_a, _ = _chunk_mlp(au, av, lw["a_gain"], lw["a_ws"], bias_tile)

    dils = [d for _, d in PATTERNS]
    stack = lambda a: jnp.stack([_deinterleave(a, batch, t, d) for d in dils])
    seg_blocks = tuple(t // d // WBLK for d in dils)
    o_d, lse_d = _win_attn(stack(q), stack(k), stack(v), seg_blocks)
    outs = [_interleave(o_d[i], batch, t, d) for i, d in enumerate(dils)]
    lses = [_interleave(lse_d[i], batch, t, d) for i, d in enumerate(dils)]

    zeros_buf = jnp.zeros((batch, CONV_W - 1, 3 * C_W), F32)
    zeros_s = jnp.zeros((batch, C_HEADS, HEAD_DIM, HEAD_DIM), F32)
    out_c, s_new = _mixer_c(lw, cqkv, cba, cz, zeros_buf, zeros_s, gmat, batch, t, 256, t, batch)
    out_c = out_c.reshape(n, C_W)

    x = _out_proj(out_a, outs, lses, out_c, x, lw["w_out"], _tile_rows(n, 512))
    x = _ffn(lw, x)

    keep = min(MAX_WINDOW, t)
    heads = lambda a: a.reshape(batch, t, B_HEADS, HEAD_DIM)[:, t - keep:]
    conv_state = cqkv.reshape(batch, t, 3 * C_W)[:, t - (CONV_W - 1):]
    return x, (heads(k), heads(v), conv_state, s_new)


def _sample_layer(lw, x, batch, t, tabs, gmat, kbuf, vbuf, conv_buf, s0):
    n = batch * t
    au, av, q, k, v, cqkv, cz, cba = _proj(x, lw["norm1_g"], lw["w_main"], lw["w_small"], tabs, n)

    eye = jnp.eye(batch, dtype=F32)
    ws_bd = jnp.stack([jnp.kron(eye, lw["a_ws"][h, :t, :t]) for h in range(A_HEADS)])
    bias_tile = jnp.tile(jnp.repeat(lw["a_bs"][:, :t].T, HEAD_DIM, axis=1), (batch, 1))
    out_a, a_rows = _chunk_mlp(au, av, lw["a_gain"], ws_bd, bias_tile)

    pad8 = lambda a: jnp.pad(a.reshape(batch, t, B_W), ((0, 0), (0, SUBLANES - t), (0, 0)))
    cache_len = kbuf.shape[1]
    out_b = _dec_attn(pad8(q), kbuf.reshape(batch, cache_len, B_W), vbuf.reshape(batch, cache_len, B_W),
                      pad8(k), pad8(v), t)[:, :t].reshape(n, B_W)

    tp = DELTA_CHUNK
    padt = lambda a: jnp.pad(a.reshape(batch, t, -1), ((0, 0), (0, tp - t), (0, 0))).reshape(batch * tp, -1)
    out_c, s_new = _mixer_c(lw, padt(cqkv), padt(cba), padt(cz), conv_buf, s0, gmat, batch, tp, tp, t, 4)
    out_c = out_c[:, :t].reshape(n, C_W)

    x = _out_proj(out_a, out_b, None, out_c, x, lw["w_out"], n)
    x = _ffn(lw, x)

    heads = lambda a: a.reshape(batch, t, B_HEADS, HEAD_DIM)
    conv_state = jnp.concatenate([conv_buf, cqkv.reshape(batch, t, 3 * C_W)], axis=1)[:, -(CONV_W - 1):]
    return x, (heads(k), heads(v), a_rows.reshape(batch, t, A_W), conv_state, s_new)


def kernel(x_prompt, x_sample, cache_win_k, cache_win_v, state_conv, state_delta, norm1_g, w_in, a_vnorm_g, a_ws, a_bs, c_conv_w, c_a_log, c_dt_bias, c_norm_g, w_out, norm2_g, w_group, w_router, w1, w3, w2, final_g):
    bp, tp, _ = x_prompt.shape
    bs, ts, _ = x_sample.shape
    depth = w_in.shape[0]
    assert tp % (PATTERNS[-1][1] * WBLK) == 0 and tp % 512 == 0 and bs * ts == CHUNK

    tabs_p = _rope_tables(jnp.arange(tp, dtype=jnp.int32))
    tabs_s = tuple(jnp.tile(a, (bs, 1)) for a in _rope_tables(PAST_LEN + jnp.arange(ts, dtype=jnp.int32)))
    hid = jnp.arange(C_W, dtype=jnp.int32) // HEAD_DIM
    gmat = (hid[:, None] == hid[None, :]).astype(BF16)

    xp = x_prompt.reshape(bp * tp, D_MODEL)
    xs = x_sample.reshape(bs * ts, D_MODEL)
    p_out = [[] for _ in range(4)]
    s_out = [[] for _ in range(5)]
    for l in range(depth):
        lw = _layer_weights(l, norm1_g, w_in, a_vnorm_g, a_ws, a_bs, c_conv_w, c_a_log, c_dt_bias, c_norm_g,
                            w_out, norm2_g, w_group, w_router, w1, w3, w2)
        xp, st = _prompt_layer(lw, xp, bp, tp, tabs_p, gmat)
        for acc, a in zip(p_out, st):
            acc.append(a)
        xs, st = _sample_layer(lw, xs, bs, ts, tabs_s, gmat, cache_win_k[l], cache_win_v[l],
                               state_conv[l], state_delta[l])
        for acc, a in zip(s_out, st):
            acc.append(a)
    fg = final_g[None, :]
    y_prompt = _final_norm(xp, fg, _tile_rows(bp * tp, 512)).reshape(bp, tp, D_MODEL)
    y_sample = _final_norm(xs, fg, bs * ts).reshape(bs, ts, D_MODEL)
    return (y_prompt, y_sample, *[jnp.stack(a) for a in p_out], *[jnp.stack(a) for a in s_out])
```

```python
import functools
import math

import jax
import jax.numpy as jnp
from jax import lax
from jax.experimental import pallas as pl
from jax.experimental.pallas import tpu as pltpu

F32 = jnp.float32
BF16 = jnp.bfloat16
HI = lax.Precision.HIGHEST

D_MODEL = 1024
HEAD_DIM = 64
A_HEADS = 4
B_HEADS = 6
C_HEADS = 6
A_W = A_HEADS * HEAD_DIM
B_W = B_HEADS * HEAD_DIM
C_W = C_HEADS * HEAD_DIM
CHUNK = 128
PATTERNS = ((128, 1), (512, 4), (2048, 16))
MAX_WINDOW = 2048
ROT_DIM = HEAD_DIM // 4
ROPE_THETA = 500000.0
CONV_W = 4
DELTA_CHUNK = 64
N_GROUPS = 4
EXP_PER_GROUP = 8
N_EXPERTS = N_GROUPS * EXP_PER_GROUP
D_EXPERT = 256
EPS = 1e-6
PAST_LEN = 16384

LANES = 128
SUBLANES = 8
WBLK = 128
PROJ_MAIN = 2 * A_W + 3 * B_W + 4 * C_W
NEG = -1e30
ROUTE_OFF = N_GROUPS
VMEM_LIMIT = 56 * 1024 * 1024
GDN_GROUP = 1


def _cparams(sem):
    return pltpu.CompilerParams(dimension_semantics=sem, vmem_limit_bytes=VMEM_LIMIT)


def _sigmoid(x):
    return 1.0 / (1.0 + jnp.exp(-x))


def _silu(x):
    return x * _sigmoid(x)


def _dot(a, b, precision=None):
    return jnp.dot(a, b, preferred_element_type=F32, precision=precision)


def _dot_nt(a, b, precision=None):
    return lax.dot_general(a, b, (((1,), (1,)), ((), ())), preferred_element_type=F32, precision=precision)


def _proj_kernel(x_ref, g_ref, w_ref, wsm_ref, c_ref, s1_ref, s2_ref,
                 au_ref, av_ref, q_ref, k_ref, v_ref, cqkv_ref, cz_ref, cba_ref, *strided_refs, dils):
    x = x_ref[...]
    h = x * lax.rsqrt(jnp.mean(x * x, axis=-1, keepdims=True) + EPS) * g_ref[...]
    hb = h.astype(BF16)

    def seg(a, b):
        return _dot(hb, w_ref[:, a:b])

    au_ref[...] = seg(0, A_W)
    av_ref[...] = seg(A_W, 2 * A_W)
    c, s1, s2 = c_ref[...], s1_ref[...], s2_ref[...]
    q0 = 2 * A_W
    k0 = q0 + B_W
    for j in range(B_W // LANES):
        for base, ref, scale in ((q0, q_ref, HEAD_DIM ** -0.5), (k0, k_ref, None)):
            xc = seg(base + LANES * j, base + LANES * (j + 1))
            r = xc * c + pltpu.roll(xc, ROT_DIM // 2, 1) * s1 + pltpu.roll(xc, LANES - ROT_DIM // 2, 1) * s2
            if scale is not None:
                r = r * scale
            ref[:, LANES * j:LANES * (j + 1)] = r
    v0 = k0 + B_W
    v_ref[...] = seg(v0, v0 + B_W)
    c0 = v0 + B_W
    cqkv_ref[...] = seg(c0, c0 + 3 * C_W)
    cz_ref[...] = seg(c0 + 3 * C_W, c0 + 4 * C_W)
    cba_ref[...] = _dot(hb, wsm_ref[...])
    if dils:
        stage = strided_refs[-1]
        outs = strided_refs[:-1]
        tm = x.shape[0]
        nblk = B_W // LANES
        for a, src in enumerate((q_ref, k_ref, v_ref)):
            for b in range(nblk):
                stage[a * nblk + b] = src[:, LANES * b:LANES * (b + 1)]
        for i, d in enumerate(dils):
            for a in range(3):
                for j in range(d):
                    for b in range(nblk):
                        outs[3 * i + a][j, :, LANES * b:LANES * (b + 1)] = stage[a * nblk + b,
                                                                                 pl.ds(j, tm // d, stride=d), :]


def _proj(x, g, w_main, w_small, tabs, tm, seq_len=None, dils=()):
    n = x.shape[0]
    nt = n // tm
    ntab = tabs[0].shape[0] // tm
    row = lambda i: (i, 0)
    fixed = lambda i: (0, 0)
    tab = lambda i: (i % ntab, 0)
    widths = (A_W, A_W, B_W, B_W, B_W, 3 * C_W, C_W, LANES)
    out_specs = [pl.BlockSpec((tm, w), row) for w in widths]
    out_shape = [jax.ShapeDtypeStruct((n, w), F32) for w in widths]
    for d in dils:
        tps = seq_len // tm
        out_specs += [pl.BlockSpec((None, d, tm // d, B_W), lambda i, tps=tps: (i // tps, 0, i % tps, 0))] * 3
        out_shape += [jax.ShapeDtypeStruct((n // seq_len, d, seq_len // d, B_W), F32)] * 3
    return pl.pallas_call(
        functools.partial(_proj_kernel, dils=dils),
        grid=(nt,),
        in_specs=[pl.BlockSpec((tm, D_MODEL), row), pl.BlockSpec((1, D_MODEL), fixed),
                  pl.BlockSpec((D_MODEL, PROJ_MAIN), fixed), pl.BlockSpec((D_MODEL, LANES), fixed),
                  pl.BlockSpec((tm, LANES), tab), pl.BlockSpec((tm, LANES), tab), pl.BlockSpec((tm, LANES), tab)],
        out_specs=out_specs,
        out_shape=out_shape,
        scratch_shapes=[pltpu.VMEM((3 * B_W // LANES, tm, LANES), F32)] if dils else [],
        compiler_params=_cparams(("parallel",)),
        name="proj",
    )(x, g, w_main, w_small, *tabs)


def _rope_tables(pos):
    half = ROT_DIM // 2
    inv_freq = jnp.power(ROPE_THETA, -jnp.arange(0, ROT_DIM, 2, dtype=F32) / ROT_DIM)
    ang = pos.astype(F32)[:, None] * inv_freq[None, :]
    cos, sin = jnp.cos(ang), jnp.sin(ang)
    p = pos.shape[0]
    z8 = jnp.zeros((p, half), F32)
    rest0 = jnp.zeros((p, HEAD_DIM - ROT_DIM), F32)
    c64 = jnp.concatenate([cos, cos, jnp.ones((p, HEAD_DIM - ROT_DIM), F32)], axis=-1)
    s1 = jnp.concatenate([z8, sin, rest0], axis=-1)
    s2 = jnp.concatenate([-sin, z8, rest0], axis=-1)
    two = lambda a: jnp.concatenate([a, a], axis=-1)
    return two(c64), two(s1), two(s2)


def _chunk_mlp_kernel(u_ref, v_ref, g_ref, w_ref, b_ref, o_ref, vn_ref):
    v = v_ref[...]
    xc = v - jnp.mean(v, axis=-1, keepdims=True)
    vn = xc * lax.rsqrt(jnp.mean(xc * xc, axis=-1, keepdims=True) + EPS) * g_ref[...]
    vn_ref[...] = vn
    vb = vn.astype(BF16)
    rows = lax.broadcasted_iota(jnp.int32, (CHUNK, CHUNK), 0)
    cols = lax.broadcasted_iota(jnp.int32, (CHUNK, CHUNK), 1)
    tril = rows >= cols
    ws = [jnp.where(tril, w_ref[h], 0.0).astype(BF16) for h in range(A_HEADS)]
    for c in range(v.shape[0] // CHUNK):
        rs = slice(c * CHUNK, (c + 1) * CHUNK)
        parts = [_dot(ws[h], vb[rs, h * HEAD_DIM:(h + 1) * HEAD_DIM]) for h in range(A_HEADS)]
        o_ref[rs, :] = u_ref[rs, :] * (jnp.concatenate(parts, axis=-1) + b_ref[...])


def _chunk_mlp(u, v, gain, ws, bias_tile):
    n = u.shape[0]
    row = lambda i: (i, 0)
    tm = _tile_rows(n, 4 * CHUNK)
    return pl.pallas_call(
        _chunk_mlp_kernel,
        grid=(n // tm,),
        in_specs=[pl.BlockSpec((tm, A_W), row), pl.BlockSpec((tm, A_W), row),
                  pl.BlockSpec((1, A_W), lambda i: (0, 0)),
                  pl.BlockSpec((A_HEADS, CHUNK, CHUNK), lambda i: (0, 0, 0)),
                  pl.BlockSpec((CHUNK, A_W), lambda i: (0, 0))],
        out_specs=[pl.BlockSpec((tm, A_W), row), pl.BlockSpec((tm, A_W), row)],
        out_shape=[jax.ShapeDtypeStruct((n, A_W), F32)] * 2,
        compiler_params=_cparams(("parallel",)),
        name="chunk_mlp",
    )(u, v, gain, ws, bias_tile)


def _win_attn_kernel(q_ref, kp_ref, kc_ref, vp_ref, vc_ref, o_ref, lse_ref, *, seg_blocks):
    s = pl.program_id(0)
    lo = jnp.where(s % seg_blocks == 0, WBLK, 0)
    rows = lax.broadcasted_iota(jnp.int32, (WBLK, 2 * WBLK), 0)
    cols = lax.broadcasted_iota(jnp.int32, (WBLK, 2 * WBLK), 1)
    dist = rows + WBLK - cols
    valid = (dist >= 0) & (dist <= WBLK) & (cols >= lo)
    bias = jnp.where(valid, 0.0, NEG)
    lane = lax.broadcasted_iota(jnp.int32, (WBLK, LANES), 1)
    k2 = jnp.concatenate([kp_ref[...], kc_ref[...]], axis=0).astype(BF16)
    v2 = jnp.concatenate([vp_ref[...], vc_ref[...]], axis=0).astype(BF16)
    q = q_ref[...]
    lse_tile = jnp.zeros((WBLK, LANES), F32)
    for hp in range(B_W // LANES):
        sl = slice(hp * LANES, (hp + 1) * LANES)
        qp, kp, vp = q[:, sl], k2[:, sl], v2[:, sl]
        acc = None
        for half in range(2):
            hm = (lane < HEAD_DIM) if half == 0 else (lane >= HEAD_DIM)
            qh = jnp.where(hm, qp, 0.0).astype(BF16)
            sc = _dot_nt(qh, kp) + bias
            m = jnp.max(sc, axis=-1, keepdims=True)
            e = jnp.exp(sc - m)
            den = jnp.sum(e, axis=-1, keepdims=True)
            pm = (e * (1.0 / den)).astype(BF16)
            o = _dot(pm, vp)
            acc = o if acc is None else jnp.where(hm, o, acc)
            lse_tile = jnp.where(lane == 2 * hp + half, m + jnp.log(den), lse_tile)
        o_ref[:, sl] = acc
    lse_ref[...] = lse_tile


def _win_attn(qd, kd, vd, seg_blocks):
    n = qd.shape[0]
    cur = lambda s: (s, 0)
    prev = lambda s: (jnp.maximum(s - 1, 0), 0)
    blk = lambda w, im: pl.BlockSpec((WBLK, w), im)
    return pl.pallas_call(
        functools.partial(_win_attn_kernel, seg_blocks=seg_blocks),
        grid=(n // WBLK,),
        in_specs=[blk(B_W, cur), blk(B_W, prev), blk(B_W, cur), blk(B_W, prev), blk(B_W, cur)],
        out_specs=[blk(B_W, cur), blk(LANES, cur)],
        out_shape=[jax.ShapeDtypeStruct((n, B_W), F32), jax.ShapeDtypeStruct((n, LANES), F32)],
        compiler_params=_cparams(("parallel",)),
        name="win_attn",
    )(qd, kd, kd, vd, vd)


def _window_rows_kernel(k_ref, v_ref, kprev_ref, vprev_ref, ko_ref, vo_ref):
    del kprev_ref, vprev_ref
    for b in range(k_ref.shape[0]):
        ko_ref[b] = k_ref[b].T
        vo_ref[b] = v_ref[b].T


def _window_rows(k, v, k_all, v_all, layer, batch, t, keep):
    first = (t - keep) // WBLK
    src = pl.BlockSpec((batch, WBLK, B_W), lambda i: (0, first + i, 0))
    dst = pl.BlockSpec((None, batch, B_W, WBLK), lambda i: (layer, 0, 0, i))
    hbm = pl.BlockSpec(memory_space=pl.ANY)
    shape = jax.ShapeDtypeStruct(k_all.shape, F32)
    return pl.pallas_call(
        _window_rows_kernel,
        grid=(keep // WBLK,),
        in_specs=[src, src, hbm, hbm],
        out_specs=[dst, dst],
        out_shape=[shape, shape],
        input_output_aliases={2: 0, 3: 1},
        compiler_params=_cparams(("parallel",)),
        name="window_rows",
    )(k.reshape(batch, t, B_W), v.reshape(batch, t, B_W), k_all, v_all)


def _dec_attn_kernel(q_ref, kc_ref, vc_ref, kn_ref, vn_ref, o_ref, *, t_new, cache_len):
    rows_c = lax.broadcasted_iota(jnp.int32, (SUBLANES, cache_len), 0)
    cols_c = lax.broadcasted_iota(jnp.int32, (SUBLANES, cache_len), 1)
    dist_c = cache_len + rows_c % t_new - cols_c
    rows_n = lax.broadcasted_iota(jnp.int32, (SUBLANES, SUBLANES), 0)
    cols_n = lax.broadcasted_iota(jnp.int32, (SUBLANES, SUBLANES), 1)
    dist_n = rows_n % t_new - cols_n
    biases = []
    for window, dil in PATTERNS:
        vc_ok = (dist_c <= window) & ((dist_c & (dil - 1)) == 0)
        vn_ok = (dist_n >= 0) & ((dist_n & (dil - 1)) == 0)
        biases.append((jnp.where(vc_ok, 0.0, NEG), jnp.where(vn_ok, 0.0, NEG)))
    row8 = lax.broadcasted_iota(jnp.int32, (SUBLANES, LANES), 0)
    lane8 = lax.broadcasted_iota(jnp.int32, (SUBLANES, LANES), 1)
    own = (lane8 < HEAD_DIM) == (row8 < t_new)
    q = q_ref[...]
    outs = []
    for hp in range(B_W // LANES):
        sl = slice(hp * LANES, (hp + 1) * LANES)
        qq = jnp.concatenate([q[0:t_new, sl], q[0:t_new, sl]], axis=0)
        q8 = jnp.where(own, qq, 0.0).astype(BF16)
        kc, vc = kc_ref[sl, :].astype(BF16), vc_ref[sl, :].astype(BF16)
        kn, vn = kn_ref[:, sl].astype(BF16), vn_ref[:, sl].astype(BF16)
        sc_c = _dot(q8, kc)
        sc_n = _dot_nt(q8, kn)
        os_, lses = [], []
        for bc, bn in biases:
            a_c, a_n = sc_c + bc, sc_n + bn
            m = jnp.maximum(jnp.max(a_c, axis=-1, keepdims=True), jnp.max(a_n, axis=-1, keepdims=True))
            e_c, e_n = jnp.exp(a_c - m), jnp.exp(a_n - m)
            den = jnp.sum(e_c, axis=-1, keepdims=True) + jnp.sum(e_n, axis=-1, keepdims=True)
            inv = 1.0 / den
            os_.append(_dot_nt((e_c * inv).astype(BF16), vc) + _dot((e_n * inv).astype(BF16), vn))
            lses.append(m + jnp.log(den))
        mx = jnp.maximum(jnp.maximum(lses[0], lses[1]), lses[2])
        ws = [jnp.exp(l - mx) for l in lses]
        tot = ws[0] + ws[1] + ws[2]
        o8 = (ws[0] / tot) * os_[0] + (ws[1] / tot) * os_[1] + (ws[2] / tot) * os_[2]
        lane4 = lane8[0:t_new]
        outs.append(jnp.where(lane4 < HEAD_DIM, o8[0:t_new], o8[t_new:2 * t_new]))
    o_ref[...] = jnp.zeros((SUBLANES, B_W), F32)
    o_ref[0:t_new, :] = jnp.concatenate(outs, axis=-1)


def _dec_attn(q8, kc, vc, kn8, vn8, t_new, layer):
    _, b, _, cache_len = kc.shape
    assert 2 * t_new == SUBLANES
    small = pl.BlockSpec((None, SUBLANES, B_W), lambda i: (i, 0, 0))
    big = pl.BlockSpec((None, None, B_W, cache_len), lambda i: (layer, i, 0, 0))
    return pl.pallas_call(
        functools.partial(_dec_attn_kernel, t_new=t_new, cache_len=cache_len),
        grid=(b,),
        in_specs=[small, big, big, small, small],
        out_specs=small,
        out_shape=jax.ShapeDtypeStruct((b, SUBLANES, B_W), F32),
        compiler_params=_cparams(("parallel",)),
        name="dec_attn",
    )(q8, kc, vc, kn8, vn8)


def _split3(x):
    hi = x.astype(BF16)
    r1 = x - hi.astype(F32)
    mid = r1.astype(BF16)
    return hi, mid, (r1 - mid.astype(F32)).astype(BF16)


def _pair_dup(xx):
    lane = lax.broadcasted_iota(jnp.int32, xx.shape, 1)
    hi = xx.astype(BF16).astype(F32)
    return jnp.where(lane < HEAD_DIM, xx, xx - hi).astype(BF16)


def _pair(x):
    return _pair_dup(jnp.concatenate([x, x], axis=1))


def _lhs4(pair):
    return jnp.concatenate([pair, pair], axis=1)


def _rhs4(y):
    hi = y.astype(BF16)
    lo = (y - hi.astype(F32)).astype(BF16)
    return jnp.concatenate([hi, hi, lo, lo], axis=0)


def _unit_lower_solves(mats, rhss):
    n = DELTA_CHUNK
    w2 = 2 * HEAD_DIM
    rows = lax.broadcasted_iota(jnp.int32, (n, w2), 0)
    cols = lax.broadcasted_iota(jnp.int32, (n, w2), 1) % HEAD_DIM
    in16 = rows // 16 == cols // 16
    eye = jnp.where(rows == cols, 1.0, 0.0)
    ds = [jnp.where(in16, a, 0.0) for a in mats]
    es = [jnp.where(in16, 0.0, a) for a in mats]
    left = lambda xx: _lhs4(_pair_dup(xx))
    pw = [_dot(left(d), _rhs4(d)) for d in ds]
    ts = [eye - d for d in ds]
    for _ in range(2):
        outs = [_dot(jnp.concatenate([left(t), left(p)], axis=0), _rhs4(p)) for t, p in zip(ts, pw)]
        ts = [t + o[0:n] for t, o in zip(ts, outs)]
        pw = [o[n:2 * n] for o in outs]
    ts = [t + _dot(left(t), _rhs4(p)) for t, p in zip(ts, pw)]
    o5 = [_dot(left(t), _rhs4(jnp.concatenate([r, e], axis=1))) for t, r, e in zip(ts, rhss, es)]
    x0 = [o[:, 0:w2] for o in o5]
    nm = [o[:, w2:2 * w2] for o in o5]
    o6 = [_dot(left(m), _rhs4(jnp.concatenate([x, m], axis=1))) for m, x in zip(nm, x0)]
    ys = [x - o[:, 0:w2] for x, o in zip(x0, o6)]
    return [y + _dot(left(o[:, w2:2 * w2]), _rhs4(y)) for y, o in zip(ys, o6)]


def _gdn_prep_kernel(x_ref, cba_ref, buf_ref, cw_ref, alog_ref, dtb_ref, gmat_ref,
                     u_ref, wq_ref, ak_ref, gt_ref,
                     xp_scr, q_scr, k_scr, v_scr, gb_scr, *, tm, valid_len, group):
    ti = pl.program_id(1)

    @pl.when(ti == 0)
    def _():
        xp_scr[0:SUBLANES, :] = buf_ref[...]

    x = x_ref[...]
    xp_scr[SUBLANES:SUBLANES + tm, :] = x
    off = SUBLANES - (CONV_W - 1)
    acc = xp_scr[off:off + tm, :] * cw_ref[0:1, :]
    for j in range(1, CONV_W):
        acc = acc + xp_scr[off + j:off + j + tm, :] * cw_ref[j:j + 1, :]
    xp_scr[0:SUBLANES, :] = x[tm - SUBLANES:tm, :]
    y = _silu(acc)
    q, k = y[:, 0:C_W], y[:, C_W:2 * C_W]
    gmat = gmat_ref[...]
    head_sum = lambda a: sum(_dot(p, gmat) for p in _split3(a))
    q_scr[...] = q * lax.rsqrt(head_sum(q * q) + EPS) * (HEAD_DIM ** -0.5)
    k_scr[...] = k * lax.rsqrt(head_sum(k * k) + EPS)
    v_scr[...] = y[:, 2 * C_W:3 * C_W]

    cba = cba_ref[...]
    lane = lax.broadcasted_iota(jnp.int32, (tm, LANES), 1)
    tpos = ti * tm + lax.broadcasted_iota(jnp.int32, (tm, LANES), 0)
    live = tpos < valid_len
    beta = _sigmoid(cba)
    z = cba + dtb_ref[...]
    softplus = jnp.maximum(z, 0.0) + jnp.log(1.0 + jnp.exp(-jnp.abs(z)))
    g = -jnp.exp(alog_ref[...]) * softplus
    is_g = (lane >= C_HEADS) & (lane < 2 * C_HEADS)
    gb_scr[...] = jnp.where(live, jnp.where(is_g, g, jnp.where(lane < C_HEADS, beta, 0.0)), 0.0)

    n = DELTA_CHUNK
    rows = lax.broadcasted_iota(jnp.int32, (n, 2 * n), 0)
    cols = lax.broadcasted_iota(jnp.int32, (n, 2 * n), 1) % n
    incl = rows >= cols
    strict = rows > cols
    ltri = jnp.where(incl[:, 0:n], 1.0, 0.0).astype(BF16)
    heads = range(C_HEADS)
    hsl = [slice(h * HEAD_DIM, (h + 1) * HEAD_DIM) for h in heads]
    gcol = [slice(C_HEADS + h, C_HEADS + h + 1) for h in heads]

    lane_c = lax.broadcasted_iota(jnp.int32, (n, LANES), 1)
    lane2 = lax.broadcasted_iota(jnp.int32, (n, 2 * HEAD_DIM), 1)

    def setup(c):
        rs = pl.ds(pl.multiple_of(c * n, n), n)
        gb = gb_scr[rs, :]
        g_only = jnp.where((lane_c >= C_HEADS) & (lane_c < 2 * C_HEADS), gb, 0.0)
        cg = sum(_dot(ltri, p) for p in _split3(g_only))
        cgt = cg.T
        cgl = cg[n - 1:n, :]
        qn, kn, vv = q_scr[rs, :], k_scr[rs, :], v_scr[rs, :]
        cgc = [cg[:, gcol[h]] for h in heads]
        cgr = [jnp.concatenate([cgt[gcol[h], :], cgt[gcol[h], :]], axis=1) for h in heads]
        decay = [jnp.where(incl, jnp.exp(jnp.minimum(cgc[h] - cgr[h], 0.0)), 0.0) for h in heads]
        bcol = [gb[:, h:h + 1] for h in heads]
        kb = [kn[:, s].astype(BF16) for s in hsl]
        kkqk = [_dot_nt(jnp.concatenate([kb[h], qn[:, hsl[h]].astype(BF16)], axis=0),
                        jnp.concatenate([kb[h], kb[h]], axis=0)) for h in heads]
        amat = [jnp.where(strict, bcol[h] * kkqk[h][0:n] * decay[h], 0.0) for h in heads]
        at_pair = [_pair_dup(kkqk[h][n:2 * n] * decay[h]) for h in heads]
        ecg = [jnp.exp(x) for x in cgc]
        rhs = [jnp.concatenate([bcol[h] * vv[:, hsl[h]], (bcol[h] * ecg[h]) * kn[:, hsl[h]]], axis=1)
               for h in heads]
        qd_pair = [_pair(qn[:, hsl[h]] * ecg[h]) for h in heads]
        kdt_pair = [_pair((kn[:, hsl[h]] * jnp.exp(cgl[:, gcol[h]] - cgc[h])).T) for h in heads]
        gt_ref[pl.ds(pl.multiple_of(c * SUBLANES, SUBLANES), SUBLANES), :] = jnp.broadcast_to(
            jnp.exp(cgl), (SUBLANES, LANES))
        return amat, rhs, at_pair, qd_pair, kdt_pair

    def finish(c, sol, at_pair, qd_pair, kdt_pair):
        u_ref[pl.ds(pl.multiple_of(c * n, n), n), :] = jnp.concatenate([s[:, 0:HEAD_DIM] for s in sol], axis=1)
        w_pair = [_pair_dup(jnp.where(lane2 < HEAD_DIM, pltpu.roll(s, HEAD_DIM, 1), s)) for s in sol]
        r2 = pl.ds(pl.multiple_of(c * 2 * n, 2 * n), 2 * n)
        wq_ref[r2, :] = jnp.concatenate(
            [jnp.concatenate([w_pair[h], qd_pair[h]], axis=0) for h in heads], axis=1)
        ak_ref[r2, :] = jnp.concatenate(
            [jnp.concatenate([at_pair[h], kdt_pair[h]], axis=0) for h in heads], axis=1)

    def chunks(i, carry):
        parts = [setup(i * group + j) for j in range(group)]
        sol = _unit_lower_solves([a for p in parts for a in p[0]], [r for p in parts for r in p[1]])
        for j, p in enumerate(parts):
            finish(i * group + j, sol[j * C_HEADS:(j + 1) * C_HEADS], *p[2:])
        return carry

    lax.fori_loop(0, tm // n // group, chunks, 0)


def _gdn_prep(cqkv, cba, buf8, conv_w, alog_row, dtb_row, gmat, batch, t, tm, valid_len):
    n = batch * t
    nt = t // tm
    row = lambda b, i: (b * nt + i, 0)
    fixed = lambda b, i: (0, 0)
    nch = tm // DELTA_CHUNK
    pair_w = C_HEADS * 2 * HEAD_DIM
    outs = [jax.ShapeDtypeStruct((n, C_W), F32), jax.ShapeDtypeStruct((2 * n, pair_w), BF16),
            jax.ShapeDtypeStruct((2 * n, pair_w), BF16),
            jax.ShapeDtypeStruct((n // DELTA_CHUNK * SUBLANES, LANES), F32)]
    return pl.pallas_call(
        functools.partial(_gdn_prep_kernel, tm=tm, valid_len=valid_len, group=min(GDN_GROUP, nch)),
        grid=(batch, nt),
        in_specs=[pl.BlockSpec((tm, 3 * C_W), row), pl.BlockSpec((tm, LANES), row),
                  pl.BlockSpec((None, SUBLANES, 3 * C_W), lambda b, i: (b, 0, 0)),
                  pl.BlockSpec((CONV_W, 3 * C_W), fixed), pl.BlockSpec((1, LANES), fixed),
                  pl.BlockSpec((1, LANES), fixed), pl.BlockSpec((C_W, C_W), fixed)],
        out_specs=[pl.BlockSpec((tm, C_W), row), pl.BlockSpec((2 * tm, pair_w), row),
                   pl.BlockSpec((2 * tm, pair_w), row), pl.BlockSpec((nch * SUBLANES, LANES), row)],
        out_shape=outs,
        scratch_shapes=[pltpu.VMEM((tm + SUBLANES, 3 * C_W), F32), pltpu.VMEM((tm, C_W), F32),
                        pltpu.VMEM((tm, C_W), F32), pltpu.VMEM((tm, C_W), F32), pltpu.VMEM((tm, LANES), F32)],
        compiler_params=_cparams(("parallel", "arbitrary")),
        name="gdn_prep",
    )(cqkv, cba, buf8, conv_w, alog_row, dtb_row, gmat)


def _gdn_scan_kernel(u_ref, wq_ref, ak_ref, gt_ref, z_ref, s0_ref, gn_ref,
                     o_ref, sfin_ref, s_scr, *, bg):
    c = pl.program_id(1)

    @pl.when(c == 0)
    def _():
        s_scr[...] = s0_ref[...]

    gn = gn_ref[...]
    n = DELTA_CHUNK
    chains = [(b, h, slice(h * HEAD_DIM, (h + 1) * HEAD_DIM), slice(h * 2 * HEAD_DIM, (h + 1) * 2 * HEAD_DIM))
              for b in range(bg) for h in range(C_HEADS)]
    st = [s_scr[b, h] for b, h, _, _ in chains]
    r1 = [_dot(_lhs4(wq_ref[b, :, ps]), _rhs4(s)) for (b, _, _, ps), s in zip(chains, st)]
    up = [u_ref[b, :, sl] - r[0:n] for (b, _, sl, _), r in zip(chains, r1)]
    r2 = [_dot(_lhs4(ak_ref[b, :, ps]), _rhs4(x)) for (b, _, _, ps), x in zip(chains, up)]
    for (b, h, _, _), s, r in zip(chains, st, r2):
        s_scr[b, h] = gt_ref[b, 0:1, C_HEADS + h:C_HEADS + h + 1] * s + r[n:2 * n]
    os_ = [a[n:2 * n] + r[0:n] for a, r in zip(r1, r2)]
    outs = [o * lax.rsqrt(jnp.mean(o * o, axis=-1, keepdims=True) + EPS) * gn * _silu(z_ref[b, :, sl])
            for (b, _, sl, _), o in zip(chains, os_)]
    for b in range(bg):
        o_ref[b] = jnp.concatenate(outs[b * C_HEADS:(b + 1) * C_HEADS], axis=1)

    @pl.when(c == pl.num_programs(1) - 1)
    def _():
        sfin_ref[...] = s_scr[...]


def _gdn_scan(u, wq, ak, gt, z, s0, gnorm, batch, t, bg):
    n = DELTA_CHUNK
    nc = t // n
    pair_w = C_HEADS * 2 * HEAD_DIM
    v3 = lambda a: a.reshape(batch, t, C_W)
    p3 = lambda a: a.reshape(batch, 2 * t, pair_w)
    tok = pl.BlockSpec((bg, n, C_W), lambda b, c: (b, c, 0))
    pair = pl.BlockSpec((bg, 2 * n, pair_w), lambda b, c: (b, c, 0))
    st = pl.BlockSpec((bg, C_HEADS, HEAD_DIM, HEAD_DIM), lambda b, c: (b, 0, 0, 0))
    return pl.pallas_call(
        functools.partial(_gdn_scan_kernel, bg=bg),
        grid=(batch // bg, nc),
        in_specs=[tok, pair, pair,
                  pl.BlockSpec((bg, SUBLANES, LANES), lambda b, c: (b, c, 0)),
                  tok, st, pl.BlockSpec((1, HEAD_DIM), lambda b, c: (0, 0))],
        out_specs=[tok, st],
        out_shape=[jax.ShapeDtypeStruct((batch, t, C_W), F32),
                   jax.ShapeDtypeStruct((batch, C_HEADS, HEAD_DIM, HEAD_DIM), F32)],
        scratch_shapes=[pltpu.VMEM((bg, C_HEADS, HEAD_DIM, HEAD_DIM), F32)],
        compiler_params=_cparams(("parallel", "arbitrary")),
        name="gdn_scan",
    )(v3(u), p3(wq), p3(ak), gt.reshape(batch, nc * SUBLANES, LANES), v3(z), s0, gnorm)


def _head_expand(wt):
    tm = wt.shape[0]
    return jnp.concatenate([jnp.broadcast_to(wt[:, h:h + 1], (tm, HEAD_DIM)) for h in range(B_HEADS)], axis=-1)


def _out_proj_kernel(*refs, dils):
    if dils:
        npat = len(dils)
        a_ref = refs[0]
        o_refs, l_refs = refs[1:1 + npat], refs[1 + npat:1 + 2 * npat]
        c_ref, x_ref, w_ref, y_ref, o_scr, l_scr = refs[1 + 2 * npat:]
        tm = x_ref.shape[0]
        os_, ls = [], []
        for i, d in enumerate(dils):
            if d == 1:
                os_.append(o_refs[i][0])
                ls.append(l_refs[i][0])
            else:
                nblk = B_W // LANES
                for j in range(d):
                    rows = pl.ds(j, tm // d, stride=d)
                    for b in range(nblk):
                        o_scr[i * nblk + b, rows, :] = o_refs[i][j, :, LANES * b:LANES * (b + 1)]
                    l_scr[i, rows, :] = l_refs[i][j]
                os_.append(jnp.concatenate([o_scr[i * nblk + b] for b in range(nblk)], axis=-1))
                ls.append(l_scr[i])
        mx = functools.reduce(jnp.maximum, ls)
        es = [jnp.exp(l - mx) for l in ls]
        tot = functools.reduce(lambda a, b: a + b, es)
        ob = functools.reduce(lambda a, b: a + b, [_head_expand(e / tot) * o for e, o in zip(es, os_)])
    else:
        a_ref, b_ref, c_ref, x_ref, w_ref, y_ref = refs
        ob = b_ref[...]
    cat = jnp.concatenate([a_ref[...], ob, c_ref[...]], axis=-1).astype(BF16)
    y_ref[...] = x_ref[...] + _dot(cat, w_ref[...])


def _out_proj(out_a, out_b, lses, out_c, x, w_out, tm, seq_len=None, dils=()):
    n = x.shape[0]
    row = lambda i: (i, 0)
    spec = lambda w: pl.BlockSpec((tm, w), row)
    scratch = []
    if dils:
        tps = seq_len // tm
        strided = lambda d, w: pl.BlockSpec((None, d, tm // d, w), lambda i: (i // tps, 0, i % tps, 0))
        ins = [out_a, *out_b, *lses, out_c, x, w_out]
        specs = ([spec(A_W)] + [strided(d, B_W) for d in dils] + [strided(d, LANES) for d in dils]
                 + [spec(C_W), spec(D_MODEL)])
        scratch = [pltpu.VMEM((len(dils) * B_W // LANES, tm, LANES), F32), pltpu.VMEM((len(dils), tm, LANES), F32)]
    else:
        ins = [out_a, out_b, out_c, x, w_out]
        specs = [spec(A_W), spec(B_W), spec(C_W), spec(D_MODEL)]
    specs.append(pl.BlockSpec((D_MODEL, D_MODEL), lambda i: (0, 0)))
    return pl.pallas_call(
        functools.partial(_out_proj_kernel, dils=dils),
        grid=(n // tm,),
        in_specs=specs,
        out_specs=spec(D_MODEL),
        out_shape=jax.ShapeDtypeStruct((n, D_MODEL), F32),
        scratch_shapes=scratch,
        compiler_params=_cparams(("parallel",)),
        name="out_proj",
    )(*ins)


def _router_kernel(x_ref, g_ref, w_ref, dense_ref):
    x = x_ref[...]
    t = x * lax.rsqrt(jnp.mean(x * x, axis=-1, keepdims=True) + EPS) * g_ref[...]
    lg = _dot(t, w_ref[...], HI)
    tm = lg.shape[0]
    lane = lax.broadcasted_iota(jnp.int32, (tm, LANES), 1).astype(F32)
    big = float(LANES)
    is_grp = lane < N_GROUPS
    gl = jnp.where(is_grp, lg, NEG)
    gmax = jnp.max(gl, axis=-1, keepdims=True)
    gsum = jnp.sum(jnp.where(is_grp, jnp.exp(gl - gmax), 0.0), axis=-1, keepdims=True)
    g_w = 1.0 / gsum
    g_idx = jnp.min(jnp.where(is_grp & (gl == gmax), lane, big), axis=-1, keepdims=True)
    lo = ROUTE_OFF + EXP_PER_GROUP * g_idx
    sel = (lane >= lo) & (lane < lo + EXP_PER_GROUP)
    el = jnp.where(sel, lg, NEG)
    m1 = jnp.max(el, axis=-1, keepdims=True)
    esum = jnp.sum(jnp.where(sel, jnp.exp(el - m1), 0.0), axis=-1, keepdims=True)
    i1 = jnp.min(jnp.where(sel & (el == m1), lane, big), axis=-1, keepdims=True)
    el2 = jnp.where(lane == i1, NEG, el)
    m2 = jnp.max(el2, axis=-1, keepdims=True)
    i2 = jnp.min(jnp.where(sel & (lane != i1) & (el2 == m2), lane, big), axis=-1, keepdims=True)
    p1 = 1.0 / esum
    p2 = jnp.exp(m2 - m1) / esum
    tot = p1 + p2
    dense_ref[...] = jnp.where(lane == i1, g_w * (p1 / tot), 0.0) + jnp.where(lane == i2, g_w * (p2 / tot), 0.0)


def _router(x, g, w_route, tm):
    n = x.shape[0]
    row = lambda i: (i, 0)
    return pl.pallas_call(
        _router_kernel,
        grid=(n // tm,),
        in_specs=[pl.BlockSpec((tm, D_MODEL), row), pl.BlockSpec((1, D_MODEL), lambda i: (0, 0)),
                  pl.BlockSpec((D_MODEL, LANES), lambda i: (0, 0))],
        out_specs=pl.BlockSpec((tm, LANES), row),
        out_shape=jax.ShapeDtypeStruct((n, LANES), F32),
        compiler_params=_cparams(("parallel",)),
        name="router",
    )(x, g, w_route)


def _moe_kernel(x_ref, g_ref, dense_ref, w1_ref, w3_ref, w2_ref, y_ref, t_scr):
    e = pl.program_id(1)

    @pl.when(e == 0)
    def _():
        x = x_ref[...]
        t_scr[...] = (x * lax.rsqrt(jnp.mean(x * x, axis=-1, keepdims=True) + EPS) * g_ref[...]).astype(BF16)
        y_ref[...] = x

    tb = t_scr[...]
    dense = dense_ref[...]
    lane = lax.broadcasted_iota(jnp.int32, dense.shape, 1)
    gate = jnp.sum(jnp.where(lane == e + ROUTE_OFF, dense, 0.0), axis=-1, keepdims=True)
    hid = _silu(_dot(tb, w1_ref[...].astype(BF16))) * _dot(tb, w3_ref[...].astype(BF16))
    y_ref[...] += _dot((hid * gate).astype(BF16), w2_ref[...].astype(BF16))


def _moe(x, g, dense, w1, w3, w2, layer, tm):
    n = x.shape[0]
    row = lambda i, e: (i, 0)
    expert = lambda i, e: (layer, e, 0, 0)
    return pl.pallas_call(
        _moe_kernel,
        grid=(n // tm, N_EXPERTS),
        in_specs=[pl.BlockSpec((tm, D_MODEL), row), pl.BlockSpec((1, D_MODEL), lambda i, e: (0, 0)),
                  pl.BlockSpec((tm, LANES), row),
                  pl.BlockSpec((None, None, D_MODEL, D_EXPERT), expert),
                  pl.BlockSpec((None, None, D_MODEL, D_EXPERT), expert),
                  pl.BlockSpec((None, None, D_EXPERT, D_MODEL), expert)],
        out_specs=pl.BlockSpec((tm, D_MODEL), row),
        out_shape=jax.ShapeDtypeStruct((n, D_MODEL), F32),
        scratch_shapes=[pltpu.VMEM((tm, D_MODEL), BF16)],
        compiler_params=_cparams(("parallel", "arbitrary")),
        name="moe",
    )(x, g, dense, w1, w3, w2)


def _final_norm_kernel(x_ref, g_ref, y_ref):
    x = x_ref[...]
    y_ref[...] = x * lax.rsqrt(jnp.mean(x * x, axis=-1, keepdims=True) + EPS) * g_ref[...]


def _final_norm(x, g, tm):
    n = x.shape[0]
    return pl.pallas_call(
        _final_norm_kernel,
        grid=(n // tm,),
        in_specs=[pl.BlockSpec((tm, D_MODEL), lambda i: (i, 0)), pl.BlockSpec((1, D_MODEL), lambda i: (0, 0))],
        out_specs=pl.BlockSpec((tm, D_MODEL), lambda i: (i, 0)),
        out_shape=jax.ShapeDtypeStruct((n, D_MODEL), F32),
        compiler_params=_cparams(("parallel",)),
        name="final_norm",
    )(x, g)


def _tile_rows(n, cap):
    tm = min(n, cap)
    assert n % tm == 0
    return tm


def _layer_weights(l, norm1_g, w_in, a_vnorm_g, a_ws, a_bs, c_conv_w, c_a_log, c_dt_bias, c_norm_g,
                   w_out, norm2_g, w_group, w_router, w1, w3, w2):
    wi = w_in[l]
    pad_l = lambda a, left: jnp.pad(a, ((0, 0), (left, LANES - left - a.shape[-1])))
    return dict(
        norm1_g=norm1_g[l][None, :],
        w_main=wi[:, :PROJ_MAIN].astype(BF16),
        w_small=pad_l(wi[:, PROJ_MAIN:], 0).astype(BF16),
        a_gain=a_vnorm_g[l][None, :],
        a_ws=a_ws[l],
        a_bs=a_bs[l],
        conv_w=c_conv_w[l],
        alog_row=pad_l(c_a_log[l][None, :], C_HEADS),
        dtb_row=pad_l(c_dt_bias[l][None, :], C_HEADS),
        gnorm=c_norm_g[l][None, :],
        w_out=w_out[l].astype(BF16),
        norm2_g=norm2_g[l][None, :],
        w_route=pad_l(jnp.concatenate([w_group[l], w_router[l]], axis=-1), 0),
        w1=w1, w3=w3, w2=w2, layer=l,
    )


def _mixer_c(lw, cqkv, cba, cz, conv_buf, s0, gmat, batch, t, tm, valid_len, bg):
    buf8 = jnp.pad(conv_buf, ((0, 0), (SUBLANES - (CONV_W - 1), 0), (0, 0)))
    u, wq, ak, gt = _gdn_prep(cqkv, cba, buf8, lw["conv_w"], lw["alog_row"], lw["dtb_row"], gmat,
                              batch, t, tm, valid_len)
    return _gdn_scan(u, wq, ak, gt, cz, s0, lw["gnorm"], batch, t, bg)


def _ffn(lw, x):
    n = x.shape[0]
    dense = _router(x, lw["norm2_g"], lw["w_route"], _tile_rows(n, 512))
    return _moe(x, lw["norm2_g"], dense, lw["w1"], lw["w3"], lw["w2"], lw["layer"], _tile_rows(n, 1024))


def _prompt_layer(lw, x, batch, t, tabs, gmat, win_k, win_v, layer):
    n = batch * t
    dils = tuple(d for _, d in PATTERNS)
    tm = _tile_rows(n, 512)
    au, av, q, k, v, cqkv, cz, cba, *strided = _proj(x, lw["norm1_g"], lw["w_main"], lw["w_small"], tabs, tm,
                                                     seq_len=t, dils=dils[1:])

    bias_tile = jnp.repeat(lw["a_bs"].T, HEAD_DIM, axis=1)
    out_a, _ = _chunk_mlp(au, av, lw["a_gain"], lw["a_ws"], bias_tile)

    qkv = [(q, k, v)] + [tuple(a.reshape(n, B_W) for a in strided[3 * i:3 * i + 3]) for i in range(len(dils) - 1)]
    outs, lses = [], []
    for d, (qd, kd, vd) in zip(dils, qkv):
        o_d, lse_d = _win_attn(qd, kd, vd, t // d // WBLK)
        outs.append(o_d.reshape(batch, d, t // d, B_W))
        lses.append(lse_d.reshape(batch, d, t // d, LANES))

    zeros_buf = jnp.zeros((batch, CONV_W - 1, 3 * C_W), F32)
    zeros_s = jnp.zeros((batch, C_HEADS, HEAD_DIM, HEAD_DIM), F32)
    out_c, s_new = _mixer_c(lw, cqkv, cba, cz, zeros_buf, zeros_s, gmat, batch, t, 256, t, batch)
    out_c = out_c.reshape(n, C_W)

    x = _out_proj(out_a, outs, lses, out_c, x, lw["w_out"], tm, seq_len=t, dils=dils)
    x = _ffn(lw, x)

    win_k, win_v = _window_rows(k, v, win_k, win_v, layer, batch, t, win_k.shape[-1])
    conv_state = cqkv.reshape(batch, t, 3 * C_W)[:, t - (CONV_W - 1):]
    return x, win_k, win_v, (conv_state, s_new)


def _sample_layer(lw, x, batch, t, tabs, gmat, kbuf, vbuf, conv_buf, s0, layer):
    n = batch * t
    au, av, q, k, v, cqkv, cz, cba = _proj(x, lw["norm1_g"], lw["w_main"], lw["w_small"], tabs, n)

    eye = jnp.eye(batch, dtype=F32)
    ws_bd = jnp.stack([jnp.kron(eye, lw["a_ws"][h, :t, :t]) for h in range(A_HEADS)])
    bias_tile = jnp.tile(jnp.repeat(lw["a_bs"][:, :t].T, HEAD_DIM, axis=1), (batch, 1))
    out_a, a_rows = _chunk_mlp(au, av, lw["a_gain"], ws_bd, bias_tile)

    pad8 = lambda a: jnp.pad(a.reshape(batch, t, B_W), ((0, 0), (0, SUBLANES - t), (0, 0)))
    out_b = _dec_attn(pad8(q), kbuf, vbuf, pad8(k), pad8(v), t, layer)[:, :t].reshape(n, B_W)

    tp = DELTA_CHUNK
    padt = lambda a: jnp.pad(a.reshape(batch, t, -1), ((0, 0), (0, tp - t), (0, 0))).reshape(batch * tp, -1)
    out_c, s_new = _mixer_c(lw, padt(cqkv), padt(cba), padt(cz), conv_buf, s0, gmat, batch, tp, tp, t, 4)
    out_c = out_c[:, :t].reshape(n, C_W)

    x = _out_proj(out_a, out_b, None, out_c, x, lw["w_out"], n)
    x = _ffn(lw, x)

    heads = lambda a: a.reshape(batch, t, B_HEADS, HEAD_DIM)
    conv_state = jnp.concatenate([conv_buf, cqkv.reshape(batch, t, 3 * C_W)], axis=1)[:, -(CONV_W - 1):]
    return x, (heads(k), heads(v), a_rows.reshape(batch, t, A_W), conv_state, s_new)


def kernel(x_prompt, x_sample, cache_win_k, cache_win_v, state_conv, state_delta, norm1_g, w_in, a_vnorm_g, a_ws, a_bs, c_conv_w, c_a_log, c_dt_bias, c_norm_g, w_out, norm2_g, w_group, w_router, w1, w3, w2, final_g):
    bp, tp, _ = x_prompt.shape
    bs, ts, _ = x_sample.shape
    depth = w_in.shape[0]
    assert tp % (PATTERNS[-1][1] * WBLK) == 0 and tp % 512 == 0 and bs * ts == CHUNK

    tabs_p = _rope_tables(jnp.arange(tp, dtype=jnp.int32))
    tabs_s = tuple(jnp.tile(a, (bs, 1)) for a in _rope_tables(PAST_LEN + jnp.arange(ts, dtype=jnp.int32)))
    hid = jnp.arange(C_W, dtype=jnp.int32) // HEAD_DIM
    gmat = (hid[:, None] == hid[None, :]).astype(BF16)

    feat_major = lambda c: jnp.transpose(c, (0, 1, 3, 4, 2)).reshape(depth, bs, B_W, c.shape[2])
    cache_k, cache_v = feat_major(cache_win_k), feat_major(cache_win_v)
    keep = min(MAX_WINDOW, tp)
    win_k = jnp.zeros((depth, bp, B_W, keep), F32)
    win_v = jnp.zeros((depth, bp, B_W, keep), F32)

    xp = x_prompt.reshape(bp * tp, D_MODEL)
    xs = x_sample.reshape(bs * ts, D_MODEL)
    p_out = [[] for _ in range(2)]
    s_out = [[] for _ in range(5)]
    for l in range(depth):
        lw = _layer_weights(l, norm1_g, w_in, a_vnorm_g, a_ws, a_bs, c_conv_w, c_a_log, c_dt_bias, c_norm_g,
                            w_out, norm2_g, w_group, w_router, w1, w3, w2)
        xp, win_k, win_v, st = _prompt_layer(lw, xp, bp, tp, tabs_p, gmat, win_k, win_v, l)
        for acc, a in zip(p_out, st):
            acc.append(a)
        xs, st = _sample_layer(lw, xs, bs, ts, tabs_s, gmat, cache_k, cache_v, state_conv[l], state_delta[l], l)
        for acc, a in zip(s_out, st):
            acc.append(a)
    fg = final_g[None, :]
    y_prompt = _final_norm(xp, fg, _tile_rows(bp * tp, 512)).reshape(bp, tp, D_MODEL)
    y_sample = _final_norm(xs, fg, bs * ts).reshape(bs, ts, D_MODEL)
    rows_major = lambda w: jnp.transpose(w.reshape(depth, bp, B_HEADS, HEAD_DIM, keep), (0, 1, 4, 2, 3))
    return (y_prompt, y_sample, rows_major(win_k), rows_major(win_v), *[jnp.stack(a) for a in p_out],
            *[jnp.stack(a) for a in s_out])
```

```python
import functools
import math

import jax
import jax.numpy as jnp
from jax import lax
from jax.experimental import pallas as pl
from jax.experimental.pallas import tpu as pltpu

F32 = jnp.float32
BF16 = jnp.bfloat16
HI = lax.Precision.HIGHEST

D_MODEL = 1024
HEAD_DIM = 64
A_HEADS = 4
B_HEADS = 6
C_HEADS = 6
A_W = A_HEADS * HEAD_DIM
B_W = B_HEADS * HEAD_DIM
C_W = C_HEADS * HEAD_DIM
CHUNK = 128
PATTERNS = ((128, 1), (512, 4), (2048, 16))
MAX_WINDOW = 2048
ROT_DIM = HEAD_DIM // 4
ROPE_THETA = 500000.0
CONV_W = 4
DELTA_CHUNK = 64
N_GROUPS = 4
EXP_PER_GROUP = 8
N_EXPERTS = N_GROUPS * EXP_PER_GROUP
D_EXPERT = 256
EPS = 1e-6
PAST_LEN = 16384

LANES = 128
SUBLANES = 8
WBLK = 128
PROJ_MAIN = 2 * A_W + 3 * B_W + 4 * C_W
NEG = -1e30
ROUTE_OFF = N_GROUPS
VMEM_LIMIT = 56 * 1024 * 1024
GDN_GROUP = 4


def _cparams(sem):
    return pltpu.CompilerParams(dimension_semantics=sem, vmem_limit_bytes=VMEM_LIMIT)


def _sigmoid(x):
    return 1.0 / (1.0 + jnp.exp(-x))


def _silu(x):
    return x * _sigmoid(x)


def _dot(a, b, precision=None):
    return jnp.dot(a, b, preferred_element_type=F32, precision=precision)


def _dot_nt(a, b, precision=None):
    return lax.dot_general(a, b, (((1,), (1,)), ((), ())), preferred_element_type=F32, precision=precision)


def _proj_kernel(x_ref, g_ref, w_ref, wsm_ref, c_ref, s1_ref, s2_ref,
                 au_ref, av_ref, q_ref, k_ref, v_ref, cqkv_ref, cz_ref, cba_ref, *strided_refs, dils):
    x = x_ref[...]
    h = x * lax.rsqrt(jnp.mean(x * x, axis=-1, keepdims=True) + EPS) * g_ref[...]
    hb = h.astype(BF16)

    def seg(a, b):
        return _dot(hb, w_ref[:, a:b])

    au_ref[...] = seg(0, A_W)
    av_ref[...] = seg(A_W, 2 * A_W)
    c, s1, s2 = c_ref[...], s1_ref[...], s2_ref[...]
    q0 = 2 * A_W
    k0 = q0 + B_W
    for j in range(B_W // LANES):
        for base, ref, scale in ((q0, q_ref, HEAD_DIM ** -0.5), (k0, k_ref, None)):
            xc = seg(base + LANES * j, base + LANES * (j + 1))
            r = xc * c + pltpu.roll(xc, ROT_DIM // 2, 1) * s1 + pltpu.roll(xc, LANES - ROT_DIM // 2, 1) * s2
            if scale is not None:
                r = r * scale
            ref[:, LANES * j:LANES * (j + 1)] = r
    v0 = k0 + B_W
    v_ref[...] = seg(v0, v0 + B_W)
    c0 = v0 + B_W
    cqkv_ref[...] = seg(c0, c0 + 3 * C_W)
    cz_ref[...] = seg(c0 + 3 * C_W, c0 + 4 * C_W)
    cba_ref[...] = _dot(hb, wsm_ref[...])
    if dils:
        stage = strided_refs[-1]
        outs = strided_refs[:-1]
        tm = x.shape[0]
        nblk = B_W // LANES
        for a, src in enumerate((q_ref, k_ref, v_ref)):
            for b in range(nblk):
                stage[a * nblk + b] = src[:, LANES * b:LANES * (b + 1)]
        for i, d in enumerate(dils):
            for a in range(3):
                for j in range(d):
                    for b in range(nblk):
                        outs[3 * i + a][j, :, LANES * b:LANES * (b + 1)] = stage[a * nblk + b,
                                                                                 pl.ds(j, tm // d, stride=d), :]


def _proj(x, g, w_in_b, layer, w_small, tabs, tm, seq_len=None, dils=()):
    n = x.shape[0]
    nt = n // tm
    ntab = tabs[0].shape[0] // tm
    row = lambda i: (i, 0)
    fixed = lambda i: (0, 0)
    tab = lambda i: (i % ntab, 0)
    widths = (A_W, A_W, B_W, B_W, B_W, 3 * C_W, C_W, LANES)
    out_specs = [pl.BlockSpec((tm, w), row) for w in widths]
    out_shape = [jax.ShapeDtypeStruct((n, w), F32) for w in widths]
    for d in dils:
        tps = seq_len // tm
        out_specs += [pl.BlockSpec((None, d, tm // d, B_W), lambda i, tps=tps: (i // tps, 0, i % tps, 0))] * 3
        out_shape += [jax.ShapeDtypeStruct((n // seq_len, d, seq_len // d, B_W), F32)] * 3
    return pl.pallas_call(
        functools.partial(_proj_kernel, dils=dils),
        grid=(nt,),
        in_specs=[pl.BlockSpec((tm, D_MODEL), row), pl.BlockSpec((1, D_MODEL), fixed),
                  pl.BlockSpec((None, D_MODEL, PROJ_MAIN), lambda i: (layer, 0, 0)),
                  pl.BlockSpec((D_MODEL, LANES), fixed),
                  pl.BlockSpec((tm, LANES), tab), pl.BlockSpec((tm, LANES), tab), pl.BlockSpec((tm, LANES), tab)],
        out_specs=out_specs,
        out_shape=out_shape,
        scratch_shapes=[pltpu.VMEM((3 * B_W // LANES, tm, LANES), F32)] if dils else [],
        compiler_params=_cparams(("parallel",)),
        name="proj",
    )(x, g, w_in_b, w_small, *tabs)


def _rope_tables(pos):
    half = ROT_DIM // 2
    inv_freq = jnp.power(ROPE_THETA, -jnp.arange(0, ROT_DIM, 2, dtype=F32) / ROT_DIM)
    ang = pos.astype(F32)[:, None] * inv_freq[None, :]
    cos, sin = jnp.cos(ang), jnp.sin(ang)
    p = pos.shape[0]
    z8 = jnp.zeros((p, half), F32)
    rest0 = jnp.zeros((p, HEAD_DIM - ROT_DIM), F32)
    c64 = jnp.concatenate([cos, cos, jnp.ones((p, HEAD_DIM - ROT_DIM), F32)], axis=-1)
    s1 = jnp.concatenate([z8, sin, rest0], axis=-1)
    s2 = jnp.concatenate([-sin, z8, rest0], axis=-1)
    two = lambda a: jnp.concatenate([a, a], axis=-1)
    return two(c64), two(s1), two(s2)


def _chunk_mlp_kernel(u_ref, v_ref, g_ref, w_ref, b_ref, o_ref, vn_ref):
    v = v_ref[...]
    xc = v - jnp.mean(v, axis=-1, keepdims=True)
    vn = xc * lax.rsqrt(jnp.mean(xc * xc, axis=-1, keepdims=True) + EPS) * g_ref[...]
    vn_ref[...] = vn
    vb = vn.astype(BF16)
    rows = lax.broadcasted_iota(jnp.int32, (CHUNK, CHUNK), 0)
    cols = lax.broadcasted_iota(jnp.int32, (CHUNK, CHUNK), 1)
    tril = rows >= cols
    ws = [jnp.where(tril, w_ref[h], 0.0).astype(BF16) for h in range(A_HEADS)]
    for c in range(v.shape[0] // CHUNK):
        rs = slice(c * CHUNK, (c + 1) * CHUNK)
        parts = [_dot(ws[h], vb[rs, h * HEAD_DIM:(h + 1) * HEAD_DIM]) for h in range(A_HEADS)]
        o_ref[rs, :] = u_ref[rs, :] * (jnp.concatenate(parts, axis=-1) + b_ref[...])


def _chunk_mlp(u, v, gain, ws, bias_tile):
    n = u.shape[0]
    row = lambda i: (i, 0)
    tm = _tile_rows(n, 4 * CHUNK)
    return pl.pallas_call(
        _chunk_mlp_kernel,
        grid=(n // tm,),
        in_specs=[pl.BlockSpec((tm, A_W), row), pl.BlockSpec((tm, A_W), row),
                  pl.BlockSpec((1, A_W), lambda i: (0, 0)),
                  pl.BlockSpec((A_HEADS, CHUNK, CHUNK), lambda i: (0, 0, 0)),
                  pl.BlockSpec((CHUNK, A_W), lambda i: (0, 0))],
        out_specs=[pl.BlockSpec((tm, A_W), row), pl.BlockSpec((tm, A_W), row)],
        out_shape=[jax.ShapeDtypeStruct((n, A_W), F32)] * 2,
        compiler_params=_cparams(("parallel",)),
        name="chunk_mlp",
    )(u, v, gain, ws, bias_tile)


def _win_attn_kernel(q_ref, kp_ref, kc_ref, vp_ref, vc_ref, o_ref, lse_ref, *, seg_blocks):
    s = pl.program_id(0)
    lo = jnp.where(s % seg_blocks == 0, WBLK, 0)
    rows = lax.broadcasted_iota(jnp.int32, (WBLK, 2 * WBLK), 0)
    cols = lax.broadcasted_iota(jnp.int32, (WBLK, 2 * WBLK), 1)
    dist = rows + WBLK - cols
    valid = (dist >= 0) & (dist <= WBLK) & (cols >= lo)
    bias = jnp.where(valid, 0.0, NEG)
    lane = lax.broadcasted_iota(jnp.int32, (WBLK, LANES), 1)
    k2 = jnp.concatenate([kp_ref[...], kc_ref[...]], axis=0).astype(BF16)
    v2 = jnp.concatenate([vp_ref[...], vc_ref[...]], axis=0).astype(BF16)
    q = q_ref[...]
    heads = range(B_HEADS)
    sls = [slice(h // 2 * LANES, (h // 2 + 1) * LANES) for h in heads]
    hms = [(lane < HEAD_DIM) if h % 2 == 0 else (lane >= HEAD_DIM) for h in heads]
    scs = [_dot_nt(jnp.where(hms[h], q[:, sls[h]], 0.0).astype(BF16), k2[:, sls[h]]) + bias for h in heads]
    ms = [jnp.max(sc, axis=-1, keepdims=True) for sc in scs]
    es = [jnp.exp(sc - m) for sc, m in zip(scs, ms)]
    dens = [jnp.sum(e, axis=-1, keepdims=True) for e in es]
    os_ = [_dot((es[h] * (1.0 / dens[h])).astype(BF16), v2[:, sls[h]]) for h in heads]
    lse_tile = jnp.zeros((WBLK, LANES), F32)
    for h in heads:
        lse_tile = jnp.where(lane == h, ms[h] + jnp.log(dens[h]), lse_tile)
    for hp in range(B_W // LANES):
        o_ref[:, sls[2 * hp]] = jnp.where(hms[2 * hp], os_[2 * hp], os_[2 * hp + 1])
    lse_ref[...] = lse_tile


def _win_attn(qd, kd, vd, seg_blocks):
    n = qd.shape[0]
    cur = lambda s: (s, 0)
    prev = lambda s: (jnp.maximum(s - 1, 0), 0)
    blk = lambda w, im: pl.BlockSpec((WBLK, w), im)
    return pl.pallas_call(
        functools.partial(_win_attn_kernel, seg_blocks=seg_blocks),
        grid=(n // WBLK,),
        in_specs=[blk(B_W, cur), blk(B_W, prev), blk(B_W, cur), blk(B_W, prev), blk(B_W, cur)],
        out_specs=[blk(B_W, cur), blk(LANES, cur)],
        out_shape=[jax.ShapeDtypeStruct((n, B_W), F32), jax.ShapeDtypeStruct((n, LANES), F32)],
        compiler_params=_cparams(("parallel",)),
        name="win_attn",
    )(qd, kd, kd, vd, vd)


def _window_rows_kernel(k_ref, v_ref, kprev_ref, vprev_ref, ko_ref, vo_ref):
    del kprev_ref, vprev_ref
    for b in range(k_ref.shape[0]):
        ko_ref[b] = k_ref[b].T
        vo_ref[b] = v_ref[b].T


def _window_rows(k, v, k_all, v_all, layer, batch, t, keep):
    first = (t - keep) // WBLK
    src = pl.BlockSpec((batch, WBLK, B_W), lambda i: (0, first + i, 0))
    dst = pl.BlockSpec((None, batch, B_W, WBLK), lambda i: (layer, 0, 0, i))
    hbm = pl.BlockSpec(memory_space=pl.ANY)
    shape = jax.ShapeDtypeStruct(k_all.shape, F32)
    return pl.pallas_call(
        _window_rows_kernel,
        grid=(keep // WBLK,),
        in_specs=[src, src, hbm, hbm],
        out_specs=[dst, dst],
        out_shape=[shape, shape],
        input_output_aliases={2: 0, 3: 1},
        compiler_params=_cparams(("parallel",)),
        name="window_rows",
    )(k.reshape(batch, t, B_W), v.reshape(batch, t, B_W), k_all, v_all)


def _dec_attn_kernel(q_ref, kc_ref, vc_ref, kn_ref, vn_ref, o_ref, *, t_new, cache_len):
    rows_c = lax.broadcasted_iota(jnp.int32, (SUBLANES, cache_len), 0)
    cols_c = lax.broadcasted_iota(jnp.int32, (SUBLANES, cache_len), 1)
    dist_c = cache_len + rows_c % t_new - cols_c
    rows_n = lax.broadcasted_iota(jnp.int32, (SUBLANES, SUBLANES), 0)
    cols_n = lax.broadcasted_iota(jnp.int32, (SUBLANES, SUBLANES), 1)
    dist_n = rows_n % t_new - cols_n
    biases = []
    for window, dil in PATTERNS:
        vc_ok = (dist_c <= window) & ((dist_c & (dil - 1)) == 0)
        vn_ok = (dist_n >= 0) & ((dist_n & (dil - 1)) == 0)
        biases.append((jnp.where(vc_ok, 0.0, NEG), jnp.where(vn_ok, 0.0, NEG)))
    row8 = lax.broadcasted_iota(jnp.int32, (SUBLANES, LANES), 0)
    lane8 = lax.broadcasted_iota(jnp.int32, (SUBLANES, LANES), 1)
    own = (lane8 < HEAD_DIM) == (row8 < t_new)
    q = q_ref[...]
    outs = []
    for hp in range(B_W // LANES):
        sl = slice(hp * LANES, (hp + 1) * LANES)
        qq = jnp.concatenate([q[0:t_new, sl], q[0:t_new, sl]], axis=0)
        q8 = jnp.where(own, qq, 0.0).astype(BF16)
        kc, vc = kc_ref[sl, :].astype(BF16), vc_ref[sl, :].astype(BF16)
        kn, vn = kn_ref[:, sl].astype(BF16), vn_ref[:, sl].astype(BF16)
        sc_c = _dot(q8, kc)
        sc_n = _dot_nt(q8, kn)
        os_, lses = [], []
        for bc, bn in biases:
            a_c, a_n = sc_c + bc, sc_n + bn
            m = jnp.maximum(jnp.max(a_c, axis=-1, keepdims=True), jnp.max(a_n, axis=-1, keepdims=True))
            e_c, e_n = jnp.exp(a_c - m), jnp.exp(a_n - m)
            den = jnp.sum(e_c, axis=-1, keepdims=True) + jnp.sum(e_n, axis=-1, keepdims=True)
            inv = 1.0 / den
            os_.append(_dot_nt((e_c * inv).astype(BF16), vc) + _dot((e_n * inv).astype(BF16), vn))
            lses.append(m + jnp.log(den))
        mx = jnp.maximum(jnp.maximum(lses[0], lses[1]), lses[2])
        ws = [jnp.exp(l - mx) for l in lses]
        tot = ws[0] + ws[1] + ws[2]
        o8 = (ws[0] / tot) * os_[0] + (ws[1] / tot) * os_[1] + (ws[2] / tot) * os_[2]
        lane4 = lane8[0:t_new]
        outs.append(jnp.where(lane4 < HEAD_DIM, o8[0:t_new], o8[t_new:2 * t_new]))
    o_ref[...] = jnp.zeros((SUBLANES, B_W), F32)
    o_ref[0:t_new, :] = jnp.concatenate(outs, axis=-1)


def _dec_attn(q8, kc, vc, kn8, vn8, t_new, layer):
    _, b, _, cache_len = kc.shape
    assert 2 * t_new == SUBLANES
    small = pl.BlockSpec((None, SUBLANES, B_W), lambda i: (i, 0, 0))
    big = pl.BlockSpec((None, None, B_W, cache_len), lambda i: (layer, i, 0, 0))
    return pl.pallas_call(
        functools.partial(_dec_attn_kernel, t_new=t_new, cache_len=cache_len),
        grid=(b,),
        in_specs=[small, big, big, small, small],
        out_specs=small,
        out_shape=jax.ShapeDtypeStruct((b, SUBLANES, B_W), F32),
        compiler_params=_cparams(("parallel",)),
        name="dec_attn",
    )(q8, kc, vc, kn8, vn8)


def _split3(x):
    hi = x.astype(BF16)
    r1 = x - hi.astype(F32)
    mid = r1.astype(BF16)
    return hi, mid, (r1 - mid.astype(F32)).astype(BF16)


def _pair_dup(xx):
    lane = lax.broadcasted_iota(jnp.int32, xx.shape, 1)
    hi = xx.astype(BF16).astype(F32)
    return jnp.where(lane < HEAD_DIM, xx, xx - hi).astype(BF16)


def _pair(x):
    return _pair_dup(jnp.concatenate([x, x], axis=1))


def _lhs4(pair):
    return jnp.concatenate([pair, pair], axis=1)


def _rhs4(y):
    hi = y.astype(BF16)
    lo = (y - hi.astype(F32)).astype(BF16)
    return jnp.concatenate([hi, hi, lo, lo], axis=0)


def _unit_lower_solves(mats, rhss):
    n = DELTA_CHUNK
    w2 = 2 * HEAD_DIM
    rows = lax.broadcasted_iota(jnp.int32, (n, w2), 0)
    cols = lax.broadcasted_iota(jnp.int32, (n, w2), 1) % HEAD_DIM
    in16 = rows // 16 == cols // 16
    eye = jnp.where(rows == cols, 1.0, 0.0)
    ds = [jnp.where(in16, a, 0.0) for a in mats]
    es = [jnp.where(in16, 0.0, a) for a in mats]
    left = lambda xx: _lhs4(_pair_dup(xx))
    pw = [_dot(left(d), _rhs4(d)) for d in ds]
    ts = [eye - d for d in ds]
    for _ in range(2):
        outs = [_dot(jnp.concatenate([left(t), left(p)], axis=0), _rhs4(p)) for t, p in zip(ts, pw)]
        ts = [t + o[0:n] for t, o in zip(ts, outs)]
        pw = [o[n:2 * n] for o in outs]
    ts = [t + _dot(left(t), _rhs4(p)) for t, p in zip(ts, pw)]
    o5 = [_dot(left(t), _rhs4(jnp.concatenate([r, e], axis=1))) for t, r, e in zip(ts, rhss, es)]
    x0 = [o[:, 0:w2] for o in o5]
    nm = [o[:, w2:2 * w2] for o in o5]
    o6 = [_dot(left(m), _rhs4(jnp.concatenate([x, m], axis=1))) for m, x in zip(nm, x0)]
    ys = [x - o[:, 0:w2] for x, o in zip(x0, o6)]
    return [y + _dot(left(o[:, w2:2 * w2]), _rhs4(y)) for y, o in zip(ys, o6)]


def _gdn_prep_kernel(x_ref, cba_ref, buf_ref, cw_ref, alog_ref, dtb_ref, gmat_ref, xmat_ref,
                     u_ref, wq_ref, ak_ref, gt_ref,
                     xp_scr, q_scr, k_scr, v_scr, gb_scr, *, tm, valid_len, group):
    ti = pl.program_id(1)

    @pl.when(ti == 0)
    def _():
        xp_scr[0:SUBLANES, :] = buf_ref[...]

    x = x_ref[...]
    xp_scr[SUBLANES:SUBLANES + tm, :] = x
    off = SUBLANES - (CONV_W - 1)
    acc = xp_scr[off:off + tm, :] * cw_ref[0:1, :]
    for j in range(1, CONV_W):
        acc = acc + xp_scr[off + j:off + j + tm, :] * cw_ref[j:j + 1, :]
    xp_scr[0:SUBLANES, :] = x[tm - SUBLANES:tm, :]
    y = _silu(acc)
    q, k = y[:, 0:C_W], y[:, C_W:2 * C_W]
    gmat = gmat_ref[...]
    head_sum = lambda a: sum(_dot(p, gmat) for p in _split3(a))
    q_scr[...] = q * lax.rsqrt(head_sum(q * q) + EPS) * (HEAD_DIM ** -0.5)
    k_scr[...] = k * lax.rsqrt(head_sum(k * k) + EPS)
    v_scr[...] = y[:, 2 * C_W:3 * C_W]

    cba = cba_ref[...]
    lane = lax.broadcasted_iota(jnp.int32, (tm, LANES), 1)
    tpos = ti * tm + lax.broadcasted_iota(jnp.int32, (tm, LANES), 0)
    live = tpos < valid_len
    beta = _sigmoid(cba)
    z = cba + dtb_ref[...]
    softplus = jnp.maximum(z, 0.0) + jnp.log(1.0 + jnp.exp(-jnp.abs(z)))
    g = -jnp.exp(alog_ref[...]) * softplus
    is_g = (lane >= C_HEADS) & (lane < 2 * C_HEADS)
    gb_scr[...] = jnp.where(live, jnp.where(is_g, g, jnp.where(lane < C_HEADS, beta, 0.0)), 0.0)

    n = DELTA_CHUNK
    rows = lax.broadcasted_iota(jnp.int32, (n, 2 * n), 0)
    cols = lax.broadcasted_iota(jnp.int32, (n, 2 * n), 1) % n
    incl = rows >= cols
    strict = rows > cols
    ltri = jnp.where(incl[:, 0:n], 1.0, 0.0).astype(BF16)
    heads = range(C_HEADS)
    tiles = range(C_W // LANES)
    tsl = [slice(t * LANES, (t + 1) * LANES) for t in tiles]
    lane_c = lax.broadcasted_iota(jnp.int32, (n, LANES), 1)
    low = lane_c < HEAD_DIM
    own = [low if h % 2 == 0 else jnp.logical_not(low) for h in heads]
    xmat = xmat_ref[...]
    wide0 = C_HEADS * LANES

    def head_pairs(tile):
        rolled = pltpu.roll(tile, HEAD_DIM, 1)
        lo = rolled - rolled.astype(BF16).astype(F32)
        lane = lax.broadcasted_iota(jnp.int32, tile.shape, 1) < HEAD_DIM
        return jnp.where(lane, tile, lo).astype(BF16), jnp.where(lane, lo, tile).astype(BF16)

    def setup(c):
        rs = pl.ds(pl.multiple_of(c * n, n), n)
        gb = gb_scr[rs, :]
        g_only = jnp.where((lane_c >= C_HEADS) & (lane_c < 2 * C_HEADS), gb, 0.0)
        cg = sum(_dot(ltri, p) for p in _split3(g_only))
        cgt = cg.T
        cgl = cg[n - 1:n, :]
        gt_ref[pl.ds(pl.multiple_of(c * SUBLANES, SUBLANES), SUBLANES), :] = jnp.broadcast_to(
            jnp.exp(cgl), (SUBLANES, LANES))
        wide = sum(_dot(p, xmat) for p in _split3(jnp.where(lane_c < C_HEADS, gb, cg)))
        b128 = [wide[:, h * LANES:(h + 1) * LANES] for h in heads]
        c128 = [wide[:, wide0 + h * LANES:wide0 + (h + 1) * LANES] for h in heads]
        cgr = [jnp.concatenate([cgt[C_HEADS + h:C_HEADS + h + 1, :]] * 2, axis=1) for h in heads]
        decay = [jnp.where(incl, jnp.exp(jnp.minimum(c128[h] - cgr[h], 0.0)), 0.0) for h in heads]
        amat, at_pair, rhs, qd_pair, kdt_pair = [], [], [], [], []
        for t in tiles:
            he, ho = 2 * t, 2 * t + 1
            qn, kn, vv = q_scr[rs, tsl[t]], k_scr[rs, tsl[t]], v_scr[rs, tsl[t]]
            bnat = jnp.where(low, b128[he], b128[ho])
            cnat = jnp.where(low, c128[he], c128[ho])
            ecg = jnp.exp(cnat)
            kb = kn.astype(BF16)
            kk2 = jnp.concatenate([kb, kb], axis=0)
            for h in (he, ho):
                masked = jnp.concatenate([jnp.where(own[h], kn, 0.0), jnp.where(own[h], qn, 0.0)], axis=0)
                kkqk = _dot_nt(masked.astype(BF16), kk2)
                amat.append(jnp.where(strict, b128[h] * kkqk[0:n] * decay[h], 0.0))
                at_pair.append(_pair_dup(kkqk[n:2 * n] * decay[h]))
            bv = bnat * vv
            bk_rolled = pltpu.roll((bnat * ecg) * kn, HEAD_DIM, 1)
            rhs += [jnp.where(low, bv, bk_rolled), jnp.where(low, bk_rolled, bv)]
            qd_pair += list(head_pairs(qn * ecg))
            kd = kn * jnp.exp(cnat[n - 1:n, :] - cnat)
            kdt = jnp.concatenate([kd, kd], axis=0).T
            kdt_pair += [_pair_dup(kdt[0:n]), _pair_dup(kdt[n:2 * n])]
        return amat, rhs, at_pair, qd_pair, kdt_pair

    def finish(c, sol, at_pair, qd_pair, kdt_pair):
        w_pair = []
        for t in tiles:
            se, so = sol[2 * t], sol[2 * t + 1]
            u_ref[pl.ds(pl.multiple_of(c * n, n), n), tsl[t]] = jnp.where(low, se, so)
            w_odd, w_even = head_pairs(jnp.where(low, so, se))
            w_pair += [w_even, w_odd]
        r2 = pl.ds(pl.multiple_of(c * 2 * n, 2 * n), 2 * n)
        wq_ref[r2, :] = jnp.concatenate(
            [jnp.concatenate([w_pair[h], qd_pair[h]], axis=0) for h in heads], axis=1)
        ak_ref[r2, :] = jnp.concatenate(
            [jnp.concatenate([at_pair[h], kdt_pair[h]], axis=0) for h in heads], axis=1)

    def chunks(i, carry):
        parts = [setup(i * group + j) for j in range(group)]
        sol = _unit_lower_solves([a for p in parts for a in p[0]], [r for p in parts for r in p[1]])
        for j, p in enumerate(parts):
            finish(i * group + j, sol[j * C_HEADS:(j + 1) * C_HEADS], *p[2:])
        return carry

    lax.fori_loop(0, tm // n // group, chunks, 0)


def _gdn_constants():
    hid = jnp.arange(C_W, dtype=jnp.int32) // HEAD_DIM
    gmat = (hid[:, None] == hid[None, :]).astype(BF16)
    src = jnp.arange(LANES, dtype=jnp.int32)[:, None]
    dst = jnp.arange(2 * C_HEADS * LANES, dtype=jnp.int32)[None, :] // LANES
    return gmat, (src == dst).astype(BF16)


def _gdn_prep(cqkv, cba, buf8, conv_w, alog_row, dtb_row, consts, batch, t, tm, valid_len):
    gmat, xmat = consts
    n = batch * t
    nt = t // tm
    row = lambda b, i: (b * nt + i, 0)
    fixed = lambda b, i: (0, 0)
    nch = tm // DELTA_CHUNK
    pair_w = C_HEADS * 2 * HEAD_DIM
    outs = [jax.ShapeDtypeStruct((n, C_W), F32), jax.ShapeDtypeStruct((2 * n, pair_w), BF16),
            jax.ShapeDtypeStruct((2 * n, pair_w), BF16),
            jax.ShapeDtypeStruct((n // DELTA_CHUNK * SUBLANES, LANES), F32)]
    return pl.pallas_call(
        functools.partial(_gdn_prep_kernel, tm=tm, valid_len=valid_len, group=min(GDN_GROUP, nch)),
        grid=(batch, nt),
        in_specs=[pl.BlockSpec((tm, 3 * C_W), row), pl.BlockSpec((tm, LANES), row),
                  pl.BlockSpec((None, SUBLANES, 3 * C_W), lambda b, i: (b, 0, 0)),
                  pl.BlockSpec((CONV_W, 3 * C_W), fixed), pl.BlockSpec((1, LANES), fixed),
                  pl.BlockSpec((1, LANES), fixed), pl.BlockSpec((C_W, C_W), fixed),
                  pl.BlockSpec(xmat.shape, fixed)],
        out_specs=[pl.BlockSpec((tm, C_W), row), pl.BlockSpec((2 * tm, pair_w), row),
                   pl.BlockSpec((2 * tm, pair_w), row), pl.BlockSpec((nch * SUBLANES, LANES), row)],
        out_shape=outs,
        scratch_shapes=[pltpu.VMEM((tm + SUBLANES, 3 * C_W), F32), pltpu.VMEM((tm, C_W), F32),
                        pltpu.VMEM((tm, C_W), F32), pltpu.VMEM((tm, C_W), F32), pltpu.VMEM((tm, LANES), F32)],
        compiler_params=_cparams(("parallel", "arbitrary")),
        name="gdn_prep",
    )(cqkv, cba, buf8, conv_w, alog_row, dtb_row, gmat, xmat)


def _gdn_scan_kernel(u_ref, wq_ref, ak_ref, gt_ref, z_ref, s0_ref, gn_ref,
                     o_ref, sfin_ref, s_scr, *, bg):
    c = pl.program_id(1)

    @pl.when(c == 0)
    def _():
        s_scr[...] = s0_ref[...]

    gn = gn_ref[...]
    n = DELTA_CHUNK
    chains = [(b, h, slice(h * HEAD_DIM, (h + 1) * HEAD_DIM), slice(h * 2 * HEAD_DIM, (h + 1) * 2 * HEAD_DIM))
              for b in range(bg) for h in range(C_HEADS)]
    st = [s_scr[b, h] for b, h, _, _ in chains]
    r1 = [_dot(_lhs4(wq_ref[b, :, ps]), _rhs4(s)) for (b, _, _, ps), s in zip(chains, st)]
    up = [u_ref[b, :, sl] - r[0:n] for (b, _, sl, _), r in zip(chains, r1)]
    r2 = [_dot(_lhs4(ak_ref[b, :, ps]), _rhs4(x)) for (b, _, _, ps), x in zip(chains, up)]
    for (b, h, _, _), s, r in zip(chains, st, r2):
        s_scr[b, h] = gt_ref[b, 0:1, C_HEADS + h:C_HEADS + h + 1] * s + r[n:2 * n]
    os_ = [a[n:2 * n] + r[0:n] for a, r in zip(r1, r2)]
    outs = [o * lax.rsqrt(jnp.mean(o * o, axis=-1, keepdims=True) + EPS) * gn * _silu(z_ref[b, :, sl])
            for (b, _, sl, _), o in zip(chains, os_)]
    for b in range(bg):
        o_ref[b] = jnp.concatenate(outs[b * C_HEADS:(b + 1) * C_HEADS], axis=1)

    @pl.when(c == pl.num_programs(1) - 1)
    def _():
        sfin_ref[...] = s_scr[...]


def _gdn_scan(u, wq, ak, gt, z, s0, gnorm, batch, t, bg):
    n = DELTA_CHUNK
    nc = t // n
    pair_w = C_HEADS * 2 * HEAD_DIM
    v3 = lambda a: a.reshape(batch, t, C_W)
    p3 = lambda a: a.reshape(batch, 2 * t, pair_w)
    tok = pl.BlockSpec((bg, n, C_W), lambda b, c: (b, c, 0))
    pair = pl.BlockSpec((bg, 2 * n, pair_w), lambda b, c: (b, c, 0))
    st = pl.BlockSpec((bg, C_HEADS, HEAD_DIM, HEAD_DIM), lambda b, c: (b, 0, 0, 0))
    return pl.pallas_call(
        functools.partial(_gdn_scan_kernel, bg=bg),
        grid=(batch // bg, nc),
        in_specs=[tok, pair, pair,
                  pl.BlockSpec((bg, SUBLANES, LANES), lambda b, c: (b, c, 0)),
                  tok, st, pl.BlockSpec((1, HEAD_DIM), lambda b, c: (0, 0))],
        out_specs=[tok, st],
        out_shape=[jax.ShapeDtypeStruct((batch, t, C_W), F32),
                   jax.ShapeDtypeStruct((batch, C_HEADS, HEAD_DIM, HEAD_DIM), F32)],
        scratch_shapes=[pltpu.VMEM((bg, C_HEADS, HEAD_DIM, HEAD_DIM), F32)],
        compiler_params=_cparams(("parallel", "arbitrary")),
        name="gdn_scan",
    )(v3(u), p3(wq), p3(ak), gt.reshape(batch, nc * SUBLANES, LANES), v3(z), s0, gnorm)


def _head_expand(wt):
    tm = wt.shape[0]
    return jnp.concatenate([jnp.broadcast_to(wt[:, h:h + 1], (tm, HEAD_DIM)) for h in range(B_HEADS)], axis=-1)


def _out_proj_kernel(*refs, dils):
    if dils:
        npat = len(dils)
        a_ref = refs[0]
        o_refs, l_refs = refs[1:1 + npat], refs[1 + npat:1 + 2 * npat]
        c_ref, x_ref, w_ref, y_ref, o_scr, l_scr = refs[1 + 2 * npat:]
        tm = x_ref.shape[0]
        os_, ls = [], []
        for i, d in enumerate(dils):
            if d == 1:
                os_.append(o_refs[i][0])
                ls.append(l_refs[i][0])
            else:
                nblk = B_W // LANES
                for j in range(d):
                    rows = pl.ds(j, tm // d, stride=d)
                    for b in range(nblk):
                        o_scr[i * nblk + b, rows, :] = o_refs[i][j, :, LANES * b:LANES * (b + 1)]
                    l_scr[i, rows, :] = l_refs[i][j]
                os_.append(jnp.concatenate([o_scr[i * nblk + b] for b in range(nblk)], axis=-1))
                ls.append(l_scr[i])
        mx = functools.reduce(jnp.maximum, ls)
        es = [jnp.exp(l - mx) for l in ls]
        tot = functools.reduce(lambda a, b: a + b, es)
        ob = functools.reduce(lambda a, b: a + b, [_head_expand(e / tot) * o for e, o in zip(es, os_)])
    else:
        a_ref, b_ref, c_ref, x_ref, w_ref, y_ref = refs
        ob = b_ref[...]
    cat = jnp.concatenate([a_ref[...], ob, c_ref[...]], axis=-1).astype(BF16)
    y_ref[...] = x_ref[...] + _dot(cat, w_ref[...])


def _out_proj(out_a, out_b, lses, out_c, x, w_out, tm, seq_len=None, dils=()):
    n = x.shape[0]
    row = lambda i: (i, 0)
    spec = lambda w: pl.BlockSpec((tm, w), row)
    scratch = []
    if dils:
        tps = seq_len // tm
        strided = lambda d, w: pl.BlockSpec((None, d, tm // d, w), lambda i: (i // tps, 0, i % tps, 0))
        ins = [out_a, *out_b, *lses, out_c, x, w_out]
        specs = ([spec(A_W)] + [strided(d, B_W) for d in dils] + [strided(d, LANES) for d in dils]
                 + [spec(C_W), spec(D_MODEL)])
        scratch = [pltpu.VMEM((len(dils) * B_W // LANES, tm, LANES), F32), pltpu.VMEM((len(dils), tm, LANES), F32)]
    else:
        ins = [out_a, out_b, out_c, x, w_out]
        specs = [spec(A_W), spec(B_W), spec(C_W), spec(D_MODEL)]
    specs.append(pl.BlockSpec((D_MODEL, D_MODEL), lambda i: (0, 0)))
    return pl.pallas_call(
        functools.partial(_out_proj_kernel, dils=dils),
        grid=(n // tm,),
        in_specs=specs,
        out_specs=spec(D_MODEL),
        out_shape=jax.ShapeDtypeStruct((n, D_MODEL), F32),
        scratch_shapes=scratch,
        compiler_params=_cparams(("parallel",)),
        name="out_proj",
    )(*ins)


def _router_kernel(x_ref, g_ref, w_ref, dense_ref):
    x = x_ref[...]
    t = x * lax.rsqrt(jnp.mean(x * x, axis=-1, keepdims=True) + EPS) * g_ref[...]
    lg = _dot(t, w_ref[...], HI)
    tm = lg.shape[0]
    lane = lax.broadcasted_iota(jnp.int32, (tm, LANES), 1).astype(F32)
    big = float(LANES)
    is_grp = lane < N_GROUPS
    gl = jnp.where(is_grp, lg, NEG)
    gmax = jnp.max(gl, axis=-1, keepdims=True)
    gsum = jnp.sum(jnp.where(is_grp, jnp.exp(gl - gmax), 0.0), axis=-1, keepdims=True)
    g_w = 1.0 / gsum
    g_idx = jnp.min(jnp.where(is_grp & (gl == gmax), lane, big), axis=-1, keepdims=True)
    lo = ROUTE_OFF + EXP_PER_GROUP * g_idx
    sel = (lane >= lo) & (lane < lo + EXP_PER_GROUP)
    el = jnp.where(sel, lg, NEG)
    m1 = jnp.max(el, axis=-1, keepdims=True)
    esum = jnp.sum(jnp.where(sel, jnp.exp(el - m1), 0.0), axis=-1, keepdims=True)
    i1 = jnp.min(jnp.where(sel & (el == m1), lane, big), axis=-1, keepdims=True)
    el2 = jnp.where(lane == i1, NEG, el)
    m2 = jnp.max(el2, axis=-1, keepdims=True)
    i2 = jnp.min(jnp.where(sel & (lane != i1) & (el2 == m2), lane, big), axis=-1, keepdims=True)
    p1 = 1.0 / esum
    p2 = jnp.exp(m2 - m1) / esum
    tot = p1 + p2
    dense_ref[...] = jnp.where(lane == i1, g_w * (p1 / tot), 0.0) + jnp.where(lane == i2, g_w * (p2 / tot), 0.0)


def _router(x, g, w_route, tm):
    n = x.shape[0]
    row = lambda i: (i, 0)
    return pl.pallas_call(
        _router_kernel,
        grid=(n // tm,),
        in_specs=[pl.BlockSpec((tm, D_MODEL), row), pl.BlockSpec((1, D_MODEL), lambda i: (0, 0)),
                  pl.BlockSpec((D_MODEL, LANES), lambda i: (0, 0))],
        out_specs=pl.BlockSpec((tm, LANES), row),
        out_shape=jax.ShapeDtypeStruct((n, LANES), F32),
        compiler_params=_cparams(("parallel",)),
        name="router",
    )(x, g, w_route)


def _moe_kernel(x_ref, g_ref, dense_ref, w1_ref, w3_ref, w2_ref, y_ref, t_scr):
    e = pl.program_id(1)

    @pl.when(e == 0)
    def _():
        x = x_ref[...]
        t_scr[...] = (x * lax.rsqrt(jnp.mean(x * x, axis=-1, keepdims=True) + EPS) * g_ref[...]).astype(BF16)
        y_ref[...] = x

    tb = t_scr[...]
    dense = dense_ref[...]
    lane = lax.broadcasted_iota(jnp.int32, dense.shape, 1)
    gate = jnp.sum(jnp.where(lane == e + ROUTE_OFF, dense, 0.0), axis=-1, keepdims=True)
    hid = _silu(_dot(tb, w1_ref[...].astype(BF16))) * _dot(tb, w3_ref[...].astype(BF16))
    y_ref[...] += _dot((hid * gate).astype(BF16), w2_ref[...].astype(BF16))


def _moe(x, g, dense, w1, w3, w2, layer, tm):
    n = x.shape[0]
    row = lambda i, e: (i, 0)
    expert = lambda i, e: (layer, e, 0, 0)
    return pl.pallas_call(
        _moe_kernel,
        grid=(n // tm, N_EXPERTS),
        in_specs=[pl.BlockSpec((tm, D_MODEL), row), pl.BlockSpec((1, D_MODEL), lambda i, e: (0, 0)),
                  pl.BlockSpec((tm, LANES), row),
                  pl.BlockSpec((None, None, D_MODEL, D_EXPERT), expert),
                  pl.BlockSpec((None, None, D_MODEL, D_EXPERT), expert),
                  pl.BlockSpec((None, None, D_EXPERT, D_MODEL), expert)],
        out_specs=pl.BlockSpec((tm, D_MODEL), row),
        out_shape=jax.ShapeDtypeStruct((n, D_MODEL), F32),
        scratch_shapes=[pltpu.VMEM((tm, D_MODEL), BF16)],
        compiler_params=_cparams(("parallel", "arbitrary")),
        name="moe",
    )(x, g, dense, w1, w3, w2)


def _final_norm_kernel(x_ref, g_ref, y_ref):
    x = x_ref[...]
    y_ref[...] = x * lax.rsqrt(jnp.mean(x * x, axis=-1, keepdims=True) + EPS) * g_ref[...]


def _final_norm(x, g, tm):
    n = x.shape[0]
    return pl.pallas_call(
        _final_norm_kernel,
        grid=(n // tm,),
        in_specs=[pl.BlockSpec((tm, D_MODEL), lambda i: (i, 0)), pl.BlockSpec((1, D_MODEL), lambda i: (0, 0))],
        out_specs=pl.BlockSpec((tm, D_MODEL), lambda i: (i, 0)),
        out_shape=jax.ShapeDtypeStruct((n, D_MODEL), F32),
        compiler_params=_cparams(("parallel",)),
        name="final_norm",
    )(x, g)


def _tile_rows(n, cap):
    tm = min(n, cap)
    assert n % tm == 0
    return tm


def _layer_weights(l, norm1_g, w_in, a_vnorm_g, a_ws, a_bs, c_conv_w, c_a_log, c_dt_bias, c_norm_g,
                   w_out, norm2_g, w_group, w_router, w1, w3, w2):
    pad_l = lambda a, left: jnp.pad(a, ((0, 0), (left, LANES - left - a.shape[-1])))
    return dict(
        norm1_g=norm1_g[l][None, :],
        w_in_b=w_in.astype(BF16),
        w_small=pad_l(w_in[l, :, PROJ_MAIN:], 0).astype(BF16),
        a_gain=a_vnorm_g[l][None, :],
        a_ws=a_ws[l],
        a_bs=a_bs[l],
        conv_w=c_conv_w[l],
        alog_row=pad_l(c_a_log[l][None, :], C_HEADS),
        dtb_row=pad_l(c_dt_bias[l][None, :], C_HEADS),
        gnorm=c_norm_g[l][None, :],
        w_out=w_out[l].astype(BF16),
        norm2_g=norm2_g[l][None, :],
        w_route=pad_l(jnp.concatenate([w_group[l], w_router[l]], axis=-1), 0),
        w1=w1, w3=w3, w2=w2, layer=l,
    )


def _mixer_c(lw, cqkv, cba, cz, conv_buf, s0, gmat, batch, t, tm, valid_len, bg):
    buf8 = jnp.pad(conv_buf, ((0, 0), (SUBLANES - (CONV_W - 1), 0), (0, 0)))
    u, wq, ak, gt = _gdn_prep(cqkv, cba, buf8, lw["conv_w"], lw["alog_row"], lw["dtb_row"], gmat,
                              batch, t, tm, valid_len)
    return _gdn_scan(u, wq, ak, gt, cz, s0, lw["gnorm"], batch, t, bg)


def _ffn(lw, x):
    n = x.shape[0]
    dense = _router(x, lw["norm2_g"], lw["w_route"], _tile_rows(n, 512))
    return _moe(x, lw["norm2_g"], dense, lw["w1"], lw["w3"], lw["w2"], lw["layer"], _tile_rows(n, 1024))


def _prompt_layer(lw, x, batch, t, tabs, gmat, win_k, win_v, layer):
    n = batch * t
    dils = tuple(d for _, d in PATTERNS)
    tm = _tile_rows(n, 512)
    au, av, q, k, v, cqkv, cz, cba, *strided = _proj(x, lw["norm1_g"], lw["w_in_b"], layer, lw["w_small"], tabs, tm,
                                                     seq_len=t, dils=dils[1:])

    bias_tile = jnp.repeat(lw["a_bs"].T, HEAD_DIM, axis=1)
    out_a, _ = _chunk_mlp(au, av, lw["a_gain"], lw["a_ws"], bias_tile)

    qkv = [(q, k, v)] + [tuple(a.reshape(n, B_W) for a in strided[3 * i:3 * i + 3]) for i in range(len(dils) - 1)]
    outs, lses = [], []
    for d, (qd, kd, vd) in zip(dils, qkv):
        o_d, lse_d = _win_attn(qd, kd, vd, t // d // WBLK)
        outs.append(o_d.reshape(batch, d, t // d, B_W))
        lses.append(lse_d.reshape(batch, d, t // d, LANES))

    zeros_buf = jnp.zeros((batch, CONV_W - 1, 3 * C_W), F32)
    zeros_s = jnp.zeros((batch, C_HEADS, HEAD_DIM, HEAD_DIM), F32)
    out_c, s_new = _mixer_c(lw, cqkv, cba, cz, zeros_buf, zeros_s, gmat, batch, t, 256, t, batch)
    out_c = out_c.reshape(n, C_W)

    x = _out_proj(out_a, outs, lses, out_c, x, lw["w_out"], tm, seq_len=t, dils=dils)
    x = _ffn(lw, x)

    win_k, win_v = _window_rows(k, v, win_k, win_v, layer, batch, t, win_k.shape[-1])
    conv_state = cqkv.reshape(batch, t, 3 * C_W)[:, t - (CONV_W - 1):]
    return x, win_k, win_v, (conv_state, s_new)


def _sample_layer(lw, x, batch, t, tabs, gmat, kbuf, vbuf, conv_buf, s0, layer):
    n = batch * t
    au, av, q, k, v, cqkv, cz, cba = _proj(x, lw["norm1_g"], lw["w_in_b"], layer, lw["w_small"], tabs, n)

    eye = jnp.eye(batch, dtype=F32)
    ws_bd = jnp.stack([jnp.kron(eye, lw["a_ws"][h, :t, :t]) for h in range(A_HEADS)])
    bias_tile = jnp.tile(jnp.repeat(lw["a_bs"][:, :t].T, HEAD_DIM, axis=1), (batch, 1))
    out_a, a_rows = _chunk_mlp(au, av, lw["a_gain"], ws_bd, bias_tile)

    pad8 = lambda a: jnp.pad(a.reshape(batch, t, B_W), ((0, 0), (0, SUBLANES - t), (0, 0)))
    out_b = _dec_attn(pad8(q), kbuf, vbuf, pad8(k), pad8(v), t, layer)[:, :t].reshape(n, B_W)

    tp = DELTA_CHUNK
    padt = lambda a: jnp.pad(a.reshape(batch, t, -1), ((0, 0), (0, tp - t), (0, 0))).reshape(batch * tp, -1)
    out_c, s_new = _mixer_c(lw, padt(cqkv), padt(cba), padt(cz), conv_buf, s0, gmat, batch, tp, tp, t, 4)
    out_c = out_c[:, :t].reshape(n, C_W)

    x = _out_proj(out_a, out_b, None, out_c, x, lw["w_out"], n)
    x = _ffn(lw, x)

    heads = lambda a: a.reshape(batch, t, B_HEADS, HEAD_DIM)
    conv_state = jnp.concatenate([conv_buf, cqkv.reshape(batch, t, 3 * C_W)], axis=1)[:, -(CONV_W - 1):]
    return x, (heads(k), heads(v), a_rows.reshape(batch, t, A_W), conv_state, s_new)


def kernel(x_prompt, x_sample, cache_win_k, cache_win_v, state_conv, state_delta, norm1_g, w_in, a_vnorm_g, a_ws, a_bs, c_conv_w, c_a_log, c_dt_bias, c_norm_g, w_out, norm2_g, w_group, w_router, w1, w3, w2, final_g):
    bp, tp, _ = x_prompt.shape
    bs, ts, _ = x_sample.shape
    depth = w_in.shape[0]
    assert tp % (PATTERNS[-1][1] * WBLK) == 0 and tp % 512 == 0 and bs * ts == CHUNK

    tabs_p = _rope_tables(jnp.arange(tp, dtype=jnp.int32))
    tabs_s = tuple(jnp.tile(a, (bs, 1)) for a in _rope_tables(PAST_LEN + jnp.arange(ts, dtype=jnp.int32)))
    gmat = _gdn_constants()

    feat_major = lambda c: jnp.transpose(c, (0, 1, 3, 4, 2)).reshape(depth, bs, B_W, c.shape[2])
    cache_k, cache_v = feat_major(cache_win_k), feat_major(cache_win_v)
    keep = min(MAX_WINDOW, tp)
    win_k = jnp.zeros((depth, bp, B_W, keep), F32)
    win_v = jnp.zeros((depth, bp, B_W, keep), F32)

    xp = x_prompt.reshape(bp * tp, D_MODEL)
    xs = x_sample.reshape(bs * ts, D_MODEL)
    p_out = [[] for _ in range(2)]
    s_out = [[] for _ in range(5)]
    for l in range(depth):
        lw = _layer_weights(l, norm1_g, w_in, a_vnorm_g, a_ws, a_bs, c_conv_w, c_a_log, c_dt_bias, c_norm_g,
                            w_out, norm2_g, w_group, w_router, w1, w3, w2)
        xp, win_k, win_v, st = _prompt_layer(lw, xp, bp, tp, tabs_p, gmat, win_k, win_v, l)
        for acc, a in zip(p_out, st):
            acc.append(a)
        xs, st = _sample_layer(lw, xs, bs, ts, tabs_s, gmat, cache_k, cache_v, state_conv[l], state_delta[l], l)
        for acc, a in zip(s_out, st):
            acc.append(a)
    fg = final_g[None, :]
    y_prompt = _final_norm(xp, fg, _tile_rows(bp * tp, 512)).reshape(bp, tp, D_MODEL)
    y_sample = _final_norm(xs, fg, bs * ts).reshape(bs, ts, D_MODEL)
    rows_major = lambda w: jnp.transpose(w.reshape(depth, bp, B_HEADS, HEAD_DIM, keep), (0, 1, 4, 2, 3))
    return (y_prompt, y_sample, rows_major(win_k), rows_major(win_v), *[jnp.stack(a) for a in p_out],
            *[jnp.stack(a) for a in s_out])
```

```python
import functools
import math

import jax
import jax.numpy as jnp
from jax import lax
from jax.experimental import pallas as pl
from jax.experimental.pallas import tpu as pltpu

F32 = jnp.float32
BF16 = jnp.bfloat16
HI = lax.Precision.HIGHEST

D_MODEL = 1024
HEAD_DIM = 64
A_HEADS = 4
B_HEADS = 6
C_HEADS = 6
A_W = A_HEADS * HEAD_DIM
B_W = B_HEADS * HEAD_DIM
C_W = C_HEADS * HEAD_DIM
CHUNK = 128
PATTERNS = ((128, 1), (512, 4), (2048, 16))
MAX_WINDOW = 2048
ROT_DIM = HEAD_DIM // 4
ROPE_THETA = 500000.0
CONV_W = 4
DELTA_CHUNK = 64
N_GROUPS = 4
EXP_PER_GROUP = 8
N_EXPERTS = N_GROUPS * EXP_PER_GROUP
D_EXPERT = 256
EPS = 1e-6
PAST_LEN = 16384

LANES = 128
SUBLANES = 8
WBLK = 128
PROJ_MAIN = 2 * A_W + 3 * B_W + 4 * C_W
NEG = -1e30
ROUTE_OFF = N_GROUPS
VMEM_LIMIT = 56 * 1024 * 1024
MOE_TILE = 1024
MOE_BLOCK = 128
GDN_GROUP = 4


def _cparams(sem):
    return pltpu.CompilerParams(dimension_semantics=sem, vmem_limit_bytes=VMEM_LIMIT)


def _sigmoid(x):
    return 1.0 / (1.0 + jnp.exp(-x))


def _silu(x):
    return x * _sigmoid(x)


def _dot(a, b, precision=None):
    return jnp.dot(a, b, preferred_element_type=F32, precision=precision)


def _dot_nt(a, b, precision=None):
    return lax.dot_general(a, b, (((1,), (1,)), ((), ())), preferred_element_type=F32, precision=precision)


def _proj_kernel(x_ref, g_ref, w_ref, wsm_ref, c_ref, s1_ref, s2_ref,
                 au_ref, av_ref, q_ref, k_ref, v_ref, cqkv_ref, cz_ref, cba_ref, *strided_refs, dils):
    x = x_ref[...]
    h = x * lax.rsqrt(jnp.mean(x * x, axis=-1, keepdims=True) + EPS) * g_ref[...]
    hb = h.astype(BF16)

    def seg(a, b):
        return _dot(hb, w_ref[:, a:b])

    au_ref[...] = seg(0, A_W)
    av_ref[...] = seg(A_W, 2 * A_W)
    c, s1, s2 = c_ref[...], s1_ref[...], s2_ref[...]
    q0 = 2 * A_W
    k0 = q0 + B_W
    for j in range(B_W // LANES):
        for base, ref, scale in ((q0, q_ref, HEAD_DIM ** -0.5), (k0, k_ref, None)):
            xc = seg(base + LANES * j, base + LANES * (j + 1))
            r = xc * c + pltpu.roll(xc, ROT_DIM // 2, 1) * s1 + pltpu.roll(xc, LANES - ROT_DIM // 2, 1) * s2
            if scale is not None:
                r = r * scale
            ref[:, LANES * j:LANES * (j + 1)] = r
    v0 = k0 + B_W
    v_ref[...] = seg(v0, v0 + B_W)
    c0 = v0 + B_W
    cqkv_ref[...] = seg(c0, c0 + 3 * C_W)
    cz_ref[...] = seg(c0 + 3 * C_W, c0 + 4 * C_W)
    cba_ref[...] = _dot(hb, wsm_ref[...])
    if dils:
        stage = strided_refs[-1]
        outs = strided_refs[:-1]
        tm = x.shape[0]
        nblk = B_W // LANES
        for a, src in enumerate((q_ref, k_ref, v_ref)):
            for b in range(nblk):
                stage[a * nblk + b] = src[:, LANES * b:LANES * (b + 1)]
        for i, d in enumerate(dils):
            for a in range(3):
                for j in range(d):
                    for b in range(nblk):
                        outs[3 * i + a][j, :, LANES * b:LANES * (b + 1)] = stage[a * nblk + b,
                                                                                 pl.ds(j, tm // d, stride=d), :]


def _proj(x, g, w_in_b, layer, w_small, tabs, tm, seq_len=None, dils=()):
    n = x.shape[0]
    nt = n // tm
    ntab = tabs[0].shape[0] // tm
    row = lambda i: (i, 0)
    fixed = lambda i: (0, 0)
    tab = lambda i: (i % ntab, 0)
    widths = (A_W, A_W, B_W, B_W, B_W, 3 * C_W, C_W, LANES)
    out_specs = [pl.BlockSpec((tm, w), row) for w in widths]
    out_shape = [jax.ShapeDtypeStruct((n, w), F32) for w in widths]
    for d in dils:
        tps = seq_len // tm
        out_specs += [pl.BlockSpec((None, d, tm // d, B_W), lambda i, tps=tps: (i // tps, 0, i % tps, 0))] * 3
        out_shape += [jax.ShapeDtypeStruct((n // seq_len, d, seq_len // d, B_W), F32)] * 3
    return pl.pallas_call(
        functools.partial(_proj_kernel, dils=dils),
        grid=(nt,),
        in_specs=[pl.BlockSpec((tm, D_MODEL), row), pl.BlockSpec((1, D_MODEL), fixed),
                  pl.BlockSpec((None, D_MODEL, PROJ_MAIN), lambda i: (layer, 0, 0)),
                  pl.BlockSpec((D_MODEL, LANES), fixed),
                  pl.BlockSpec((tm, LANES), tab), pl.BlockSpec((tm, LANES), tab), pl.BlockSpec((tm, LANES), tab)],
        out_specs=out_specs,
        out_shape=out_shape,
        scratch_shapes=[pltpu.VMEM((3 * B_W // LANES, tm, LANES), F32)] if dils else [],
        compiler_params=_cparams(("parallel",)),
        name="proj",
    )(x, g, w_in_b, w_small, *tabs)


def _rope_tables(pos):
    half = ROT_DIM // 2
    inv_freq = jnp.power(ROPE_THETA, -jnp.arange(0, ROT_DIM, 2, dtype=F32) / ROT_DIM)
    ang = pos.astype(F32)[:, None] * inv_freq[None, :]
    cos, sin = jnp.cos(ang), jnp.sin(ang)
    p = pos.shape[0]
    z8 = jnp.zeros((p, half), F32)
    rest0 = jnp.zeros((p, HEAD_DIM - ROT_DIM), F32)
    c64 = jnp.concatenate([cos, cos, jnp.ones((p, HEAD_DIM - ROT_DIM), F32)], axis=-1)
    s1 = jnp.concatenate([z8, sin, rest0], axis=-1)
    s2 = jnp.concatenate([-sin, z8, rest0], axis=-1)
    two = lambda a: jnp.concatenate([a, a], axis=-1)
    return two(c64), two(s1), two(s2)


def _chunk_mlp_kernel(u_ref, v_ref, g_ref, w_ref, b_ref, o_ref, vn_ref):
    v = v_ref[...]
    xc = v - jnp.mean(v, axis=-1, keepdims=True)
    vn = xc * lax.rsqrt(jnp.mean(xc * xc, axis=-1, keepdims=True) + EPS) * g_ref[...]
    vn_ref[...] = vn
    vb = vn.astype(BF16)
    rows = lax.broadcasted_iota(jnp.int32, (CHUNK, CHUNK), 0)
    cols = lax.broadcasted_iota(jnp.int32, (CHUNK, CHUNK), 1)
    tril = rows >= cols
    ws = [jnp.where(tril, w_ref[h], 0.0).astype(BF16) for h in range(A_HEADS)]
    for c in range(v.shape[0] // CHUNK):
        rs = slice(c * CHUNK, (c + 1) * CHUNK)
        parts = [_dot(ws[h], vb[rs, h * HEAD_DIM:(h + 1) * HEAD_DIM]) for h in range(A_HEADS)]
        o_ref[rs, :] = u_ref[rs, :] * (jnp.concatenate(parts, axis=-1) + b_ref[...])


def _chunk_mlp(u, v, gain, ws, bias_tile):
    n = u.shape[0]
    row = lambda i: (i, 0)
    tm = _tile_rows(n, 4 * CHUNK)
    return pl.pallas_call(
        _chunk_mlp_kernel,
        grid=(n // tm,),
        in_specs=[pl.BlockSpec((tm, A_W), row), pl.BlockSpec((tm, A_W), row),
                  pl.BlockSpec((1, A_W), lambda i: (0, 0)),
                  pl.BlockSpec((A_HEADS, CHUNK, CHUNK), lambda i: (0, 0, 0)),
                  pl.BlockSpec((CHUNK, A_W), lambda i: (0, 0))],
        out_specs=[pl.BlockSpec((tm, A_W), row), pl.BlockSpec((tm, A_W), row)],
        out_shape=[jax.ShapeDtypeStruct((n, A_W), F32)] * 2,
        compiler_params=_cparams(("parallel",)),
        name="chunk_mlp",
    )(u, v, gain, ws, bias_tile)


def _win_attn_kernel(q_ref, kp_ref, kc_ref, vp_ref, vc_ref, o_ref, lse_ref, *, seg_blocks):
    s = pl.program_id(0)
    rows = lax.broadcasted_iota(jnp.int32, (WBLK, 2 * WBLK), 0)
    cols = lax.broadcasted_iota(jnp.int32, (WBLK, 2 * WBLK), 1)
    dist = rows + WBLK - cols
    band = (dist >= 0) & (dist <= WBLK)
    lo = jnp.where((2 * s) % seg_blocks == 0, WBLK, 0)
    biases = [jnp.where(band & (cols >= lo), 0.0, NEG), jnp.where(band, 0.0, NEG)]
    lane = lax.broadcasted_iota(jnp.int32, (WBLK, LANES), 1)
    k3 = jnp.concatenate([kp_ref[...], kc_ref[...]], axis=0).astype(BF16)
    v3 = jnp.concatenate([vp_ref[...], vc_ref[...]], axis=0).astype(BF16)
    heads = range(B_HEADS)
    sls = [slice(h // 2 * LANES, (h // 2 + 1) * LANES) for h in heads]
    hms = [(lane < HEAD_DIM) if h % 2 == 0 else (lane >= HEAD_DIM) for h in heads]
    work = [(j, h) for j in range(2) for h in heads]
    qs = [q_ref[j * WBLK:(j + 1) * WBLK, :] for j in range(2)]
    ks = [k3[j * WBLK:(j + 2) * WBLK, :] for j in range(2)]
    vs = [v3[j * WBLK:(j + 2) * WBLK, :] for j in range(2)]
    scs = [_dot_nt(jnp.where(hms[h], qs[j][:, sls[h]], 0.0).astype(BF16), ks[j][:, sls[h]]) + biases[j]
           for j, h in work]
    ms = [jnp.max(sc, axis=-1, keepdims=True) for sc in scs]
    es = [jnp.exp(sc - m) for sc, m in zip(scs, ms)]
    dens = [jnp.sum(e, axis=-1, keepdims=True) for e in es]
    os_ = [_dot((e * (1.0 / d)).astype(BF16), vs[j][:, sls[h]]) for (j, h), e, d in zip(work, es, dens)]
    for j in range(2):
        base = j * B_HEADS
        lse_tile = jnp.zeros((WBLK, LANES), F32)
        for h in heads:
            lse_tile = jnp.where(lane == h, ms[base + h] + jnp.log(dens[base + h]), lse_tile)
        lse_ref[j * WBLK:(j + 1) * WBLK, :] = lse_tile
        for hp in range(B_W // LANES):
            o_ref[j * WBLK:(j + 1) * WBLK, sls[2 * hp]] = jnp.where(hms[2 * hp], os_[base + 2 * hp],
                                                                  os_[base + 2 * hp + 1])


def _win_attn(qd, kd, vd, seg_blocks):
    n = qd.shape[0]
    assert seg_blocks % 2 == 0
    cur = lambda s: (s, 0)
    prev = lambda s: (jnp.maximum(2 * s - 1, 0), 0)
    blk = lambda w, im: pl.BlockSpec((2 * WBLK, w), im)
    pblk = pl.BlockSpec((WBLK, B_W), prev)
    return pl.pallas_call(
        functools.partial(_win_attn_kernel, seg_blocks=seg_blocks),
        grid=(n // (2 * WBLK),),
        in_specs=[blk(B_W, cur), pblk, blk(B_W, cur), pblk, blk(B_W, cur)],
        out_specs=[blk(B_W, cur), blk(LANES, cur)],
        out_shape=[jax.ShapeDtypeStruct((n, B_W), F32), jax.ShapeDtypeStruct((n, LANES), F32)],
        compiler_params=_cparams(("parallel",)),
        name="win_attn",
    )(qd, kd, kd, vd, vd)


def _window_rows_kernel(k_ref, v_ref, kprev_ref, vprev_ref, ko_ref, vo_ref):
    del kprev_ref, vprev_ref
    for b in range(k_ref.shape[0]):
        ko_ref[b] = k_ref[b].T
        vo_ref[b] = v_ref[b].T


def _window_rows(k, v, k_all, v_all, layer, batch, t, keep):
    first = (t - keep) // WBLK
    src = pl.BlockSpec((batch, WBLK, B_W), lambda i: (0, first + i, 0))
    dst = pl.BlockSpec((None, batch, B_W, WBLK), lambda i: (layer, 0, 0, i))
    hbm = pl.BlockSpec(memory_space=pl.ANY)
    shape = jax.ShapeDtypeStruct(k_all.shape, F32)
    return pl.pallas_call(
        _window_rows_kernel,
        grid=(keep // WBLK,),
        in_specs=[src, src, hbm, hbm],
        out_specs=[dst, dst],
        out_shape=[shape, shape],
        input_output_aliases={2: 0, 3: 1},
        compiler_params=_cparams(("parallel",)),
        name="window_rows",
    )(k.reshape(batch, t, B_W), v.reshape(batch, t, B_W), k_all, v_all)


def _dec_attn_kernel(q_ref, kc_ref, vc_ref, kn_ref, vn_ref, o_ref, *, t_new, cache_len):
    rows_c = lax.broadcasted_iota(jnp.int32, (SUBLANES, cache_len), 0)
    cols_c = lax.broadcasted_iota(jnp.int32, (SUBLANES, cache_len), 1)
    dist_c = cache_len + rows_c % t_new - cols_c
    rows_n = lax.broadcasted_iota(jnp.int32, (SUBLANES, SUBLANES), 0)
    cols_n = lax.broadcasted_iota(jnp.int32, (SUBLANES, SUBLANES), 1)
    dist_n = rows_n % t_new - cols_n
    biases = []
    for window, dil in PATTERNS:
        vc_ok = (dist_c <= window) & ((dist_c & (dil - 1)) == 0)
        vn_ok = (dist_n >= 0) & ((dist_n & (dil - 1)) == 0)
        biases.append((jnp.where(vc_ok, 0.0, NEG), jnp.where(vn_ok, 0.0, NEG)))
    row8 = lax.broadcasted_iota(jnp.int32, (SUBLANES, LANES), 0)
    lane8 = lax.broadcasted_iota(jnp.int32, (SUBLANES, LANES), 1)
    own = (lane8 < HEAD_DIM) == (row8 < t_new)
    q = q_ref[...]
    outs = []
    for hp in range(B_W // LANES):
        sl = slice(hp * LANES, (hp + 1) * LANES)
        qq = jnp.concatenate([q[0:t_new, sl], q[0:t_new, sl]], axis=0)
        q8 = jnp.where(own, qq, 0.0).astype(BF16)
        kc, vc = kc_ref[sl, :].astype(BF16), vc_ref[sl, :].astype(BF16)
        kn, vn = kn_ref[:, sl].astype(BF16), vn_ref[:, sl].astype(BF16)
        sc_c = _dot(q8, kc)
        sc_n = _dot_nt(q8, kn)
        os_, lses = [], []
        for bc, bn in biases:
            a_c, a_n = sc_c + bc, sc_n + bn
            m = jnp.maximum(jnp.max(a_c, axis=-1, keepdims=True), jnp.max(a_n, axis=-1, keepdims=True))
            e_c, e_n = jnp.exp(a_c - m), jnp.exp(a_n - m)
            den = jnp.sum(e_c, axis=-1, keepdims=True) + jnp.sum(e_n, axis=-1, keepdims=True)
            inv = 1.0 / den
            os_.append(_dot_nt((e_c * inv).astype(BF16), vc) + _dot((e_n * inv).astype(BF16), vn))
            lses.append(m + jnp.log(den))
        mx = jnp.maximum(jnp.maximum(lses[0], lses[1]), lses[2])
        ws = [jnp.exp(l - mx) for l in lses]
        tot = ws[0] + ws[1] + ws[2]
        o8 = (ws[0] / tot) * os_[0] + (ws[1] / tot) * os_[1] + (ws[2] / tot) * os_[2]
        lane4 = lane8[0:t_new]
        outs.append(jnp.where(lane4 < HEAD_DIM, o8[0:t_new], o8[t_new:2 * t_new]))
    o_ref[...] = jnp.zeros((SUBLANES, B_W), F32)
    o_ref[0:t_new, :] = jnp.concatenate(outs, axis=-1)


def _dec_attn(q8, kc, vc, kn8, vn8, t_new, layer):
    _, b, _, cache_len = kc.shape
    assert 2 * t_new == SUBLANES
    small = pl.BlockSpec((None, SUBLANES, B_W), lambda i: (i, 0, 0))
    big = pl.BlockSpec((None, None, B_W, cache_len), lambda i: (layer, i, 0, 0))
    return pl.pallas_call(
        functools.partial(_dec_attn_kernel, t_new=t_new, cache_len=cache_len),
        grid=(b,),
        in_specs=[small, big, big, small, small],
        out_specs=small,
        out_shape=jax.ShapeDtypeStruct((b, SUBLANES, B_W), F32),
        compiler_params=_cparams(("parallel",)),
        name="dec_attn",
    )(q8, kc, vc, kn8, vn8)


def _split3(x):
    hi = x.astype(BF16)
    r1 = x - hi.astype(F32)
    mid = r1.astype(BF16)
    return hi, mid, (r1 - mid.astype(F32)).astype(BF16)


def _pair_dup(xx):
    lane = lax.broadcasted_iota(jnp.int32, xx.shape, 1)
    hi = xx.astype(BF16).astype(F32)
    return jnp.where(lane < HEAD_DIM, xx, xx - hi).astype(BF16)


def _pair(x):
    return _pair_dup(jnp.concatenate([x, x], axis=1))


def _lhs4(pair):
    return jnp.concatenate([pair, pair], axis=1)


def _rhs4(y):
    hi = y.astype(BF16)
    lo = (y - hi.astype(F32)).astype(BF16)
    return jnp.concatenate([hi, hi, lo, lo], axis=0)


def _unit_lower_solves(mats, rhss):
    n = DELTA_CHUNK
    w2 = 2 * HEAD_DIM
    rows = lax.broadcasted_iota(jnp.int32, (n, w2), 0)
    cols = lax.broadcasted_iota(jnp.int32, (n, w2), 1) % HEAD_DIM
    in16 = rows // 16 == cols // 16
    eye = jnp.where(rows == cols, 1.0, 0.0)
    ds = [jnp.where(in16, a, 0.0) for a in mats]
    es = [jnp.where(in16, 0.0, a) for a in mats]
    left = lambda xx: _lhs4(_pair_dup(xx))
    pw = [_dot(left(d), _rhs4(d)) for d in ds]
    ts = [eye - d for d in ds]
    for _ in range(2):
        outs = [_dot(jnp.concatenate([left(t), left(p)], axis=0), _rhs4(p)) for t, p in zip(ts, pw)]
        ts = [t + o[0:n] for t, o in zip(ts, outs)]
        pw = [o[n:2 * n] for o in outs]
    ts = [t + _dot(left(t), _rhs4(p)) for t, p in zip(ts, pw)]
    o5 = [_dot(left(t), _rhs4(jnp.concatenate([r, e], axis=1))) for t, r, e in zip(ts, rhss, es)]
    x0 = [o[:, 0:w2] for o in o5]
    nm = [o[:, w2:2 * w2] for o in o5]
    o6 = [_dot(left(m), _rhs4(jnp.concatenate([x, m], axis=1))) for m, x in zip(nm, x0)]
    ys = [x - o[:, 0:w2] for x, o in zip(x0, o6)]
    return [y + _dot(left(o[:, w2:2 * w2]), _rhs4(y)) for y, o in zip(ys, o6)]


def _gdn_prep_kernel(x_ref, cba_ref, buf_ref, cw_ref, alog_ref, dtb_ref, gmat_ref, xmat_ref,
                     u_ref, wq_ref, ak_ref, gt_ref,
                     xp_scr, q_scr, k_scr, v_scr, gb_scr, *, tm, valid_len, group):
    ti = pl.program_id(1)

    @pl.when(ti == 0)
    def _():
        xp_scr[0:SUBLANES, :] = buf_ref[...]

    x = x_ref[...]
    xp_scr[SUBLANES:SUBLANES + tm, :] = x
    off = SUBLANES - (CONV_W - 1)
    acc = xp_scr[off:off + tm, :] * cw_ref[0:1, :]
    for j in range(1, CONV_W):
        acc = acc + xp_scr[off + j:off + j + tm, :] * cw_ref[j:j + 1, :]
    xp_scr[0:SUBLANES, :] = x[tm - SUBLANES:tm, :]
    y = _silu(acc)
    q, k = y[:, 0:C_W], y[:, C_W:2 * C_W]
    gmat = gmat_ref[...]
    head_sum = lambda a: sum(_dot(p, gmat) for p in _split3(a))
    q_scr[...] = q * lax.rsqrt(head_sum(q * q) + EPS) * (HEAD_DIM ** -0.5)
    k_scr[...] = k * lax.rsqrt(head_sum(k * k) + EPS)
    v_scr[...] = y[:, 2 * C_W:3 * C_W]

    cba = cba_ref[...]
    lane = lax.broadcasted_iota(jnp.int32, (tm, LANES), 1)
    tpos = ti * tm + lax.broadcasted_iota(jnp.int32, (tm, LANES), 0)
    live = tpos < valid_len
    beta = _sigmoid(cba)
    z = cba + dtb_ref[...]
    softplus = jnp.maximum(z, 0.0) + jnp.log(1.0 + jnp.exp(-jnp.abs(z)))
    g = -jnp.exp(alog_ref[...]) * softplus
    is_g = (lane >= C_HEADS) & (lane < 2 * C_HEADS)
    gb_scr[...] = jnp.where(live, jnp.where(is_g, g, jnp.where(lane < C_HEADS, beta, 0.0)), 0.0)

    n = DELTA_CHUNK
    rows = lax.broadcasted_iota(jnp.int32, (n, 2 * n), 0)
    cols = lax.broadcasted_iota(jnp.int32, (n, 2 * n), 1) % n
    incl = rows >= cols
    strict = rows > cols
    ltri = jnp.where(incl[:, 0:n], 1.0, 0.0).astype(BF16)
    heads = range(C_HEADS)
    tiles = range(C_W // LANES)
    tsl = [slice(t * LANES, (t + 1) * LANES) for t in tiles]
    lane_c = lax.broadcasted_iota(jnp.int32, (n, LANES), 1)
    low = lane_c < HEAD_DIM
    own = [low if h % 2 == 0 else jnp.logical_not(low) for h in heads]
    xmat = xmat_ref[...]
    wide0 = C_HEADS * LANES

    def head_pairs(tile):
        rolled = pltpu.roll(tile, HEAD_DIM, 1)
        lo = rolled - rolled.astype(BF16).astype(F32)
        lane = lax.broadcasted_iota(jnp.int32, tile.shape, 1) < HEAD_DIM
        return jnp.where(lane, tile, lo).astype(BF16), jnp.where(lane, lo, tile).astype(BF16)

    def setup(c):
        rs = pl.ds(pl.multiple_of(c * n, n), n)
        gb = gb_scr[rs, :]
        g_only = jnp.where((lane_c >= C_HEADS) & (lane_c < 2 * C_HEADS), gb, 0.0)
        cg = sum(_dot(ltri, p) for p in _split3(g_only))
        cgt = cg.T
        cgl = cg[n - 1:n, :]
        gt_ref[pl.ds(pl.multiple_of(c * SUBLANES, SUBLANES), SUBLANES), :] = jnp.broadcast_to(
            jnp.exp(cgl), (SUBLANES, LANES))
        wide = sum(_dot(p, xmat) for p in _split3(jnp.where(lane_c < C_HEADS, gb, cg)))
        b128 = [wide[:, h * LANES:(h + 1) * LANES] for h in heads]
        c128 = [wide[:, wide0 + h * LANES:wide0 + (h + 1) * LANES] for h in heads]
        cgr = [jnp.concatenate([cgt[C_HEADS + h:C_HEADS + h + 1, :]] * 2, axis=1) for h in heads]
        decay = [jnp.where(incl, jnp.exp(jnp.minimum(c128[h] - cgr[h], 0.0)), 0.0) for h in heads]
        amat, at_pair, rhs, qd_pair, kdt_pair = [], [], [], [], []
        for t in tiles:
            he, ho = 2 * t, 2 * t + 1
            qn, kn, vv = q_scr[rs, tsl[t]], k_scr[rs, tsl[t]], v_scr[rs, tsl[t]]
            bnat = jnp.where(low, b128[he], b128[ho])
            cnat = jnp.where(low, c128[he], c128[ho])
            ecg = jnp.exp(cnat)
            kb = kn.astype(BF16)
            kk2 = jnp.concatenate([kb, kb], axis=0)
            for h in (he, ho):
                masked = jnp.concatenate([jnp.where(own[h], kn, 0.0), jnp.where(own[h], qn, 0.0)], axis=0)
                kkqk = _dot_nt(masked.astype(BF16), kk2)
                amat.append(jnp.where(strict, b128[h] * kkqk[0:n] * decay[h], 0.0))
                at_pair.append(_pair_dup(kkqk[n:2 * n] * decay[h]))
            bv = bnat * vv
            bk_rolled = pltpu.roll((bnat * ecg) * kn, HEAD_DIM, 1)
            rhs += [jnp.where(low, bv, bk_rolled), jnp.where(low, bk_rolled, bv)]
            qd_pair += list(head_pairs(qn * ecg))
            kd = kn * jnp.exp(cnat[n - 1:n, :] - cnat)
            kdt = jnp.concatenate([kd, kd], axis=0).T
            kdt_pair += [_pair_dup(kdt[0:n]), _pair_dup(kdt[n:2 * n])]
        return amat, rhs, at_pair, qd_pair, kdt_pair

    def finish(c, sol, at_pair, qd_pair, kdt_pair):
        w_pair = []
        for t in tiles:
            se, so = sol[2 * t], sol[2 * t + 1]
            u_ref[pl.ds(pl.multiple_of(c * n, n), n), tsl[t]] = jnp.where(low, se, so)
            w_odd, w_even = head_pairs(jnp.where(low, so, se))
            w_pair += [w_even, w_odd]
        r2 = pl.ds(pl.multiple_of(c * 2 * n, 2 * n), 2 * n)
        wq_ref[r2, :] = jnp.concatenate(
            [jnp.concatenate([w_pair[h], qd_pair[h]], axis=0) for h in heads], axis=1)
        ak_ref[r2, :] = jnp.concatenate(
            [jnp.concatenate([at_pair[h], kdt_pair[h]], axis=0) for h in heads], axis=1)

    def chunks(i, carry):
        parts = [setup(i * group + j) for j in range(group)]
        sol = _unit_lower_solves([a for p in parts for a in p[0]], [r for p in parts for r in p[1]])
        for j, p in enumerate(parts):
            finish(i * group + j, sol[j * C_HEADS:(j + 1) * C_HEADS], *p[2:])
        return carry

    lax.fori_loop(0, tm // n // group, chunks, 0)


def _gdn_constants():
    hid = jnp.arange(C_W, dtype=jnp.int32) // HEAD_DIM
    gmat = (hid[:, None] == hid[None, :]).astype(BF16)
    src = jnp.arange(LANES, dtype=jnp.int32)[:, None]
    dst = jnp.arange(2 * C_HEADS * LANES, dtype=jnp.int32)[None, :] // LANES
    return gmat, (src == dst).astype(BF16)


def _gdn_prep(cqkv, cba, buf8, conv_w, alog_row, dtb_row, consts, batch, t, tm, valid_len):
    gmat, xmat = consts
    n = batch * t
    nt = t // tm
    row = lambda b, i: (b * nt + i, 0)
    fixed = lambda b, i: (0, 0)
    nch = tm // DELTA_CHUNK
    pair_w = C_HEADS * 2 * HEAD_DIM
    outs = [jax.ShapeDtypeStruct((n, C_W), F32), jax.ShapeDtypeStruct((2 * n, pair_w), BF16),
            jax.ShapeDtypeStruct((2 * n, pair_w), BF16),
            jax.ShapeDtypeStruct((n // DELTA_CHUNK * SUBLANES, LANES), F32)]
    return pl.pallas_call(
        functools.partial(_gdn_prep_kernel, tm=tm, valid_len=valid_len, group=min(GDN_GROUP, nch)),
        grid=(batch, nt),
        in_specs=[pl.BlockSpec((tm, 3 * C_W), row), pl.BlockSpec((tm, LANES), row),
                  pl.BlockSpec((None, SUBLANES, 3 * C_W), lambda b, i: (b, 0, 0)),
                  pl.BlockSpec((CONV_W, 3 * C_W), fixed), pl.BlockSpec((1, LANES), fixed),
                  pl.BlockSpec((1, LANES), fixed), pl.BlockSpec((C_W, C_W), fixed),
                  pl.BlockSpec(xmat.shape, fixed)],
        out_specs=[pl.BlockSpec((tm, C_W), row), pl.BlockSpec((2 * tm, pair_w), row),
                   pl.BlockSpec((2 * tm, pair_w), row), pl.BlockSpec((nch * SUBLANES, LANES), row)],
        out_shape=outs,
        scratch_shapes=[pltpu.VMEM((tm + SUBLANES, 3 * C_W), F32), pltpu.VMEM((tm, C_W), F32),
                        pltpu.VMEM((tm, C_W), F32), pltpu.VMEM((tm, C_W), F32), pltpu.VMEM((tm, LANES), F32)],
        compiler_params=_cparams(("parallel", "arbitrary")),
        name="gdn_prep",
    )(cqkv, cba, buf8, conv_w, alog_row, dtb_row, gmat, xmat)


def _gdn_scan_kernel(u_ref, wq_ref, ak_ref, gt_ref, z_ref, s0_ref, gn_ref,
                     o_ref, sfin_ref, s_scr, *, bg):
    c = pl.program_id(1)

    @pl.when(c == 0)
    def _():
        s_scr[...] = s0_ref[...]

    gn = gn_ref[...]
    n = DELTA_CHUNK
    chains = [(b, h, slice(h * HEAD_DIM, (h + 1) * HEAD_DIM), slice(h * 2 * HEAD_DIM, (h + 1) * 2 * HEAD_DIM))
              for b in range(bg) for h in range(C_HEADS)]
    st = [s_scr[b, h] for b, h, _, _ in chains]
    r1 = [_dot(_lhs4(wq_ref[b, :, ps]), _rhs4(s)) for (b, _, _, ps), s in zip(chains, st)]
    up = [u_ref[b, :, sl] - r[0:n] for (b, _, sl, _), r in zip(chains, r1)]
    r2 = [_dot(_lhs4(ak_ref[b, :, ps]), _rhs4(x)) for (b, _, _, ps), x in zip(chains, up)]
    for (b, h, _, _), s, r in zip(chains, st, r2):
        s_scr[b, h] = gt_ref[b, 0:1, C_HEADS + h:C_HEADS + h + 1] * s + r[n:2 * n]
    os_ = [a[n:2 * n] + r[0:n] for a, r in zip(r1, r2)]
    outs = [o * lax.rsqrt(jnp.mean(o * o, axis=-1, keepdims=True) + EPS) * gn * _silu(z_ref[b, :, sl])
            for (b, _, sl, _), o in zip(chains, os_)]
    for b in range(bg):
        o_ref[b] = jnp.concatenate(outs[b * C_HEADS:(b + 1) * C_HEADS], axis=1)

    @pl.when(c == pl.num_programs(1) - 1)
    def _():
        sfin_ref[...] = s_scr[...]


def _gdn_scan(u, wq, ak, gt, z, s0, gnorm, batch, t, bg):
    n = DELTA_CHUNK
    nc = t // n
    pair_w = C_HEADS * 2 * HEAD_DIM
    v3 = lambda a: a.reshape(batch, t, C_W)
    p3 = lambda a: a.reshape(batch, 2 * t, pair_w)
    tok = pl.BlockSpec((bg, n, C_W), lambda b, c: (b, c, 0))
    pair = pl.BlockSpec((bg, 2 * n, pair_w), lambda b, c: (b, c, 0))
    st = pl.BlockSpec((bg, C_HEADS, HEAD_DIM, HEAD_DIM), lambda b, c: (b, 0, 0, 0))
    return pl.pallas_call(
        functools.partial(_gdn_scan_kernel, bg=bg),
        grid=(batch // bg, nc),
        in_specs=[tok, pair, pair,
                  pl.BlockSpec((bg, SUBLANES, LANES), lambda b, c: (b, c, 0)),
                  tok, st, pl.BlockSpec((1, HEAD_DIM), lambda b, c: (0, 0))],
        out_specs=[tok, st],
        out_shape=[jax.ShapeDtypeStruct((batch, t, C_W), F32),
                   jax.ShapeDtypeStruct((batch, C_HEADS, HEAD_DIM, HEAD_DIM), F32)],
        scratch_shapes=[pltpu.VMEM((bg, C_HEADS, HEAD_DIM, HEAD_DIM), F32)],
        compiler_params=_cparams(("parallel", "arbitrary")),
        name="gdn_scan",
    )(v3(u), p3(wq), p3(ak), gt.reshape(batch, nc * SUBLANES, LANES), v3(z), s0, gnorm)


def _head_expand(wt):
    tm = wt.shape[0]
    return jnp.concatenate([jnp.broadcast_to(wt[:, h:h + 1], (tm, HEAD_DIM)) for h in range(B_HEADS)], axis=-1)


def _out_proj_kernel(*refs, dils):
    if dils:
        npat = len(dils)
        a_ref = refs[0]
        o_refs, l_refs = refs[1:1 + npat], refs[1 + npat:1 + 2 * npat]
        c_ref, x_ref, w_ref, g2_ref, wr_ref, y_ref, dense_ref, o_scr, l_scr = refs[1 + 2 * npat:]
        tm = x_ref.shape[0]
        os_, ls = [], []
        for i, d in enumerate(dils):
            if d == 1:
                os_.append(o_refs[i][0])
                ls.append(l_refs[i][0])
            else:
                nblk = B_W // LANES
                for j in range(d):
                    rows = pl.ds(j, tm // d, stride=d)
                    for b in range(nblk):
                        o_scr[i * nblk + b, rows, :] = o_refs[i][j, :, LANES * b:LANES * (b + 1)]
                    l_scr[i, rows, :] = l_refs[i][j]
                os_.append(jnp.concatenate([o_scr[i * nblk + b] for b in range(nblk)], axis=-1))
                ls.append(l_scr[i])
        mx = functools.reduce(jnp.maximum, ls)
        es = [jnp.exp(l - mx) for l in ls]
        tot = functools.reduce(lambda a, b: a + b, es)
        ob = functools.reduce(lambda a, b: a + b, [_head_expand(e / tot) * o for e, o in zip(es, os_)])
    else:
        a_ref, b_ref, c_ref, x_ref, w_ref, g2_ref, wr_ref, y_ref, dense_ref = refs
        ob = b_ref[...]
    cat = jnp.concatenate([a_ref[...], ob, c_ref[...]], axis=-1).astype(BF16)
    y = x_ref[...] + _dot(cat, w_ref[...])
    y_ref[...] = y
    dense_ref[...] = _route_tile(y, g2_ref[...], wr_ref[...])


def _out_proj(out_a, out_b, lses, out_c, x, w_out, g2, w_route, tm, seq_len=None, dils=()):
    n = x.shape[0]
    row = lambda i: (i, 0)
    spec = lambda w: pl.BlockSpec((tm, w), row)
    scratch = []
    if dils:
        tps = seq_len // tm
        strided = lambda d, w: pl.BlockSpec((None, d, tm // d, w), lambda i: (i // tps, 0, i % tps, 0))
        ins = [out_a, *out_b, *lses, out_c, x, w_out]
        specs = ([spec(A_W)] + [strided(d, B_W) for d in dils] + [strided(d, LANES) for d in dils]
                 + [spec(C_W), spec(D_MODEL)])
        scratch = [pltpu.VMEM((len(dils) * B_W // LANES, tm, LANES), F32), pltpu.VMEM((len(dils), tm, LANES), F32)]
    else:
        ins = [out_a, out_b, out_c, x, w_out]
        specs = [spec(A_W), spec(B_W), spec(C_W), spec(D_MODEL)]
    fixed = lambda i: (0, 0)
    specs += [pl.BlockSpec((D_MODEL, D_MODEL), fixed), pl.BlockSpec((1, D_MODEL), fixed),
              pl.BlockSpec((D_MODEL, LANES), fixed)]
    return pl.pallas_call(
        functools.partial(_out_proj_kernel, dils=dils),
        grid=(n // tm,),
        in_specs=specs,
        out_specs=[spec(D_MODEL), spec(LANES)],
        out_shape=[jax.ShapeDtypeStruct((n, D_MODEL), F32), jax.ShapeDtypeStruct((n, LANES), F32)],
        scratch_shapes=scratch,
        compiler_params=_cparams(("parallel",)),
        name="out_proj",
    )(*ins, g2, w_route)


def _route_tile(x, g, w):
    t = x * lax.rsqrt(jnp.mean(x * x, axis=-1, keepdims=True) + EPS) * g
    lg = _dot(t, w, HI)
    tm = lg.shape[0]
    lane = lax.broadcasted_iota(jnp.int32, (tm, LANES), 1).astype(F32)
    big = float(LANES)
    is_grp = lane < N_GROUPS
    gl = jnp.where(is_grp, lg, NEG)
    gmax = jnp.max(gl, axis=-1, keepdims=True)
    gsum = jnp.sum(jnp.where(is_grp, jnp.exp(gl - gmax), 0.0), axis=-1, keepdims=True)
    g_w = 1.0 / gsum
    g_idx = jnp.min(jnp.where(is_grp & (gl == gmax), lane, big), axis=-1, keepdims=True)
    lo = ROUTE_OFF + EXP_PER_GROUP * g_idx
    sel = (lane >= lo) & (lane < lo + EXP_PER_GROUP)
    el = jnp.where(sel, lg, NEG)
    m1 = jnp.max(el, axis=-1, keepdims=True)
    esum = jnp.sum(jnp.where(sel, jnp.exp(el - m1), 0.0), axis=-1, keepdims=True)
    i1 = jnp.min(jnp.where(sel & (el == m1), lane, big), axis=-1, keepdims=True)
    el2 = jnp.where(lane == i1, NEG, el)
    m2 = jnp.max(el2, axis=-1, keepdims=True)
    i2 = jnp.min(jnp.where(sel & (lane != i1) & (el2 == m2), lane, big), axis=-1, keepdims=True)
    p1 = 1.0 / esum
    p2 = jnp.exp(m2 - m1) / esum
    tot = p1 + p2
    gates = jnp.where(lane == i1, g_w * (p1 / tot), 0.0) + jnp.where(lane == i2, g_w * (p2 / tot), 0.0)
    return jnp.where(lane == 0.0, g_idx, gates)


def _moe_kernel(x_ref, g_ref, dense_ref, w1_ref, w3_ref, w2_ref, y_ref, t_scr):
    e = pl.program_id(1)

    @pl.when(e == 0)
    def _():
        x = x_ref[...]
        t_scr[...] = (x * lax.rsqrt(jnp.mean(x * x, axis=-1, keepdims=True) + EPS) * g_ref[...]).astype(BF16)
        y_ref[...] = x

    tb = t_scr[...]
    dense = dense_ref[...]
    lane = lax.broadcasted_iota(jnp.int32, dense.shape, 1)
    gate = jnp.sum(jnp.where(lane == e + ROUTE_OFF, dense, 0.0), axis=-1, keepdims=True)
    hid = _silu(_dot(tb, w1_ref[...].astype(BF16))) * _dot(tb, w3_ref[...].astype(BF16))
    y_ref[...] += _dot((hid * gate).astype(BF16), w2_ref[...].astype(BF16))


def _moe(x, g, dense, w1, w3, w2, layer, tm):
    n = x.shape[0]
    row = lambda i, e: (i, 0)
    expert = lambda i, e: (layer, e, 0, 0)
    return pl.pallas_call(
        _moe_kernel,
        grid=(n // tm, N_EXPERTS),
        in_specs=[pl.BlockSpec((tm, D_MODEL), row), pl.BlockSpec((1, D_MODEL), lambda i, e: (0, 0)),
                  pl.BlockSpec((tm, LANES), row),
                  pl.BlockSpec((None, None, D_MODEL, D_EXPERT), expert),
                  pl.BlockSpec((None, None, D_MODEL, D_EXPERT), expert),
                  pl.BlockSpec((None, None, D_EXPERT, D_MODEL), expert)],
        out_specs=pl.BlockSpec((tm, D_MODEL), row),
        out_shape=jax.ShapeDtypeStruct((n, D_MODEL), F32),
        scratch_shapes=[pltpu.VMEM((tm, D_MODEL), BF16)],
        compiler_params=_cparams(("parallel", "arbitrary")),
        name="moe",
    )(x, g, dense, w1, w3, w2)


def _moe_grouped_kernel(x_ref, g_ref, dense_ref, gidr_ref, tri_ref, w1_ref, w3_ref, w2_ref, y_ref,
                        ts_scr, gs_scr, ys_scr, pt_scr, meta, *, tm, slots):
    e = pl.program_id(1)
    blk = MOE_BLOCK

    @pl.when(e == 0)
    def _():
        x = x_ref[...]
        t = (x * lax.rsqrt(jnp.mean(x * x, axis=-1, keepdims=True) + EPS) * g_ref[...]).astype(BF16)
        dense = dense_ref[...]
        tri = tri_ref[...]
        lane = lax.broadcasted_iota(jnp.int32, (tm, LANES), 1)
        ohc = jnp.where(lane.astype(F32) == dense[:, 0:1], 1.0, 0.0)
        rankc = _dot(tri, ohc.astype(BF16))
        sub = lax.broadcasted_iota(jnp.int32, (SUBLANES, tm), 0)
        ohr = jnp.where(sub.astype(F32) == gidr_ref[...], 1.0, 0.0)
        rankr = _dot_nt(ohr.astype(BF16), tri)
        start = jnp.int32(0)
        s_lane = jnp.zeros((tm, LANES), F32)
        s_sub = jnp.zeros((SUBLANES, tm), F32)
        for g in range(N_GROUPS):
            count = jnp.sum(jnp.where(lane == g, ohc, 0.0)).astype(jnp.int32)
            nblk = (count + blk - 1) // blk
            meta[g] = start
            meta[N_GROUPS + g] = nblk
            s_lane = jnp.where(lane == g, start.astype(F32), s_lane)
            s_sub = jnp.where(sub == g, start.astype(F32), s_sub)
            start = start + nblk * blk
        destc = jnp.sum(ohc * (s_lane + rankc), axis=-1, keepdims=True)
        destr = jnp.sum(ohr * (s_sub + rankr), axis=0, keepdims=True)
        slot_r = lax.broadcasted_iota(jnp.int32, (slots, tm), 0).astype(F32)
        p = jnp.where(slot_r == destr, 1.0, 0.0).astype(BF16)
        ts_scr[...] = _dot(p, t).astype(BF16)
        gs_scr[...] = sum(_dot(p, piece) for piece in _split3(dense))
        slot_c = lax.broadcasted_iota(jnp.int32, (tm, slots), 1).astype(F32)
        pt_scr[...] = jnp.where(slot_c == destc, 1.0, 0.0).astype(BF16)
        ys_scr[...] = jnp.zeros((slots, D_MODEL), F32)

    g = e // EXP_PER_GROUP
    start = meta[g]
    nblk = meta[N_GROUPS + g]
    w1, w3, w2 = w1_ref[...], w3_ref[...], w2_ref[...]

    def rows_update(r0, m):
        rows = pl.ds(pl.multiple_of(r0, blk), m)
        tb = ts_scr[rows, :]
        lane = lax.broadcasted_iota(jnp.int32, (m, LANES), 1)
        gate = jnp.sum(jnp.where(lane == e + ROUTE_OFF, gs_scr[rows, :], 0.0), axis=-1, keepdims=True)
        hid = _silu(_dot(tb, w1)) * _dot(tb, w3)
        ys_scr[rows, :] += _dot((hid * gate).astype(BF16), w2)

    def pair(j, carry):
        rows_update(start + j * 2 * blk, 2 * blk)
        return carry

    lax.fori_loop(0, nblk // 2, pair, 0)

    @pl.when(nblk % 2 == 1)
    def _():
        rows_update(start + (nblk - 1) * blk, blk)

    @pl.when(e == pl.num_programs(1) - 1)
    def _():
        pt = pt_scr[...]
        ys = ys_scr[...]
        hi = ys.astype(BF16)
        lo = (ys - hi.astype(F32)).astype(BF16)
        y_ref[...] = x_ref[...] + (_dot(pt, hi) + _dot(pt, lo))


def _moe_grouped(x, g, dense, w1, w3, w2, layer, tm):
    n = x.shape[0]
    slots = tm + N_GROUPS * MOE_BLOCK
    row = lambda i, e: (i, 0)
    expert = lambda i, e: (layer, e, 0, 0)
    gid_rows = dense[:, 0].reshape(n // tm, 1, tm)
    idx = jnp.arange(tm, dtype=jnp.int32)
    tri = (idx[None, :] < idx[:, None]).astype(BF16)
    return pl.pallas_call(
        functools.partial(_moe_grouped_kernel, tm=tm, slots=slots),
        grid=(n // tm, N_EXPERTS),
        in_specs=[pl.BlockSpec((tm, D_MODEL), row), pl.BlockSpec((1, D_MODEL), lambda i, e: (0, 0)),
                  pl.BlockSpec((tm, LANES), row),
                  pl.BlockSpec((None, 1, tm), lambda i, e: (i, 0, 0)),
                  pl.BlockSpec((tm, tm), lambda i, e: (0, 0)),
                  pl.BlockSpec((None, None, D_MODEL, D_EXPERT), expert),
                  pl.BlockSpec((None, None, D_MODEL, D_EXPERT), expert),
                  pl.BlockSpec((None, None, D_EXPERT, D_MODEL), expert)],
        out_specs=pl.BlockSpec((tm, D_MODEL), row),
        out_shape=jax.ShapeDtypeStruct((n, D_MODEL), F32),
        scratch_shapes=[pltpu.VMEM((slots, D_MODEL), BF16), pltpu.VMEM((slots, LANES), F32),
                        pltpu.VMEM((slots, D_MODEL), F32), pltpu.VMEM((tm, slots), BF16),
                        pltpu.SMEM((2 * N_GROUPS,), jnp.int32)],
        compiler_params=_cparams(("parallel", "arbitrary")),
        name="moe_grouped",
    )(x, g, dense, gid_rows, tri, w1, w3, w2)


def _final_norm_kernel(x_ref, g_ref, y_ref):
    x = x_ref[...]
    y_ref[...] = x * lax.rsqrt(jnp.mean(x * x, axis=-1, keepdims=True) + EPS) * g_ref[...]


def _final_norm(x, g, tm):
    n = x.shape[0]
    return pl.pallas_call(
        _final_norm_kernel,
        grid=(n // tm,),
        in_specs=[pl.BlockSpec((tm, D_MODEL), lambda i: (i, 0)), pl.BlockSpec((1, D_MODEL), lambda i: (0, 0))],
        out_specs=pl.BlockSpec((tm, D_MODEL), lambda i: (i, 0)),
        out_shape=jax.ShapeDtypeStruct((n, D_MODEL), F32),
        compiler_params=_cparams(("parallel",)),
        name="final_norm",
    )(x, g)


def _tile_rows(n, cap):
    tm = min(n, cap)
    assert n % tm == 0
    return tm


def _layer_weights(l, norm1_g, w_in, a_vnorm_g, a_ws, a_bs, c_conv_w, c_a_log, c_dt_bias, c_norm_g,
                   w_out, norm2_g, w_group, w_router, w1, w3, w2):
    pad_l = lambda a, left: jnp.pad(a, ((0, 0), (left, LANES - left - a.shape[-1])))
    return dict(
        norm1_g=norm1_g[l][None, :],
        w_in_b=w_in.astype(BF16),
        w_small=pad_l(w_in[l, :, PROJ_MAIN:], 0).astype(BF16),
        a_gain=a_vnorm_g[l][None, :],
        a_ws=a_ws[l],
        a_bs=a_bs[l],
        conv_w=c_conv_w[l],
        alog_row=pad_l(c_a_log[l][None, :], C_HEADS),
        dtb_row=pad_l(c_dt_bias[l][None, :], C_HEADS),
        gnorm=c_norm_g[l][None, :],
        w_out=w_out[l].astype(BF16),
        norm2_g=norm2_g[l][None, :],
        w_route=pad_l(jnp.concatenate([w_group[l], w_router[l]], axis=-1), 0),
        w1=w1, w3=w3, w2=w2, layer=l,
    )


def _mixer_c(lw, cqkv, cba, cz, conv_buf, s0, gmat, batch, t, tm, valid_len, bg):
    buf8 = jnp.pad(conv_buf, ((0, 0), (SUBLANES - (CONV_W - 1), 0), (0, 0)))
    u, wq, ak, gt = _gdn_prep(cqkv, cba, buf8, lw["conv_w"], lw["alog_row"], lw["dtb_row"], gmat,
                              batch, t, tm, valid_len)
    return _gdn_scan(u, wq, ak, gt, cz, s0, lw["gnorm"], batch, t, bg)


def _ffn(lw, x, dense):
    n = x.shape[0]
    if n % MOE_TILE == 0:
        return _moe_grouped(x, lw["norm2_g"], dense, lw["w1"].astype(BF16), lw["w3"].astype(BF16),
                            lw["w2"].astype(BF16), lw["layer"], MOE_TILE)
    return _moe(x, lw["norm2_g"], dense, lw["w1"], lw["w3"], lw["w2"], lw["layer"], n)


def _prompt_layer(lw, x, batch, t, tabs, gmat, win_k, win_v, layer):
    n = batch * t
    dils = tuple(d for _, d in PATTERNS)
    tm = _tile_rows(n, 512)
    au, av, q, k, v, cqkv, cz, cba, *strided = _proj(x, lw["norm1_g"], lw["w_in_b"], layer, lw["w_small"], tabs, tm,
                                                     seq_len=t, dils=dils[1:])

    bias_tile = jnp.repeat(lw["a_bs"].T, HEAD_DIM, axis=1)
    out_a, _ = _chunk_mlp(au, av, lw["a_gain"], lw["a_ws"], bias_tile)

    qkv = [(q, k, v)] + [tuple(a.reshape(n, B_W) for a in strided[3 * i:3 * i + 3]) for i in range(len(dils) - 1)]
    outs, lses = [], []
    for d, (qd, kd, vd) in zip(dils, qkv):
        o_d, lse_d = _win_attn(qd, kd, vd, t // d // WBLK)
        outs.append(o_d.reshape(batch, d, t // d, B_W))
        lses.append(lse_d.reshape(batch, d, t // d, LANES))

    zeros_buf = jnp.zeros((batch, CONV_W - 1, 3 * C_W), F32)
    zeros_s = jnp.zeros((batch, C_HEADS, HEAD_DIM, HEAD_DIM), F32)
    out_c, s_new = _mixer_c(lw, cqkv, cba, cz, zeros_buf, zeros_s, gmat, batch, t, 256, t, batch)
    out_c = out_c.reshape(n, C_W)

    x, dense = _out_proj(out_a, outs, lses, out_c, x, lw["w_out"], lw["norm2_g"], lw["w_route"], tm,
                         seq_len=t, dils=dils)
    x = _ffn(lw, x, dense)

    win_k, win_v = _window_rows(k, v, win_k, win_v, layer, batch, t, win_k.shape[-1])
    conv_state = cqkv.reshape(batch, t, 3 * C_W)[:, t - (CONV_W - 1):]
    return x, win_k, win_v, (conv_state, s_new)


def _sample_layer(lw, x, batch, t, tabs, gmat, kbuf, vbuf, conv_buf, s0, layer):
    n = batch * t
    au, av, q, k, v, cqkv, cz, cba = _proj(x, lw["norm1_g"], lw["w_in_b"], layer, lw["w_small"], tabs, n)

    eye = jnp.eye(batch, dtype=F32)
    ws_bd = jnp.stack([jnp.kron(eye, lw["a_ws"][h, :t, :t]) for h in range(A_HEADS)])
    bias_tile = jnp.tile(jnp.repeat(lw["a_bs"][:, :t].T, HEAD_DIM, axis=1), (batch, 1))
    out_a, a_rows = _chunk_mlp(au, av, lw["a_gain"], ws_bd, bias_tile)

    pad8 = lambda a: jnp.pad(a.reshape(batch, t, B_W), ((0, 0), (0, SUBLANES - t), (0, 0)))
    out_b = _dec_attn(pad8(q), kbuf, vbuf, pad8(k), pad8(v), t, layer)[:, :t].reshape(n, B_W)

    tp = DELTA_CHUNK
    padt = lambda a: jnp.pad(a.reshape(batch, t, -1), ((0, 0), (0, tp - t), (0, 0))).reshape(batch * tp, -1)
    out_c, s_new = _mixer_c(lw, padt(cqkv), padt(cba), padt(cz), conv_buf, s0, gmat, batch, tp, tp, t, 4)
    out_c = out_c[:, :t].reshape(n, C_W)

    x, dense = _out_proj(out_a, out_b, None, out_c, x, lw["w_out"], lw["norm2_g"], lw["w_route"], n)
    x = _ffn(lw, x, dense)

    heads = lambda a: a.reshape(batch, t, B_HEADS, HEAD_DIM)
    conv_state = jnp.concatenate([conv_buf, cqkv.reshape(batch, t, 3 * C_W)], axis=1)[:, -(CONV_W - 1):]
    return x, (heads(k), heads(v), a_rows.reshape(batch, t, A_W), conv_state, s_new)


def kernel(x_prompt, x_sample, cache_win_k, cache_win_v, state_conv, state_delta, norm1_g, w_in, a_vnorm_g, a_ws, a_bs, c_conv_w, c_a_log, c_dt_bias, c_norm_g, w_out, norm2_g, w_group, w_router, w1, w3, w2, final_g):
    bp, tp, _ = x_prompt.shape
    bs, ts, _ = x_sample.shape
    depth = w_in.shape[0]
    assert tp % (PATTERNS[-1][1] * WBLK) == 0 and tp % 512 == 0 and bs * ts == CHUNK

    tabs_p = _rope_tables(jnp.arange(tp, dtype=jnp.int32))
    tabs_s = tuple(jnp.tile(a, (bs, 1)) for a in _rope_tables(PAST_LEN + jnp.arange(ts, dtype=jnp.int32)))
    gmat = _gdn_constants()

    feat_major = lambda c: jnp.transpose(c, (0, 1, 3, 4, 2)).reshape(depth, bs, B_W, c.shape[2])
    cache_k, cache_v = feat_major(cache_win_k), feat_major(cache_win_v)
    keep = min(MAX_WINDOW, tp)
    win_k = jnp.zeros((depth, bp, B_W, keep), F32)
    win_v = jnp.zeros((depth, bp, B_W, keep), F32)

    xp = x_prompt.reshape(bp * tp, D_MODEL)
    xs = x_sample.reshape(bs * ts, D_MODEL)
    p_out = [[] for _ in range(2)]
    s_out = [[] for _ in range(5)]
    for l in range(depth):
        lw = _layer_weights(l, norm1_g, w_in, a_vnorm_g, a_ws, a_bs, c_conv_w, c_a_log, c_dt_bias, c_norm_g,
                            w_out, norm2_g, w_group, w_router, w1, w3, w2)
        xp, win_k, win_v, st = _prompt_layer(lw, xp, bp, tp, tabs_p, gmat, win_k, win_v, l)
        for acc, a in zip(p_out, st):
            acc.append(a)
        xs, st = _sample_layer(lw, xs, bs, ts, tabs_s, gmat, cache_k, cache_v, state_conv[l], state_delta[l], l)
        for acc, a in zip(s_out, st):
            acc.append(a)
    fg = final_g[None, :]
    y_prompt = _final_norm(xp, fg, _tile_rows(bp * tp, 512)).reshape(bp, tp, D_MODEL)
    y_sample = _final_norm(xs, fg, bs * ts).reshape(bs, ts, D_MODEL)
    rows_major = lambda w: jnp.transpose(w.reshape(depth, bp, B_HEADS, HEAD_DIM, keep), (0, 1, 4, 2, 3))
    return (y_prompt, y_sample, rows_major(win_k), rows_major(win_v), *[jnp.stack(a) for a in p_out],
            *[jnp.stack(a) for a in s_out])
```

```python
import functools
import math

import jax
import jax.numpy as jnp
from jax import lax
from jax.experimental import pallas as pl
from jax.experimental.pallas import tpu as pltpu

F32 = jnp.float32
BF16 = jnp.bfloat16
HI = lax.Precision.HIGHEST

D_MODEL = 1024
HEAD_DIM = 64
A_HEADS = 4
B_HEADS = 6
C_HEADS = 6
A_W = A_HEADS * HEAD_DIM
B_W = B_HEADS * HEAD_DIM
C_W = C_HEADS * HEAD_DIM
CHUNK = 128
PATTERNS = ((128, 1), (512, 4), (2048, 16))
MAX_WINDOW = 2048
ROT_DIM = HEAD_DIM // 4
ROPE_THETA = 500000.0
CONV_W = 4
DELTA_CHUNK = 64
N_GROUPS = 4
EXP_PER_GROUP = 8
N_EXPERTS = N_GROUPS * EXP_PER_GROUP
D_EXPERT = 256
EPS = 1e-6
PAST_LEN = 16384

LANES = 128
SUBLANES = 8
WBLK = 128
PROJ_MAIN = 2 * A_W + 3 * B_W + 4 * C_W
NEG = -1e30
ROUTE_OFF = N_GROUPS
VMEM_LIMIT = 56 * 1024 * 1024
MOE_TILE = 1024
MOE_BLOCK = 128
GDN_GROUP = 4


def _cparams(sem):
    return pltpu.CompilerParams(dimension_semantics=sem, vmem_limit_bytes=VMEM_LIMIT)


def _sigmoid(x):
    return 1.0 / (1.0 + jnp.exp(-x))


def _silu(x):
    return x * _sigmoid(x)


def _dot(a, b, precision=None):
    return jnp.dot(a, b, preferred_element_type=F32, precision=precision)


def _dot_nt(a, b, precision=None):
    return lax.dot_general(a, b, (((1,), (1,)), ((), ())), preferred_element_type=F32, precision=precision)


def _proj_kernel(x_ref, g_ref, w_ref, wsm_ref, c_ref, s1_ref, s2_ref,
                 au_ref, av_ref, q_ref, k_ref, v_ref, cqkv_ref, cz_ref, cba_ref, *strided_refs, dils):
    x = x_ref[...]
    h = x * lax.rsqrt(jnp.mean(x * x, axis=-1, keepdims=True) + EPS) * g_ref[...]
    hb = h.astype(BF16)

    def seg(a, b):
        return _dot(hb, w_ref[:, a:b])

    au_ref[...] = seg(0, A_W)
    av_ref[...] = seg(A_W, 2 * A_W)
    c, s1, s2 = c_ref[...], s1_ref[...], s2_ref[...]
    q0 = 2 * A_W
    k0 = q0 + B_W
    nblk = B_W // LANES
    stage = strided_refs[-1] if dils else None
    for j in range(nblk):
        cols = slice(LANES * j, LANES * (j + 1))
        for a, (base, ref, scale) in enumerate(((q0, q_ref, HEAD_DIM ** -0.5), (k0, k_ref, None))):
            xc = seg(base + LANES * j, base + LANES * (j + 1))
            r = xc * c + pltpu.roll(xc, ROT_DIM // 2, 1) * s1 + pltpu.roll(xc, LANES - ROT_DIM // 2, 1) * s2
            if scale is not None:
                r = r * scale
            ref[:, cols] = r.astype(ref.dtype)
            if dils:
                stage[a * nblk + j] = r
    v0 = k0 + B_W
    v_ref[...] = seg(v0, v0 + B_W)
    c0 = v0 + B_W
    cqkv_ref[...] = seg(c0, c0 + 3 * C_W)
    cz_ref[...] = seg(c0 + 3 * C_W, c0 + 4 * C_W)
    cba_ref[...] = _dot(hb, wsm_ref[...])
    if dils:
        outs = strided_refs[:-1]
        tm = x.shape[0]
        for b in range(nblk):
            stage[2 * nblk + b] = v_ref[:, LANES * b:LANES * (b + 1)]
        for i, d in enumerate(dils):
            for a in range(3):
                dst = outs[3 * i + a]
                for j in range(d):
                    for b in range(nblk):
                        dst[j, :, LANES * b:LANES * (b + 1)] = stage[a * nblk + b,
                                                                     pl.ds(j, tm // d, stride=d), :].astype(dst.dtype)


def _proj(x, g, w_in_b, layer, w_small, tabs, tm, seq_len=None, dils=()):
    n = x.shape[0]
    nt = n // tm
    ntab = tabs[0].shape[0] // tm
    row = lambda i: (i, 0)
    fixed = lambda i: (0, 0)
    tab = lambda i: (i % ntab, 0)
    widths = (A_W, A_W, B_W, B_W, B_W, 3 * C_W, C_W, LANES)
    out_specs = [pl.BlockSpec((tm, w), row) for w in widths]
    out_shape = [jax.ShapeDtypeStruct((n, w), BF16 if i == 2 else F32) for i, w in enumerate(widths)]
    for d in dils:
        tps = seq_len // tm
        out_specs += [pl.BlockSpec((None, d, tm // d, B_W), lambda i, tps=tps: (i // tps, 0, i % tps, 0))] * 3
        out_shape += [jax.ShapeDtypeStruct((n // seq_len, d, seq_len // d, B_W), BF16)] * 3
    return pl.pallas_call(
        functools.partial(_proj_kernel, dils=dils),
        grid=(nt,),
        in_specs=[pl.BlockSpec((tm, D_MODEL), row), pl.BlockSpec((1, D_MODEL), fixed),
                  pl.BlockSpec((None, D_MODEL, PROJ_MAIN), lambda i: (layer, 0, 0)),
                  pl.BlockSpec((D_MODEL, LANES), fixed),
                  pl.BlockSpec((tm, LANES), tab), pl.BlockSpec((tm, LANES), tab), pl.BlockSpec((tm, LANES), tab)],
        out_specs=out_specs,
        out_shape=out_shape,
        scratch_shapes=[pltpu.VMEM((3 * B_W // LANES, tm, LANES), F32)] if dils else [],
        compiler_params=_cparams(("parallel",)),
        name="proj",
    )(x, g, w_in_b, w_small, *tabs)


def _rope_tables(pos):
    half = ROT_DIM // 2
    inv_freq = jnp.power(ROPE_THETA, -jnp.arange(0, ROT_DIM, 2, dtype=F32) / ROT_DIM)
    ang = pos.astype(F32)[:, None] * inv_freq[None, :]
    cos, sin = jnp.cos(ang), jnp.sin(ang)
    p = pos.shape[0]
    z8 = jnp.zeros((p, half), F32)
    rest0 = jnp.zeros((p, HEAD_DIM - ROT_DIM), F32)
    c64 = jnp.concatenate([cos, cos, jnp.ones((p, HEAD_DIM - ROT_DIM), F32)], axis=-1)
    s1 = jnp.concatenate([z8, sin, rest0], axis=-1)
    s2 = jnp.concatenate([-sin, z8, rest0], axis=-1)
    two = lambda a: jnp.concatenate([a, a], axis=-1)
    return two(c64), two(s1), two(s2)


def _chunk_mlp_kernel(u_ref, v_ref, g_ref, w_ref, b_ref, o_ref, vn_ref):
    v = v_ref[...]
    xc = v - jnp.mean(v, axis=-1, keepdims=True)
    vn = xc * lax.rsqrt(jnp.mean(xc * xc, axis=-1, keepdims=True) + EPS) * g_ref[...]
    vn_ref[...] = vn
    vb = vn.astype(BF16)
    rows = lax.broadcasted_iota(jnp.int32, (CHUNK, CHUNK), 0)
    cols = lax.broadcasted_iota(jnp.int32, (CHUNK, CHUNK), 1)
    tril = rows >= cols
    ws = [jnp.where(tril, w_ref[h], 0.0).astype(BF16) for h in range(A_HEADS)]
    for c in range(v.shape[0] // CHUNK):
        rs = slice(c * CHUNK, (c + 1) * CHUNK)
        parts = [_dot(ws[h], vb[rs, h * HEAD_DIM:(h + 1) * HEAD_DIM]) for h in range(A_HEADS)]
        o_ref[rs, :] = u_ref[rs, :] * (jnp.concatenate(parts, axis=-1) + b_ref[...])


def _chunk_mlp(u, v, gain, ws, bias_tile):
    n = u.shape[0]
    row = lambda i: (i, 0)
    tm = _tile_rows(n, 4 * CHUNK)
    return pl.pallas_call(
        _chunk_mlp_kernel,
        grid=(n // tm,),
        in_specs=[pl.BlockSpec((tm, A_W), row), pl.BlockSpec((tm, A_W), row),
                  pl.BlockSpec((1, A_W), lambda i: (0, 0)),
                  pl.BlockSpec((A_HEADS, CHUNK, CHUNK), lambda i: (0, 0, 0)),
                  pl.BlockSpec((CHUNK, A_W), lambda i: (0, 0))],
        out_specs=[pl.BlockSpec((tm, A_W), row), pl.BlockSpec((tm, A_W), row)],
        out_shape=[jax.ShapeDtypeStruct((n, A_W), F32)] * 2,
        compiler_params=_cparams(("parallel",)),
        name="chunk_mlp",
    )(u, v, gain, ws, bias_tile)


def _win_attn_kernel(q_ref, kp_ref, kc_ref, vp_ref, vc_ref, o_ref, lse_ref, *, seg_blocks):
    s = pl.program_id(0)
    rows = lax.broadcasted_iota(jnp.int32, (WBLK, 2 * WBLK), 0)
    cols = lax.broadcasted_iota(jnp.int32, (WBLK, 2 * WBLK), 1)
    dist = rows + WBLK - cols
    band = (dist >= 0) & (dist <= WBLK)
    lo = jnp.where((2 * s) % seg_blocks == 0, WBLK, 0)
    biases = [jnp.where(band & (cols >= lo), 0.0, NEG), jnp.where(band, 0.0, NEG)]
    lane = lax.broadcasted_iota(jnp.int32, (WBLK, LANES), 1)
    k3 = jnp.concatenate([kp_ref[...], kc_ref[...]], axis=0).astype(BF16)
    v3 = jnp.concatenate([vp_ref[...], vc_ref[...]], axis=0).astype(BF16)
    heads = range(B_HEADS)
    sls = [slice(h // 2 * LANES, (h // 2 + 1) * LANES) for h in heads]
    hms = [(lane < HEAD_DIM) if h % 2 == 0 else (lane >= HEAD_DIM) for h in heads]
    work = [(j, h) for j in range(2) for h in heads]
    qs = [q_ref[j * WBLK:(j + 1) * WBLK, :] for j in range(2)]
    ks = [k3[j * WBLK:(j + 2) * WBLK, :] for j in range(2)]
    vs = [v3[j * WBLK:(j + 2) * WBLK, :] for j in range(2)]
    scs = [_dot_nt(jnp.where(hms[h], qs[j][:, sls[h]], 0.0).astype(BF16), ks[j][:, sls[h]]) + biases[j]
           for j, h in work]
    ms = [jnp.max(sc, axis=-1, keepdims=True) for sc in scs]
    es = [jnp.exp(sc - m) for sc, m in zip(scs, ms)]
    dens = [jnp.sum(e, axis=-1, keepdims=True) for e in es]
    os_ = [_dot((e * (1.0 / d)).astype(BF16), vs[j][:, sls[h]]) for (j, h), e, d in zip(work, es, dens)]
    for j in range(2):
        base = j * B_HEADS
        lse_tile = jnp.zeros((WBLK, LANES), F32)
        for h in heads:
            lse_tile = jnp.where(lane == h, ms[base + h] + jnp.log(dens[base + h]), lse_tile)
        lse_ref[j * WBLK:(j + 1) * WBLK, :] = lse_tile
        for hp in range(B_W // LANES):
            o_ref[j * WBLK:(j + 1) * WBLK, sls[2 * hp]] = jnp.where(hms[2 * hp], os_[base + 2 * hp],
                                                                  os_[base + 2 * hp + 1])


def _win_attn(qd, kd, vd, seg_blocks):
    n = qd.shape[0]
    assert seg_blocks % 2 == 0
    cur = lambda s: (s, 0)
    prev = lambda s: (jnp.maximum(2 * s - 1, 0), 0)
    blk = lambda w, im: pl.BlockSpec((2 * WBLK, w), im)
    pblk = pl.BlockSpec((WBLK, B_W), prev)
    return pl.pallas_call(
        functools.partial(_win_attn_kernel, seg_blocks=seg_blocks),
        grid=(n // (2 * WBLK),),
        in_specs=[blk(B_W, cur), pblk, blk(B_W, cur), pblk, blk(B_W, cur)],
        out_specs=[blk(B_W, cur), blk(LANES, cur)],
        out_shape=[jax.ShapeDtypeStruct((n, B_W), F32), jax.ShapeDtypeStruct((n, LANES), F32)],
        compiler_params=_cparams(("parallel",)),
        name="win_attn",
    )(qd, kd, kd, vd, vd)


def _window_rows_kernel(k_ref, v_ref, kprev_ref, vprev_ref, ko_ref, vo_ref):
    del kprev_ref, vprev_ref
    for b in range(k_ref.shape[0]):
        ko_ref[b] = k_ref[b].T
        vo_ref[b] = v_ref[b].T


def _window_rows(k, v, k_all, v_all, layer, batch, t, keep):
    first = (t - keep) // WBLK
    src = pl.BlockSpec((batch, WBLK, B_W), lambda i: (0, first + i, 0))
    dst = pl.BlockSpec((None, batch, B_W, WBLK), lambda i: (layer, 0, 0, i))
    hbm = pl.BlockSpec(memory_space=pl.ANY)
    shape = jax.ShapeDtypeStruct(k_all.shape, F32)
    return pl.pallas_call(
        _window_rows_kernel,
        grid=(keep // WBLK,),
        in_specs=[src, src, hbm, hbm],
        out_specs=[dst, dst],
        out_shape=[shape, shape],
        input_output_aliases={2: 0, 3: 1},
        compiler_params=_cparams(("parallel",)),
        name="window_rows",
    )(k.reshape(batch, t, B_W), v.reshape(batch, t, B_W), k_all, v_all)


def _dec_attn_kernel(q_ref, kc_ref, vc_ref, kn_ref, vn_ref, o_ref, *, t_new, cache_len):
    rows_c = lax.broadcasted_iota(jnp.int32, (SUBLANES, cache_len), 0)
    cols_c = lax.broadcasted_iota(jnp.int32, (SUBLANES, cache_len), 1)
    dist_c = cache_len + rows_c % t_new - cols_c
    rows_n = lax.broadcasted_iota(jnp.int32, (SUBLANES, SUBLANES), 0)
    cols_n = lax.broadcasted_iota(jnp.int32, (SUBLANES, SUBLANES), 1)
    dist_n = rows_n % t_new - cols_n
    biases = []
    for window, dil in PATTERNS:
        vc_ok = (dist_c <= window) & ((dist_c & (dil - 1)) == 0)
        vn_ok = (dist_n >= 0) & ((dist_n & (dil - 1)) == 0)
        biases.append((jnp.where(vc_ok, 0.0, NEG), jnp.where(vn_ok, 0.0, NEG)))
    row8 = lax.broadcasted_iota(jnp.int32, (SUBLANES, LANES), 0)
    lane8 = lax.broadcasted_iota(jnp.int32, (SUBLANES, LANES), 1)
    own = (lane8 < HEAD_DIM) == (row8 < t_new)
    q = q_ref[...].astype(F32)
    outs = []
    for hp in range(B_W // LANES):
        sl = slice(hp * LANES, (hp + 1) * LANES)
        qq = jnp.concatenate([q[0:t_new, sl], q[0:t_new, sl]], axis=0)
        q8 = jnp.where(own, qq, 0.0).astype(BF16)
        kc, vc = kc_ref[sl, :].astype(BF16), vc_ref[sl, :].astype(BF16)
        kn, vn = kn_ref[:, sl].astype(BF16), vn_ref[:, sl].astype(BF16)
        sc_c = _dot(q8, kc)
        sc_n = _dot_nt(q8, kn)
        os_, lses = [], []
        for bc, bn in biases:
            a_c, a_n = sc_c + bc, sc_n + bn
            m = jnp.maximum(jnp.max(a_c, axis=-1, keepdims=True), jnp.max(a_n, axis=-1, keepdims=True))
            e_c, e_n = jnp.exp(a_c - m), jnp.exp(a_n - m)
            den = jnp.sum(e_c, axis=-1, keepdims=True) + jnp.sum(e_n, axis=-1, keepdims=True)
            inv = 1.0 / den
            os_.append(_dot_nt((e_c * inv).astype(BF16), vc) + _dot((e_n * inv).astype(BF16), vn))
            lses.append(m + jnp.log(den))
        mx = jnp.maximum(jnp.maximum(lses[0], lses[1]), lses[2])
        ws = [jnp.exp(l - mx) for l in lses]
        tot = ws[0] + ws[1] + ws[2]
        o8 = (ws[0] / tot) * os_[0] + (ws[1] / tot) * os_[1] + (ws[2] / tot) * os_[2]
        lane4 = lane8[0:t_new]
        outs.append(jnp.where(lane4 < HEAD_DIM, o8[0:t_new], o8[t_new:2 * t_new]))
    o_ref[...] = jnp.zeros((SUBLANES, B_W), F32)
    o_ref[0:t_new, :] = jnp.concatenate(outs, axis=-1)


def _dec_attn(q8, kc, vc, kn8, vn8, t_new, layer):
    _, b, _, cache_len = kc.shape
    assert 2 * t_new == SUBLANES
    small = pl.BlockSpec((None, SUBLANES, B_W), lambda i: (i, 0, 0))
    big = pl.BlockSpec((None, None, B_W, cache_len), lambda i: (layer, i, 0, 0))
    return pl.pallas_call(
        functools.partial(_dec_attn_kernel, t_new=t_new, cache_len=cache_len),
        grid=(b,),
        in_specs=[small, big, big, small, small],
        out_specs=small,
        out_shape=jax.ShapeDtypeStruct((b, SUBLANES, B_W), F32),
        compiler_params=_cparams(("parallel",)),
        name="dec_attn",
    )(q8, kc, vc, kn8, vn8)


def _split3(x):
    hi = x.astype(BF16)
    r1 = x - hi.astype(F32)
    mid = r1.astype(BF16)
    return hi, mid, (r1 - mid.astype(F32)).astype(BF16)


def _pair_dup(xx):
    lane = lax.broadcasted_iota(jnp.int32, xx.shape, 1)
    hi = xx.astype(BF16).astype(F32)
    return jnp.where(lane < HEAD_DIM, xx, xx - hi).astype(BF16)


def _pair(x):
    return _pair_dup(jnp.concatenate([x, x], axis=1))


def _lhs4(pair):
    return jnp.concatenate([pair, pair], axis=1)


def _rhs4(y):
    hi = y.astype(BF16)
    lo = (y - hi.astype(F32)).astype(BF16)
    return jnp.concatenate([hi, hi, lo, lo], axis=0)


def _unit_lower_solves(mats, rhss):
    n = DELTA_CHUNK
    w2 = 2 * HEAD_DIM
    rows = lax.broadcasted_iota(jnp.int32, (n, w2), 0)
    cols = lax.broadcasted_iota(jnp.int32, (n, w2), 1) % HEAD_DIM
    in16 = rows // 16 == cols // 16
    eye = jnp.where(rows == cols, 1.0, 0.0)
    ds = [jnp.where(in16, a, 0.0) for a in mats]
    es = [jnp.where(in16, 0.0, a) for a in mats]
    left = lambda xx: _lhs4(_pair_dup(xx))
    pw = [_dot(left(d), _rhs4(d)) for d in ds]
    ts = [eye - d for d in ds]
    for _ in range(2):
        outs = [_dot(jnp.concatenate([left(t), left(p)], axis=0), _rhs4(p)) for t, p in zip(ts, pw)]
        ts = [t + o[0:n] for t, o in zip(ts, outs)]
        pw = [o[n:2 * n] for o in outs]
    ts = [t + _dot(left(t), _rhs4(p)) for t, p in zip(ts, pw)]
    o5 = [_dot(left(t), _rhs4(jnp.concatenate([r, e], axis=1))) for t, r, e in zip(ts, rhss, es)]
    x0 = [o[:, 0:w2] for o in o5]
    nm = [o[:, w2:2 * w2] for o in o5]
    o6 = [_dot(left(m), _rhs4(jnp.concatenate([x, m], axis=1))) for m, x in zip(nm, x0)]
    ys = [x - o[:, 0:w2] for x, o in zip(x0, o6)]
    return [y + _dot(left(o[:, w2:2 * w2]), _rhs4(y)) for y, o in zip(ys, o6)]


def _gdn_prep_kernel(x_ref, cba_ref, buf_ref, cw_ref, alog_ref, dtb_ref, gmat_ref, xmat_ref,
                     u_ref, wq_ref, ak_ref, gt_ref,
                     xp_scr, q_scr, k_scr, v_scr, gb_scr, *, tm, valid_len, group):
    ti = pl.program_id(1)

    @pl.when(ti == 0)
    def _():
        xp_scr[0:SUBLANES, :] = buf_ref[...]

    x = x_ref[...]
    xp_scr[SUBLANES:SUBLANES + tm, :] = x
    off = SUBLANES - (CONV_W - 1)
    acc = xp_scr[off:off + tm, :] * cw_ref[0:1, :]
    for j in range(1, CONV_W):
        acc = acc + xp_scr[off + j:off + j + tm, :] * cw_ref[j:j + 1, :]
    xp_scr[0:SUBLANES, :] = x[tm - SUBLANES:tm, :]
    y = _silu(acc)
    q, k = y[:, 0:C_W], y[:, C_W:2 * C_W]
    gmat = gmat_ref[...]
    head_sum = lambda a: sum(_dot(p, gmat) for p in _split3(a))
    q_scr[...] = q * lax.rsqrt(head_sum(q * q) + EPS) * (HEAD_DIM ** -0.5)
    k_scr[...] = k * lax.rsqrt(head_sum(k * k) + EPS)
    v_scr[...] = y[:, 2 * C_W:3 * C_W]

    cba = cba_ref[...]
    lane = lax.broadcasted_iota(jnp.int32, (tm, LANES), 1)
    tpos = ti * tm + lax.broadcasted_iota(jnp.int32, (tm, LANES), 0)
    live = tpos < valid_len
    beta = _sigmoid(cba)
    z = cba + dtb_ref[...]
    softplus = jnp.maximum(z, 0.0) + jnp.log(1.0 + jnp.exp(-jnp.abs(z)))
    g = -jnp.exp(alog_ref[...]) * softplus
    is_g = (lane >= C_HEADS) & (lane < 2 * C_HEADS)
    gb_scr[...] = jnp.where(live, jnp.where(is_g, g, jnp.where(lane < C_HEADS, beta, 0.0)), 0.0)

    n = DELTA_CHUNK
    rows = lax.broadcasted_iota(jnp.int32, (n, 2 * n), 0)
    cols = lax.broadcasted_iota(jnp.int32, (n, 2 * n), 1) % n
    incl = rows >= cols
    strict = rows > cols
    ltri = jnp.where(incl[:, 0:n], 1.0, 0.0).astype(BF16)
    heads = range(C_HEADS)
    tiles = range(C_W // LANES)
    tsl = [slice(t * LANES, (t + 1) * LANES) for t in tiles]
    lane_c = lax.broadcasted_iota(jnp.int32, (n, LANES), 1)
    low = lane_c < HEAD_DIM
    own = [low if h % 2 == 0 else jnp.logical_not(low) for h in heads]
    xmat = xmat_ref[...]
    wide0 = C_HEADS * LANES

    def head_pairs(tile):
        rolled = pltpu.roll(tile, HEAD_DIM, 1)
        lo = rolled - rolled.astype(BF16).astype(F32)
        lane = lax.broadcasted_iota(jnp.int32, tile.shape, 1) < HEAD_DIM
        return jnp.where(lane, tile, lo).astype(BF16), jnp.where(lane, lo, tile).astype(BF16)

    def setup(c):
        rs = pl.ds(pl.multiple_of(c * n, n), n)
        gb = gb_scr[rs, :]
        g_only = jnp.where((lane_c >= C_HEADS) & (lane_c < 2 * C_HEADS), gb, 0.0)
        cg = sum(_dot(ltri, p) for p in _split3(g_only))
        cgt = cg.T
        cgl = cg[n - 1:n, :]
        gt_ref[pl.ds(pl.multiple_of(c * SUBLANES, SUBLANES), SUBLANES), :] = jnp.broadcast_to(
            jnp.exp(cgl), (SUBLANES, LANES))
        wide = sum(_dot(p, xmat) for p in _split3(jnp.where(lane_c < C_HEADS, gb, cg)))
        b128 = [wide[:, h * LANES:(h + 1) * LANES] for h in heads]
        c128 = [wide[:, wide0 + h * LANES:wide0 + (h + 1) * LANES] for h in heads]
        cgr = [jnp.concatenate([cgt[C_HEADS + h:C_HEADS + h + 1, :]] * 2, axis=1) for h in heads]
        decay = [jnp.where(incl, jnp.exp(jnp.minimum(c128[h] - cgr[h], 0.0)), 0.0) for h in heads]
        amat, at_pair, rhs, qd_pair, kdt_pair = [], [], [], [], []
        for t in tiles:
            he, ho = 2 * t, 2 * t + 1
            qn, kn, vv = q_scr[rs, tsl[t]], k_scr[rs, tsl[t]], v_scr[rs, tsl[t]]
            bnat = jnp.where(low, b128[he], b128[ho])
            cnat = jnp.where(low, c128[he], c128[ho])
            ecg = jnp.exp(cnat)
            kb = kn.astype(BF16)
            kk2 = jnp.concatenate([kb, kb], axis=0)
            for h in (he, ho):
                masked = jnp.concatenate([jnp.where(own[h], kn, 0.0), jnp.where(own[h], qn, 0.0)], axis=0)
                kkqk = _dot_nt(masked.astype(BF16), kk2)
                amat.append(jnp.where(strict, b128[h] * kkqk[0:n] * decay[h], 0.0))
                at_pair.append(_pair_dup(kkqk[n:2 * n] * decay[h]))
            bv = bnat * vv
            bk_rolled = pltpu.roll((bnat * ecg) * kn, HEAD_DIM, 1)
            rhs += [jnp.where(low, bv, bk_rolled), jnp.where(low, bk_rolled, bv)]
            qd_pair += list(head_pairs(qn * ecg))
            kd = kn * jnp.exp(cnat[n - 1:n, :] - cnat)
            kdt = jnp.concatenate([kd, kd], axis=0).T
            kdt_pair += [_pair_dup(kdt[0:n]), _pair_dup(kdt[n:2 * n])]
        return amat, rhs, at_pair, qd_pair, kdt_pair

    def finish(c, sol, at_pair, qd_pair, kdt_pair):
        w_pair = []
        for t in tiles:
            se, so = sol[2 * t], sol[2 * t + 1]
            u_ref[pl.ds(pl.multiple_of(c * n, n), n), tsl[t]] = jnp.where(low, se, so)
            w_odd, w_even = head_pairs(jnp.where(low, so, se))
            w_pair += [w_even, w_odd]
        r2 = pl.ds(pl.multiple_of(c * 2 * n, 2 * n), 2 * n)
        wq_ref[r2, :] = jnp.concatenate(
            [jnp.concatenate([w_pair[h], qd_pair[h]], axis=0) for h in heads], axis=1)
        ak_ref[r2, :] = jnp.concatenate(
            [jnp.concatenate([at_pair[h], kdt_pair[h]], axis=0) for h in heads], axis=1)

    def chunks(i, carry):
        parts = [setup(i * group + j) for j in range(group)]
        sol = _unit_lower_solves([a for p in parts for a in p[0]], [r for p in parts for r in p[1]])
        for j, p in enumerate(parts):
            finish(i * group + j, sol[j * C_HEADS:(j + 1) * C_HEADS], *p[2:])
        return carry

    lax.fori_loop(0, tm // n // group, chunks, 0)


def _gdn_constants():
    hid = jnp.arange(C_W, dtype=jnp.int32) // HEAD_DIM
    gmat = (hid[:, None] == hid[None, :]).astype(BF16)
    src = jnp.arange(LANES, dtype=jnp.int32)[:, None]
    dst = jnp.arange(2 * C_HEADS * LANES, dtype=jnp.int32)[None, :] // LANES
    return gmat, (src == dst).astype(BF16)


def _gdn_prep(cqkv, cba, buf8, conv_w, alog_row, dtb_row, consts, batch, t, tm, valid_len):
    gmat, xmat = consts
    n = batch * t
    nt = t // tm
    row = lambda b, i: (b * nt + i, 0)
    fixed = lambda b, i: (0, 0)
    nch = tm // DELTA_CHUNK
    pair_w = C_HEADS * 2 * HEAD_DIM
    outs = [jax.ShapeDtypeStruct((n, C_W), F32), jax.ShapeDtypeStruct((2 * n, pair_w), BF16),
            jax.ShapeDtypeStruct((2 * n, pair_w), BF16),
            jax.ShapeDtypeStruct((n // DELTA_CHUNK * SUBLANES, LANES), F32)]
    return pl.pallas_call(
        functools.partial(_gdn_prep_kernel, tm=tm, valid_len=valid_len, group=min(GDN_GROUP, nch)),
        grid=(batch, nt),
        in_specs=[pl.BlockSpec((tm, 3 * C_W), row), pl.BlockSpec((tm, LANES), row),
                  pl.BlockSpec((None, SUBLANES, 3 * C_W), lambda b, i: (b, 0, 0)),
                  pl.BlockSpec((CONV_W, 3 * C_W), fixed), pl.BlockSpec((1, LANES), fixed),
                  pl.BlockSpec((1, LANES), fixed), pl.BlockSpec((C_W, C_W), fixed),
                  pl.BlockSpec(xmat.shape, fixed)],
        out_specs=[pl.BlockSpec((tm, C_W), row), pl.BlockSpec((2 * tm, pair_w), row),
                   pl.BlockSpec((2 * tm, pair_w), row), pl.BlockSpec((nch * SUBLANES, LANES), row)],
        out_shape=outs,
        scratch_shapes=[pltpu.VMEM((tm + SUBLANES, 3 * C_W), F32), pltpu.VMEM((tm, C_W), F32),
                        pltpu.VMEM((tm, C_W), F32), pltpu.VMEM((tm, C_W), F32), pltpu.VMEM((tm, LANES), F32)],
        compiler_params=_cparams(("parallel", "arbitrary")),
        name="gdn_prep",
    )(cqkv, cba, buf8, conv_w, alog_row, dtb_row, gmat, xmat)


def _gdn_scan_kernel(u_ref, wq_ref, ak_ref, gt_ref, z_ref, s0_ref, gn_ref,
                     o_ref, sfin_ref, s_scr, *, bg):
    c = pl.program_id(1)

    @pl.when(c == 0)
    def _():
        s_scr[...] = s0_ref[...]

    gn = gn_ref[...]
    n = DELTA_CHUNK
    chains = [(b, h, slice(h * HEAD_DIM, (h + 1) * HEAD_DIM), slice(h * 2 * HEAD_DIM, (h + 1) * 2 * HEAD_DIM))
              for b in range(bg) for h in range(C_HEADS)]
    st = [s_scr[b, h] for b, h, _, _ in chains]
    r1 = [_dot(_lhs4(wq_ref[b, :, ps]), _rhs4(s)) for (b, _, _, ps), s in zip(chains, st)]
    up = [u_ref[b, :, sl] - r[0:n] for (b, _, sl, _), r in zip(chains, r1)]
    r2 = [_dot(_lhs4(ak_ref[b, :, ps]), _rhs4(x)) for (b, _, _, ps), x in zip(chains, up)]
    for (b, h, _, _), s, r in zip(chains, st, r2):
        s_scr[b, h] = gt_ref[b, 0:1, C_HEADS + h:C_HEADS + h + 1] * s + r[n:2 * n]
    os_ = [a[n:2 * n] + r[0:n] for a, r in zip(r1, r2)]
    outs = [o * lax.rsqrt(jnp.mean(o * o, axis=-1, keepdims=True) + EPS) * gn * _silu(z_ref[b, :, sl])
            for (b, _, sl, _), o in zip(chains, os_)]
    for b in range(bg):
        o_ref[b] = jnp.concatenate(outs[b * C_HEADS:(b + 1) * C_HEADS], axis=1)

    @pl.when(c == pl.num_programs(1) - 1)
    def _():
        sfin_ref[...] = s_scr[...]


def _gdn_scan(u, wq, ak, gt, z, s0, gnorm, batch, t, bg):
    n = DELTA_CHUNK
    nc = t // n
    pair_w = C_HEADS * 2 * HEAD_DIM
    v3 = lambda a: a.reshape(batch, t, C_W)
    p3 = lambda a: a.reshape(batch, 2 * t, pair_w)
    tok = pl.BlockSpec((bg, n, C_W), lambda b, c: (b, c, 0))
    pair = pl.BlockSpec((bg, 2 * n, pair_w), lambda b, c: (b, c, 0))
    st = pl.BlockSpec((bg, C_HEADS, HEAD_DIM, HEAD_DIM), lambda b, c: (b, 0, 0, 0))
    return pl.pallas_call(
        functools.partial(_gdn_scan_kernel, bg=bg),
        grid=(batch // bg, nc),
        in_specs=[tok, pair, pair,
                  pl.BlockSpec((bg, SUBLANES, LANES), lambda b, c: (b, c, 0)),
                  tok, st, pl.BlockSpec((1, HEAD_DIM), lambda b, c: (0, 0))],
        out_specs=[tok, st],
        out_shape=[jax.ShapeDtypeStruct((batch, t, C_W), F32),
                   jax.ShapeDtypeStruct((batch, C_HEADS, HEAD_DIM, HEAD_DIM), F32)],
        scratch_shapes=[pltpu.VMEM((bg, C_HEADS, HEAD_DIM, HEAD_DIM), F32)],
        compiler_params=_cparams(("parallel", "arbitrary")),
        name="gdn_scan",
    )(v3(u), p3(wq), p3(ak), gt.reshape(batch, nc * SUBLANES, LANES), v3(z), s0, gnorm)


def _head_expand(wt):
    tm = wt.shape[0]
    return jnp.concatenate([jnp.broadcast_to(wt[:, h:h + 1], (tm, HEAD_DIM)) for h in range(B_HEADS)], axis=-1)


def _out_proj_kernel(*refs, dils):
    if dils:
        npat = len(dils)
        a_ref = refs[0]
        o_refs, l_refs = refs[1:1 + npat], refs[1 + npat:1 + 2 * npat]
        c_ref, x_ref, w_ref, g2_ref, wr_ref, y_ref, dense_ref, o_scr, l_scr = refs[1 + 2 * npat:]
        tm = x_ref.shape[0]
        os_, ls = [], []
        for i, d in enumerate(dils):
            if d == 1:
                os_.append(o_refs[i][0])
                ls.append(l_refs[i][0])
            else:
                nblk = B_W // LANES
                for j in range(d):
                    rows = pl.ds(j, tm // d, stride=d)
                    for b in range(nblk):
                        o_scr[i * nblk + b, rows, :] = o_refs[i][j, :, LANES * b:LANES * (b + 1)]
                    l_scr[i, rows, :] = l_refs[i][j]
                os_.append(jnp.concatenate([o_scr[i * nblk + b] for b in range(nblk)], axis=-1))
                ls.append(l_scr[i])
        mx = functools.reduce(jnp.maximum, ls)
        es = [jnp.exp(l - mx) for l in ls]
        tot = functools.reduce(lambda a, b: a + b, es)
        ob = functools.reduce(lambda a, b: a + b, [_head_expand(e / tot) * o for e, o in zip(es, os_)])
    else:
        a_ref, b_ref, c_ref, x_ref, w_ref, g2_ref, wr_ref, y_ref, dense_ref = refs
        ob = b_ref[...]
    cat = jnp.concatenate([a_ref[...], ob, c_ref[...]], axis=-1).astype(BF16)
    y = x_ref[...] + _dot(cat, w_ref[...])
    y_ref[...] = y
    dense_ref[...] = _route_tile(y, g2_ref[...], wr_ref[...])


def _out_proj(out_a, out_b, lses, out_c, x, w_out, g2, w_route, tm, seq_len=None, dils=()):
    n = x.shape[0]
    row = lambda i: (i, 0)
    spec = lambda w: pl.BlockSpec((tm, w), row)
    scratch = []
    if dils:
        tps = seq_len // tm
        strided = lambda d, w: pl.BlockSpec((None, d, tm // d, w), lambda i: (i // tps, 0, i % tps, 0))
        ins = [out_a, *out_b, *lses, out_c, x, w_out]
        specs = ([spec(A_W)] + [strided(d, B_W) for d in dils] + [strided(d, LANES) for d in dils]
                 + [spec(C_W), spec(D_MODEL)])
        scratch = [pltpu.VMEM((len(dils) * B_W // LANES, tm, LANES), F32), pltpu.VMEM((len(dils), tm, LANES), F32)]
    else:
        ins = [out_a, out_b, out_c, x, w_out]
        specs = [spec(A_W), spec(B_W), spec(C_W), spec(D_MODEL)]
    fixed = lambda i: (0, 0)
    specs += [pl.BlockSpec((D_MODEL, D_MODEL), fixed), pl.BlockSpec((1, D_MODEL), fixed),
              pl.BlockSpec((D_MODEL, LANES), fixed)]
    return pl.pallas_call(
        functools.partial(_out_proj_kernel, dils=dils),
        grid=(n // tm,),
        in_specs=specs,
        out_specs=[spec(D_MODEL), spec(LANES)],
        out_shape=[jax.ShapeDtypeStruct((n, D_MODEL), F32), jax.ShapeDtypeStruct((n, LANES), F32)],
        scratch_shapes=scratch,
        compiler_params=_cparams(("parallel",)),
        name="out_proj",
    )(*ins, g2, w_route)


def _route_tile(x, g, w):
    t = x * lax.rsqrt(jnp.mean(x * x, axis=-1, keepdims=True) + EPS) * g
    t_hi = t.astype(BF16)
    t_lo = (t - t_hi.astype(F32)).astype(BF16)
    w_hi = w.astype(BF16)
    w_lo = (w - w_hi.astype(F32)).astype(BF16)
    lg = _dot(t_hi, w_hi) + (_dot(t_hi, w_lo) + _dot(t_lo, w_hi))
    tm = lg.shape[0]
    nrow = ROUTE_OFF + N_EXPERTS + ROUTE_OFF
    lt = lg.T[0:nrow, :]
    row = lax.broadcasted_iota(jnp.int32, (nrow, tm), 0).astype(F32)
    big = float(LANES)
    down = lambda op, a: op(a, axis=0, keepdims=True)
    is_grp = row < N_GROUPS
    gl = jnp.where(is_grp, lt, NEG)
    gmax = down(jnp.max, gl)
    gsum = down(jnp.sum, jnp.where(is_grp, jnp.exp(gl - gmax), 0.0))
    g_w = 1.0 / gsum
    g_idx = down(jnp.min, jnp.where(is_grp & (gl == gmax), row, big))
    lo = ROUTE_OFF + EXP_PER_GROUP * g_idx
    sel = (row >= lo) & (row < lo + EXP_PER_GROUP)
    el = jnp.where(sel, lt, NEG)
    m1 = down(jnp.max, el)
    esum = down(jnp.sum, jnp.where(sel, jnp.exp(el - m1), 0.0))
    i1 = down(jnp.min, jnp.where(sel & (el == m1), row, big))
    el2 = jnp.where(row == i1, NEG, el)
    m2 = down(jnp.max, el2)
    i2 = down(jnp.min, jnp.where(sel & (row != i1) & (el2 == m2), row, big))
    p1 = 1.0 / esum
    p2 = jnp.exp(m2 - m1) / esum
    tot = p1 + p2
    gates = jnp.where(row == i1, g_w * (p1 / tot), 0.0) + jnp.where(row == i2, g_w * (p2 / tot), 0.0)
    full = jnp.concatenate([jnp.where(row == 0.0, g_idx, gates), jnp.zeros((LANES - nrow, tm), F32)], axis=0)
    return full.T


def _rms_rows(y, g):
    return y * lax.rsqrt(jnp.mean(y * y, axis=-1, keepdims=True) + EPS) * g


def _moe_kernel(x_ref, g_ref, dense_ref, w1_ref, w3_ref, w2_ref, *rest):
    fg_ref = rest[0] if len(rest) == 3 else None
    y_ref, t_scr = rest[-2:]
    e = pl.program_id(1)

    @pl.when(e == 0)
    def _():
        x = x_ref[...]
        t_scr[...] = (x * lax.rsqrt(jnp.mean(x * x, axis=-1, keepdims=True) + EPS) * g_ref[...]).astype(BF16)
        y_ref[...] = x

    tb = t_scr[...]
    dense = dense_ref[...]
    lane = lax.broadcasted_iota(jnp.int32, dense.shape, 1)
    gate = jnp.sum(jnp.where(lane == e + ROUTE_OFF, dense, 0.0), axis=-1, keepdims=True)
    hid = _silu(_dot(tb, w1_ref[...].astype(BF16))) * _dot(tb, w3_ref[...].astype(BF16))
    y_ref[...] += _dot((hid * gate).astype(BF16), w2_ref[...].astype(BF16))

    if fg_ref is not None:
        @pl.when(e == pl.num_programs(1) - 1)
        def _():
            y_ref[...] = _rms_rows(y_ref[...], fg_ref[...])


def _moe(x, g, dense, w1, w3, w2, layer, tm, final_g=None):
    n = x.shape[0]
    row = lambda i, e: (i, 0)
    expert = lambda i, e: (layer, e, 0, 0)
    extra = [] if final_g is None else [final_g]
    return pl.pallas_call(
        _moe_kernel,
        grid=(n // tm, N_EXPERTS),
        in_specs=[pl.BlockSpec((tm, D_MODEL), row), pl.BlockSpec((1, D_MODEL), lambda i, e: (0, 0)),
                  pl.BlockSpec((tm, LANES), row),
                  pl.BlockSpec((None, None, D_MODEL, D_EXPERT), expert),
                  pl.BlockSpec((None, None, D_MODEL, D_EXPERT), expert),
                  pl.BlockSpec((None, None, D_EXPERT, D_MODEL), expert)]
        + [pl.BlockSpec((1, D_MODEL), lambda i, e: (0, 0))] * len(extra),
        out_specs=pl.BlockSpec((tm, D_MODEL), row),
        out_shape=jax.ShapeDtypeStruct((n, D_MODEL), F32),
        scratch_shapes=[pltpu.VMEM((tm, D_MODEL), BF16)],
        compiler_params=_cparams(("parallel", "arbitrary")),
        name="moe",
    )(x, g, dense, w1, w3, w2, *extra)


def _moe_grouped_kernel(x_ref, g_ref, dense_ref, gidr_ref, tri_ref, w1_ref, w3_ref, w2_ref, *rest,
                        tm, slots, final):
    fg_ref = rest[0] if final else None
    y_ref, ts_scr, gs_scr, ys_scr, pt_scr, meta = rest[-6:]
    e = pl.program_id(1)
    blk = MOE_BLOCK

    @pl.when(e == 0)
    def _():
        x = x_ref[...]
        t = (x * lax.rsqrt(jnp.mean(x * x, axis=-1, keepdims=True) + EPS) * g_ref[...]).astype(BF16)
        dense = dense_ref[...]
        tri = tri_ref[...]
        lane = lax.broadcasted_iota(jnp.int32, (tm, LANES), 1)
        ohc = jnp.where(lane.astype(F32) == dense[:, 0:1], 1.0, 0.0)
        rankc = _dot(tri, ohc.astype(BF16))
        sub = lax.broadcasted_iota(jnp.int32, (SUBLANES, tm), 0)
        ohr = jnp.where(sub.astype(F32) == gidr_ref[...], 1.0, 0.0)
        rankr = _dot_nt(ohr.astype(BF16), tri)
        start = jnp.int32(0)
        s_lane = jnp.zeros((tm, LANES), F32)
        s_sub = jnp.zeros((SUBLANES, tm), F32)
        for g in range(N_GROUPS):
            count = jnp.sum(jnp.where(lane == g, ohc, 0.0)).astype(jnp.int32)
            nblk = (count + blk - 1) // blk
            meta[g] = start
            meta[N_GROUPS + g] = nblk
            s_lane = jnp.where(lane == g, start.astype(F32), s_lane)
            s_sub = jnp.where(sub == g, start.astype(F32), s_sub)
            start = start + nblk * blk
        destc = jnp.sum(ohc * (s_lane + rankc), axis=-1, keepdims=True)
        destr = jnp.sum(ohr * (s_sub + rankr), axis=0, keepdims=True)
        slot_r = lax.broadcasted_iota(jnp.int32, (slots, tm), 0).astype(F32)
        p = jnp.where(slot_r == destr, 1.0, 0.0).astype(BF16)
        ts_scr[...] = _dot(p, t).astype(BF16)
        gs_scr[...] = sum(_dot(p, piece) for piece in _split3(dense))
        slot_c = lax.broadcasted_iota(jnp.int32, (tm, slots), 1).astype(F32)
        pt_scr[...] = jnp.where(slot_c == destc, 1.0, 0.0).astype(BF16)
        ys_scr[...] = jnp.zeros((slots, D_MODEL), F32)

    g = e // EXP_PER_GROUP
    start = meta[g]
    nblk = meta[N_GROUPS + g]
    w1, w3, w2 = w1_ref[...], w3_ref[...], w2_ref[...]

    def rows_update(r0, m):
        rows = pl.ds(pl.multiple_of(r0, blk), m)
        tb = ts_scr[rows, :]
        lane = lax.broadcasted_iota(jnp.int32, (m, LANES), 1)
        gate = jnp.sum(jnp.where(lane == e + ROUTE_OFF, gs_scr[rows, :], 0.0), axis=-1, keepdims=True)
        hid = _silu(_dot(tb, w1)) * _dot(tb, w3)
        ys_scr[rows, :] += _dot((hid * gate).astype(BF16), w2)

    def pair(j, carry):
        rows_update(start + j * 2 * blk, 2 * blk)
        return carry

    lax.fori_loop(0, nblk // 2, pair, 0)

    @pl.when(nblk % 2 == 1)
    def _():
        rows_update(start + (nblk - 1) * blk, blk)

    @pl.when(e == pl.num_programs(1) - 1)
    def _():
        pt = pt_scr[...]
        ys = ys_scr[...]
        hi = ys.astype(BF16)
        lo = (ys - hi.astype(F32)).astype(BF16)
        y = x_ref[...] + (_dot(pt, hi) + _dot(pt, lo))
        y_ref[...] = _rms_rows(y, fg_ref[...]) if final else y


def _moe_grouped(x, g, dense, w1, w3, w2, layer, tm, final_g=None):
    extra = [] if final_g is None else [final_g]
    n = x.shape[0]
    slots = tm + N_GROUPS * MOE_BLOCK
    row = lambda i, e: (i, 0)
    expert = lambda i, e: (layer, e, 0, 0)
    gid_rows = dense[:, 0].reshape(n // tm, 1, tm)
    idx = jnp.arange(tm, dtype=jnp.int32)
    tri = (idx[None, :] < idx[:, None]).astype(BF16)
    return pl.pallas_call(
        functools.partial(_moe_grouped_kernel, tm=tm, slots=slots, final=final_g is not None),
        grid=(n // tm, N_EXPERTS),
        in_specs=[pl.BlockSpec((tm, D_MODEL), row), pl.BlockSpec((1, D_MODEL), lambda i, e: (0, 0)),
                  pl.BlockSpec((tm, LANES), row),
                  pl.BlockSpec((None, 1, tm), lambda i, e: (i, 0, 0)),
                  pl.BlockSpec((tm, tm), lambda i, e: (0, 0)),
                  pl.BlockSpec((None, None, D_MODEL, D_EXPERT), expert),
                  pl.BlockSpec((None, None, D_MODEL, D_EXPERT), expert),
                  pl.BlockSpec((None, None, D_EXPERT, D_MODEL), expert)]
        + [pl.BlockSpec((1, D_MODEL), lambda i, e: (0, 0))] * len(extra),
        out_specs=pl.BlockSpec((tm, D_MODEL), row),
        out_shape=jax.ShapeDtypeStruct((n, D_MODEL), F32),
        scratch_shapes=[pltpu.VMEM((slots, D_MODEL), BF16), pltpu.VMEM((slots, LANES), F32),
                        pltpu.VMEM((slots, D_MODEL), F32), pltpu.VMEM((tm, slots), BF16),
                        pltpu.SMEM((2 * N_GROUPS,), jnp.int32)],
        compiler_params=_cparams(("parallel", "arbitrary")),
        name="moe_grouped",
    )(x, g, dense, gid_rows, tri, w1, w3, w2, *extra)


def _tile_rows(n, cap):
    tm = min(n, cap)
    assert n % tm == 0
    return tm


def _layer_weights(l, norm1_g, w_in, a_vnorm_g, a_ws, a_bs, c_conv_w, c_a_log, c_dt_bias, c_norm_g,
                   w_out, norm2_g, w_group, w_router, w1, w3, w2):
    pad_l = lambda a, left: jnp.pad(a, ((0, 0), (left, LANES - left - a.shape[-1])))
    return dict(
        norm1_g=norm1_g[l][None, :],
        w_in_b=w_in.astype(BF16),
        w_small=pad_l(w_in[l, :, PROJ_MAIN:], 0).astype(BF16),
        a_gain=a_vnorm_g[l][None, :],
        a_ws=a_ws[l],
        a_bs=a_bs[l],
        conv_w=c_conv_w[l],
        alog_row=pad_l(c_a_log[l][None, :], C_HEADS),
        dtb_row=pad_l(c_dt_bias[l][None, :], C_HEADS),
        gnorm=c_norm_g[l][None, :],
        w_out=w_out[l].astype(BF16),
        norm2_g=norm2_g[l][None, :],
        w_route=pad_l(jnp.concatenate([w_group[l], w_router[l]], axis=-1), 0),
        w1=w1, w3=w3, w2=w2, layer=l,
    )


def _mixer_c(lw, cqkv, cba, cz, conv_buf, s0, gmat, batch, t, tm, valid_len, bg):
    buf8 = jnp.pad(conv_buf, ((0, 0), (SUBLANES - (CONV_W - 1), 0), (0, 0)))
    u, wq, ak, gt = _gdn_prep(cqkv, cba, buf8, lw["conv_w"], lw["alog_row"], lw["dtb_row"], gmat,
                              batch, t, tm, valid_len)
    return _gdn_scan(u, wq, ak, gt, cz, s0, lw["gnorm"], batch, t, bg)


def _ffn(lw, x, dense):
    n = x.shape[0]
    if n % MOE_TILE == 0:
        return _moe_grouped(x, lw["norm2_g"], dense, lw["w1"].astype(BF16), lw["w3"].astype(BF16),
                            lw["w2"].astype(BF16), lw["layer"], MOE_TILE, lw["final_g"])
    return _moe(x, lw["norm2_g"], dense, lw["w1"], lw["w3"], lw["w2"], lw["layer"], n, lw["final_g"])


def _prompt_layer(lw, x, batch, t, tabs, gmat, win_k, win_v, layer):
    n = batch * t
    dils = tuple(d for _, d in PATTERNS)
    tm = _tile_rows(n, 512)
    au, av, q, k, v, cqkv, cz, cba, *strided = _proj(x, lw["norm1_g"], lw["w_in_b"], layer, lw["w_small"], tabs, tm,
                                                     seq_len=t, dils=dils[1:])

    bias_tile = jnp.repeat(lw["a_bs"].T, HEAD_DIM, axis=1)
    out_a, _ = _chunk_mlp(au, av, lw["a_gain"], lw["a_ws"], bias_tile)

    qkv = [(q, k, v)] + [tuple(a.reshape(n, B_W) for a in strided[3 * i:3 * i + 3]) for i in range(len(dils) - 1)]
    outs, lses = [], []
    for d, (qd, kd, vd) in zip(dils, qkv):
        o_d, lse_d = _win_attn(qd, kd, vd, t // d // WBLK)
        outs.append(o_d.reshape(batch, d, t // d, B_W))
        lses.append(lse_d.reshape(batch, d, t // d, LANES))

    zeros_buf = jnp.zeros((batch, CONV_W - 1, 3 * C_W), F32)
    zeros_s = jnp.zeros((batch, C_HEADS, HEAD_DIM, HEAD_DIM), F32)
    out_c, s_new = _mixer_c(lw, cqkv, cba, cz, zeros_buf, zeros_s, gmat, batch, t, 256, t, batch)
    out_c = out_c.reshape(n, C_W)

    x, dense = _out_proj(out_a, outs, lses, out_c, x, lw["w_out"], lw["norm2_g"], lw["w_route"], tm,
                         seq_len=t, dils=dils)
    x = _ffn(lw, x, dense)

    win_k, win_v = _window_rows(k, v, win_k, win_v, layer, batch, t, win_k.shape[-1])
    conv_state = cqkv.reshape(batch, t, 3 * C_W)[:, t - (CONV_W - 1):]
    return x, win_k, win_v, (conv_state, s_new)


def _sample_layer(lw, x, batch, t, tabs, gmat, kbuf, vbuf, conv_buf, s0, layer):
    n = batch * t
    au, av, q, k, v, cqkv, cz, cba = _proj(x, lw["norm1_g"], lw["w_in_b"], layer, lw["w_small"], tabs, n)

    eye = jnp.eye(batch, dtype=F32)
    ws_bd = jnp.stack([jnp.kron(eye, lw["a_ws"][h, :t, :t]) for h in range(A_HEADS)])
    bias_tile = jnp.tile(jnp.repeat(lw["a_bs"][:, :t].T, HEAD_DIM, axis=1), (batch, 1))
    out_a, a_rows = _chunk_mlp(au, av, lw["a_gain"], ws_bd, bias_tile)

    pad8 = lambda a: jnp.pad(a.reshape(batch, t, B_W), ((0, 0), (0, SUBLANES - t), (0, 0)))
    out_b = _dec_attn(pad8(q), kbuf, vbuf, pad8(k), pad8(v), t, layer)[:, :t].reshape(n, B_W)

    tp = DELTA_CHUNK
    padt = lambda a: jnp.pad(a.reshape(batch, t, -1), ((0, 0), (0, tp - t), (0, 0))).reshape(batch * tp, -1)
    out_c, s_new = _mixer_c(lw, padt(cqkv), padt(cba), padt(cz), conv_buf, s0, gmat, batch, tp, tp, t, 4)
    out_c = out_c[:, :t].reshape(n, C_W)

    x, dense = _out_proj(out_a, out_b, None, out_c, x, lw["w_out"], lw["norm2_g"], lw["w_route"], n)
    x = _ffn(lw, x, dense)

    heads = lambda a: a.reshape(batch, t, B_HEADS, HEAD_DIM)
    conv_state = jnp.concatenate([conv_buf, cqkv.reshape(batch, t, 3 * C_W)], axis=1)[:, -(CONV_W - 1):]
    return x, (heads(k), heads(v), a_rows.reshape(batch, t, A_W), conv_state, s_new)


def kernel(x_prompt, x_sample, cache_win_k, cache_win_v, state_conv, state_delta, norm1_g, w_in, a_vnorm_g, a_ws, a_bs, c_conv_w, c_a_log, c_dt_bias, c_norm_g, w_out, norm2_g, w_group, w_router, w1, w3, w2, final_g):
    bp, tp, _ = x_prompt.shape
    bs, ts, _ = x_sample.shape
    depth = w_in.shape[0]
    assert tp % (PATTERNS[-1][1] * WBLK) == 0 and tp % 512 == 0 and bs * ts == CHUNK

    tabs_p = _rope_tables(jnp.arange(tp, dtype=jnp.int32))
    tabs_s = tuple(jnp.tile(a, (bs, 1)) for a in _rope_tables(PAST_LEN + jnp.arange(ts, dtype=jnp.int32)))
    gmat = _gdn_constants()

    feat_major = lambda c: jnp.transpose(c, (0, 1, 3, 4, 2)).reshape(depth, bs, B_W, c.shape[2])
    cache_k, cache_v = feat_major(cache_win_k), feat_major(cache_win_v)
    keep = min(MAX_WINDOW, tp)
    win_k = jnp.zeros((depth, bp, B_W, keep), F32)
    win_v = jnp.zeros((depth, bp, B_W, keep), F32)

    xp = x_prompt.reshape(bp * tp, D_MODEL)
    xs = x_sample.reshape(bs * ts, D_MODEL)
    p_out = [[] for _ in range(2)]
    s_out = [[] for _ in range(5)]
    for l in range(depth):
        lw = _layer_weights(l, norm1_g, w_in, a_vnorm_g, a_ws, a_bs, c_conv_w, c_a_log, c_dt_bias, c_norm_g,
                            w_out, norm2_g, w_group, w_router, w1, w3, w2)
        lw["final_g"] = final_g[None, :] if l == depth - 1 else None
        xp, win_k, win_v, st = _prompt_layer(lw, xp, bp, tp, tabs_p, gmat, win_k, win_v, l)
        for acc, a in zip(p_out, st):
            acc.append(a)
        xs, st = _sample_layer(lw, xs, bs, ts, tabs_s, gmat, cache_k, cache_v, state_conv[l], state_delta[l], l)
        for acc, a in zip(s_out, st):
            acc.append(a)
    y_prompt = xp.reshape(bp, tp, D_MODEL)
    y_sample = xs.reshape(bs, ts, D_MODEL)
    rows_major = lambda w: jnp.transpose(w.reshape(depth, bp, B_HEADS, HEAD_DIM, keep), (0, 1, 4, 2, 3))
    return (y_prompt, y_sample, rows_major(win_k), rows_major(win_v), *[jnp.stack(a) for a in p_out],
            *[jnp.stack(a) for a in s_out])
```

```python
import functools
import math

import jax
import jax.numpy as jnp
from jax import lax
from jax.experimental import pallas as pl
from jax.experimental.pallas import tpu as pltpu

F32 = jnp.float32
BF16 = jnp.bfloat16
HI = lax.Precision.HIGHEST

D_MODEL = 1024
HEAD_DIM = 64
A_HEADS = 4
B_HEADS = 6
C_HEADS = 6
A_W = A_HEADS * HEAD_DIM
B_W = B_HEADS * HEAD_DIM
C_W = C_HEADS * HEAD_DIM
CHUNK = 128
PATTERNS = ((128, 1), (512, 4), (2048, 16))
MAX_WINDOW = 2048
ROT_DIM = HEAD_DIM // 4
ROPE_THETA = 500000.0
CONV_W = 4
DELTA_CHUNK = 64
N_GROUPS = 4
EXP_PER_GROUP = 8
N_EXPERTS = N_GROUPS * EXP_PER_GROUP
D_EXPERT = 256
EPS = 1e-6
PAST_LEN = 16384

LANES = 128
SUBLANES = 8
WBLK = 128
PROJ_MAIN = 2 * A_W + 3 * B_W + 4 * C_W
NEG = -1e30
ROUTE_OFF = N_GROUPS
VMEM_LIMIT = 56 * 1024 * 1024
MOE_TILE = 1024
MOE_BLOCK = 128
GDN_GROUP = 4


def _cparams(sem):
    return pltpu.CompilerParams(dimension_semantics=sem, vmem_limit_bytes=VMEM_LIMIT)


def _sigmoid(x):
    return 1.0 / (1.0 + jnp.exp(-x))


def _silu(x):
    return x * _sigmoid(x)


def _dot(a, b, precision=None):
    return jnp.dot(a, b, preferred_element_type=F32, precision=precision)


def _dot_nt(a, b, precision=None):
    return lax.dot_general(a, b, (((1,), (1,)), ((), ())), preferred_element_type=F32, precision=precision)


def _chunk_mlp_rows(u, v, gain, w_ref, bias, o_ref, vn_ref):
    xc = v - jnp.mean(v, axis=-1, keepdims=True)
    vn = xc * lax.rsqrt(jnp.mean(xc * xc, axis=-1, keepdims=True) + EPS) * gain
    vn_ref[...] = vn
    vb = vn.astype(BF16)
    rows = lax.broadcasted_iota(jnp.int32, (CHUNK, CHUNK), 0)
    cols = lax.broadcasted_iota(jnp.int32, (CHUNK, CHUNK), 1)
    tril = rows >= cols
    ws = [jnp.where(tril, w_ref[h], 0.0).astype(BF16) for h in range(A_HEADS)]
    for c in range(v.shape[0] // CHUNK):
        rs = slice(c * CHUNK, (c + 1) * CHUNK)
        parts = [_dot(ws[h], vb[rs, h * HEAD_DIM:(h + 1) * HEAD_DIM]) for h in range(A_HEADS)]
        o_ref[rs, :] = u[rs, :] * (jnp.concatenate(parts, axis=-1) + bias)


def _proj_kernel(x_ref, g_ref, w_ref, wsm_ref, c_ref, s1_ref, s2_ref, ag_ref, aw_ref, ab_ref,
                 oa_ref, vn_ref, q_ref, k_ref, v_ref, cqkv_ref, cz_ref, cba_ref, *strided_refs, dils):
    x = x_ref[...]
    h = x * lax.rsqrt(jnp.mean(x * x, axis=-1, keepdims=True) + EPS) * g_ref[...]
    hb = h.astype(BF16)

    def seg(a, b):
        return _dot(hb, w_ref[:, a:b])

    _chunk_mlp_rows(seg(0, A_W), seg(A_W, 2 * A_W), ag_ref[...], aw_ref, ab_ref[...], oa_ref, vn_ref)
    c, s1, s2 = c_ref[...], s1_ref[...], s2_ref[...]
    q0 = 2 * A_W
    k0 = q0 + B_W
    nblk = B_W // LANES
    stage = strided_refs[-1] if dils else None
    for j in range(nblk):
        cols = slice(LANES * j, LANES * (j + 1))
        for a, (base, ref, scale) in enumerate(((q0, q_ref, HEAD_DIM ** -0.5), (k0, k_ref, None))):
            xc = seg(base + LANES * j, base + LANES * (j + 1))
            r = xc * c + pltpu.roll(xc, ROT_DIM // 2, 1) * s1 + pltpu.roll(xc, LANES - ROT_DIM // 2, 1) * s2
            if scale is not None:
                r = r * scale
            ref[:, cols] = r.astype(ref.dtype)
            if dils:
                stage[a * nblk + j] = r
    v0 = k0 + B_W
    v_ref[...] = seg(v0, v0 + B_W)
    c0 = v0 + B_W
    cqkv_ref[...] = seg(c0, c0 + 3 * C_W)
    cz_ref[...] = seg(c0 + 3 * C_W, c0 + 4 * C_W)
    cba_ref[...] = _dot(hb, wsm_ref[...])
    if dils:
        outs = strided_refs[:-1]
        tm = x.shape[0]
        for b in range(nblk):
            stage[2 * nblk + b] = v_ref[:, LANES * b:LANES * (b + 1)]
        for i, d in enumerate(dils):
            for a in range(3):
                dst = outs[3 * i + a]
                for j in range(d):
                    for b in range(nblk):
                        dst[j, :, LANES * b:LANES * (b + 1)] = stage[a * nblk + b,
                                                                     pl.ds(j, tm // d, stride=d), :].astype(dst.dtype)


def _proj(x, g, w_in_b, layer, w_small, tabs, mixer_a, tm, seq_len=None, dils=()):
    n = x.shape[0]
    nt = n // tm
    ntab = tabs[0].shape[0] // tm
    row = lambda i: (i, 0)
    fixed = lambda i: (0, 0)
    tab = lambda i: (i % ntab, 0)
    widths = (A_W, A_W, B_W, B_W, B_W, 3 * C_W, C_W, LANES)
    out_specs = [pl.BlockSpec((tm, w), row) for w in widths]
    out_shape = [jax.ShapeDtypeStruct((n, w), BF16 if i == 2 else F32) for i, w in enumerate(widths)]
    for d in dils:
        tps = seq_len // tm
        out_specs += [pl.BlockSpec((None, d, tm // d, B_W), lambda i, tps=tps: (i // tps, 0, i % tps, 0))] * 3
        out_shape += [jax.ShapeDtypeStruct((n // seq_len, d, seq_len // d, B_W), BF16)] * 3
    return pl.pallas_call(
        functools.partial(_proj_kernel, dils=dils),
        grid=(nt,),
        in_specs=[pl.BlockSpec((tm, D_MODEL), row), pl.BlockSpec((1, D_MODEL), fixed),
                  pl.BlockSpec((None, D_MODEL, PROJ_MAIN), lambda i: (layer, 0, 0)),
                  pl.BlockSpec((D_MODEL, LANES), fixed),
                  pl.BlockSpec((tm, LANES), tab), pl.BlockSpec((tm, LANES), tab), pl.BlockSpec((tm, LANES), tab),
                  pl.BlockSpec((1, A_W), fixed), pl.BlockSpec((A_HEADS, CHUNK, CHUNK), lambda i: (0, 0, 0)),
                  pl.BlockSpec((CHUNK, A_W), fixed)],
        out_specs=out_specs,
        out_shape=out_shape,
        scratch_shapes=[pltpu.VMEM((3 * B_W // LANES, tm, LANES), F32)] if dils else [],
        compiler_params=_cparams(("parallel",)),
        name="proj",
    )(x, g, w_in_b, w_small, *tabs, *mixer_a)


def _rope_tables(pos):
    half = ROT_DIM // 2
    inv_freq = jnp.power(ROPE_THETA, -jnp.arange(0, ROT_DIM, 2, dtype=F32) / ROT_DIM)
    ang = pos.astype(F32)[:, None] * inv_freq[None, :]
    cos, sin = jnp.cos(ang), jnp.sin(ang)
    p = pos.shape[0]
    z8 = jnp.zeros((p, half), F32)
    rest0 = jnp.zeros((p, HEAD_DIM - ROT_DIM), F32)
    c64 = jnp.concatenate([cos, cos, jnp.ones((p, HEAD_DIM - ROT_DIM), F32)], axis=-1)
    s1 = jnp.concatenate([z8, sin, rest0], axis=-1)
    s2 = jnp.concatenate([-sin, z8, rest0], axis=-1)
    two = lambda a: jnp.concatenate([a, a], axis=-1)
    return two(c64), two(s1), two(s2)


def _win_attn_kernel(q_ref, kp_ref, kc_ref, vp_ref, vc_ref, o_ref, lse_ref, *, seg_blocks):
    s = pl.program_id(0)
    rows = lax.broadcasted_iota(jnp.int32, (WBLK, 2 * WBLK), 0)
    cols = lax.broadcasted_iota(jnp.int32, (WBLK, 2 * WBLK), 1)
    dist = rows + WBLK - cols
    band = (dist >= 0) & (dist <= WBLK)
    lo = jnp.where((2 * s) % seg_blocks == 0, WBLK, 0)
    biases = [jnp.where(band & (cols >= lo), 0.0, NEG), jnp.where(band, 0.0, NEG)]
    lane = lax.broadcasted_iota(jnp.int32, (WBLK, LANES), 1)
    k3 = jnp.concatenate([kp_ref[...], kc_ref[...]], axis=0).astype(BF16)
    v3 = jnp.concatenate([vp_ref[...], vc_ref[...]], axis=0).astype(BF16)
    heads = range(B_HEADS)
    sls = [slice(h // 2 * LANES, (h // 2 + 1) * LANES) for h in heads]
    hms = [(lane < HEAD_DIM) if h % 2 == 0 else (lane >= HEAD_DIM) for h in heads]
    work = [(j, h) for j in range(2) for h in heads]
    qs = [q_ref[j * WBLK:(j + 1) * WBLK, :] for j in range(2)]
    ks = [k3[j * WBLK:(j + 2) * WBLK, :] for j in range(2)]
    vs = [v3[j * WBLK:(j + 2) * WBLK, :] for j in range(2)]
    scs = [_dot_nt(jnp.where(hms[h], qs[j][:, sls[h]], 0.0).astype(BF16), ks[j][:, sls[h]]) + biases[j]
           for j, h in work]
    ms = [jnp.max(sc, axis=-1, keepdims=True) for sc in scs]
    es = [jnp.exp(sc - m) for sc, m in zip(scs, ms)]
    dens = [jnp.sum(e, axis=-1, keepdims=True) for e in es]
    os_ = [_dot((e * (1.0 / d)).astype(BF16), vs[j][:, sls[h]]) for (j, h), e, d in zip(work, es, dens)]
    for j in range(2):
        base = j * B_HEADS
        lse_tile = jnp.zeros((WBLK, LANES), F32)
        for h in heads:
            lse_tile = jnp.where(lane == h, ms[base + h] + jnp.log(dens[base + h]), lse_tile)
        lse_ref[j * WBLK:(j + 1) * WBLK, :] = lse_tile
        for hp in range(B_W // LANES):
            o_ref[j * WBLK:(j + 1) * WBLK, sls[2 * hp]] = jnp.where(hms[2 * hp], os_[base + 2 * hp],
                                                                  os_[base + 2 * hp + 1])


def _win_attn(qd, kd, vd, seg_blocks):
    n = qd.shape[0]
    assert seg_blocks % 2 == 0
    cur = lambda s: (s, 0)
    prev = lambda s: (jnp.maximum(2 * s - 1, 0), 0)
    blk = lambda w, im: pl.BlockSpec((2 * WBLK, w), im)
    pblk = pl.BlockSpec((WBLK, B_W), prev)
    return pl.pallas_call(
        functools.partial(_win_attn_kernel, seg_blocks=seg_blocks),
        grid=(n // (2 * WBLK),),
        in_specs=[blk(B_W, cur), pblk, blk(B_W, cur), pblk, blk(B_W, cur)],
        out_specs=[blk(B_W, cur), blk(LANES, cur)],
        out_shape=[jax.ShapeDtypeStruct((n, B_W), F32), jax.ShapeDtypeStruct((n, LANES), F32)],
        compiler_params=_cparams(("parallel",)),
        name="win_attn",
    )(qd, kd, kd, vd, vd)


def _window_rows_kernel(k_ref, v_ref, kprev_ref, vprev_ref, ko_ref, vo_ref):
    del kprev_ref, vprev_ref
    for b in range(k_ref.shape[0]):
        ko_ref[b] = k_ref[b].T
        vo_ref[b] = v_ref[b].T


def _window_rows(k, v, k_all, v_all, layer, batch, t, keep):
    first = (t - keep) // WBLK
    src = pl.BlockSpec((batch, WBLK, B_W), lambda i: (0, first + i, 0))
    dst = pl.BlockSpec((None, batch, B_W, WBLK), lambda i: (layer, 0, 0, i))
    hbm = pl.BlockSpec(memory_space=pl.ANY)
    shape = jax.ShapeDtypeStruct(k_all.shape, F32)
    return pl.pallas_call(
        _window_rows_kernel,
        grid=(keep // WBLK,),
        in_specs=[src, src, hbm, hbm],
        out_specs=[dst, dst],
        out_shape=[shape, shape],
        input_output_aliases={2: 0, 3: 1},
        compiler_params=_cparams(("parallel",)),
        name="window_rows",
    )(k.reshape(batch, t, B_W), v.reshape(batch, t, B_W), k_all, v_all)


def _dec_attn_kernel(q_ref, kc_ref, vc_ref, kn_ref, vn_ref, o_ref, *, t_new, cache_len):
    rows_c = lax.broadcasted_iota(jnp.int32, (SUBLANES, cache_len), 0)
    cols_c = lax.broadcasted_iota(jnp.int32, (SUBLANES, cache_len), 1)
    dist_c = cache_len + rows_c % t_new - cols_c
    rows_n = lax.broadcasted_iota(jnp.int32, (SUBLANES, SUBLANES), 0)
    cols_n = lax.broadcasted_iota(jnp.int32, (SUBLANES, SUBLANES), 1)
    dist_n = rows_n % t_new - cols_n
    biases = []
    for window, dil in PATTERNS:
        vc_ok = (dist_c <= window) & ((dist_c & (dil - 1)) == 0)
        vn_ok = (dist_n >= 0) & ((dist_n & (dil - 1)) == 0)
        biases.append((jnp.where(vc_ok, 0.0, NEG), jnp.where(vn_ok, 0.0, NEG)))
    row8 = lax.broadcasted_iota(jnp.int32, (SUBLANES, LANES), 0)
    lane8 = lax.broadcasted_iota(jnp.int32, (SUBLANES, LANES), 1)
    own = (lane8 < HEAD_DIM) == (row8 < t_new)
    q = q_ref[...].astype(F32)
    outs = []
    for hp in range(B_W // LANES):
        sl = slice(hp * LANES, (hp + 1) * LANES)
        qq = jnp.concatenate([q[0:t_new, sl], q[0:t_new, sl]], axis=0)
        q8 = jnp.where(own, qq, 0.0).astype(BF16)
        kc, vc = kc_ref[sl, :].astype(BF16), vc_ref[sl, :].astype(BF16)
        kn, vn = kn_ref[:, sl].astype(BF16), vn_ref[:, sl].astype(BF16)
        sc_c = _dot(q8, kc)
        sc_n = _dot_nt(q8, kn)
        os_, lses = [], []
        for bc, bn in biases:
            a_c, a_n = sc_c + bc, sc_n + bn
            m = jnp.maximum(jnp.max(a_c, axis=-1, keepdims=True), jnp.max(a_n, axis=-1, keepdims=True))
            e_c, e_n = jnp.exp(a_c - m), jnp.exp(a_n - m)
            den = jnp.sum(e_c, axis=-1, keepdims=True) + jnp.sum(e_n, axis=-1, keepdims=True)
            inv = 1.0 / den
            os_.append(_dot_nt((e_c * inv).astype(BF16), vc) + _dot((e_n * inv).astype(BF16), vn))
            lses.append(m + jnp.log(den))
        mx = jnp.maximum(jnp.maximum(lses[0], lses[1]), lses[2])
        ws = [jnp.exp(l - mx) for l in lses]
        tot = ws[0] + ws[1] + ws[2]
        o8 = (ws[0] / tot) * os_[0] + (ws[1] / tot) * os_[1] + (ws[2] / tot) * os_[2]
        lane4 = lane8[0:t_new]
        outs.append(jnp.where(lane4 < HEAD_DIM, o8[0:t_new], o8[t_new:2 * t_new]))
    o_ref[...] = jnp.zeros((SUBLANES, B_W), F32)
    o_ref[0:t_new, :] = jnp.concatenate(outs, axis=-1)


def _dec_attn(q8, kc, vc, kn8, vn8, t_new, layer):
    _, b, _, cache_len = kc.shape
    assert 2 * t_new == SUBLANES
    small = pl.BlockSpec((None, SUBLANES, B_W), lambda i: (i, 0, 0))
    big = pl.BlockSpec((None, None, B_W, cache_len), lambda i: (layer, i, 0, 0))
    return pl.pallas_call(
        functools.partial(_dec_attn_kernel, t_new=t_new, cache_len=cache_len),
        grid=(b,),
        in_specs=[small, big, big, small, small],
        out_specs=small,
        out_shape=jax.ShapeDtypeStruct((b, SUBLANES, B_W), F32),
        compiler_params=_cparams(("parallel",)),
        name="dec_attn",
    )(q8, kc, vc, kn8, vn8)


def _split3(x):
    hi = x.astype(BF16)
    r1 = x - hi.astype(F32)
    mid = r1.astype(BF16)
    return hi, mid, (r1 - mid.astype(F32)).astype(BF16)


def _pair_dup(xx):
    lane = lax.broadcasted_iota(jnp.int32, xx.shape, 1)
    hi = xx.astype(BF16).astype(F32)
    return jnp.where(lane < HEAD_DIM, xx, xx - hi).astype(BF16)


def _pair(x):
    return _pair_dup(jnp.concatenate([x, x], axis=1))


def _lhs4(pair):
    return jnp.concatenate([pair, pair], axis=1)


def _rhs4(y):
    hi = y.astype(BF16)
    lo = (y - hi.astype(F32)).astype(BF16)
    return jnp.concatenate([hi, hi, lo, lo], axis=0)


def _unit_lower_solves(mats, rhss):
    n = DELTA_CHUNK
    w2 = 2 * HEAD_DIM
    rows = lax.broadcasted_iota(jnp.int32, (n, w2), 0)
    cols = lax.broadcasted_iota(jnp.int32, (n, w2), 1) % HEAD_DIM
    in16 = rows // 16 == cols // 16
    eye = jnp.where(rows == cols, 1.0, 0.0)
    ds = [jnp.where(in16, a, 0.0) for a in mats]
    es = [jnp.where(in16, 0.0, a) for a in mats]
    left = lambda xx: _lhs4(_pair_dup(xx))
    pw = [_dot(left(d), _rhs4(d)) for d in ds]
    ts = [eye - d for d in ds]
    for _ in range(2):
        outs = [_dot(jnp.concatenate([left(t), left(p)], axis=0), _rhs4(p)) for t, p in zip(ts, pw)]
        ts = [t + o[0:n] for t, o in zip(ts, outs)]
        pw = [o[n:2 * n] for o in outs]
    ts = [t + _dot(left(t), _rhs4(p)) for t, p in zip(ts, pw)]
    o5 = [_dot(left(t), _rhs4(jnp.concatenate([r, e], axis=1))) for t, r, e in zip(ts, rhss, es)]
    x0 = [o[:, 0:w2] for o in o5]
    nm = [o[:, w2:2 * w2] for o in o5]
    o6 = [_dot(left(m), _rhs4(jnp.concatenate([x, m], axis=1))) for m, x in zip(nm, x0)]
    ys = [x - o[:, 0:w2] for x, o in zip(x0, o6)]
    return [y + _dot(left(o[:, w2:2 * w2]), _rhs4(y)) for y, o in zip(ys, o6)]


def _gdn_prep_kernel(x_ref, cba_ref, buf_ref, cw_ref, alog_ref, dtb_ref, gmat_ref, xmat_ref,
                     u_ref, wq_ref, ak_ref, gt_ref,
                     xp_scr, q_scr, k_scr, v_scr, gb_scr, *, tm, valid_len, group):
    ti = pl.program_id(1)

    @pl.when(ti == 0)
    def _():
        xp_scr[0:SUBLANES, :] = buf_ref[...]

    x = x_ref[...]
    xp_scr[SUBLANES:SUBLANES + tm, :] = x
    off = SUBLANES - (CONV_W - 1)
    acc = xp_scr[off:off + tm, :] * cw_ref[0:1, :]
    for j in range(1, CONV_W):
        acc = acc + xp_scr[off + j:off + j + tm, :] * cw_ref[j:j + 1, :]
    xp_scr[0:SUBLANES, :] = x[tm - SUBLANES:tm, :]
    y = _silu(acc)
    q, k = y[:, 0:C_W], y[:, C_W:2 * C_W]
    gmat = gmat_ref[...]
    head_sum = lambda a: sum(_dot(p, gmat) for p in _split3(a))
    q_scr[...] = q * lax.rsqrt(head_sum(q * q) + EPS) * (HEAD_DIM ** -0.5)
    k_scr[...] = k * lax.rsqrt(head_sum(k * k) + EPS)
    v_scr[...] = y[:, 2 * C_W:3 * C_W]

    cba = cba_ref[...]
    lane = lax.broadcasted_iota(jnp.int32, (tm, LANES), 1)
    tpos = ti * tm + lax.broadcasted_iota(jnp.int32, (tm, LANES), 0)
    live = tpos < valid_len
    beta = _sigmoid(cba)
    z = cba + dtb_ref[...]
    softplus = jnp.maximum(z, 0.0) + jnp.log(1.0 + jnp.exp(-jnp.abs(z)))
    g = -jnp.exp(alog_ref[...]) * softplus
    is_g = (lane >= C_HEADS) & (lane < 2 * C_HEADS)
    gb_scr[...] = jnp.where(live, jnp.where(is_g, g, jnp.where(lane < C_HEADS, beta, 0.0)), 0.0)

    n = DELTA_CHUNK
    rows = lax.broadcasted_iota(jnp.int32, (n, 2 * n), 0)
    cols = lax.broadcasted_iota(jnp.int32, (n, 2 * n), 1) % n
    incl = rows >= cols
    strict = rows > cols
    ltri = jnp.where(incl[:, 0:n], 1.0, 0.0).astype(BF16)
    heads = range(C_HEADS)
    tiles = range(C_W // LANES)
    tsl = [slice(t * LANES, (t + 1) * LANES) for t in tiles]
    lane_c = lax.broadcasted_iota(jnp.int32, (n, LANES), 1)
    low = lane_c < HEAD_DIM
    own = [low if h % 2 == 0 else jnp.logical_not(low) for h in heads]
    xmat = xmat_ref[...]
    wide0 = C_HEADS * LANES

    def head_pairs(tile):
        rolled = pltpu.roll(tile, HEAD_DIM, 1)
        lo = rolled - rolled.astype(BF16).astype(F32)
        lane = lax.broadcasted_iota(jnp.int32, tile.shape, 1) < HEAD_DIM
        return jnp.where(lane, tile, lo).astype(BF16), jnp.where(lane, lo, tile).astype(BF16)

    def setup(c):
        rs = pl.ds(pl.multiple_of(c * n, n), n)
        gb = gb_scr[rs, :]
        g_only = jnp.where((lane_c >= C_HEADS) & (lane_c < 2 * C_HEADS), gb, 0.0)
        cg = sum(_dot(ltri, p) for p in _split3(g_only))
        cgt = cg.T
        cgl = cg[n - 1:n, :]
        gt_ref[pl.ds(pl.multiple_of(c * SUBLANES, SUBLANES), SUBLANES), :] = jnp.broadcast_to(
            jnp.exp(cgl), (SUBLANES, LANES))
        wide = sum(_dot(p, xmat) for p in _split3(jnp.where(lane_c < C_HEADS, gb, cg)))
        b128 = [wide[:, h * LANES:(h + 1) * LANES] for h in heads]
        c128 = [wide[:, wide0 + h * LANES:wide0 + (h + 1) * LANES] for h in heads]
        cgr = [jnp.concatenate([cgt[C_HEADS + h:C_HEADS + h + 1, :]] * 2, axis=1) for h in heads]
        decay = [jnp.where(incl, jnp.exp(jnp.minimum(c128[h] - cgr[h], 0.0)), 0.0) for h in heads]
        amat, at_pair, rhs, qd_pair, kdt_pair = [], [], [], [], []
        for t in tiles:
            he, ho = 2 * t, 2 * t + 1
            qn, kn, vv = q_scr[rs, tsl[t]], k_scr[rs, tsl[t]], v_scr[rs, tsl[t]]
            bnat = jnp.where(low, b128[he], b128[ho])
            cnat = jnp.where(low, c128[he], c128[ho])
            ecg = jnp.exp(cnat)
            kb = kn.astype(BF16)
            kk2 = jnp.concatenate([kb, kb], axis=0)
            for h in (he, ho):
                masked = jnp.concatenate([jnp.where(own[h], kn, 0.0), jnp.where(own[h], qn, 0.0)], axis=0)
                kkqk = _dot_nt(masked.astype(BF16), kk2)
                amat.append(jnp.where(strict, b128[h] * kkqk[0:n] * decay[h], 0.0))
                at_pair.append(_pair_dup(kkqk[n:2 * n] * decay[h]))
            bv = bnat * vv
            bk_rolled = pltpu.roll((bnat * ecg) * kn, HEAD_DIM, 1)
            rhs += [jnp.where(low, bv, bk_rolled), jnp.where(low, bk_rolled, bv)]
            qd_pair += list(head_pairs(qn * ecg))
            kd = kn * jnp.exp(cnat[n - 1:n, :] - cnat)
            kdt = jnp.concatenate([kd, kd], axis=0).T
            kdt_pair += [_pair_dup(kdt[0:n]), _pair_dup(kdt[n:2 * n])]
        return amat, rhs, at_pair, qd_pair, kdt_pair

    def finish(c, sol, at_pair, qd_pair, kdt_pair):
        w_pair = []
        for t in tiles:
            se, so = sol[2 * t], sol[2 * t + 1]
            u_ref[pl.ds(pl.multiple_of(c * n, n), n), tsl[t]] = jnp.where(low, se, so)
            w_odd, w_even = head_pairs(jnp.where(low, so, se))
            w_pair += [w_even, w_odd]
        r2 = pl.ds(pl.multiple_of(c * 2 * n, 2 * n), 2 * n)
        wq_ref[r2, :] = jnp.concatenate(
            [jnp.concatenate([w_pair[h], qd_pair[h]], axis=0) for h in heads], axis=1)
        ak_ref[r2, :] = jnp.concatenate(
            [jnp.concatenate([at_pair[h], kdt_pair[h]], axis=0) for h in heads], axis=1)

    def chunks(i, carry):
        parts = [setup(i * group + j) for j in range(group)]
        sol = _unit_lower_solves([a for p in parts for a in p[0]], [r for p in parts for r in p[1]])
        for j, p in enumerate(parts):
            finish(i * group + j, sol[j * C_HEADS:(j + 1) * C_HEADS], *p[2:])
        return carry

    lax.fori_loop(0, tm // n // group, chunks, 0)


def _gdn_constants():
    hid = jnp.arange(C_W, dtype=jnp.int32) // HEAD_DIM
    gmat = (hid[:, None] == hid[None, :]).astype(BF16)
    src = jnp.arange(LANES, dtype=jnp.int32)[:, None]
    dst = jnp.arange(2 * C_HEADS * LANES, dtype=jnp.int32)[None, :] // LANES
    return gmat, (src == dst).astype(BF16)


def _gdn_prep(cqkv, cba, buf8, conv_w, alog_row, dtb_row, consts, batch, t, tm, valid_len):
    gmat, xmat = consts
    n = batch * t
    nt = t // tm
    row = lambda b, i: (b * nt + i, 0)
    fixed = lambda b, i: (0, 0)
    nch = tm // DELTA_CHUNK
    pair_w = C_HEADS * 2 * HEAD_DIM
    outs = [jax.ShapeDtypeStruct((n, C_W), F32), jax.ShapeDtypeStruct((2 * n, pair_w), BF16),
            jax.ShapeDtypeStruct((2 * n, pair_w), BF16),
            jax.ShapeDtypeStruct((n // DELTA_CHUNK * SUBLANES, LANES), F32)]
    return pl.pallas_call(
        functools.partial(_gdn_prep_kernel, tm=tm, valid_len=valid_len, group=min(GDN_GROUP, nch)),
        grid=(batch, nt),
        in_specs=[pl.BlockSpec((tm, 3 * C_W), row), pl.BlockSpec((tm, LANES), row),
                  pl.BlockSpec((None, SUBLANES, 3 * C_W), lambda b, i: (b, 0, 0)),
                  pl.BlockSpec((CONV_W, 3 * C_W), fixed), pl.BlockSpec((1, LANES), fixed),
                  pl.BlockSpec((1, LANES), fixed), pl.BlockSpec((C_W, C_W), fixed),
                  pl.BlockSpec(xmat.shape, fixed)],
        out_specs=[pl.BlockSpec((tm, C_W), row), pl.BlockSpec((2 * tm, pair_w), row),
                   pl.BlockSpec((2 * tm, pair_w), row), pl.BlockSpec((nch * SUBLANES, LANES), row)],
        out_shape=outs,
        scratch_shapes=[pltpu.VMEM((tm + SUBLANES, 3 * C_W), F32), pltpu.VMEM((tm, C_W), F32),
                        pltpu.VMEM((tm, C_W), F32), pltpu.VMEM((tm, C_W), F32), pltpu.VMEM((tm, LANES), F32)],
        compiler_params=_cparams(("parallel", "arbitrary")),
        name="gdn_prep",
    )(cqkv, cba, buf8, conv_w, alog_row, dtb_row, gmat, xmat)


def _gdn_scan_kernel(u_ref, wq_ref, ak_ref, gt_ref, z_ref, s0_ref, gn_ref,
                     o_ref, sfin_ref, s_scr, *, bg):
    c = pl.program_id(1)

    @pl.when(c == 0)
    def _():
        s_scr[...] = s0_ref[...]

    gn = gn_ref[...]
    n = DELTA_CHUNK
    chains = [(b, h, slice(h * HEAD_DIM, (h + 1) * HEAD_DIM), slice(h * 2 * HEAD_DIM, (h + 1) * 2 * HEAD_DIM))
              for b in range(bg) for h in range(C_HEADS)]
    st = [s_scr[b, h] for b, h, _, _ in chains]
    r1 = [_dot(_lhs4(wq_ref[b, :, ps]), _rhs4(s)) for (b, _, _, ps), s in zip(chains, st)]
    up = [u_ref[b, :, sl] - r[0:n] for (b, _, sl, _), r in zip(chains, r1)]
    r2 = [_dot(_lhs4(ak_ref[b, :, ps]), _rhs4(x)) for (b, _, _, ps), x in zip(chains, up)]
    for (b, h, _, _), s, r in zip(chains, st, r2):
        s_scr[b, h] = gt_ref[b, 0:1, C_HEADS + h:C_HEADS + h + 1] * s + r[n:2 * n]
    os_ = [a[n:2 * n] + r[0:n] for a, r in zip(r1, r2)]
    outs = [o * lax.rsqrt(jnp.mean(o * o, axis=-1, keepdims=True) + EPS) * gn * _silu(z_ref[b, :, sl])
            for (b, _, sl, _), o in zip(chains, os_)]
    for b in range(bg):
        o_ref[b] = jnp.concatenate(outs[b * C_HEADS:(b + 1) * C_HEADS], axis=1)

    @pl.when(c == pl.num_programs(1) - 1)
    def _():
        sfin_ref[...] = s_scr[...]


def _gdn_scan(u, wq, ak, gt, z, s0, gnorm, batch, t, bg):
    n = DELTA_CHUNK
    nc = t // n
    pair_w = C_HEADS * 2 * HEAD_DIM
    v3 = lambda a: a.reshape(batch, t, C_W)
    p3 = lambda a: a.reshape(batch, 2 * t, pair_w)
    tok = pl.BlockSpec((bg, n, C_W), lambda b, c: (b, c, 0))
    pair = pl.BlockSpec((bg, 2 * n, pair_w), lambda b, c: (b, c, 0))
    st = pl.BlockSpec((bg, C_HEADS, HEAD_DIM, HEAD_DIM), lambda b, c: (b, 0, 0, 0))
    return pl.pallas_call(
        functools.partial(_gdn_scan_kernel, bg=bg),
        grid=(batch // bg, nc),
        in_specs=[tok, pair, pair,
                  pl.BlockSpec((bg, SUBLANES, LANES), lambda b, c: (b, c, 0)),
                  tok, st, pl.BlockSpec((1, HEAD_DIM), lambda b, c: (0, 0))],
        out_specs=[tok, st],
        out_shape=[jax.ShapeDtypeStruct((batch, t, C_W), F32),
                   jax.ShapeDtypeStruct((batch, C_HEADS, HEAD_DIM, HEAD_DIM), F32)],
        scratch_shapes=[pltpu.VMEM((bg, C_HEADS, HEAD_DIM, HEAD_DIM), F32)],
        compiler_params=_cparams(("parallel", "arbitrary")),
        name="gdn_scan",
    )(v3(u), p3(wq), p3(ak), gt.reshape(batch, nc * SUBLANES, LANES), v3(z), s0, gnorm)


def _head_expand(wt):
    src = lax.broadcasted_iota(jnp.int32, (LANES, B_W), 0)
    dst = lax.broadcasted_iota(jnp.int32, (LANES, B_W), 1) // HEAD_DIM
    spread = jnp.where(src == dst, 1.0, 0.0).astype(BF16)
    return sum(_dot(p, spread) for p in _split3(wt))


def _out_proj_kernel(*refs, dils):
    if dils:
        npat = len(dils)
        a_ref = refs[0]
        o_refs, l_refs = refs[1:1 + npat], refs[1 + npat:1 + 2 * npat]
        c_ref, x_ref, w_ref, g2_ref, wr_ref, y_ref, dense_ref, o_scr, l_scr = refs[1 + 2 * npat:]
        tm = x_ref.shape[0]
        os_, ls = [], []
        for i, d in enumerate(dils):
            if d == 1:
                os_.append(o_refs[i][0])
                ls.append(l_refs[i][0])
            else:
                nblk = B_W // LANES
                for j in range(d):
                    rows = pl.ds(j, tm // d, stride=d)
                    for b in range(nblk):
                        o_scr[i * nblk + b, rows, :] = o_refs[i][j, :, LANES * b:LANES * (b + 1)]
                    l_scr[i, rows, :] = l_refs[i][j]
                os_.append(jnp.concatenate([o_scr[i * nblk + b] for b in range(nblk)], axis=-1))
                ls.append(l_scr[i])
        mx = functools.reduce(jnp.maximum, ls)
        es = [jnp.exp(l - mx) for l in ls]
        tot = functools.reduce(lambda a, b: a + b, es)
        ob = functools.reduce(lambda a, b: a + b, [_head_expand(e / tot) * o for e, o in zip(es, os_)])
    else:
        a_ref, b_ref, c_ref, x_ref, w_ref, g2_ref, wr_ref, y_ref, dense_ref = refs
        ob = b_ref[...]
    cat = jnp.concatenate([a_ref[...], ob, c_ref[...]], axis=-1).astype(BF16)
    y = x_ref[...] + _dot(cat, w_ref[...])
    y_ref[...] = y
    dense_ref[...] = _route_tile(y, g2_ref[...], wr_ref[...])


def _out_proj(out_a, out_b, lses, out_c, x, w_out, g2, w_route, tm, seq_len=None, dils=()):
    n = x.shape[0]
    row = lambda i: (i, 0)
    spec = lambda w: pl.BlockSpec((tm, w), row)
    scratch = []
    if dils:
        tps = seq_len // tm
        strided = lambda d, w: pl.BlockSpec((None, d, tm // d, w), lambda i: (i // tps, 0, i % tps, 0))
        ins = [out_a, *out_b, *lses, out_c, x, w_out]
        specs = ([spec(A_W)] + [strided(d, B_W) for d in dils] + [strided(d, LANES) for d in dils]
                 + [spec(C_W), spec(D_MODEL)])
        scratch = [pltpu.VMEM((len(dils) * B_W // LANES, tm, LANES), F32), pltpu.VMEM((len(dils), tm, LANES), F32)]
    else:
        ins = [out_a, out_b, out_c, x, w_out]
        specs = [spec(A_W), spec(B_W), spec(C_W), spec(D_MODEL)]
    fixed = lambda i: (0, 0)
    specs += [pl.BlockSpec((D_MODEL, D_MODEL), fixed), pl.BlockSpec((1, D_MODEL), fixed),
              pl.BlockSpec((D_MODEL, LANES), fixed)]
    return pl.pallas_call(
        functools.partial(_out_proj_kernel, dils=dils),
        grid=(n // tm,),
        in_specs=specs,
        out_specs=[spec(D_MODEL), spec(LANES)],
        out_shape=[jax.ShapeDtypeStruct((n, D_MODEL), F32), jax.ShapeDtypeStruct((n, LANES), F32)],
        scratch_shapes=scratch,
        compiler_params=_cparams(("parallel",)),
        name="out_proj",
    )(*ins, g2, w_route)


def _route_tile(x, g, w):
    t = x * lax.rsqrt(jnp.mean(x * x, axis=-1, keepdims=True) + EPS) * g
    t_hi = t.astype(BF16)
    t_lo = (t - t_hi.astype(F32)).astype(BF16)
    w_hi = w.astype(BF16)
    w_lo = (w - w_hi.astype(F32)).astype(BF16)
    lg = _dot(t_hi, w_hi) + (_dot(t_hi, w_lo) + _dot(t_lo, w_hi))
    tm = lg.shape[0]
    nrow = ROUTE_OFF + N_EXPERTS + ROUTE_OFF
    lt = lg.T[0:nrow, :]
    row = lax.broadcasted_iota(jnp.int32, (nrow, tm), 0).astype(F32)
    big = float(LANES)
    down = lambda op, a: op(a, axis=0, keepdims=True)
    is_grp = row < N_GROUPS
    gl = jnp.where(is_grp, lt, NEG)
    gmax = down(jnp.max, gl)
    gsum = down(jnp.sum, jnp.where(is_grp, jnp.exp(gl - gmax), 0.0))
    g_w = 1.0 / gsum
    g_idx = down(jnp.min, jnp.where(is_grp & (gl == gmax), row, big))
    lo = ROUTE_OFF + EXP_PER_GROUP * g_idx
    sel = (row >= lo) & (row < lo + EXP_PER_GROUP)
    el = jnp.where(sel, lt, NEG)
    m1 = down(jnp.max, el)
    esum = down(jnp.sum, jnp.where(sel, jnp.exp(el - m1), 0.0))
    i1 = down(jnp.min, jnp.where(sel & (el == m1), row, big))
    el2 = jnp.where(row == i1, NEG, el)
    m2 = down(jnp.max, el2)
    i2 = down(jnp.min, jnp.where(sel & (row != i1) & (el2 == m2), row, big))
    p1 = 1.0 / esum
    p2 = jnp.exp(m2 - m1) / esum
    tot = p1 + p2
    gates = jnp.where(row == i1, g_w * (p1 / tot), 0.0) + jnp.where(row == i2, g_w * (p2 / tot), 0.0)
    full = jnp.concatenate([jnp.where(row == 0.0, g_idx, gates), jnp.zeros((LANES - nrow, tm), F32)], axis=0)
    return full.T


def _rms_rows(y, g):
    return y * lax.rsqrt(jnp.mean(y * y, axis=-1, keepdims=True) + EPS) * g


def _moe_kernel(x_ref, g_ref, dense_ref, w1_ref, w3_ref, w2_ref, *rest):
    fg_ref = rest[0] if len(rest) == 3 else None
    y_ref, t_scr = rest[-2:]
    e = pl.program_id(1)

    @pl.when(e == 0)
    def _():
        x = x_ref[...]
        t_scr[...] = (x * lax.rsqrt(jnp.mean(x * x, axis=-1, keepdims=True) + EPS) * g_ref[...]).astype(BF16)
        y_ref[...] = x

    tb = t_scr[...]
    dense = dense_ref[...]
    lane = lax.broadcasted_iota(jnp.int32, dense.shape, 1)
    gate = jnp.sum(jnp.where(lane == e + ROUTE_OFF, dense, 0.0), axis=-1, keepdims=True)
    hid = _silu(_dot(tb, w1_ref[...].astype(BF16))) * _dot(tb, w3_ref[...].astype(BF16))
    y_ref[...] += _dot((hid * gate).astype(BF16), w2_ref[...].astype(BF16))

    if fg_ref is not None:
        @pl.when(e == pl.num_programs(1) - 1)
        def _():
            y_ref[...] = _rms_rows(y_ref[...], fg_ref[...])


def _moe(x, g, dense, w1, w3, w2, layer, tm, final_g=None):
    n = x.shape[0]
    row = lambda i, e: (i, 0)
    expert = lambda i, e: (layer, e, 0, 0)
    extra = [] if final_g is None else [final_g]
    return pl.pallas_call(
        _moe_kernel,
        grid=(n // tm, N_EXPERTS),
        in_specs=[pl.BlockSpec((tm, D_MODEL), row), pl.BlockSpec((1, D_MODEL), lambda i, e: (0, 0)),
                  pl.BlockSpec((tm, LANES), row),
                  pl.BlockSpec((None, None, D_MODEL, D_EXPERT), expert),
                  pl.BlockSpec((None, None, D_MODEL, D_EXPERT), expert),
                  pl.BlockSpec((None, None, D_EXPERT, D_MODEL), expert)]
        + [pl.BlockSpec((1, D_MODEL), lambda i, e: (0, 0))] * len(extra),
        out_specs=pl.BlockSpec((tm, D_MODEL), row),
        out_shape=jax.ShapeDtypeStruct((n, D_MODEL), F32),
        scratch_shapes=[pltpu.VMEM((tm, D_MODEL), BF16)],
        compiler_params=_cparams(("parallel", "arbitrary")),
        name="moe",
    )(x, g, dense, w1, w3, w2, *extra)


def _moe_grouped_kernel(x_ref, g_ref, dense_ref, gidr_ref, tri_ref, w1_ref, w3_ref, w2_ref, *rest,
                        tm, slots, final):
    fg_ref = rest[0] if final else None
    y_ref, ts_scr, gs_scr, ys_scr, pt_scr, meta = rest[-6:]
    e = pl.program_id(1)
    blk = MOE_BLOCK

    @pl.when(e == 0)
    def _():
        x = x_ref[...]
        t = (x * lax.rsqrt(jnp.mean(x * x, axis=-1, keepdims=True) + EPS) * g_ref[...]).astype(BF16)
        dense = dense_ref[...]
        tri = tri_ref[...]
        lane = lax.broadcasted_iota(jnp.int32, (tm, LANES), 1)
        ohc = jnp.where(lane.astype(F32) == dense[:, 0:1], 1.0, 0.0)
        rankc = _dot(tri, ohc.astype(BF16))
        sub = lax.broadcasted_iota(jnp.int32, (SUBLANES, tm), 0)
        ohr = jnp.where(sub.astype(F32) == gidr_ref[...], 1.0, 0.0)
        rankr = _dot_nt(ohr.astype(BF16), tri)
        start = jnp.int32(0)
        s_lane = jnp.zeros((tm, LANES), F32)
        s_sub = jnp.zeros((SUBLANES, tm), F32)
        for g in range(N_GROUPS):
            count = jnp.sum(jnp.where(lane == g, ohc, 0.0)).astype(jnp.int32)
            nblk = (count + blk - 1) // blk
            meta[g] = start
            meta[N_GROUPS + g] = nblk
            s_lane = jnp.where(lane == g, start.astype(F32), s_lane)
            s_sub = jnp.where(sub == g, start.astype(F32), s_sub)
            start = start + nblk * blk
        destc = jnp.sum(ohc * (s_lane + rankc), axis=-1, keepdims=True)
        destr = jnp.sum(ohr * (s_sub + rankr), axis=0, keepdims=True)
        slot_r = lax.broadcasted_iota(jnp.int32, (slots, tm), 0).astype(F32)
        p = jnp.where(slot_r == destr, 1.0, 0.0).astype(BF16)
        ts_scr[...] = _dot(p, t).astype(BF16)
        gs_scr[...] = sum(_dot(p, piece) for piece in _split3(dense))
        slot_c = lax.broadcasted_iota(jnp.int32, (tm, slots), 1).astype(F32)
        pt_scr[...] = jnp.where(slot_c == destc, 1.0, 0.0).astype(BF16)
        ys_scr[...] = jnp.zeros((slots, D_MODEL), F32)

    g = e // EXP_PER_GROUP
    start = meta[g]
    nblk = meta[N_GROUPS + g]
    w1, w3, w2 = w1_ref[...], w3_ref[...], w2_ref[...]

    def rows_update(r0, m):
        rows = pl.ds(pl.multiple_of(r0, blk), m)
        tb = ts_scr[rows, :]
        lane = lax.broadcasted_iota(jnp.int32, (m, LANES), 1)
        gate = jnp.sum(jnp.where(lane == e + ROUTE_OFF, gs_scr[rows, :], 0.0), axis=-1, keepdims=True)
        hid = _silu(_dot(tb, w1)) * _dot(tb, w3)
        ys_scr[rows, :] += _dot((hid * gate).astype(BF16), w2)

    def quad(j, carry):
        rows_update(start + j * 4 * blk, 4 * blk)
        return carry

    lax.fori_loop(0, nblk // 4, quad, 0)
    for rem in (1, 2, 3):
        @pl.when(nblk % 4 == rem)
        def _(rem=rem):
            rows_update(start + (nblk // 4) * 4 * blk, rem * blk)

    @pl.when(e == pl.num_programs(1) - 1)
    def _():
        pt = pt_scr[...]
        ys = ys_scr[...]
        hi = ys.astype(BF16)
        lo = (ys - hi.astype(F32)).astype(BF16)
        y = x_ref[...] + (_dot(pt, hi) + _dot(pt, lo))
        y_ref[...] = _rms_rows(y, fg_ref[...]) if final else y


def _moe_grouped(x, g, dense, w1, w3, w2, layer, tm, final_g=None):
    extra = [] if final_g is None else [final_g]
    n = x.shape[0]
    slots = tm + N_GROUPS * MOE_BLOCK
    row = lambda i, e: (i, 0)
    expert = lambda i, e: (layer, e, 0, 0)
    gid_rows = dense[:, 0].reshape(n // tm, 1, tm)
    idx = jnp.arange(tm, dtype=jnp.int32)
    tri = (idx[None, :] < idx[:, None]).astype(BF16)
    return pl.pallas_call(
        functools.partial(_moe_grouped_kernel, tm=tm, slots=slots, final=final_g is not None),
        grid=(n // tm, N_EXPERTS),
        in_specs=[pl.BlockSpec((tm, D_MODEL), row), pl.BlockSpec((1, D_MODEL), lambda i, e: (0, 0)),
                  pl.BlockSpec((tm, LANES), row),
                  pl.BlockSpec((None, 1, tm), lambda i, e: (i, 0, 0)),
                  pl.BlockSpec((tm, tm), lambda i, e: (0, 0)),
                  pl.BlockSpec((None, None, D_MODEL, D_EXPERT), expert),
                  pl.BlockSpec((None, None, D_MODEL, D_EXPERT), expert),
                  pl.BlockSpec((None, None, D_EXPERT, D_MODEL), expert)]
        + [pl.BlockSpec((1, D_MODEL), lambda i, e: (0, 0))] * len(extra),
        out_specs=pl.BlockSpec((tm, D_MODEL), row),
        out_shape=jax.ShapeDtypeStruct((n, D_MODEL), F32),
        scratch_shapes=[pltpu.VMEM((slots, D_MODEL), BF16), pltpu.VMEM((slots, LANES), F32),
                        pltpu.VMEM((slots, D_MODEL), F32), pltpu.VMEM((tm, slots), BF16),
                        pltpu.SMEM((2 * N_GROUPS,), jnp.int32)],
        compiler_params=_cparams(("parallel", "arbitrary")),
        name="moe_grouped",
    )(x, g, dense, gid_rows, tri, w1, w3, w2, *extra)


def _tile_rows(n, cap):
    tm = min(n, cap)
    assert n % tm == 0
    return tm


def _layer_weights(l, norm1_g, w_in, a_vnorm_g, a_ws, a_bs, c_conv_w, c_a_log, c_dt_bias, c_norm_g,
                   w_out, norm2_g, w_group, w_router, w1, w3, w2):
    pad_l = lambda a, left: jnp.pad(a, ((0, 0), (left, LANES - left - a.shape[-1])))
    return dict(
        norm1_g=norm1_g[l][None, :],
        w_in_b=w_in.astype(BF16),
        w_small=pad_l(w_in[l, :, PROJ_MAIN:], 0).astype(BF16),
        a_gain=a_vnorm_g[l][None, :],
        a_ws=a_ws[l],
        a_bs=a_bs[l],
        conv_w=c_conv_w[l],
        alog_row=pad_l(c_a_log[l][None, :], C_HEADS),
        dtb_row=pad_l(c_dt_bias[l][None, :], C_HEADS),
        gnorm=c_norm_g[l][None, :],
        w_out=w_out[l].astype(BF16),
        norm2_g=norm2_g[l][None, :],
        w_route=pad_l(jnp.concatenate([w_group[l], w_router[l]], axis=-1), 0),
        w1=w1, w3=w3, w2=w2, layer=l,
    )


def _mixer_c(lw, cqkv, cba, cz, conv_buf, s0, gmat, batch, t, tm, valid_len, bg):
    buf8 = jnp.pad(conv_buf, ((0, 0), (SUBLANES - (CONV_W - 1), 0), (0, 0)))
    u, wq, ak, gt = _gdn_prep(cqkv, cba, buf8, lw["conv_w"], lw["alog_row"], lw["dtb_row"], gmat,
                              batch, t, tm, valid_len)
    return _gdn_scan(u, wq, ak, gt, cz, s0, lw["gnorm"], batch, t, bg)


def _ffn(lw, x, dense):
    n = x.shape[0]
    if n % MOE_TILE == 0:
        return _moe_grouped(x, lw["norm2_g"], dense, lw["w1"].astype(BF16), lw["w3"].astype(BF16),
                            lw["w2"].astype(BF16), lw["layer"], MOE_TILE, lw["final_g"])
    return _moe(x, lw["norm2_g"], dense, lw["w1"], lw["w3"], lw["w2"], lw["layer"], n, lw["final_g"])


def _prompt_layer(lw, x, batch, t, tabs, gmat, win_k, win_v, layer):
    n = batch * t
    dils = tuple(d for _, d in PATTERNS)
    tm = _tile_rows(n, 512)
    mixer_a = (lw["a_gain"], lw["a_ws"], jnp.repeat(lw["a_bs"].T, HEAD_DIM, axis=1))
    out_a, _, q, k, v, cqkv, cz, cba, *strided = _proj(x, lw["norm1_g"], lw["w_in_b"], layer, lw["w_small"], tabs,
                                                       mixer_a, tm, seq_len=t, dils=dils[1:])

    qkv = [(q, k, v)] + [tuple(a.reshape(n, B_W) for a in strided[3 * i:3 * i + 3]) for i in range(len(dils) - 1)]
    outs, lses = [], []
    for d, (qd, kd, vd) in zip(dils, qkv):
        o_d, lse_d = _win_attn(qd, kd, vd, t // d // WBLK)
        outs.append(o_d.reshape(batch, d, t // d, B_W))
        lses.append(lse_d.reshape(batch, d, t // d, LANES))

    zeros_buf = jnp.zeros((batch, CONV_W - 1, 3 * C_W), F32)
    zeros_s = jnp.zeros((batch, C_HEADS, HEAD_DIM, HEAD_DIM), F32)
    out_c, s_new = _mixer_c(lw, cqkv, cba, cz, zeros_buf, zeros_s, gmat, batch, t, 256, t, batch)
    out_c = out_c.reshape(n, C_W)

    x, dense = _out_proj(out_a, outs, lses, out_c, x, lw["w_out"], lw["norm2_g"], lw["w_route"], tm,
                         seq_len=t, dils=dils)
    x = _ffn(lw, x, dense)

    win_k, win_v = _window_rows(k, v, win_k, win_v, layer, batch, t, win_k.shape[-1])
    conv_state = cqkv.reshape(batch, t, 3 * C_W)[:, t - (CONV_W - 1):]
    return x, win_k, win_v, (conv_state, s_new)


def _sample_layer(lw, x, batch, t, tabs, gmat, kbuf, vbuf, conv_buf, s0, layer):
    n = batch * t
    eye = jnp.eye(batch, dtype=F32)
    ws_bd = jnp.stack([jnp.kron(eye, lw["a_ws"][h, :t, :t]) for h in range(A_HEADS)])
    bias_tile = jnp.tile(jnp.repeat(lw["a_bs"][:, :t].T, HEAD_DIM, axis=1), (batch, 1))
    out_a, a_rows, q, k, v, cqkv, cz, cba = _proj(x, lw["norm1_g"], lw["w_in_b"], layer, lw["w_small"], tabs,
                                                  (lw["a_gain"], ws_bd, bias_tile), n)

    pad8 = lambda a: jnp.pad(a.reshape(batch, t, B_W), ((0, 0), (0, SUBLANES - t), (0, 0)))
    out_b = _dec_attn(pad8(q), kbuf, vbuf, pad8(k), pad8(v), t, layer)[:, :t].reshape(n, B_W)

    tp = DELTA_CHUNK
    padt = lambda a: jnp.pad(a.reshape(batch, t, -1), ((0, 0), (0, tp - t), (0, 0))).reshape(batch * tp, -1)
    out_c, s_new = _mixer_c(lw, padt(cqkv), padt(cba), padt(cz), conv_buf, s0, gmat, batch, tp, tp, t, 4)
    out_c = out_c[:, :t].reshape(n, C_W)

    x, dense = _out_proj(out_a, out_b, None, out_c, x, lw["w_out"], lw["norm2_g"], lw["w_route"], n)
    x = _ffn(lw, x, dense)

    heads = lambda a: a.reshape(batch, t, B_HEADS, HEAD_DIM)
    conv_state = jnp.concatenate([conv_buf, cqkv.reshape(batch, t, 3 * C_W)], axis=1)[:, -(CONV_W - 1):]
    return x, (heads(k), heads(v), a_rows.reshape(batch, t, A_W), conv_state, s_new)


def kernel(x_prompt, x_sample, cache_win_k, cache_win_v, state_conv, state_delta, norm1_g, w_in, a_vnorm_g, a_ws, a_bs, c_conv_w, c_a_log, c_dt_bias, c_norm_g, w_out, norm2_g, w_group, w_router, w1, w3, w2, final_g):
    bp, tp, _ = x_prompt.shape
    bs, ts, _ = x_sample.shape
    depth = w_in.shape[0]
    assert tp % (PATTERNS[-1][1] * WBLK) == 0 and tp % 512 == 0 and bs * ts == CHUNK

    tabs_p = _rope_tables(jnp.arange(tp, dtype=jnp.int32))
    tabs_s = tuple(jnp.tile(a, (bs, 1)) for a in _rope_tables(PAST_LEN + jnp.arange(ts, dtype=jnp.int32)))
    gmat = _gdn_constants()

    feat_major = lambda c: jnp.transpose(c, (0, 1, 3, 4, 2)).reshape(depth, bs, B_W, c.shape[2])
    cache_k, cache_v = feat_major(cache_win_k), feat_major(cache_win_v)
    keep = min(MAX_WINDOW, tp)
    win_k = jnp.zeros((depth, bp, B_W, keep), F32)
    win_v = jnp.zeros((depth, bp, B_W, keep), F32)

    xp = x_prompt.reshape(bp * tp, D_MODEL)
    xs = x_sample.reshape(bs * ts, D_MODEL)
    p_out = [[] for _ in range(2)]
    s_out = [[] for _ in range(5)]
    for l in range(depth):
        lw = _layer_weights(l, norm1_g, w_in, a_vnorm_g, a_ws, a_bs, c_conv_w, c_a_log, c_dt_bias, c_norm_g,
                            w_out, norm2_g, w_group, w_router, w1, w3, w2)
        lw["final_g"] = final_g[None, :] if l == depth - 1 else None
        xp, win_k, win_v, st = _prompt_layer(lw, xp, bp, tp, tabs_p, gmat, win_k, win_v, l)
        for acc, a in zip(p_out, st):
            acc.append(a)
        xs, st = _sample_layer(lw, xs, bs, ts, tabs_s, gmat, cache_k, cache_v, state_conv[l], state_delta[l], l)
        for acc, a in zip(s_out, st):
            acc.append(a)
    y_prompt = xp.reshape(bp, tp, D_MODEL)
    y_sample = xs.reshape(bs, ts, D_MODEL)
    rows_major = lambda w: jnp.transpose(w.reshape(depth, bp, B_HEADS, HEAD_DIM, keep), (0, 1, 4, 2, 3))
    return (y_prompt, y_sample, rows_major(win_k), rows_major(win_v), *[jnp.stack(a) for a in p_out],
            *[jnp.stack(a) for a in s_out])
```

```python
import functools
import math

import jax
import jax.numpy as jnp
from jax import lax
from jax.experimental import pallas as pl
from jax.experimental.pallas import tpu as pltpu

F32 = jnp.float32
BF16 = jnp.bfloat16
HI = lax.Precision.HIGHEST

D_MODEL = 1024
HEAD_DIM = 64
A_HEADS = 4
B_HEADS = 6
C_HEADS = 6
A_W = A_HEADS * HEAD_DIM
B_W = B_HEADS * HEAD_DIM
C_W = C_HEADS * HEAD_DIM
CHUNK = 128
PATTERNS = ((128, 1), (512, 4), (2048, 16))
MAX_WINDOW = 2048
ROT_DIM = HEAD_DIM // 4
ROPE_THETA = 500000.0
CONV_W = 4
DELTA_CHUNK = 64
N_GROUPS = 4
EXP_PER_GROUP = 8
N_EXPERTS = N_GROUPS * EXP_PER_GROUP
D_EXPERT = 256
EPS = 1e-6
PAST_LEN = 16384

LANES = 128
SUBLANES = 8
WBLK = 128
PROJ_MAIN = 2 * A_W + 3 * B_W + 4 * C_W
NEG = -1e30
ROUTE_OFF = N_GROUPS
VMEM_LIMIT = 56 * 1024 * 1024
MOE_TILE = 1024
MOE_STEP_EXPERTS = 2
MOE_BLOCK = 128
GDN_GROUP = 4


def _cparams(sem):
    return pltpu.CompilerParams(dimension_semantics=sem, vmem_limit_bytes=VMEM_LIMIT)


def _sigmoid(x):
    return 1.0 / (1.0 + jnp.exp(-x))


def _silu(x):
    return x * _sigmoid(x)


def _dot(a, b, precision=None):
    return jnp.dot(a, b, preferred_element_type=F32, precision=precision)


def _dot_nt(a, b, precision=None):
    return lax.dot_general(a, b, (((1,), (1,)), ((), ())), preferred_element_type=F32, precision=precision)


def _chunk_mlp_rows(u, v, gain, w_ref, bias, o_ref, vn_ref):
    xc = v - jnp.mean(v, axis=-1, keepdims=True)
    vn = xc * lax.rsqrt(jnp.mean(xc * xc, axis=-1, keepdims=True) + EPS) * gain
    vn_ref[...] = vn
    vb = vn.astype(BF16)
    rows = lax.broadcasted_iota(jnp.int32, (CHUNK, CHUNK), 0)
    cols = lax.broadcasted_iota(jnp.int32, (CHUNK, CHUNK), 1)
    tril = rows >= cols
    ws = [jnp.where(tril, w_ref[h], 0.0).astype(BF16) for h in range(A_HEADS)]
    for c in range(v.shape[0] // CHUNK):
        rs = slice(c * CHUNK, (c + 1) * CHUNK)
        parts = [_dot(ws[h], vb[rs, h * HEAD_DIM:(h + 1) * HEAD_DIM]) for h in range(A_HEADS)]
        o_ref[rs, :] = u[rs, :] * (jnp.concatenate(parts, axis=-1) + bias)


def _proj_kernel(x_ref, g_ref, w_ref, wsm_ref, c_ref, s1_ref, s2_ref, ag_ref, aw_ref, ab_ref,
                 oa_ref, vn_ref, q_ref, k_ref, v_ref, cqkv_ref, cz_ref, cba_ref, *strided_refs, dils):
    x = x_ref[...]
    h = x * lax.rsqrt(jnp.mean(x * x, axis=-1, keepdims=True) + EPS) * g_ref[...]
    hb = h.astype(BF16)

    def seg(a, b):
        return _dot(hb, w_ref[:, a:b])

    _chunk_mlp_rows(seg(0, A_W), seg(A_W, 2 * A_W), ag_ref[...], aw_ref, ab_ref[...], oa_ref, vn_ref)
    c, s1, s2 = c_ref[...], s1_ref[...], s2_ref[...]
    q0 = 2 * A_W
    k0 = q0 + B_W
    nblk = B_W // LANES
    stage = strided_refs[-1] if dils else None
    for j in range(nblk):
        cols = slice(LANES * j, LANES * (j + 1))
        for a, (base, ref, scale) in enumerate(((q0, q_ref, HEAD_DIM ** -0.5), (k0, k_ref, None))):
            xc = seg(base + LANES * j, base + LANES * (j + 1))
            r = xc * c + pltpu.roll(xc, ROT_DIM // 2, 1) * s1 + pltpu.roll(xc, LANES - ROT_DIM // 2, 1) * s2
            if scale is not None:
                r = r * scale
            ref[:, cols] = r.astype(ref.dtype)
            if dils:
                stage[a * nblk + j] = r
    v0 = k0 + B_W
    v_ref[...] = seg(v0, v0 + B_W)
    c0 = v0 + B_W
    cqkv_ref[...] = seg(c0, c0 + 3 * C_W)
    cz_ref[...] = seg(c0 + 3 * C_W, c0 + 4 * C_W)
    cba_ref[...] = _dot(hb, wsm_ref[...])
    if dils:
        outs = strided_refs[:-1]
        tm = x.shape[0]
        for b in range(nblk):
            stage[2 * nblk + b] = v_ref[:, LANES * b:LANES * (b + 1)]
        for i, d in enumerate(dils):
            for a in range(3):
                dst = outs[3 * i + a]
                for j in range(d):
                    for b in range(nblk):
                        dst[j, :, LANES * b:LANES * (b + 1)] = stage[a * nblk + b,
                                                                     pl.ds(j, tm // d, stride=d), :].astype(dst.dtype)


def _proj(x, g, w_in_b, layer, w_small, tabs, mixer_a, tm, seq_len=None, dils=()):
    n = x.shape[0]
    nt = n // tm
    ntab = tabs[0].shape[0] // tm
    row = lambda i: (i, 0)
    fixed = lambda i: (0, 0)
    tab = lambda i: (i % ntab, 0)
    widths = (A_W, A_W, B_W, B_W, B_W, 3 * C_W, C_W, LANES)
    out_specs = [pl.BlockSpec((tm, w), row) for w in widths]
    out_shape = [jax.ShapeDtypeStruct((n, w), BF16 if i == 2 else F32) for i, w in enumerate(widths)]
    for d in dils:
        tps = seq_len // tm
        out_specs += [pl.BlockSpec((None, d, tm // d, B_W), lambda i, tps=tps: (i // tps, 0, i % tps, 0))] * 3
        out_shape += [jax.ShapeDtypeStruct((n // seq_len, d, seq_len // d, B_W), BF16)] * 3
    return pl.pallas_call(
        functools.partial(_proj_kernel, dils=dils),
        grid=(nt,),
        in_specs=[pl.BlockSpec((tm, D_MODEL), row), pl.BlockSpec((1, D_MODEL), fixed),
                  pl.BlockSpec((None, D_MODEL, PROJ_MAIN), lambda i: (layer, 0, 0)),
                  pl.BlockSpec((D_MODEL, LANES), fixed),
                  pl.BlockSpec((tm, LANES), tab), pl.BlockSpec((tm, LANES), tab), pl.BlockSpec((tm, LANES), tab),
                  pl.BlockSpec((1, A_W), fixed), pl.BlockSpec((A_HEADS, CHUNK, CHUNK), lambda i: (0, 0, 0)),
                  pl.BlockSpec((CHUNK, A_W), fixed)],
        out_specs=out_specs,
        out_shape=out_shape,
        scratch_shapes=[pltpu.VMEM((3 * B_W // LANES, tm, LANES), F32)] if dils else [],
        compiler_params=_cparams(("parallel",)),
        name="proj",
    )(x, g, w_in_b, w_small, *tabs, *mixer_a)


def _rope_tables(pos):
    half = ROT_DIM // 2
    inv_freq = jnp.power(ROPE_THETA, -jnp.arange(0, ROT_DIM, 2, dtype=F32) / ROT_DIM)
    ang = pos.astype(F32)[:, None] * inv_freq[None, :]
    cos, sin = jnp.cos(ang), jnp.sin(ang)
    p = pos.shape[0]
    z8 = jnp.zeros((p, half), F32)
    rest0 = jnp.zeros((p, HEAD_DIM - ROT_DIM), F32)
    c64 = jnp.concatenate([cos, cos, jnp.ones((p, HEAD_DIM - ROT_DIM), F32)], axis=-1)
    s1 = jnp.concatenate([z8, sin, rest0], axis=-1)
    s2 = jnp.concatenate([-sin, z8, rest0], axis=-1)
    two = lambda a: jnp.concatenate([a, a], axis=-1)
    return two(c64), two(s1), two(s2)


def _win_attn_kernel(q_ref, kp_ref, kc_ref, vp_ref, vc_ref, o_ref, lse_ref, *, seg_blocks):
    s = pl.program_id(0)
    rows = lax.broadcasted_iota(jnp.int32, (WBLK, 2 * WBLK), 0)
    cols = lax.broadcasted_iota(jnp.int32, (WBLK, 2 * WBLK), 1)
    dist = rows + WBLK - cols
    band = (dist >= 0) & (dist <= WBLK)
    lo = jnp.where((2 * s) % seg_blocks == 0, WBLK, 0)
    biases = [jnp.where(band & (cols >= lo), 0.0, NEG), jnp.where(band, 0.0, NEG)]
    lane = lax.broadcasted_iota(jnp.int32, (WBLK, LANES), 1)
    k3 = jnp.concatenate([kp_ref[...], kc_ref[...]], axis=0).astype(BF16)
    v3 = jnp.concatenate([vp_ref[...], vc_ref[...]], axis=0).astype(BF16)
    heads = range(B_HEADS)
    sls = [slice(h // 2 * LANES, (h // 2 + 1) * LANES) for h in heads]
    hms = [(lane < HEAD_DIM) if h % 2 == 0 else (lane >= HEAD_DIM) for h in heads]
    work = [(j, h) for j in range(2) for h in heads]
    qs = [q_ref[j * WBLK:(j + 1) * WBLK, :] for j in range(2)]
    ks = [k3[j * WBLK:(j + 2) * WBLK, :] for j in range(2)]
    vs = [v3[j * WBLK:(j + 2) * WBLK, :] for j in range(2)]
    scs = [_dot_nt(jnp.where(hms[h], qs[j][:, sls[h]], 0.0).astype(BF16), ks[j][:, sls[h]]) + biases[j]
           for j, h in work]
    ms = [jnp.max(sc, axis=-1, keepdims=True) for sc in scs]
    es = [jnp.exp(sc - m) for sc, m in zip(scs, ms)]
    dens = [jnp.sum(e, axis=-1, keepdims=True) for e in es]
    os_ = [_dot((e * (1.0 / d)).astype(BF16), vs[j][:, sls[h]]) for (j, h), e, d in zip(work, es, dens)]
    for j in range(2):
        base = j * B_HEADS
        lse_tile = jnp.zeros((WBLK, LANES), F32)
        for h in heads:
            lse_tile = jnp.where(lane == h, ms[base + h] + jnp.log(dens[base + h]), lse_tile)
        lse_ref[j * WBLK:(j + 1) * WBLK, :] = lse_tile
        for hp in range(B_W // LANES):
            o_ref[j * WBLK:(j + 1) * WBLK, sls[2 * hp]] = jnp.where(hms[2 * hp], os_[base + 2 * hp],
                                                                  os_[base + 2 * hp + 1])


def _win_attn(qd, kd, vd, seg_blocks):
    n = qd.shape[0]
    assert seg_blocks % 2 == 0
    cur = lambda s: (s, 0)
    prev = lambda s: (jnp.maximum(2 * s - 1, 0), 0)
    blk = lambda w, im: pl.BlockSpec((2 * WBLK, w), im)
    pblk = pl.BlockSpec((WBLK, B_W), prev)
    return pl.pallas_call(
        functools.partial(_win_attn_kernel, seg_blocks=seg_blocks),
        grid=(n // (2 * WBLK),),
        in_specs=[blk(B_W, cur), pblk, blk(B_W, cur), pblk, blk(B_W, cur)],
        out_specs=[blk(B_W, cur), blk(LANES, cur)],
        out_shape=[jax.ShapeDtypeStruct((n, B_W), F32), jax.ShapeDtypeStruct((n, LANES), F32)],
        compiler_params=_cparams(("parallel",)),
        name="win_attn",
    )(qd, kd, kd, vd, vd)


def _window_rows_kernel(k_ref, v_ref, kprev_ref, vprev_ref, ko_ref, vo_ref):
    del kprev_ref, vprev_ref
    for b in range(k_ref.shape[0]):
        ko_ref[b] = k_ref[b].T
        vo_ref[b] = v_ref[b].T


def _window_rows(k, v, k_all, v_all, layer, batch, t, keep):
    first = (t - keep) // WBLK
    src = pl.BlockSpec((batch, WBLK, B_W), lambda i: (0, first + i, 0))
    dst = pl.BlockSpec((None, batch, B_W, WBLK), lambda i: (layer, 0, 0, i))
    hbm = pl.BlockSpec(memory_space=pl.ANY)
    shape = jax.ShapeDtypeStruct(k_all.shape, F32)
    return pl.pallas_call(
        _window_rows_kernel,
        grid=(keep // WBLK,),
        in_specs=[src, src, hbm, hbm],
        out_specs=[dst, dst],
        out_shape=[shape, shape],
        input_output_aliases={2: 0, 3: 1},
        compiler_params=_cparams(("parallel",)),
        name="window_rows",
    )(k.reshape(batch, t, B_W), v.reshape(batch, t, B_W), k_all, v_all)


def _dec_attn_kernel(q_ref, kc_ref, vc_ref, kn_ref, vn_ref, o_ref, *, t_new, cache_len):
    rows_c = lax.broadcasted_iota(jnp.int32, (SUBLANES, cache_len), 0)
    cols_c = lax.broadcasted_iota(jnp.int32, (SUBLANES, cache_len), 1)
    dist_c = cache_len + rows_c % t_new - cols_c
    rows_n = lax.broadcasted_iota(jnp.int32, (SUBLANES, SUBLANES), 0)
    cols_n = lax.broadcasted_iota(jnp.int32, (SUBLANES, SUBLANES), 1)
    dist_n = rows_n % t_new - cols_n
    biases = []
    for window, dil in PATTERNS:
        vc_ok = (dist_c <= window) & ((dist_c & (dil - 1)) == 0)
        vn_ok = (dist_n >= 0) & ((dist_n & (dil - 1)) == 0)
        biases.append((jnp.where(vc_ok, 0.0, NEG), jnp.where(vn_ok, 0.0, NEG)))
    row8 = lax.broadcasted_iota(jnp.int32, (SUBLANES, LANES), 0)
    lane8 = lax.broadcasted_iota(jnp.int32, (SUBLANES, LANES), 1)
    own = (lane8 < HEAD_DIM) == (row8 < t_new)
    q = q_ref[...].astype(F32)
    outs = []
    for hp in range(B_W // LANES):
        sl = slice(hp * LANES, (hp + 1) * LANES)
        qq = jnp.concatenate([q[0:t_new, sl], q[0:t_new, sl]], axis=0)
        q8 = jnp.where(own, qq, 0.0).astype(BF16)
        kc, vc = kc_ref[sl, :].astype(BF16), vc_ref[sl, :].astype(BF16)
        kn, vn = kn_ref[:, sl].astype(BF16), vn_ref[:, sl].astype(BF16)
        sc_c = _dot(q8, kc)
        sc_n = _dot_nt(q8, kn)
        os_, lses = [], []
        for bc, bn in biases:
            a_c, a_n = sc_c + bc, sc_n + bn
            m = jnp.maximum(jnp.max(a_c, axis=-1, keepdims=True), jnp.max(a_n, axis=-1, keepdims=True))
            e_c, e_n = jnp.exp(a_c - m), jnp.exp(a_n - m)
            den = jnp.sum(e_c, axis=-1, keepdims=True) + jnp.sum(e_n, axis=-1, keepdims=True)
            inv = 1.0 / den
            os_.append(_dot_nt((e_c * inv).astype(BF16), vc) + _dot((e_n * inv).astype(BF16), vn))
            lses.append(m + jnp.log(den))
        mx = jnp.maximum(jnp.maximum(lses[0], lses[1]), lses[2])
        ws = [jnp.exp(l - mx) for l in lses]
        tot = ws[0] + ws[1] + ws[2]
        o8 = (ws[0] / tot) * os_[0] + (ws[1] / tot) * os_[1] + (ws[2] / tot) * os_[2]
        lane4 = lane8[0:t_new]
        outs.append(jnp.where(lane4 < HEAD_DIM, o8[0:t_new], o8[t_new:2 * t_new]))
    o_ref[...] = jnp.zeros((SUBLANES, B_W), F32)
    o_ref[0:t_new, :] = jnp.concatenate(outs, axis=-1)


def _dec_attn(q8, kc, vc, kn8, vn8, t_new, layer):
    _, b, _, cache_len = kc.shape
    assert 2 * t_new == SUBLANES
    small = pl.BlockSpec((None, SUBLANES, B_W), lambda i: (i, 0, 0))
    big = pl.BlockSpec((None, None, B_W, cache_len), lambda i: (layer, i, 0, 0))
    return pl.pallas_call(
        functools.partial(_dec_attn_kernel, t_new=t_new, cache_len=cache_len),
        grid=(b,),
        in_specs=[small, big, big, small, small],
        out_specs=small,
        out_shape=jax.ShapeDtypeStruct((b, SUBLANES, B_W), F32),
        compiler_params=_cparams(("parallel",)),
        name="dec_attn",
    )(q8, kc, vc, kn8, vn8)


def _split3(x):
    hi = x.astype(BF16)
    r1 = x - hi.astype(F32)
    mid = r1.astype(BF16)
    return hi, mid, (r1 - mid.astype(F32)).astype(BF16)


def _pair_dup(xx):
    lane = lax.broadcasted_iota(jnp.int32, xx.shape, 1)
    hi = xx.astype(BF16).astype(F32)
    return jnp.where(lane < HEAD_DIM, xx, xx - hi).astype(BF16)


def _pair(x):
    return _pair_dup(jnp.concatenate([x, x], axis=1))


def _lhs4(pair):
    return jnp.concatenate([pair, pair], axis=1)


def _rhs4(y):
    hi = y.astype(BF16)
    lo = (y - hi.astype(F32)).astype(BF16)
    return jnp.concatenate([hi, hi, lo, lo], axis=0)


def _unit_lower_solves(mats, rhss):
    n = DELTA_CHUNK
    w2 = 2 * HEAD_DIM
    rows = lax.broadcasted_iota(jnp.int32, (n, w2), 0)
    cols = lax.broadcasted_iota(jnp.int32, (n, w2), 1) % HEAD_DIM
    in16 = rows // 16 == cols // 16
    eye = jnp.where(rows == cols, 1.0, 0.0)
    ds = [jnp.where(in16, a, 0.0) for a in mats]
    es = [jnp.where(in16, 0.0, a) for a in mats]
    left = lambda xx: _lhs4(_pair_dup(xx))
    pw = [_dot(left(d), _rhs4(d)) for d in ds]
    ts = [eye - d for d in ds]
    for _ in range(2):
        outs = [_dot(jnp.concatenate([left(t), left(p)], axis=0), _rhs4(p)) for t, p in zip(ts, pw)]
        ts = [t + o[0:n] for t, o in zip(ts, outs)]
        pw = [o[n:2 * n] for o in outs]
    ts = [t + _dot(left(t), _rhs4(p)) for t, p in zip(ts, pw)]
    o5 = [_dot(left(t), _rhs4(jnp.concatenate([r, e], axis=1))) for t, r, e in zip(ts, rhss, es)]
    x0 = [o[:, 0:w2] for o in o5]
    nm = [o[:, w2:2 * w2] for o in o5]
    o6 = [_dot(left(m), _rhs4(jnp.concatenate([x, m], axis=1))) for m, x in zip(nm, x0)]
    ys = [x - o[:, 0:w2] for x, o in zip(x0, o6)]
    return [y + _dot(left(o[:, w2:2 * w2]), _rhs4(y)) for y, o in zip(ys, o6)]


def _gdn_prep_kernel(x_ref, cba_ref, buf_ref, cw_ref, alog_ref, dtb_ref, gmat_ref, xmat_ref,
                     u_ref, wq_ref, ak_ref, gt_ref,
                     xp_scr, q_scr, k_scr, v_scr, gb_scr, *, tm, valid_len, group):
    ti = pl.program_id(1)

    @pl.when(ti == 0)
    def _():
        xp_scr[0:SUBLANES, :] = buf_ref[...]

    x = x_ref[...]
    xp_scr[SUBLANES:SUBLANES + tm, :] = x
    off = SUBLANES - (CONV_W - 1)
    acc = xp_scr[off:off + tm, :] * cw_ref[0:1, :]
    for j in range(1, CONV_W):
        acc = acc + xp_scr[off + j:off + j + tm, :] * cw_ref[j:j + 1, :]
    xp_scr[0:SUBLANES, :] = x[tm - SUBLANES:tm, :]
    y = _silu(acc)
    q, k = y[:, 0:C_W], y[:, C_W:2 * C_W]
    gmat = gmat_ref[...]
    head_sum = lambda a: sum(_dot(p, gmat) for p in _split3(a))
    q_scr[...] = q * lax.rsqrt(head_sum(q * q) + EPS) * (HEAD_DIM ** -0.5)
    k_scr[...] = k * lax.rsqrt(head_sum(k * k) + EPS)
    v_scr[...] = y[:, 2 * C_W:3 * C_W]

    cba = cba_ref[...]
    lane = lax.broadcasted_iota(jnp.int32, (tm, LANES), 1)
    tpos = ti * tm + lax.broadcasted_iota(jnp.int32, (tm, LANES), 0)
    live = tpos < valid_len
    beta = _sigmoid(cba)
    z = cba + dtb_ref[...]
    softplus = jnp.maximum(z, 0.0) + jnp.log(1.0 + jnp.exp(-jnp.abs(z)))
    g = -jnp.exp(alog_ref[...]) * softplus
    is_g = (lane >= C_HEADS) & (lane < 2 * C_HEADS)
    gb_scr[...] = jnp.where(live, jnp.where(is_g, g, jnp.where(lane < C_HEADS, beta, 0.0)), 0.0)

    n = DELTA_CHUNK
    rows = lax.broadcasted_iota(jnp.int32, (n, 2 * n), 0)
    cols = lax.broadcasted_iota(jnp.int32, (n, 2 * n), 1) % n
    incl = rows >= cols
    strict = rows > cols
    ltri = jnp.where(incl[:, 0:n], 1.0, 0.0).astype(BF16)
    heads = range(C_HEADS)
    tiles = range(C_W // LANES)
    tsl = [slice(t * LANES, (t + 1) * LANES) for t in tiles]
    lane_c = lax.broadcasted_iota(jnp.int32, (n, LANES), 1)
    low = lane_c < HEAD_DIM
    own = [low if h % 2 == 0 else jnp.logical_not(low) for h in heads]
    xmat = xmat_ref[...]
    wide0 = C_HEADS * LANES

    def head_pairs(tile):
        rolled = pltpu.roll(tile, HEAD_DIM, 1)
        lo = rolled - rolled.astype(BF16).astype(F32)
        lane = lax.broadcasted_iota(jnp.int32, tile.shape, 1) < HEAD_DIM
        return jnp.where(lane, tile, lo).astype(BF16), jnp.where(lane, lo, tile).astype(BF16)

    def setup(c):
        rs = pl.ds(pl.multiple_of(c * n, n), n)
        gb = gb_scr[rs, :]
        g_only = jnp.where((lane_c >= C_HEADS) & (lane_c < 2 * C_HEADS), gb, 0.0)
        cg = sum(_dot(ltri, p) for p in _split3(g_only))
        cgt = cg.T
        cgl = cg[n - 1:n, :]
        gt_ref[pl.ds(pl.multiple_of(c * SUBLANES, SUBLANES), SUBLANES), :] = jnp.broadcast_to(
            jnp.exp(cgl), (SUBLANES, LANES))
        wide = sum(_dot(p, xmat) for p in _split3(jnp.where(lane_c < C_HEADS, gb, cg)))
        b128 = [wide[:, h * LANES:(h + 1) * LANES] for h in heads]
        c128 = [wide[:, wide0 + h * LANES:wide0 + (h + 1) * LANES] for h in heads]
        cgr = [jnp.concatenate([cgt[C_HEADS + h:C_HEADS + h + 1, :]] * 2, axis=1) for h in heads]
        decay = [jnp.where(incl, jnp.exp(jnp.minimum(c128[h] - cgr[h], 0.0)), 0.0) for h in heads]
        amat, at_pair, rhs, qd_pair, kdt_pair = [], [], [], [], []
        for t in tiles:
            he, ho = 2 * t, 2 * t + 1
            qn, kn, vv = q_scr[rs, tsl[t]], k_scr[rs, tsl[t]], v_scr[rs, tsl[t]]
            bnat = jnp.where(low, b128[he], b128[ho])
            cnat = jnp.where(low, c128[he], c128[ho])
            ecg = jnp.exp(cnat)
            kb = kn.astype(BF16)
            kk2 = jnp.concatenate([kb, kb], axis=0)
            for h in (he, ho):
                masked = jnp.concatenate([jnp.where(own[h], kn, 0.0), jnp.where(own[h], qn, 0.0)], axis=0)
                kkqk = _dot_nt(masked.astype(BF16), kk2)
                amat.append(jnp.where(strict, b128[h] * kkqk[0:n] * decay[h], 0.0))
                at_pair.append(_pair_dup(kkqk[n:2 * n] * decay[h]))
            bv = bnat * vv
            bk_rolled = pltpu.roll((bnat * ecg) * kn, HEAD_DIM, 1)
            rhs += [jnp.where(low, bv, bk_rolled), jnp.where(low, bk_rolled, bv)]
            qd_pair += list(head_pairs(qn * ecg))
            kd = kn * jnp.exp(cnat[n - 1:n, :] - cnat)
            kdt = jnp.concatenate([kd, kd], axis=0).T
            kdt_pair += [_pair_dup(kdt[0:n]), _pair_dup(kdt[n:2 * n])]
        return amat, rhs, at_pair, qd_pair, kdt_pair

    def finish(c, sol, at_pair, qd_pair, kdt_pair):
        w_pair = []
        for t in tiles:
            se, so = sol[2 * t], sol[2 * t + 1]
            u_ref[pl.ds(pl.multiple_of(c * n, n), n), tsl[t]] = jnp.where(low, se, so)
            w_odd, w_even = head_pairs(jnp.where(low, so, se))
            w_pair += [w_even, w_odd]
        r2 = pl.ds(pl.multiple_of(c * 2 * n, 2 * n), 2 * n)
        wq_ref[r2, :] = jnp.concatenate(
            [jnp.concatenate([w_pair[h], qd_pair[h]], axis=0) for h in heads], axis=1)
        ak_ref[r2, :] = jnp.concatenate(
            [jnp.concatenate([at_pair[h], kdt_pair[h]], axis=0) for h in heads], axis=1)

    def chunks(i, carry):
        parts = [setup(i * group + j) for j in range(group)]
        sol = _unit_lower_solves([a for p in parts for a in p[0]], [r for p in parts for r in p[1]])
        for j, p in enumerate(parts):
            finish(i * group + j, sol[j * C_HEADS:(j + 1) * C_HEADS], *p[2:])
        return carry

    lax.fori_loop(0, tm // n // group, chunks, 0)


def _gdn_constants():
    hid = jnp.arange(C_W, dtype=jnp.int32) // HEAD_DIM
    gmat = (hid[:, None] == hid[None, :]).astype(BF16)
    src = jnp.arange(LANES, dtype=jnp.int32)[:, None]
    dst = jnp.arange(2 * C_HEADS * LANES, dtype=jnp.int32)[None, :] // LANES
    return gmat, (src == dst).astype(BF16)


def _gdn_prep(cqkv, cba, buf8, conv_w, alog_row, dtb_row, consts, batch, t, tm, valid_len):
    gmat, xmat = consts
    n = batch * t
    nt = t // tm
    row = lambda b, i: (b * nt + i, 0)
    fixed = lambda b, i: (0, 0)
    nch = tm // DELTA_CHUNK
    pair_w = C_HEADS * 2 * HEAD_DIM
    outs = [jax.ShapeDtypeStruct((n, C_W), F32), jax.ShapeDtypeStruct((2 * n, pair_w), BF16),
            jax.ShapeDtypeStruct((2 * n, pair_w), BF16),
            jax.ShapeDtypeStruct((n // DELTA_CHUNK * SUBLANES, LANES), F32)]
    return pl.pallas_call(
        functools.partial(_gdn_prep_kernel, tm=tm, valid_len=valid_len, group=min(GDN_GROUP, nch)),
        grid=(batch, nt),
        in_specs=[pl.BlockSpec((tm, 3 * C_W), row), pl.BlockSpec((tm, LANES), row),
                  pl.BlockSpec((None, SUBLANES, 3 * C_W), lambda b, i: (b, 0, 0)),
                  pl.BlockSpec((CONV_W, 3 * C_W), fixed), pl.BlockSpec((1, LANES), fixed),
                  pl.BlockSpec((1, LANES), fixed), pl.BlockSpec((C_W, C_W), fixed),
                  pl.BlockSpec(xmat.shape, fixed)],
        out_specs=[pl.BlockSpec((tm, C_W), row), pl.BlockSpec((2 * tm, pair_w), row),
                   pl.BlockSpec((2 * tm, pair_w), row), pl.BlockSpec((nch * SUBLANES, LANES), row)],
        out_shape=outs,
        scratch_shapes=[pltpu.VMEM((tm + SUBLANES, 3 * C_W), F32), pltpu.VMEM((tm, C_W), F32),
                        pltpu.VMEM((tm, C_W), F32), pltpu.VMEM((tm, C_W), F32), pltpu.VMEM((tm, LANES), F32)],
        compiler_params=_cparams(("parallel", "arbitrary")),
        name="gdn_prep",
    )(cqkv, cba, buf8, conv_w, alog_row, dtb_row, gmat, xmat)


def _gdn_scan_kernel(u_ref, wq_ref, ak_ref, gt_ref, z_ref, s0_ref, gn_ref,
                     o_ref, sfin_ref, s_scr, *, bg):
    c = pl.program_id(1)

    @pl.when(c == 0)
    def _():
        s_scr[...] = s0_ref[...]

    gn = gn_ref[...]
    n = DELTA_CHUNK
    chains = [(b, h, slice(h * HEAD_DIM, (h + 1) * HEAD_DIM), slice(h * 2 * HEAD_DIM, (h + 1) * 2 * HEAD_DIM))
              for b in range(bg) for h in range(C_HEADS)]
    st = [s_scr[b, h] for b, h, _, _ in chains]
    r1 = [_dot(_lhs4(wq_ref[b, :, ps]), _rhs4(s)) for (b, _, _, ps), s in zip(chains, st)]
    up = [u_ref[b, :, sl] - r[0:n] for (b, _, sl, _), r in zip(chains, r1)]
    r2 = [_dot(_lhs4(ak_ref[b, :, ps]), _rhs4(x)) for (b, _, _, ps), x in zip(chains, up)]
    for (b, h, _, _), s, r in zip(chains, st, r2):
        s_scr[b, h] = gt_ref[b, 0:1, C_HEADS + h:C_HEADS + h + 1] * s + r[n:2 * n]
    os_ = [a[n:2 * n] + r[0:n] for a, r in zip(r1, r2)]
    outs = [o * lax.rsqrt(jnp.mean(o * o, axis=-1, keepdims=True) + EPS) * gn * _silu(z_ref[b, :, sl])
            for (b, _, sl, _), o in zip(chains, os_)]
    for b in range(bg):
        o_ref[b] = jnp.concatenate(outs[b * C_HEADS:(b + 1) * C_HEADS], axis=1)

    @pl.when(c == pl.num_programs(1) - 1)
    def _():
        sfin_ref[...] = s_scr[...]


def _gdn_scan(u, wq, ak, gt, z, s0, gnorm, batch, t, bg):
    n = DELTA_CHUNK
    nc = t // n
    pair_w = C_HEADS * 2 * HEAD_DIM
    v3 = lambda a: a.reshape(batch, t, C_W)
    p3 = lambda a: a.reshape(batch, 2 * t, pair_w)
    tok = pl.BlockSpec((bg, n, C_W), lambda b, c: (b, c, 0))
    pair = pl.BlockSpec((bg, 2 * n, pair_w), lambda b, c: (b, c, 0))
    st = pl.BlockSpec((bg, C_HEADS, HEAD_DIM, HEAD_DIM), lambda b, c: (b, 0, 0, 0))
    return pl.pallas_call(
        functools.partial(_gdn_scan_kernel, bg=bg),
        grid=(batch // bg, nc),
        in_specs=[tok, pair, pair,
                  pl.BlockSpec((bg, SUBLANES, LANES), lambda b, c: (b, c, 0)),
                  tok, st, pl.BlockSpec((1, HEAD_DIM), lambda b, c: (0, 0))],
        out_specs=[tok, st],
        out_shape=[jax.ShapeDtypeStruct((batch, t, C_W), F32),
                   jax.ShapeDtypeStruct((batch, C_HEADS, HEAD_DIM, HEAD_DIM), F32)],
        scratch_shapes=[pltpu.VMEM((bg, C_HEADS, HEAD_DIM, HEAD_DIM), F32)],
        compiler_params=_cparams(("parallel", "arbitrary")),
        name="gdn_scan",
    )(v3(u), p3(wq), p3(ak), gt.reshape(batch, nc * SUBLANES, LANES), v3(z), s0, gnorm)


def _head_expand(wt):
    src = lax.broadcasted_iota(jnp.int32, (LANES, B_W), 0)
    dst = lax.broadcasted_iota(jnp.int32, (LANES, B_W), 1) // HEAD_DIM
    spread = jnp.where(src == dst, 1.0, 0.0).astype(BF16)
    return sum(_dot(p, spread) for p in _split3(wt))


def _out_proj_kernel(*refs, dils):
    if dils:
        npat = len(dils)
        a_ref = refs[0]
        o_refs, l_refs = refs[1:1 + npat], refs[1 + npat:1 + 2 * npat]
        c_ref, x_ref, w_ref, g2_ref, wr_ref, y_ref, dense_ref, o_scr, l_scr = refs[1 + 2 * npat:]
        tm = x_ref.shape[0]
        os_, ls = [], []
        for i, d in enumerate(dils):
            if d == 1:
                os_.append(o_refs[i][0])
                ls.append(l_refs[i][0])
            else:
                nblk = B_W // LANES
                for j in range(d):
                    rows = pl.ds(j, tm // d, stride=d)
                    for b in range(nblk):
                        o_scr[i * nblk + b, rows, :] = o_refs[i][j, :, LANES * b:LANES * (b + 1)]
                    l_scr[i, rows, :] = l_refs[i][j]
                os_.append(jnp.concatenate([o_scr[i * nblk + b] for b in range(nblk)], axis=-1))
                ls.append(l_scr[i])
        mx = functools.reduce(jnp.maximum, ls)
        es = [jnp.exp(l - mx) for l in ls]
        tot = functools.reduce(lambda a, b: a + b, es)
        ob = functools.reduce(lambda a, b: a + b, [_head_expand(e / tot) * o for e, o in zip(es, os_)])
    else:
        a_ref, b_ref, c_ref, x_ref, w_ref, g2_ref, wr_ref, y_ref, dense_ref = refs
        ob = b_ref[...]
    cat = jnp.concatenate([a_ref[...], ob, c_ref[...]], axis=-1).astype(BF16)
    y = x_ref[...] + _dot(cat, w_ref[...])
    y_ref[...] = y
    dense_ref[...] = _route_tile(y, g2_ref[...], wr_ref[...])


def _out_proj(out_a, out_b, lses, out_c, x, w_out, g2, w_route, tm, seq_len=None, dils=()):
    n = x.shape[0]
    row = lambda i: (i, 0)
    spec = lambda w: pl.BlockSpec((tm, w), row)
    scratch = []
    if dils:
        tps = seq_len // tm
        strided = lambda d, w: pl.BlockSpec((None, d, tm // d, w), lambda i: (i // tps, 0, i % tps, 0))
        ins = [out_a, *out_b, *lses, out_c, x, w_out]
        specs = ([spec(A_W)] + [strided(d, B_W) for d in dils] + [strided(d, LANES) for d in dils]
                 + [spec(C_W), spec(D_MODEL)])
        scratch = [pltpu.VMEM((len(dils) * B_W // LANES, tm, LANES), F32), pltpu.VMEM((len(dils), tm, LANES), F32)]
    else:
        ins = [out_a, out_b, out_c, x, w_out]
        specs = [spec(A_W), spec(B_W), spec(C_W), spec(D_MODEL)]
    fixed = lambda i: (0, 0)
    specs += [pl.BlockSpec((D_MODEL, D_MODEL), fixed), pl.BlockSpec((1, D_MODEL), fixed),
              pl.BlockSpec((D_MODEL, LANES), fixed)]
    return pl.pallas_call(
        functools.partial(_out_proj_kernel, dils=dils),
        grid=(n // tm,),
        in_specs=specs,
        out_specs=[spec(D_MODEL), spec(LANES)],
        out_shape=[jax.ShapeDtypeStruct((n, D_MODEL), F32), jax.ShapeDtypeStruct((n, LANES), F32)],
        scratch_shapes=scratch,
        compiler_params=_cparams(("parallel",)),
        name="out_proj",
    )(*ins, g2, w_route)


def _route_tile(x, g, w):
    t = x * lax.rsqrt(jnp.mean(x * x, axis=-1, keepdims=True) + EPS) * g
    t_hi = t.astype(BF16)
    t_lo = (t - t_hi.astype(F32)).astype(BF16)
    w_hi = w.astype(BF16)
    w_lo = (w - w_hi.astype(F32)).astype(BF16)
    lg = _dot(t_hi, w_hi) + (_dot(t_hi, w_lo) + _dot(t_lo, w_hi))
    tm = lg.shape[0]
    nrow = ROUTE_OFF + N_EXPERTS + ROUTE_OFF
    lt = lg.T[0:nrow, :]
    row = lax.broadcasted_iota(jnp.int32, (nrow, tm), 0).astype(F32)
    big = float(LANES)
    down = lambda op, a: op(a, axis=0, keepdims=True)
    is_grp = row < N_GROUPS
    gl = jnp.where(is_grp, lt, NEG)
    gmax = down(jnp.max, gl)
    gsum = down(jnp.sum, jnp.where(is_grp, jnp.exp(gl - gmax), 0.0))
    g_w = 1.0 / gsum
    g_idx = down(jnp.min, jnp.where(is_grp & (gl == gmax), row, big))
    lo = ROUTE_OFF + EXP_PER_GROUP * g_idx
    sel = (row >= lo) & (row < lo + EXP_PER_GROUP)
    el = jnp.where(sel, lt, NEG)
    m1 = down(jnp.max, el)
    esum = down(jnp.sum, jnp.where(sel, jnp.exp(el - m1), 0.0))
    i1 = down(jnp.min, jnp.where(sel & (el == m1), row, big))
    el2 = jnp.where(row == i1, NEG, el)
    m2 = down(jnp.max, el2)
    i2 = down(jnp.min, jnp.where(sel & (row != i1) & (el2 == m2), row, big))
    p1 = 1.0 / esum
    p2 = jnp.exp(m2 - m1) / esum
    tot = p1 + p2
    gates = jnp.where(row == i1, g_w * (p1 / tot), 0.0) + jnp.where(row == i2, g_w * (p2 / tot), 0.0)
    full = jnp.concatenate([jnp.where(row == 0.0, g_idx, gates), jnp.zeros((LANES - nrow, tm), F32)], axis=0)
    return full.T


def _rms_rows(y, g):
    return y * lax.rsqrt(jnp.mean(y * y, axis=-1, keepdims=True) + EPS) * g


def _moe_kernel(x_ref, g_ref, dense_ref, w1_ref, w3_ref, w2_ref, *rest):
    fg_ref = rest[0] if len(rest) == 3 else None
    y_ref, t_scr = rest[-2:]
    e = pl.program_id(1)

    @pl.when(e == 0)
    def _():
        x = x_ref[...]
        t_scr[...] = (x * lax.rsqrt(jnp.mean(x * x, axis=-1, keepdims=True) + EPS) * g_ref[...]).astype(BF16)
        y_ref[...] = x

    tb = t_scr[...]
    dense = dense_ref[...]
    lane = lax.broadcasted_iota(jnp.int32, dense.shape, 1)
    gate = jnp.sum(jnp.where(lane == e + ROUTE_OFF, dense, 0.0), axis=-1, keepdims=True)
    hid = _silu(_dot(tb, w1_ref[...].astype(BF16))) * _dot(tb, w3_ref[...].astype(BF16))
    y_ref[...] += _dot((hid * gate).astype(BF16), w2_ref[...].astype(BF16))

    if fg_ref is not None:
        @pl.when(e == pl.num_programs(1) - 1)
        def _():
            y_ref[...] = _rms_rows(y_ref[...], fg_ref[...])


def _moe(x, g, dense, w1, w3, w2, layer, tm, final_g=None):
    n = x.shape[0]
    row = lambda i, e: (i, 0)
    expert = lambda i, e: (layer, e, 0, 0)
    extra = [] if final_g is None else [final_g]
    return pl.pallas_call(
        _moe_kernel,
        grid=(n // tm, N_EXPERTS),
        in_specs=[pl.BlockSpec((tm, D_MODEL), row), pl.BlockSpec((1, D_MODEL), lambda i, e: (0, 0)),
                  pl.BlockSpec((tm, LANES), row),
                  pl.BlockSpec((None, None, D_MODEL, D_EXPERT), expert),
                  pl.BlockSpec((None, None, D_MODEL, D_EXPERT), expert),
                  pl.BlockSpec((None, None, D_EXPERT, D_MODEL), expert)]
        + [pl.BlockSpec((1, D_MODEL), lambda i, e: (0, 0))] * len(extra),
        out_specs=pl.BlockSpec((tm, D_MODEL), row),
        out_shape=jax.ShapeDtypeStruct((n, D_MODEL), F32),
        scratch_shapes=[pltpu.VMEM((tm, D_MODEL), BF16)],
        compiler_params=_cparams(("parallel", "arbitrary")),
        name="moe",
    )(x, g, dense, w1, w3, w2, *extra)


def _moe_grouped_kernel(x_ref, g_ref, dense_ref, gidr_ref, tri_ref, w1_ref, w3_ref, w2_ref, *rest,
                        tm, slots, final):
    fg_ref = rest[0] if final else None
    y_ref, ts_scr, gs_scr, ys_scr, pt_scr, meta = rest[-6:]
    e = pl.program_id(1)
    blk = MOE_BLOCK

    @pl.when(e == 0)
    def _():
        x = x_ref[...]
        t = (x * lax.rsqrt(jnp.mean(x * x, axis=-1, keepdims=True) + EPS) * g_ref[...]).astype(BF16)
        dense = dense_ref[...]
        tri = tri_ref[...]
        lane = lax.broadcasted_iota(jnp.int32, (tm, LANES), 1)
        ohc = jnp.where(lane.astype(F32) == dense[:, 0:1], 1.0, 0.0)
        rankc = _dot(tri, ohc.astype(BF16))
        sub = lax.broadcasted_iota(jnp.int32, (SUBLANES, tm), 0)
        ohr = jnp.where(sub.astype(F32) == gidr_ref[...], 1.0, 0.0)
        rankr = _dot_nt(ohr.astype(BF16), tri)
        start = jnp.int32(0)
        s_lane = jnp.zeros((tm, LANES), F32)
        s_sub = jnp.zeros((SUBLANES, tm), F32)
        for g in range(N_GROUPS):
            count = jnp.sum(jnp.where(lane == g, ohc, 0.0)).astype(jnp.int32)
            nblk = (count + blk - 1) // blk
            meta[g] = start
            meta[N_GROUPS + g] = nblk
            s_lane = jnp.where(lane == g, start.astype(F32), s_lane)
            s_sub = jnp.where(sub == g, start.astype(F32), s_sub)
            start = start + nblk * blk
        destc = jnp.sum(ohc * (s_lane + rankc), axis=-1, keepdims=True)
        destr = jnp.sum(ohr * (s_sub + rankr), axis=0, keepdims=True)
        slot_r = lax.broadcasted_iota(jnp.int32, (slots, tm), 0).astype(F32)
        p = jnp.where(slot_r == destr, 1.0, 0.0).astype(BF16)
        ts_scr[...] = _dot(p, t).astype(BF16)
        gs_scr[...] = sum(_dot(p, piece) for piece in _split3(dense))
        slot_c = lax.broadcasted_iota(jnp.int32, (tm, slots), 1).astype(F32)
        pt_scr[...] = jnp.where(slot_c == destc, 1.0, 0.0).astype(BF16)
        ys_scr[...] = jnp.zeros((slots, D_MODEL), F32)

    ne = MOE_STEP_EXPERTS
    g = (e * ne) // EXP_PER_GROUP
    start = meta[g]
    nblk = meta[N_GROUPS + g]
    w13 = jnp.concatenate([w1_ref[j] for j in range(ne)] + [w3_ref[j] for j in range(ne)], axis=1)
    w2 = jnp.concatenate([w2_ref[j] for j in range(ne)], axis=0)
    half = ne * D_EXPERT

    def rows_update(r0, m):
        rows = pl.ds(pl.multiple_of(r0, blk), m)
        tb = ts_scr[rows, :]
        gs = gs_scr[rows, :]
        lane = lax.broadcasted_iota(jnp.int32, (m, LANES), 1)
        h13 = _dot(tb, w13)
        hid = _silu(h13[:, 0:half]) * h13[:, half:2 * half]
        gated = [hid[:, j * D_EXPERT:(j + 1) * D_EXPERT]
                 * jnp.sum(jnp.where(lane == e * ne + j + ROUTE_OFF, gs, 0.0), axis=-1, keepdims=True)
                 for j in range(ne)]
        ys_scr[rows, :] += _dot(jnp.concatenate(gated, axis=1).astype(BF16), w2)

    def quad(j, carry):
        rows_update(start + j * 4 * blk, 4 * blk)
        return carry

    lax.fori_loop(0, nblk // 4, quad, 0)
    for rem in (1, 2, 3):
        @pl.when(nblk % 4 == rem)
        def _(rem=rem):
            rows_update(start + (nblk // 4) * 4 * blk, rem * blk)

    @pl.when(e == pl.num_programs(1) - 1)
    def _():
        pt = pt_scr[...]
        ys = ys_scr[...]
        hi = ys.astype(BF16)
        lo = (ys - hi.astype(F32)).astype(BF16)
        y = x_ref[...] + (_dot(pt, hi) + _dot(pt, lo))
        y_ref[...] = _rms_rows(y, fg_ref[...]) if final else y


def _moe_grouped(x, g, dense, w1, w3, w2, layer, tm, final_g=None):
    extra = [] if final_g is None else [final_g]
    n = x.shape[0]
    slots = tm + N_GROUPS * MOE_BLOCK
    ne = MOE_STEP_EXPERTS
    assert EXP_PER_GROUP % ne == 0
    row = lambda i, e: (i, 0)
    expert = lambda i, e: (layer, e, 0, 0)
    gid_rows = dense[:, 0].reshape(n // tm, 1, tm)
    idx = jnp.arange(tm, dtype=jnp.int32)
    tri = (idx[None, :] < idx[:, None]).astype(BF16)
    return pl.pallas_call(
        functools.partial(_moe_grouped_kernel, tm=tm, slots=slots, final=final_g is not None),
        grid=(n // tm, N_EXPERTS // ne),
        in_specs=[pl.BlockSpec((tm, D_MODEL), row), pl.BlockSpec((1, D_MODEL), lambda i, e: (0, 0)),
                  pl.BlockSpec((tm, LANES), row),
                  pl.BlockSpec((None, 1, tm), lambda i, e: (i, 0, 0)),
                  pl.BlockSpec((tm, tm), lambda i, e: (0, 0)),
                  pl.BlockSpec((None, ne, D_MODEL, D_EXPERT), expert),
                  pl.BlockSpec((None, ne, D_MODEL, D_EXPERT), expert),
                  pl.BlockSpec((None, ne, D_EXPERT, D_MODEL), expert)]
        + [pl.BlockSpec((1, D_MODEL), lambda i, e: (0, 0))] * len(extra),
        out_specs=pl.BlockSpec((tm, D_MODEL), row),
        out_shape=jax.ShapeDtypeStruct((n, D_MODEL), F32),
        scratch_shapes=[pltpu.VMEM((slots, D_MODEL), BF16), pltpu.VMEM((slots, LANES), F32),
                        pltpu.VMEM((slots, D_MODEL), F32), pltpu.VMEM((tm, slots), BF16),
                        pltpu.SMEM((2 * N_GROUPS,), jnp.int32)],
        compiler_params=_cparams(("parallel", "arbitrary")),
        name="moe_grouped",
    )(x, g, dense, gid_rows, tri, w1, w3, w2, *extra)


def _tile_rows(n, cap):
    tm = min(n, cap)
    assert n % tm == 0
    return tm


def _layer_weights(l, norm1_g, w_in, a_vnorm_g, a_ws, a_bs, c_conv_w, c_a_log, c_dt_bias, c_norm_g,
                   w_out, norm2_g, w_group, w_router, w1, w3, w2):
    pad_l = lambda a, left: jnp.pad(a, ((0, 0), (left, LANES - left - a.shape[-1])))
    return dict(
        norm1_g=norm1_g[l][None, :],
        w_in_b=w_in.astype(BF16),
        w_small=pad_l(w_in[l, :, PROJ_MAIN:], 0).astype(BF16),
        a_gain=a_vnorm_g[l][None, :],
        a_ws=a_ws[l],
        a_bs=a_bs[l],
        conv_w=c_conv_w[l],
        alog_row=pad_l(c_a_log[l][None, :], C_HEADS),
        dtb_row=pad_l(c_dt_bias[l][None, :], C_HEADS),
        gnorm=c_norm_g[l][None, :],
        w_out=w_out[l].astype(BF16),
        norm2_g=norm2_g[l][None, :],
        w_route=pad_l(jnp.concatenate([w_group[l], w_router[l]], axis=-1), 0),
        w1=w1, w3=w3, w2=w2, layer=l,
    )


def _mixer_c(lw, cqkv, cba, cz, conv_buf, s0, gmat, batch, t, tm, valid_len, bg):
    buf8 = jnp.pad(conv_buf, ((0, 0), (SUBLANES - (CONV_W - 1), 0), (0, 0)))
    u, wq, ak, gt = _gdn_prep(cqkv, cba, buf8, lw["conv_w"], lw["alog_row"], lw["dtb_row"], gmat,
                              batch, t, tm, valid_len)
    return _gdn_scan(u, wq, ak, gt, cz, s0, lw["gnorm"], batch, t, bg)


def _ffn(lw, x, dense):
    n = x.shape[0]
    w1, w3, w2 = lw["w1"].astype(BF16), lw["w3"].astype(BF16), lw["w2"].astype(BF16)
    if n % MOE_TILE == 0:
        return _moe_grouped(x, lw["norm2_g"], dense, w1, w3, w2, lw["layer"], MOE_TILE, lw["final_g"])
    return _moe(x, lw["norm2_g"], dense, w1, w3, w2, lw["layer"], n, lw["final_g"])


def _prompt_layer(lw, x, batch, t, tabs, gmat, win_k, win_v, layer):
    n = batch * t
    dils = tuple(d for _, d in PATTERNS)
    tm = _tile_rows(n, 512)
    mixer_a = (lw["a_gain"], lw["a_ws"], jnp.repeat(lw["a_bs"].T, HEAD_DIM, axis=1))
    out_a, _, q, k, v, cqkv, cz, cba, *strided = _proj(x, lw["norm1_g"], lw["w_in_b"], layer, lw["w_small"], tabs,
                                                       mixer_a, tm, seq_len=t, dils=dils[1:])

    qkv = [(q, k, v)] + [tuple(a.reshape(n, B_W) for a in strided[3 * i:3 * i + 3]) for i in range(len(dils) - 1)]
    outs, lses = [], []
    for d, (qd, kd, vd) in zip(dils, qkv):
        o_d, lse_d = _win_attn(qd, kd, vd, t // d // WBLK)
        outs.append(o_d.reshape(batch, d, t // d, B_W))
        lses.append(lse_d.reshape(batch, d, t // d, LANES))

    zeros_buf = jnp.zeros((batch, CONV_W - 1, 3 * C_W), F32)
    zeros_s = jnp.zeros((batch, C_HEADS, HEAD_DIM, HEAD_DIM), F32)
    out_c, s_new = _mixer_c(lw, cqkv, cba, cz, zeros_buf, zeros_s, gmat, batch, t, 256, t, batch)
    out_c = out_c.reshape(n, C_W)

    x, dense = _out_proj(out_a, outs, lses, out_c, x, lw["w_out"], lw["norm2_g"], lw["w_route"], tm,
                         seq_len=t, dils=dils)
    x = _ffn(lw, x, dense)

    win_k, win_v = _window_rows(k, v, win_k, win_v, layer, batch, t, win_k.shape[-1])
    conv_state = cqkv.reshape(batch, t, 3 * C_W)[:, t - (CONV_W - 1):]
    return x, win_k, win_v, (conv_state, s_new)


def _sample_layer(lw, x, batch, t, tabs, gmat, kbuf, vbuf, conv_buf, s0, layer):
    n = batch * t
    eye = jnp.eye(batch, dtype=F32)
    ws_bd = jnp.stack([jnp.kron(eye, lw["a_ws"][h, :t, :t]) for h in range(A_HEADS)])
    bias_tile = jnp.tile(jnp.repeat(lw["a_bs"][:, :t].T, HEAD_DIM, axis=1), (batch, 1))
    out_a, a_rows, q, k, v, cqkv, cz, cba = _proj(x, lw["norm1_g"], lw["w_in_b"], layer, lw["w_small"], tabs,
                                                  (lw["a_gain"], ws_bd, bias_tile), n)

    pad8 = lambda a: jnp.pad(a.reshape(batch, t, B_W), ((0, 0), (0, SUBLANES - t), (0, 0)))
    out_b = _dec_attn(pad8(q), kbuf, vbuf, pad8(k), pad8(v), t, layer)[:, :t].reshape(n, B_W)

    tp = DELTA_CHUNK
    padt = lambda a: jnp.pad(a.reshape(batch, t, -1), ((0, 0), (0, tp - t), (0, 0))).reshape(batch * tp, -1)
    out_c, s_new = _mixer_c(lw, padt(cqkv), padt(cba), padt(cz), conv_buf, s0, gmat, batch, tp, tp, t, 4)
    out_c = out_c[:, :t].reshape(n, C_W)

    x, dense = _out_proj(out_a, out_b, None, out_c, x, lw["w_out"], lw["norm2_g"], lw["w_route"], n)
    x = _ffn(lw, x, dense)

    heads = lambda a: a.reshape(batch, t, B_HEADS, HEAD_DIM)
    conv_state = jnp.concatenate([conv_buf, cqkv.reshape(batch, t, 3 * C_W)], axis=1)[:, -(CONV_W - 1):]
    return x, (heads(k), heads(v), a_rows.reshape(batch, t, A_W), conv_state, s_new)


def kernel(x_prompt, x_sample, cache_win_k, cache_win_v, state_conv, state_delta, norm1_g, w_in, a_vnorm_g, a_ws, a_bs, c_conv_w, c_a_log, c_dt_bias, c_norm_g, w_out, norm2_g, w_group, w_router, w1, w3, w2, final_g):
    bp, tp, _ = x_prompt.shape
    bs, ts, _ = x_sample.shape
    depth = w_in.shape[0]
    assert tp % (PATTERNS[-1][1] * WBLK) == 0 and tp % 512 == 0 and bs * ts == CHUNK

    tabs_p = _rope_tables(jnp.arange(tp, dtype=jnp.int32))
    tabs_s = tuple(jnp.tile(a, (bs, 1)) for a in _rope_tables(PAST_LEN + jnp.arange(ts, dtype=jnp.int32)))
    gmat = _gdn_constants()

    feat_major = lambda c: jnp.transpose(c, (0, 1, 3, 4, 2)).reshape(depth, bs, B_W, c.shape[2])
    cache_k, cache_v = feat_major(cache_win_k), feat_major(cache_win_v)
    keep = min(MAX_WINDOW, tp)
    win_k = jnp.zeros((depth, bp, B_W, keep), F32)
    win_v = jnp.zeros((depth, bp, B_W, keep), F32)

    xp = x_prompt.reshape(bp * tp, D_MODEL)
    xs = x_sample.reshape(bs * ts, D_MODEL)
    p_out = [[] for _ in range(2)]
    s_out = [[] for _ in range(5)]
    for l in range(depth):
        lw = _layer_weights(l, norm1_g, w_in, a_vnorm_g, a_ws, a_bs, c_conv_w, c_a_log, c_dt_bias, c_norm_g,
                            w_out, norm2_g, w_group, w_router, w1, w3, w2)
        lw["final_g"] = final_g[None, :] if l == depth - 1 else None
        xp, win_k, win_v, st = _prompt_layer(lw, xp, bp, tp, tabs_p, gmat, win_k, win_v, l)
        for acc, a in zip(p_out, st):
            acc.append(a)
        xs, st = _sample_layer(lw, xs, bs, ts, tabs_s, gmat, cache_k, cache_v, state_conv[l], state_delta[l], l)
        for acc, a in zip(s_out, st):
            acc.append(a)
    y_prompt = xp.reshape(bp, tp, D_MODEL)
    y_sample = xs.reshape(bs, ts, D_MODEL)
    rows_major = lambda w: jnp.transpose(w.reshape(depth, bp, B_HEADS, HEAD_DIM, keep), (0, 1, 4, 2, 3))
    return (y_prompt, y_sample, rows_major(win_k), rows_major(win_v), *[jnp.stack(a) for a in p_out],
            *[jnp.stack(a) for a in s_out])
```

```python
import functools
import math

import jax
import jax.numpy as jnp
from jax import lax
from jax.experimental import pallas as pl
from jax.experimental.pallas import tpu as pltpu

F32 = jnp.float32
BF16 = jnp.bfloat16
HI = lax.Precision.HIGHEST

D_MODEL = 1024
HEAD_DIM = 64
A_HEADS = 4
B_HEADS = 6
C_HEADS = 6
A_W = A_HEADS * HEAD_DIM
B_W = B_HEADS * HEAD_DIM
C_W = C_HEADS * HEAD_DIM
CHUNK = 128
PATTERNS = ((128, 1), (512, 4), (2048, 16))
MAX_WINDOW = 2048
ROT_DIM = HEAD_DIM // 4
ROPE_THETA = 500000.0
CONV_W = 4
DELTA_CHUNK = 64
N_GROUPS = 4
EXP_PER_GROUP = 8
N_EXPERTS = N_GROUPS * EXP_PER_GROUP
D_EXPERT = 256
EPS = 1e-6
PAST_LEN = 16384

LANES = 128
SUBLANES = 8
WBLK = 128
PROJ_MAIN = 2 * A_W + 3 * B_W + 4 * C_W
NEG = -1e30
ROUTE_OFF = N_GROUPS
VMEM_LIMIT = 56 * 1024 * 1024
MOE_TILE = 1024
MOE_STEP_EXPERTS = 4
MOE_BLOCK = 128
GDN_GROUP = 4


def _cparams(sem):
    return pltpu.CompilerParams(dimension_semantics=sem, vmem_limit_bytes=VMEM_LIMIT)


def _sigmoid(x):
    return 1.0 / (1.0 + jnp.exp(-x))


def _silu(x):
    return x * _sigmoid(x)


def _dot(a, b, precision=None):
    return jnp.dot(a, b, preferred_element_type=F32, precision=precision)


def _dot_nt(a, b, precision=None):
    return lax.dot_general(a, b, (((1,), (1,)), ((), ())), preferred_element_type=F32, precision=precision)


def _chunk_mlp_rows(u, v, gain, w_ref, bias, o_ref, vn_ref):
    xc = v - jnp.mean(v, axis=-1, keepdims=True)
    vn = xc * lax.rsqrt(jnp.mean(xc * xc, axis=-1, keepdims=True) + EPS) * gain
    vn_ref[...] = vn
    vb = vn.astype(BF16)
    rows = lax.broadcasted_iota(jnp.int32, (CHUNK, CHUNK), 0)
    cols = lax.broadcasted_iota(jnp.int32, (CHUNK, CHUNK), 1)
    tril = rows >= cols
    ws = [jnp.where(tril, w_ref[h], 0.0).astype(BF16) for h in range(A_HEADS)]
    for c in range(v.shape[0] // CHUNK):
        rs = slice(c * CHUNK, (c + 1) * CHUNK)
        parts = [_dot(ws[h], vb[rs, h * HEAD_DIM:(h + 1) * HEAD_DIM]) for h in range(A_HEADS)]
        o_ref[rs, :] = u[rs, :] * (jnp.concatenate(parts, axis=-1) + bias)


def _proj_kernel(x_ref, g_ref, w_ref, wsm_ref, c_ref, s1_ref, s2_ref, ag_ref, aw_ref, ab_ref,
                 oa_ref, vn_ref, q_ref, k_ref, v_ref, cqkv_ref, cz_ref, cba_ref, *strided_refs, dils):
    x = x_ref[...]
    h = x * lax.rsqrt(jnp.mean(x * x, axis=-1, keepdims=True) + EPS) * g_ref[...]
    hb = h.astype(BF16)

    def seg(a, b):
        return _dot(hb, w_ref[:, a:b])

    _chunk_mlp_rows(seg(0, A_W), seg(A_W, 2 * A_W), ag_ref[...], aw_ref, ab_ref[...], oa_ref, vn_ref)
    c, s1, s2 = c_ref[...], s1_ref[...], s2_ref[...]
    q0 = 2 * A_W
    k0 = q0 + B_W
    nblk = B_W // LANES
    stage = strided_refs[-1] if dils else None
    for j in range(nblk):
        cols = slice(LANES * j, LANES * (j + 1))
        for a, (base, ref, scale) in enumerate(((q0, q_ref, HEAD_DIM ** -0.5), (k0, k_ref, None))):
            xc = seg(base + LANES * j, base + LANES * (j + 1))
            r = xc * c + pltpu.roll(xc, ROT_DIM // 2, 1) * s1 + pltpu.roll(xc, LANES - ROT_DIM // 2, 1) * s2
            if scale is not None:
                r = r * scale
            ref[:, cols] = r.astype(ref.dtype)
            if dils:
                stage[a * nblk + j] = r
    v0 = k0 + B_W
    v_ref[...] = seg(v0, v0 + B_W)
    c0 = v0 + B_W
    cqkv_ref[...] = seg(c0, c0 + 3 * C_W)
    cz_ref[...] = seg(c0 + 3 * C_W, c0 + 4 * C_W)
    cba_ref[...] = _dot(hb, wsm_ref[...])
    if dils:
        outs = strided_refs[:-1]
        tm = x.shape[0]
        for b in range(nblk):
            stage[2 * nblk + b] = v_ref[:, LANES * b:LANES * (b + 1)]
        for i, d in enumerate(dils):
            for a in range(3):
                dst = outs[3 * i + a]
                for j in range(d):
                    for b in range(nblk):
                        dst[j, :, LANES * b:LANES * (b + 1)] = stage[a * nblk + b,
                                                                     pl.ds(j, tm // d, stride=d), :].astype(dst.dtype)


def _proj(x, g, w_in_b, layer, w_small, tabs, mixer_a, tm, seq_len=None, dils=()):
    n = x.shape[0]
    nt = n // tm
    ntab = tabs[0].shape[0] // tm
    row = lambda i: (i, 0)
    fixed = lambda i: (0, 0)
    tab = lambda i: (i % ntab, 0)
    widths = (A_W, A_W, B_W, B_W, B_W, 3 * C_W, C_W, LANES)
    out_specs = [pl.BlockSpec((tm, w), row) for w in widths]
    out_shape = [jax.ShapeDtypeStruct((n, w), BF16 if i == 2 else F32) for i, w in enumerate(widths)]
    for d in dils:
        tps = seq_len // tm
        out_specs += [pl.BlockSpec((None, d, tm // d, B_W), lambda i, tps=tps: (i // tps, 0, i % tps, 0))] * 3
        out_shape += [jax.ShapeDtypeStruct((n // seq_len, d, seq_len // d, B_W), BF16)] * 3
    return pl.pallas_call(
        functools.partial(_proj_kernel, dils=dils),
        grid=(nt,),
        in_specs=[pl.BlockSpec((tm, D_MODEL), row), pl.BlockSpec((1, D_MODEL), fixed),
                  pl.BlockSpec((None, D_MODEL, PROJ_MAIN), lambda i: (layer, 0, 0)),
                  pl.BlockSpec((D_MODEL, LANES), fixed),
                  pl.BlockSpec((tm, LANES), tab), pl.BlockSpec((tm, LANES), tab), pl.BlockSpec((tm, LANES), tab),
                  pl.BlockSpec((1, A_W), fixed), pl.BlockSpec((A_HEADS, CHUNK, CHUNK), lambda i: (0, 0, 0)),
                  pl.BlockSpec((CHUNK, A_W), fixed)],
        out_specs=out_specs,
        out_shape=out_shape,
        scratch_shapes=[pltpu.VMEM((3 * B_W // LANES, tm, LANES), F32)] if dils else [],
        compiler_params=_cparams(("parallel",)),
        name="proj",
    )(x, g, w_in_b, w_small, *tabs, *mixer_a)


def _rope_tables(pos):
    half = ROT_DIM // 2
    inv_freq = jnp.power(ROPE_THETA, -jnp.arange(0, ROT_DIM, 2, dtype=F32) / ROT_DIM)
    ang = pos.astype(F32)[:, None] * inv_freq[None, :]
    cos, sin = jnp.cos(ang), jnp.sin(ang)
    p = pos.shape[0]
    z8 = jnp.zeros((p, half), F32)
    rest0 = jnp.zeros((p, HEAD_DIM - ROT_DIM), F32)
    c64 = jnp.concatenate([cos, cos, jnp.ones((p, HEAD_DIM - ROT_DIM), F32)], axis=-1)
    s1 = jnp.concatenate([z8, sin, rest0], axis=-1)
    s2 = jnp.concatenate([-sin, z8, rest0], axis=-1)
    two = lambda a: jnp.concatenate([a, a], axis=-1)
    return two(c64), two(s1), two(s2)


def _win_attn_kernel(q_ref, kp_ref, kc_ref, vp_ref, vc_ref, o_ref, lse_ref, *, seg_blocks):
    s = pl.program_id(0)
    rows = lax.broadcasted_iota(jnp.int32, (WBLK, 2 * WBLK), 0)
    cols = lax.broadcasted_iota(jnp.int32, (WBLK, 2 * WBLK), 1)
    dist = rows + WBLK - cols
    band = (dist >= 0) & (dist <= WBLK)
    lo = jnp.where((2 * s) % seg_blocks == 0, WBLK, 0)
    biases = [jnp.where(band & (cols >= lo), 0.0, NEG), jnp.where(band, 0.0, NEG)]
    lane = lax.broadcasted_iota(jnp.int32, (WBLK, LANES), 1)
    k3 = jnp.concatenate([kp_ref[...], kc_ref[...]], axis=0).astype(BF16)
    v3 = jnp.concatenate([vp_ref[...], vc_ref[...]], axis=0).astype(BF16)
    heads = range(B_HEADS)
    sls = [slice(h // 2 * LANES, (h // 2 + 1) * LANES) for h in heads]
    hms = [(lane < HEAD_DIM) if h % 2 == 0 else (lane >= HEAD_DIM) for h in heads]
    work = [(j, h) for j in range(2) for h in heads]
    qs = [q_ref[j * WBLK:(j + 1) * WBLK, :] for j in range(2)]
    ks = [k3[j * WBLK:(j + 2) * WBLK, :] for j in range(2)]
    vs = [v3[j * WBLK:(j + 2) * WBLK, :] for j in range(2)]
    scs = [_dot_nt(jnp.where(hms[h], qs[j][:, sls[h]], 0.0).astype(BF16), ks[j][:, sls[h]]) + biases[j]
           for j, h in work]
    ms = [jnp.max(sc, axis=-1, keepdims=True) for sc in scs]
    es = [jnp.exp(sc - m) for sc, m in zip(scs, ms)]
    dens = [jnp.sum(e, axis=-1, keepdims=True) for e in es]
    os_ = [_dot((e * (1.0 / d)).astype(BF16), vs[j][:, sls[h]]) for (j, h), e, d in zip(work, es, dens)]
    for j in range(2):
        base = j * B_HEADS
        lse_tile = jnp.zeros((WBLK, LANES), F32)
        for h in heads:
            lse_tile = jnp.where(lane == h, ms[base + h] + jnp.log(dens[base + h]), lse_tile)
        lse_ref[j * WBLK:(j + 1) * WBLK, :] = lse_tile
        for hp in range(B_W // LANES):
            o_ref[j * WBLK:(j + 1) * WBLK, sls[2 * hp]] = jnp.where(hms[2 * hp], os_[base + 2 * hp],
                                                                  os_[base + 2 * hp + 1])


def _win_attn(qd, kd, vd, seg_blocks):
    n = qd.shape[0]
    assert seg_blocks % 2 == 0
    cur = lambda s: (s, 0)
    prev = lambda s: (jnp.maximum(2 * s - 1, 0), 0)
    blk = lambda w, im: pl.BlockSpec((2 * WBLK, w), im)
    pblk = pl.BlockSpec((WBLK, B_W), prev)
    return pl.pallas_call(
        functools.partial(_win_attn_kernel, seg_blocks=seg_blocks),
        grid=(n // (2 * WBLK),),
        in_specs=[blk(B_W, cur), pblk, blk(B_W, cur), pblk, blk(B_W, cur)],
        out_specs=[blk(B_W, cur), blk(LANES, cur)],
        out_shape=[jax.ShapeDtypeStruct((n, B_W), F32), jax.ShapeDtypeStruct((n, LANES), F32)],
        compiler_params=_cparams(("parallel",)),
        name="win_attn",
    )(qd, kd, kd, vd, vd)


def _window_rows_kernel(k_ref, v_ref, kprev_ref, vprev_ref, ko_ref, vo_ref):
    del kprev_ref, vprev_ref
    for b in range(k_ref.shape[0]):
        ko_ref[b] = k_ref[b].T
        vo_ref[b] = v_ref[b].T


def _window_rows(k, v, k_all, v_all, layer, batch, t, keep):
    first = (t - keep) // WBLK
    src = pl.BlockSpec((batch, WBLK, B_W), lambda i: (0, first + i, 0))
    dst = pl.BlockSpec((None, batch, B_W, WBLK), lambda i: (layer, 0, 0, i))
    hbm = pl.BlockSpec(memory_space=pl.ANY)
    shape = jax.ShapeDtypeStruct(k_all.shape, F32)
    return pl.pallas_call(
        _window_rows_kernel,
        grid=(keep // WBLK,),
        in_specs=[src, src, hbm, hbm],
        out_specs=[dst, dst],
        out_shape=[shape, shape],
        input_output_aliases={2: 0, 3: 1},
        compiler_params=_cparams(("parallel",)),
        name="window_rows",
    )(k.reshape(batch, t, B_W), v.reshape(batch, t, B_W), k_all, v_all)


def _dec_attn_kernel(q_ref, kc_ref, vc_ref, kn_ref, vn_ref, o_ref, *, t_new, cache_len):
    rows_c = lax.broadcasted_iota(jnp.int32, (SUBLANES, cache_len), 0)
    cols_c = lax.broadcasted_iota(jnp.int32, (SUBLANES, cache_len), 1)
    dist_c = cache_len + rows_c % t_new - cols_c
    rows_n = lax.broadcasted_iota(jnp.int32, (SUBLANES, SUBLANES), 0)
    cols_n = lax.broadcasted_iota(jnp.int32, (SUBLANES, SUBLANES), 1)
    dist_n = rows_n % t_new - cols_n
    biases = []
    for window, dil in PATTERNS:
        vc_ok = (dist_c <= window) & ((dist_c & (dil - 1)) == 0)
        vn_ok = (dist_n >= 0) & ((dist_n & (dil - 1)) == 0)
        biases.append((jnp.where(vc_ok, 0.0, NEG), jnp.where(vn_ok, 0.0, NEG)))
    row8 = lax.broadcasted_iota(jnp.int32, (SUBLANES, LANES), 0)
    lane8 = lax.broadcasted_iota(jnp.int32, (SUBLANES, LANES), 1)
    own = (lane8 < HEAD_DIM) == (row8 < t_new)
    q = q_ref[...].astype(F32)
    outs = []
    for hp in range(B_W // LANES):
        sl = slice(hp * LANES, (hp + 1) * LANES)
        qq = jnp.concatenate([q[0:t_new, sl], q[0:t_new, sl]], axis=0)
        q8 = jnp.where(own, qq, 0.0).astype(BF16)
        kc, vc = kc_ref[sl, :].astype(BF16), vc_ref[sl, :].astype(BF16)
        kn, vn = kn_ref[:, sl].astype(BF16), vn_ref[:, sl].astype(BF16)
        sc_c = _dot(q8, kc)
        sc_n = _dot_nt(q8, kn)
        os_, lses = [], []
        for bc, bn in biases:
            a_c, a_n = sc_c + bc, sc_n + bn
            m = jnp.maximum(jnp.max(a_c, axis=-1, keepdims=True), jnp.max(a_n, axis=-1, keepdims=True))
            e_c, e_n = jnp.exp(a_c - m), jnp.exp(a_n - m)
            den = jnp.sum(e_c, axis=-1, keepdims=True) + jnp.sum(e_n, axis=-1, keepdims=True)
            inv = 1.0 / den
            os_.append(_dot_nt((e_c * inv).astype(BF16), vc) + _dot((e_n * inv).astype(BF16), vn))
            lses.append(m + jnp.log(den))
        mx = jnp.maximum(jnp.maximum(lses[0], lses[1]), lses[2])
        ws = [jnp.exp(l - mx) for l in lses]
        tot = ws[0] + ws[1] + ws[2]
        o8 = (ws[0] / tot) * os_[0] + (ws[1] / tot) * os_[1] + (ws[2] / tot) * os_[2]
        lane4 = lane8[0:t_new]
        outs.append(jnp.where(lane4 < HEAD_DIM, o8[0:t_new], o8[t_new:2 * t_new]))
    o_ref[...] = jnp.zeros((SUBLANES, B_W), F32)
    o_ref[0:t_new, :] = jnp.concatenate(outs, axis=-1)


def _dec_attn(q8, kc, vc, kn8, vn8, t_new, layer):
    _, b, _, cache_len = kc.shape
    assert 2 * t_new == SUBLANES
    small = pl.BlockSpec((None, SUBLANES, B_W), lambda i: (i, 0, 0))
    big = pl.BlockSpec((None, None, B_W, cache_len), lambda i: (layer, i, 0, 0))
    return pl.pallas_call(
        functools.partial(_dec_attn_kernel, t_new=t_new, cache_len=cache_len),
        grid=(b,),
        in_specs=[small, big, big, small, small],
        out_specs=small,
        out_shape=jax.ShapeDtypeStruct((b, SUBLANES, B_W), F32),
        compiler_params=_cparams(("parallel",)),
        name="dec_attn",
    )(q8, kc, vc, kn8, vn8)


def _split3(x):
    hi = x.astype(BF16)
    r1 = x - hi.astype(F32)
    mid = r1.astype(BF16)
    return hi, mid, (r1 - mid.astype(F32)).astype(BF16)


def _pair_dup(xx):
    lane = lax.broadcasted_iota(jnp.int32, xx.shape, 1)
    hi = xx.astype(BF16).astype(F32)
    return jnp.where(lane < HEAD_DIM, xx, xx - hi).astype(BF16)


def _pair(x):
    return _pair_dup(jnp.concatenate([x, x], axis=1))


def _lhs4(pair):
    return jnp.concatenate([pair, pair], axis=1)


def _rhs4(y):
    hi = y.astype(BF16)
    lo = (y - hi.astype(F32)).astype(BF16)
    return jnp.concatenate([hi, hi, lo, lo], axis=0)


def _unit_lower_solves(mats, rhss):
    n = DELTA_CHUNK
    w2 = 2 * HEAD_DIM
    rows = lax.broadcasted_iota(jnp.int32, (n, w2), 0)
    cols = lax.broadcasted_iota(jnp.int32, (n, w2), 1) % HEAD_DIM
    in16 = rows // 16 == cols // 16
    eye = jnp.where(rows == cols, 1.0, 0.0)
    ds = [jnp.where(in16, a, 0.0) for a in mats]
    es = [jnp.where(in16, 0.0, a) for a in mats]
    left = lambda xx: _lhs4(_pair_dup(xx))
    pw = [_dot(left(d), _rhs4(d)) for d in ds]
    ts = [eye - d for d in ds]
    for _ in range(2):
        outs = [_dot(jnp.concatenate([left(t), left(p)], axis=0), _rhs4(p)) for t, p in zip(ts, pw)]
        ts = [t + o[0:n] for t, o in zip(ts, outs)]
        pw = [o[n:2 * n] for o in outs]
    ts = [t + _dot(left(t), _rhs4(p)) for t, p in zip(ts, pw)]
    o5 = [_dot(left(t), _rhs4(jnp.concatenate([r, e], axis=1))) for t, r, e in zip(ts, rhss, es)]
    x0 = [o[:, 0:w2] for o in o5]
    nm = [o[:, w2:2 * w2] for o in o5]
    o6 = [_dot(left(m), _rhs4(jnp.concatenate([x, m], axis=1))) for m, x in zip(nm, x0)]
    ys = [x - o[:, 0:w2] for x, o in zip(x0, o6)]
    return [y + _dot(left(o[:, w2:2 * w2]), _rhs4(y)) for y, o in zip(ys, o6)]


def _gdn_prep_kernel(x_ref, cba_ref, buf_ref, cw_ref, alog_ref, dtb_ref, gmat_ref, xmat_ref,
                     u_ref, wq_ref, ak_ref, gt_ref,
                     xp_scr, q_scr, k_scr, v_scr, gb_scr, *, tm, valid_len, group):
    ti = pl.program_id(1)

    @pl.when(ti == 0)
    def _():
        xp_scr[0:SUBLANES, :] = buf_ref[...]

    x = x_ref[...]
    xp_scr[SUBLANES:SUBLANES + tm, :] = x
    off = SUBLANES - (CONV_W - 1)
    acc = xp_scr[off:off + tm, :] * cw_ref[0:1, :]
    for j in range(1, CONV_W):
        acc = acc + xp_scr[off + j:off + j + tm, :] * cw_ref[j:j + 1, :]
    xp_scr[0:SUBLANES, :] = x[tm - SUBLANES:tm, :]
    y = _silu(acc)
    q, k = y[:, 0:C_W], y[:, C_W:2 * C_W]
    gmat = gmat_ref[...]
    head_sum = lambda a: sum(_dot(p, gmat) for p in _split3(a))
    q_scr[...] = q * lax.rsqrt(head_sum(q * q) + EPS) * (HEAD_DIM ** -0.5)
    k_scr[...] = k * lax.rsqrt(head_sum(k * k) + EPS)
    v_scr[...] = y[:, 2 * C_W:3 * C_W]

    cba = cba_ref[...]
    lane = lax.broadcasted_iota(jnp.int32, (tm, LANES), 1)
    tpos = ti * tm + lax.broadcasted_iota(jnp.int32, (tm, LANES), 0)
    live = tpos < valid_len
    beta = _sigmoid(cba)
    z = cba + dtb_ref[...]
    softplus = jnp.maximum(z, 0.0) + jnp.log(1.0 + jnp.exp(-jnp.abs(z)))
    g = -jnp.exp(alog_ref[...]) * softplus
    is_g = (lane >= C_HEADS) & (lane < 2 * C_HEADS)
    gb_scr[...] = jnp.where(live, jnp.where(is_g, g, jnp.where(lane < C_HEADS, beta, 0.0)), 0.0)

    n = DELTA_CHUNK
    rows = lax.broadcasted_iota(jnp.int32, (n, 2 * n), 0)
    cols = lax.broadcasted_iota(jnp.int32, (n, 2 * n), 1) % n
    incl = rows >= cols
    strict = rows > cols
    ltri = jnp.where(incl[:, 0:n], 1.0, 0.0).astype(BF16)
    heads = range(C_HEADS)
    tiles = range(C_W // LANES)
    tsl = [slice(t * LANES, (t + 1) * LANES) for t in tiles]
    lane_c = lax.broadcasted_iota(jnp.int32, (n, LANES), 1)
    low = lane_c < HEAD_DIM
    own = [low if h % 2 == 0 else jnp.logical_not(low) for h in heads]
    xmat = xmat_ref[...]
    wide0 = C_HEADS * LANES

    def head_pairs(tile):
        rolled = pltpu.roll(tile, HEAD_DIM, 1)
        lo = rolled - rolled.astype(BF16).astype(F32)
        lane = lax.broadcasted_iota(jnp.int32, tile.shape, 1) < HEAD_DIM
        return jnp.where(lane, tile, lo).astype(BF16), jnp.where(lane, lo, tile).astype(BF16)

    def setup(c):
        rs = pl.ds(pl.multiple_of(c * n, n), n)
        gb = gb_scr[rs, :]
        g_only = jnp.where((lane_c >= C_HEADS) & (lane_c < 2 * C_HEADS), gb, 0.0)
        cg = sum(_dot(ltri, p) for p in _split3(g_only))
        cgt = cg.T
        cgl = cg[n - 1:n, :]
        gt_ref[pl.ds(pl.multiple_of(c * SUBLANES, SUBLANES), SUBLANES), :] = jnp.broadcast_to(
            jnp.exp(cgl), (SUBLANES, LANES))
        wide = sum(_dot(p, xmat) for p in _split3(jnp.where(lane_c < C_HEADS, gb, cg)))
        b128 = [wide[:, h * LANES:(h + 1) * LANES] for h in heads]
        c128 = [wide[:, wide0 + h * LANES:wide0 + (h + 1) * LANES] for h in heads]
        cgr = [jnp.concatenate([cgt[C_HEADS + h:C_HEADS + h + 1, :]] * 2, axis=1) for h in heads]
        decay = [jnp.where(incl, jnp.exp(jnp.minimum(c128[h] - cgr[h], 0.0)), 0.0) for h in heads]
        amat, at_pair, rhs, qd_pair, kdt_pair = [], [], [], [], []
        for t in tiles:
            he, ho = 2 * t, 2 * t + 1
            qn, kn, vv = q_scr[rs, tsl[t]], k_scr[rs, tsl[t]], v_scr[rs, tsl[t]]
            bnat = jnp.where(low, b128[he], b128[ho])
            cnat = jnp.where(low, c128[he], c128[ho])
            ecg = jnp.exp(cnat)
            kb = kn.astype(BF16)
            kk2 = jnp.concatenate([kb, kb], axis=0)
            for h in (he, ho):
                masked = jnp.concatenate([jnp.where(own[h], kn, 0.0), jnp.where(own[h], qn, 0.0)], axis=0)
                kkqk = _dot_nt(masked.astype(BF16), kk2)
                amat.append(jnp.where(strict, b128[h] * kkqk[0:n] * decay[h], 0.0))
                at_pair.append(_pair_dup(kkqk[n:2 * n] * decay[h]))
            bv = bnat * vv
            bk_rolled = pltpu.roll((bnat * ecg) * kn, HEAD_DIM, 1)
            rhs += [jnp.where(low, bv, bk_rolled), jnp.where(low, bk_rolled, bv)]
            qd_pair += list(head_pairs(qn * ecg))
            kd = kn * jnp.exp(cnat[n - 1:n, :] - cnat)
            kdt = jnp.concatenate([kd, kd], axis=0).T
            kdt_pair += [_pair_dup(kdt[0:n]), _pair_dup(kdt[n:2 * n])]
        return amat, rhs, at_pair, qd_pair, kdt_pair

    def finish(c, sol, at_pair, qd_pair, kdt_pair):
        w_pair = []
        for t in tiles:
            se, so = sol[2 * t], sol[2 * t + 1]
            u_ref[pl.ds(pl.multiple_of(c * n, n), n), tsl[t]] = jnp.where(low, se, so)
            w_odd, w_even = head_pairs(jnp.where(low, so, se))
            w_pair += [w_even, w_odd]
        r2 = pl.ds(pl.multiple_of(c * 2 * n, 2 * n), 2 * n)
        wq_ref[r2, :] = jnp.concatenate(
            [jnp.concatenate([w_pair[h], qd_pair[h]], axis=0) for h in heads], axis=1)
        ak_ref[r2, :] = jnp.concatenate(
            [jnp.concatenate([at_pair[h], kdt_pair[h]], axis=0) for h in heads], axis=1)

    def chunks(i, carry):
        parts = [setup(i * group + j) for j in range(group)]
        sol = _unit_lower_solves([a for p in parts for a in p[0]], [r for p in parts for r in p[1]])
        for j, p in enumerate(parts):
            finish(i * group + j, sol[j * C_HEADS:(j + 1) * C_HEADS], *p[2:])
        return carry

    lax.fori_loop(0, tm // n // group, chunks, 0)


def _gdn_constants():
    hid = jnp.arange(C_W, dtype=jnp.int32) // HEAD_DIM
    gmat = (hid[:, None] == hid[None, :]).astype(BF16)
    src = jnp.arange(LANES, dtype=jnp.int32)[:, None]
    dst = jnp.arange(2 * C_HEADS * LANES, dtype=jnp.int32)[None, :] // LANES
    return gmat, (src == dst).astype(BF16)


def _gdn_prep(cqkv, cba, buf8, conv_w, alog_row, dtb_row, consts, batch, t, tm, valid_len):
    gmat, xmat = consts
    n = batch * t
    nt = t // tm
    row = lambda b, i: (b * nt + i, 0)
    fixed = lambda b, i: (0, 0)
    nch = tm // DELTA_CHUNK
    pair_w = C_HEADS * 2 * HEAD_DIM
    outs = [jax.ShapeDtypeStruct((n, C_W), F32), jax.ShapeDtypeStruct((2 * n, pair_w), BF16),
            jax.ShapeDtypeStruct((2 * n, pair_w), BF16),
            jax.ShapeDtypeStruct((n // DELTA_CHUNK * SUBLANES, LANES), F32)]
    return pl.pallas_call(
        functools.partial(_gdn_prep_kernel, tm=tm, valid_len=valid_len, group=min(GDN_GROUP, nch)),
        grid=(batch, nt),
        in_specs=[pl.BlockSpec((tm, 3 * C_W), row), pl.BlockSpec((tm, LANES), row),
                  pl.BlockSpec((None, SUBLANES, 3 * C_W), lambda b, i: (b, 0, 0)),
                  pl.BlockSpec((CONV_W, 3 * C_W), fixed), pl.BlockSpec((1, LANES), fixed),
                  pl.BlockSpec((1, LANES), fixed), pl.BlockSpec((C_W, C_W), fixed),
                  pl.BlockSpec(xmat.shape, fixed)],
        out_specs=[pl.BlockSpec((tm, C_W), row), pl.BlockSpec((2 * tm, pair_w), row),
                   pl.BlockSpec((2 * tm, pair_w), row), pl.BlockSpec((nch * SUBLANES, LANES), row)],
        out_shape=outs,
        scratch_shapes=[pltpu.VMEM((tm + SUBLANES, 3 * C_W), F32), pltpu.VMEM((tm, C_W), F32),
                        pltpu.VMEM((tm, C_W), F32), pltpu.VMEM((tm, C_W), F32), pltpu.VMEM((tm, LANES), F32)],
        compiler_params=_cparams(("parallel", "arbitrary")),
        name="gdn_prep",
    )(cqkv, cba, buf8, conv_w, alog_row, dtb_row, gmat, xmat)


def _gdn_scan_kernel(u_ref, wq_ref, ak_ref, gt_ref, z_ref, s0_ref, gn_ref,
                     o_ref, sfin_ref, s_scr, *, bg):
    c = pl.program_id(1)

    @pl.when(c == 0)
    def _():
        s_scr[...] = s0_ref[...]

    gn = gn_ref[...]
    n = DELTA_CHUNK
    chains = [(b, h, slice(h * HEAD_DIM, (h + 1) * HEAD_DIM), slice(h * 2 * HEAD_DIM, (h + 1) * 2 * HEAD_DIM))
              for b in range(bg) for h in range(C_HEADS)]
    st = [s_scr[b, h] for b, h, _, _ in chains]
    r1 = [_dot(_lhs4(wq_ref[b, :, ps]), _rhs4(s)) for (b, _, _, ps), s in zip(chains, st)]
    up = [u_ref[b, :, sl] - r[0:n] for (b, _, sl, _), r in zip(chains, r1)]
    r2 = [_dot(_lhs4(ak_ref[b, :, ps]), _rhs4(x)) for (b, _, _, ps), x in zip(chains, up)]
    for (b, h, _, _), s, r in zip(chains, st, r2):
        s_scr[b, h] = gt_ref[b, 0:1, C_HEADS + h:C_HEADS + h + 1] * s + r[n:2 * n]
    os_ = [a[n:2 * n] + r[0:n] for a, r in zip(r1, r2)]
    outs = [o * lax.rsqrt(jnp.mean(o * o, axis=-1, keepdims=True) + EPS) * gn * _silu(z_ref[b, :, sl])
            for (b, _, sl, _), o in zip(chains, os_)]
    for b in range(bg):
        o_ref[b] = jnp.concatenate(outs[b * C_HEADS:(b + 1) * C_HEADS], axis=1)

    @pl.when(c == pl.num_programs(1) - 1)
    def _():
        sfin_ref[...] = s_scr[...]


def _gdn_scan(u, wq, ak, gt, z, s0, gnorm, batch, t, bg):
    n = DELTA_CHUNK
    nc = t // n
    pair_w = C_HEADS * 2 * HEAD_DIM
    v3 = lambda a: a.reshape(batch, t, C_W)
    p3 = lambda a: a.reshape(batch, 2 * t, pair_w)
    tok = pl.BlockSpec((bg, n, C_W), lambda b, c: (b, c, 0))
    pair = pl.BlockSpec((bg, 2 * n, pair_w), lambda b, c: (b, c, 0))
    st = pl.BlockSpec((bg, C_HEADS, HEAD_DIM, HEAD_DIM), lambda b, c: (b, 0, 0, 0))
    return pl.pallas_call(
        functools.partial(_gdn_scan_kernel, bg=bg),
        grid=(batch // bg, nc),
        in_specs=[tok, pair, pair,
                  pl.BlockSpec((bg, SUBLANES, LANES), lambda b, c: (b, c, 0)),
                  tok, st, pl.BlockSpec((1, HEAD_DIM), lambda b, c: (0, 0))],
        out_specs=[tok, st],
        out_shape=[jax.ShapeDtypeStruct((batch, t, C_W), F32),
                   jax.ShapeDtypeStruct((batch, C_HEADS, HEAD_DIM, HEAD_DIM), F32)],
        scratch_shapes=[pltpu.VMEM((bg, C_HEADS, HEAD_DIM, HEAD_DIM), F32)],
        compiler_params=_cparams(("parallel", "arbitrary")),
        name="gdn_scan",
    )(v3(u), p3(wq), p3(ak), gt.reshape(batch, nc * SUBLANES, LANES), v3(z), s0, gnorm)


def _head_expand(wt):
    src = lax.broadcasted_iota(jnp.int32, (LANES, B_W), 0)
    dst = lax.broadcasted_iota(jnp.int32, (LANES, B_W), 1) // HEAD_DIM
    spread = jnp.where(src == dst, 1.0, 0.0).astype(BF16)
    return sum(_dot(p, spread) for p in _split3(wt))


def _out_proj_kernel(*refs, dils):
    if dils:
        npat = len(dils)
        a_ref = refs[0]
        o_refs, l_refs = refs[1:1 + npat], refs[1 + npat:1 + 2 * npat]
        c_ref, x_ref, w_ref, g2_ref, wr_ref, y_ref, dense_ref, o_scr, l_scr = refs[1 + 2 * npat:]
        tm = x_ref.shape[0]
        os_, ls = [], []
        for i, d in enumerate(dils):
            if d == 1:
                os_.append(o_refs[i][0])
                ls.append(l_refs[i][0])
            else:
                nblk = B_W // LANES
                for j in range(d):
                    rows = pl.ds(j, tm // d, stride=d)
                    for b in range(nblk):
                        o_scr[i * nblk + b, rows, :] = o_refs[i][j, :, LANES * b:LANES * (b + 1)]
                    l_scr[i, rows, :] = l_refs[i][j]
                os_.append(jnp.concatenate([o_scr[i * nblk + b] for b in range(nblk)], axis=-1))
                ls.append(l_scr[i])
        mx = functools.reduce(jnp.maximum, ls)
        es = [jnp.exp(l - mx) for l in ls]
        tot = functools.reduce(lambda a, b: a + b, es)
        ob = functools.reduce(lambda a, b: a + b, [_head_expand(e / tot) * o for e, o in zip(es, os_)])
    else:
        a_ref, b_ref, c_ref, x_ref, w_ref, g2_ref, wr_ref, y_ref, dense_ref = refs
        ob = b_ref[...]
    cat = jnp.concatenate([a_ref[...], ob, c_ref[...]], axis=-1).astype(BF16)
    y = x_ref[...] + _dot(cat, w_ref[...])
    y_ref[...] = y
    dense_ref[...] = _route_tile(y, g2_ref[...], wr_ref[...])


def _out_proj(out_a, out_b, lses, out_c, x, w_out, g2, w_route, tm, seq_len=None, dils=()):
    n = x.shape[0]
    row = lambda i: (i, 0)
    spec = lambda w: pl.BlockSpec((tm, w), row)
    scratch = []
    if dils:
        tps = seq_len // tm
        strided = lambda d, w: pl.BlockSpec((None, d, tm // d, w), lambda i: (i // tps, 0, i % tps, 0))
        ins = [out_a, *out_b, *lses, out_c, x, w_out]
        specs = ([spec(A_W)] + [strided(d, B_W) for d in dils] + [strided(d, LANES) for d in dils]
                 + [spec(C_W), spec(D_MODEL)])
        scratch = [pltpu.VMEM((len(dils) * B_W // LANES, tm, LANES), F32), pltpu.VMEM((len(dils), tm, LANES), F32)]
    else:
        ins = [out_a, out_b, out_c, x, w_out]
        specs = [spec(A_W), spec(B_W), spec(C_W), spec(D_MODEL)]
    fixed = lambda i: (0, 0)
    specs += [pl.BlockSpec((D_MODEL, D_MODEL), fixed), pl.BlockSpec((1, D_MODEL), fixed),
              pl.BlockSpec((D_MODEL, LANES), fixed)]
    return pl.pallas_call(
        functools.partial(_out_proj_kernel, dils=dils),
        grid=(n // tm,),
        in_specs=specs,
        out_specs=[spec(D_MODEL), spec(LANES)],
        out_shape=[jax.ShapeDtypeStruct((n, D_MODEL), F32), jax.ShapeDtypeStruct((n, LANES), F32)],
        scratch_shapes=scratch,
        compiler_params=_cparams(("parallel",)),
        name="out_proj",
    )(*ins, g2, w_route)


def _route_tile(x, g, w):
    t = x * lax.rsqrt(jnp.mean(x * x, axis=-1, keepdims=True) + EPS) * g
    t_hi = t.astype(BF16)
    t_lo = (t - t_hi.astype(F32)).astype(BF16)
    w_hi = w.astype(BF16)
    w_lo = (w - w_hi.astype(F32)).astype(BF16)
    lg = _dot(t_hi, w_hi) + (_dot(t_hi, w_lo) + _dot(t_lo, w_hi))
    tm = lg.shape[0]
    nrow = ROUTE_OFF + N_EXPERTS + ROUTE_OFF
    lt = lg.T[0:nrow, :]
    row = lax.broadcasted_iota(jnp.int32, (nrow, tm), 0).astype(F32)
    big = float(LANES)
    down = lambda op, a: op(a, axis=0, keepdims=True)
    is_grp = row < N_GROUPS
    gl = jnp.where(is_grp, lt, NEG)
    gmax = down(jnp.max, gl)
    gsum = down(jnp.sum, jnp.where(is_grp, jnp.exp(gl - gmax), 0.0))
    g_w = 1.0 / gsum
    g_idx = down(jnp.min, jnp.where(is_grp & (gl == gmax), row, big))
    lo = ROUTE_OFF + EXP_PER_GROUP * g_idx
    sel = (row >= lo) & (row < lo + EXP_PER_GROUP)
    el = jnp.where(sel, lt, NEG)
    m1 = down(jnp.max, el)
    esum = down(jnp.sum, jnp.where(sel, jnp.exp(el - m1), 0.0))
    i1 = down(jnp.min, jnp.where(sel & (el == m1), row, big))
    el2 = jnp.where(row == i1, NEG, el)
    m2 = down(jnp.max, el2)
    i2 = down(jnp.min, jnp.where(sel & (row != i1) & (el2 == m2), row, big))
    p1 = 1.0 / esum
    p2 = jnp.exp(m2 - m1) / esum
    tot = p1 + p2
    gates = jnp.where(row == i1, g_w * (p1 / tot), 0.0) + jnp.where(row == i2, g_w * (p2 / tot), 0.0)
    full = jnp.concatenate([jnp.where(row == 0.0, g_idx, gates), jnp.zeros((LANES - nrow, tm), F32)], axis=0)
    return full.T


def _rms_rows(y, g):
    return y * lax.rsqrt(jnp.mean(y * y, axis=-1, keepdims=True) + EPS) * g


def _moe_kernel(x_ref, g_ref, dense_ref, w1_ref, w3_ref, w2_ref, *rest):
    fg_ref = rest[0] if len(rest) == 3 else None
    y_ref, t_scr = rest[-2:]
    e = pl.program_id(1)

    @pl.when(e == 0)
    def _():
        x = x_ref[...]
        t_scr[...] = (x * lax.rsqrt(jnp.mean(x * x, axis=-1, keepdims=True) + EPS) * g_ref[...]).astype(BF16)
        y_ref[...] = x

    tb = t_scr[...]
    dense = dense_ref[...]
    lane = lax.broadcasted_iota(jnp.int32, dense.shape, 1)
    gate = jnp.sum(jnp.where(lane == e + ROUTE_OFF, dense, 0.0), axis=-1, keepdims=True)
    hid = _silu(_dot(tb, w1_ref[...].astype(BF16))) * _dot(tb, w3_ref[...].astype(BF16))
    y_ref[...] += _dot((hid * gate).astype(BF16), w2_ref[...].astype(BF16))

    if fg_ref is not None:
        @pl.when(e == pl.num_programs(1) - 1)
        def _():
            y_ref[...] = _rms_rows(y_ref[...], fg_ref[...])


def _moe(x, g, dense, w1, w3, w2, layer, tm, final_g=None):
    n = x.shape[0]
    row = lambda i, e: (i, 0)
    expert = lambda i, e: (layer, e, 0, 0)
    extra = [] if final_g is None else [final_g]
    return pl.pallas_call(
        _moe_kernel,
        grid=(n // tm, N_EXPERTS),
        in_specs=[pl.BlockSpec((tm, D_MODEL), row), pl.BlockSpec((1, D_MODEL), lambda i, e: (0, 0)),
                  pl.BlockSpec((tm, LANES), row),
                  pl.BlockSpec((None, None, D_MODEL, D_EXPERT), expert),
                  pl.BlockSpec((None, None, D_MODEL, D_EXPERT), expert),
                  pl.BlockSpec((None, None, D_EXPERT, D_MODEL), expert)]
        + [pl.BlockSpec((1, D_MODEL), lambda i, e: (0, 0))] * len(extra),
        out_specs=pl.BlockSpec((tm, D_MODEL), row),
        out_shape=jax.ShapeDtypeStruct((n, D_MODEL), F32),
        scratch_shapes=[pltpu.VMEM((tm, D_MODEL), BF16)],
        compiler_params=_cparams(("parallel", "arbitrary")),
        name="moe",
    )(x, g, dense, w1, w3, w2, *extra)


def _moe_grouped_kernel(x_ref, g_ref, dense_ref, gidr_ref, tri_ref, w1_ref, w3_ref, w2_ref, *rest,
                        tm, slots, final):
    fg_ref = rest[0] if final else None
    y_ref, ts_scr, gs_scr, ys_scr, pt_scr, meta = rest[-6:]
    e = pl.program_id(1)
    blk = MOE_BLOCK

    @pl.when(e == 0)
    def _():
        x = x_ref[...]
        t = (x * lax.rsqrt(jnp.mean(x * x, axis=-1, keepdims=True) + EPS) * g_ref[...]).astype(BF16)
        dense = dense_ref[...]
        tri = tri_ref[...]
        lane = lax.broadcasted_iota(jnp.int32, (tm, LANES), 1)
        ohc = jnp.where(lane.astype(F32) == dense[:, 0:1], 1.0, 0.0)
        rankc = _dot(tri, ohc.astype(BF16))
        sub = lax.broadcasted_iota(jnp.int32, (SUBLANES, tm), 0)
        ohr = jnp.where(sub.astype(F32) == gidr_ref[...], 1.0, 0.0)
        rankr = _dot_nt(ohr.astype(BF16), tri)
        start = jnp.int32(0)
        s_lane = jnp.zeros((tm, LANES), F32)
        s_sub = jnp.zeros((SUBLANES, tm), F32)
        for g in range(N_GROUPS):
            count = jnp.sum(jnp.where(lane == g, ohc, 0.0)).astype(jnp.int32)
            nblk = (count + blk - 1) // blk
            meta[g] = start
            meta[N_GROUPS + g] = nblk
            s_lane = jnp.where(lane == g, start.astype(F32), s_lane)
            s_sub = jnp.where(sub == g, start.astype(F32), s_sub)
            start = start + nblk * blk
        destc = jnp.sum(ohc * (s_lane + rankc), axis=-1, keepdims=True)
        destr = jnp.sum(ohr * (s_sub + rankr), axis=0, keepdims=True)
        slot_r = lax.broadcasted_iota(jnp.int32, (slots, tm), 0).astype(F32)
        p = jnp.where(slot_r == destr, 1.0, 0.0).astype(BF16)
        ts_scr[...] = _dot(p, t).astype(BF16)
        g3 = _dot(p, jnp.concatenate(_split3(dense), axis=1))
        gs_scr[...] = g3[:, 0:LANES] + (g3[:, LANES:2 * LANES] + g3[:, 2 * LANES:3 * LANES])
        slot_c = lax.broadcasted_iota(jnp.int32, (tm, slots), 1).astype(F32)
        pt_scr[...] = jnp.where(slot_c == destc, 1.0, 0.0).astype(BF16)
        ys_scr[...] = jnp.zeros((slots, D_MODEL), F32)

    ne = MOE_STEP_EXPERTS
    g = (e * ne) // EXP_PER_GROUP
    start = meta[g]
    nblk = meta[N_GROUPS + g]
    w13 = jnp.concatenate([w1_ref[j] for j in range(ne)] + [w3_ref[j] for j in range(ne)], axis=1)
    w2 = jnp.concatenate([w2_ref[j] for j in range(ne)], axis=0)
    half = ne * D_EXPERT

    def rows_update(r0, m):
        rows = pl.ds(pl.multiple_of(r0, blk), m)
        tb = ts_scr[rows, :]
        gs = gs_scr[rows, :]
        lane = lax.broadcasted_iota(jnp.int32, (m, LANES), 1)
        h13 = _dot(tb, w13)
        hid = _silu(h13[:, 0:half]) * h13[:, half:2 * half]
        gated = [hid[:, j * D_EXPERT:(j + 1) * D_EXPERT]
                 * jnp.sum(jnp.where(lane == e * ne + j + ROUTE_OFF, gs, 0.0), axis=-1, keepdims=True)
                 for j in range(ne)]
        ys_scr[rows, :] += _dot(jnp.concatenate(gated, axis=1).astype(BF16), w2)

    def quad(j, carry):
        rows_update(start + j * 4 * blk, 4 * blk)
        return carry

    lax.fori_loop(0, nblk // 4, quad, 0)
    for rem in (1, 2, 3):
        @pl.when(nblk % 4 == rem)
        def _(rem=rem):
            rows_update(start + (nblk // 4) * 4 * blk, rem * blk)

    @pl.when(e == pl.num_programs(1) - 1)
    def _():
        pt = pt_scr[...]
        ys = ys_scr[...]
        hi = ys.astype(BF16)
        lo = (ys - hi.astype(F32)).astype(BF16)
        y = x_ref[...] + (_dot(pt, hi) + _dot(pt, lo))
        y_ref[...] = _rms_rows(y, fg_ref[...]) if final else y


def _moe_grouped(x, g, dense, w1, w3, w2, layer, tm, final_g=None):
    extra = [] if final_g is None else [final_g]
    n = x.shape[0]
    slots = tm + N_GROUPS * MOE_BLOCK
    ne = MOE_STEP_EXPERTS
    assert EXP_PER_GROUP % ne == 0
    once = pl.Buffered(1)
    row = lambda i, e: (i, 0)
    expert = lambda i, e: (layer, e, 0, 0)
    gid_rows = dense[:, 0].reshape(n // tm, 1, tm)
    idx = jnp.arange(tm, dtype=jnp.int32)
    tri = (idx[None, :] < idx[:, None]).astype(BF16)
    return pl.pallas_call(
        functools.partial(_moe_grouped_kernel, tm=tm, slots=slots, final=final_g is not None),
        grid=(n // tm, N_EXPERTS // ne),
        in_specs=[pl.BlockSpec((tm, D_MODEL), row, pipeline_mode=once),
                  pl.BlockSpec((1, D_MODEL), lambda i, e: (0, 0)),
                  pl.BlockSpec((tm, LANES), row, pipeline_mode=once),
                  pl.BlockSpec((None, 1, tm), lambda i, e: (i, 0, 0)),
                  pl.BlockSpec((tm, tm), lambda i, e: (0, 0), pipeline_mode=once),
                  pl.BlockSpec((None, ne, D_MODEL, D_EXPERT), expert),
                  pl.BlockSpec((None, ne, D_MODEL, D_EXPERT), expert),
                  pl.BlockSpec((None, ne, D_EXPERT, D_MODEL), expert)]
        + [pl.BlockSpec((1, D_MODEL), lambda i, e: (0, 0))] * len(extra),
        out_specs=pl.BlockSpec((tm, D_MODEL), row),
        out_shape=jax.ShapeDtypeStruct((n, D_MODEL), F32),
        scratch_shapes=[pltpu.VMEM((slots, D_MODEL), BF16), pltpu.VMEM((slots, LANES), F32),
                        pltpu.VMEM((slots, D_MODEL), F32), pltpu.VMEM((tm, slots), BF16),
                        pltpu.SMEM((2 * N_GROUPS,), jnp.int32)],
        compiler_params=_cparams(("parallel", "arbitrary")),
        name="moe_grouped",
    )(x, g, dense, gid_rows, tri, w1, w3, w2, *extra)


def _tile_rows(n, cap):
    tm = min(n, cap)
    assert n % tm == 0
    return tm


def _layer_weights(l, norm1_g, w_in, a_vnorm_g, a_ws, a_bs, c_conv_w, c_a_log, c_dt_bias, c_norm_g,
                   w_out, norm2_g, w_group, w_router, w1, w3, w2):
    pad_l = lambda a, left: jnp.pad(a, ((0, 0), (left, LANES - left - a.shape[-1])))
    return dict(
        norm1_g=norm1_g[l][None, :],
        w_in_b=w_in.astype(BF16),
        w_small=pad_l(w_in[l, :, PROJ_MAIN:], 0).astype(BF16),
        a_gain=a_vnorm_g[l][None, :],
        a_ws=a_ws[l],
        a_bs=a_bs[l],
        conv_w=c_conv_w[l],
        alog_row=pad_l(c_a_log[l][None, :], C_HEADS),
        dtb_row=pad_l(c_dt_bias[l][None, :], C_HEADS),
        gnorm=c_norm_g[l][None, :],
        w_out=w_out[l].astype(BF16),
        norm2_g=norm2_g[l][None, :],
        w_route=pad_l(jnp.concatenate([w_group[l], w_router[l]], axis=-1), 0),
        w1=w1, w3=w3, w2=w2, layer=l,
    )


def _mixer_c(lw, cqkv, cba, cz, conv_buf, s0, gmat, batch, t, tm, valid_len, bg):
    buf8 = jnp.pad(conv_buf, ((0, 0), (SUBLANES - (CONV_W - 1), 0), (0, 0)))
    u, wq, ak, gt = _gdn_prep(cqkv, cba, buf8, lw["conv_w"], lw["alog_row"], lw["dtb_row"], gmat,
                              batch, t, tm, valid_len)
    return _gdn_scan(u, wq, ak, gt, cz, s0, lw["gnorm"], batch, t, bg)


def _ffn(lw, x, dense):
    n = x.shape[0]
    w1, w3, w2 = lw["w1"].astype(BF16), lw["w3"].astype(BF16), lw["w2"].astype(BF16)
    if n % MOE_TILE == 0:
        return _moe_grouped(x, lw["norm2_g"], dense, w1, w3, w2, lw["layer"], MOE_TILE, lw["final_g"])
    return _moe(x, lw["norm2_g"], dense, w1, w3, w2, lw["layer"], n, lw["final_g"])


def _prompt_layer(lw, x, batch, t, tabs, gmat, win_k, win_v, layer):
    n = batch * t
    dils = tuple(d for _, d in PATTERNS)
    tm = _tile_rows(n, 512)
    mixer_a = (lw["a_gain"], lw["a_ws"], jnp.repeat(lw["a_bs"].T, HEAD_DIM, axis=1))
    out_a, _, q, k, v, cqkv, cz, cba, *strided = _proj(x, lw["norm1_g"], lw["w_in_b"], layer, lw["w_small"], tabs,
                                                       mixer_a, tm, seq_len=t, dils=dils[1:])

    qkv = [(q, k, v)] + [tuple(a.reshape(n, B_W) for a in strided[3 * i:3 * i + 3]) for i in range(len(dils) - 1)]
    outs, lses = [], []
    for d, (qd, kd, vd) in zip(dils, qkv):
        o_d, lse_d = _win_attn(qd, kd, vd, t // d // WBLK)
        outs.append(o_d.reshape(batch, d, t // d, B_W))
        lses.append(lse_d.reshape(batch, d, t // d, LANES))

    zeros_buf = jnp.zeros((batch, CONV_W - 1, 3 * C_W), F32)
    zeros_s = jnp.zeros((batch, C_HEADS, HEAD_DIM, HEAD_DIM), F32)
    out_c, s_new = _mixer_c(lw, cqkv, cba, cz, zeros_buf, zeros_s, gmat, batch, t, 256, t, batch)
    out_c = out_c.reshape(n, C_W)

    x, dense = _out_proj(out_a, outs, lses, out_c, x, lw["w_out"], lw["norm2_g"], lw["w_route"], tm,
                         seq_len=t, dils=dils)
    x = _ffn(lw, x, dense)

    win_k, win_v = _window_rows(k, v, win_k, win_v, layer, batch, t, win_k.shape[-1])
    conv_state = cqkv.reshape(batch, t, 3 * C_W)[:, t - (CONV_W - 1):]
    return x, win_k, win_v, (conv_state, s_new)


def _sample_layer(lw, x, batch, t, tabs, gmat, kbuf, vbuf, conv_buf, s0, layer):
    n = batch * t
    eye = jnp.eye(batch, dtype=F32)
    ws_bd = jnp.stack([jnp.kron(eye, lw["a_ws"][h, :t, :t]) for h in range(A_HEADS)])
    bias_tile = jnp.tile(jnp.repeat(lw["a_bs"][:, :t].T, HEAD_DIM, axis=1), (batch, 1))
    out_a, a_rows, q, k, v, cqkv, cz, cba = _proj(x, lw["norm1_g"], lw["w_in_b"], layer, lw["w_small"], tabs,
                                                  (lw["a_gain"], ws_bd, bias_tile), n)

    pad8 = lambda a: jnp.pad(a.reshape(batch, t, B_W), ((0, 0), (0, SUBLANES - t), (0, 0)))
    out_b = _dec_attn(pad8(q), kbuf, vbuf, pad8(k), pad8(v), t, layer)[:, :t].reshape(n, B_W)

    tp = DELTA_CHUNK
    padt = lambda a: jnp.pad(a.reshape(batch, t, -1), ((0, 0), (0, tp - t), (0, 0))).reshape(batch * tp, -1)
    out_c, s_new = _mixer_c(lw, padt(cqkv), padt(cba), padt(cz), conv_buf, s0, gmat, batch, tp, tp, t, 4)
    out_c = out_c[:, :t].reshape(n, C_W)

    x, dense = _out_proj(out_a, out_b, None, out_c, x, lw["w_out"], lw["norm2_g"], lw["w_route"], n)
    x = _ffn(lw, x, dense)

    heads = lambda a: a.reshape(batch, t, B_HEADS, HEAD_DIM)
    conv_state = jnp.concatenate([conv_buf, cqkv.reshape(batch, t, 3 * C_W)], axis=1)[:, -(CONV_W - 1):]
    return x, (heads(k), heads(v), a_rows.reshape(batch, t, A_W), conv_state, s_new)


def kernel(x_prompt, x_sample, cache_win_k, cache_win_v, state_conv, state_delta, norm1_g, w_in, a_vnorm_g, a_ws, a_bs, c_conv_w, c_a_log, c_dt_bias, c_norm_g, w_out, norm2_g, w_group, w_router, w1, w3, w2, final_g):
    bp, tp, _ = x_prompt.shape
    bs, ts, _ = x_sample.shape
    depth = w_in.shape[0]
    assert tp % (PATTERNS[-1][1] * WBLK) == 0 and tp % 512 == 0 and bs * ts == CHUNK

    tabs_p = _rope_tables(jnp.arange(tp, dtype=jnp.int32))
    tabs_s = tuple(jnp.tile(a, (bs, 1)) for a in _rope_tables(PAST_LEN + jnp.arange(ts, dtype=jnp.int32)))
    gmat = _gdn_constants()

    feat_major = lambda c: jnp.transpose(c, (0, 1, 3, 4, 2)).reshape(depth, bs, B_W, c.shape[2])
    cache_k, cache_v = feat_major(cache_win_k), feat_major(cache_win_v)
    keep = min(MAX_WINDOW, tp)
    win_k = jnp.zeros((depth, bp, B_W, keep), F32)
    win_v = jnp.zeros((depth, bp, B_W, keep), F32)

    xp = x_prompt.reshape(bp * tp, D_MODEL)
    xs = x_sample.reshape(bs * ts, D_MODEL)
    p_out = [[] for _ in range(2)]
    s_out = [[] for _ in range(5)]
    for l in range(depth):
        lw = _layer_weights(l, norm1_g, w_in, a_vnorm_g, a_ws, a_bs, c_conv_w, c_a_log, c_dt_bias, c_norm_g,
                            w_out, norm2_g, w_group, w_router, w1, w3, w2)
        lw["final_g"] = final_g[None, :] if l == depth - 1 else None
        xp, win_k, win_v, st = _prompt_layer(lw, xp, bp, tp, tabs_p, gmat, win_k, win_v, l)
        for acc, a in zip(p_out, st):
            acc.append(a)
        xs, st = _sample_layer(lw, xs, bs, ts, tabs_s, gmat, cache_k, cache_v, state_conv[l], state_delta[l], l)
        for acc, a in zip(s_out, st):
            acc.append(a)
    y_prompt = xp.reshape(bp, tp, D_MODEL)
    y_sample = xs.reshape(bs, ts, D_MODEL)
    rows_major = lambda w: jnp.transpose(w.reshape(depth, bp, B_HEADS, HEAD_DIM, keep), (0, 1, 4, 2, 3))
    return (y_prompt, y_sample, rows_major(win_k), rows_major(win_v), *[jnp.stack(a) for a in p_out],
            *[jnp.stack(a) for a in s_out])
```

```python
import functools

import jax
import jax.numpy as jnp
from jax import lax
from jax.experimental import pallas as pl
from jax.experimental.pallas import tpu as pltpu

F32 = jnp.float32
BF16 = jnp.bfloat16

D_MODEL = 1024
HEAD_DIM = 64
A_HEADS = 4
B_HEADS = 6
C_HEADS = 6
A_W = A_HEADS * HEAD_DIM
B_W = B_HEADS * HEAD_DIM
C_W = C_HEADS * HEAD_DIM
CHUNK = 128
PATTERNS = ((128, 1), (512, 4), (2048, 16))
MAX_WINDOW = 2048
ROT_DIM = HEAD_DIM // 4
ROPE_THETA = 500000.0
CONV_W = 4
DELTA_CHUNK = 64
N_GROUPS = 4
EXP_PER_GROUP = 8
N_EXPERTS = N_GROUPS * EXP_PER_GROUP
D_EXPERT = 256
EPS = 1e-6
PAST_LEN = 16384

LANES = 128
SUBLANES = 8
WBLK = 128
PROJ_MAIN = 2 * A_W + 3 * B_W + 4 * C_W
NEG = -1e30
ROUTE_OFF = N_GROUPS
VMEM_LIMIT = 56 * 1024 * 1024
MOE_TILE = 1024
MOE_STEP_EXPERTS = 4
MOE_BLOCK = 128
GDN_TILE = 256
GDN_GROUP = 4


def _cparams(sem):
    return pltpu.CompilerParams(dimension_semantics=sem, vmem_limit_bytes=VMEM_LIMIT)


def _sigmoid(x):
    return 1.0 / (1.0 + jnp.exp(-x))


def _silu(x):
    return x * _sigmoid(x)


def _dot(a, b):
    return jnp.dot(a, b, preferred_element_type=F32)


def _dot_nt(a, b):
    return lax.dot_general(a, b, (((1,), (1,)), ((), ())), preferred_element_type=F32)


def _chunk_mlp_rows(u, v, gain, w_ref, bias, o_ref, vn_ref):
    xc = v - jnp.mean(v, axis=-1, keepdims=True)
    vn = xc * lax.rsqrt(jnp.mean(xc * xc, axis=-1, keepdims=True) + EPS) * gain
    vn_ref[...] = vn
    vb = vn.astype(BF16)
    rows = lax.broadcasted_iota(jnp.int32, (CHUNK, CHUNK), 0)
    cols = lax.broadcasted_iota(jnp.int32, (CHUNK, CHUNK), 1)
    tril = rows >= cols
    ws = [jnp.where(tril, w_ref[h], 0.0).astype(BF16) for h in range(A_HEADS)]
    for c in range(v.shape[0] // CHUNK):
        rs = slice(c * CHUNK, (c + 1) * CHUNK)
        parts = [_dot(ws[h], vb[rs, h * HEAD_DIM:(h + 1) * HEAD_DIM]) for h in range(A_HEADS)]
        o_ref[rs, :] = u[rs, :] * (jnp.concatenate(parts, axis=-1) + bias)


def _proj_kernel(x_ref, g_ref, w_ref, wsm_ref, c_ref, s1_ref, s2_ref, ag_ref, aw_ref, ab_ref,
                 oa_ref, vn_ref, q_ref, k_ref, v_ref, cqkv_ref, cz_ref, cba_ref, *strided_refs, dils):
    x = x_ref[...]
    h = x * lax.rsqrt(jnp.mean(x * x, axis=-1, keepdims=True) + EPS) * g_ref[...]
    hb = h.astype(BF16)

    def seg(a, b):
        return _dot(hb, w_ref[:, a:b])

    _chunk_mlp_rows(seg(0, A_W), seg(A_W, 2 * A_W), ag_ref[...], aw_ref, ab_ref[...], oa_ref, vn_ref)
    c, s1, s2 = c_ref[...], s1_ref[...], s2_ref[...]
    q0 = 2 * A_W
    k0 = q0 + B_W
    nblk = B_W // LANES
    stage = strided_refs[-1] if dils else None
    for j in range(nblk):
        cols = slice(LANES * j, LANES * (j + 1))
        for a, (base, ref, scale) in enumerate(((q0, q_ref, HEAD_DIM ** -0.5), (k0, k_ref, None))):
            xc = seg(base + LANES * j, base + LANES * (j + 1))
            r = xc * c + pltpu.roll(xc, ROT_DIM // 2, 1) * s1 + pltpu.roll(xc, LANES - ROT_DIM // 2, 1) * s2
            if scale is not None:
                r = r * scale
            ref[:, cols] = r.astype(ref.dtype)
            if dils:
                stage[a * nblk + j] = r
    v0 = k0 + B_W
    v_ref[...] = seg(v0, v0 + B_W)
    c0 = v0 + B_W
    cqkv_ref[...] = seg(c0, c0 + 3 * C_W)
    cz_ref[...] = seg(c0 + 3 * C_W, c0 + 4 * C_W)
    cba_ref[...] = _dot(hb, wsm_ref[...])
    if dils:
        outs = strided_refs[:-1]
        tm = x.shape[0]
        for b in range(nblk):
            stage[2 * nblk + b] = v_ref[:, LANES * b:LANES * (b + 1)]
        for i, d in enumerate(dils):
            for a in range(3):
                dst = outs[3 * i + a]
                for j in range(d):
                    for b in range(nblk):
                        dst[j, :, LANES * b:LANES * (b + 1)] = stage[a * nblk + b,
                                                                     pl.ds(j, tm // d, stride=d), :].astype(dst.dtype)


def _proj(x, g, w_in_b, layer, w_small, tabs, mixer_a, tm, seq_len=None, dils=()):
    n = x.shape[0]
    nt = n // tm
    ntab = tabs[0].shape[0] // tm
    row = lambda i: (i, 0)
    fixed = lambda i: (0, 0)
    tab = lambda i: (i % ntab, 0)
    widths = (A_W, A_W, B_W, B_W, B_W, 3 * C_W, C_W, LANES)
    out_specs = [pl.BlockSpec((tm, w), row) for w in widths]
    out_shape = [jax.ShapeDtypeStruct((n, w), BF16 if i == 2 else F32) for i, w in enumerate(widths)]
    for d in dils:
        tps = seq_len // tm
        out_specs += [pl.BlockSpec((None, d, tm // d, B_W), lambda i, tps=tps: (i // tps, 0, i % tps, 0))] * 3
        out_shape += [jax.ShapeDtypeStruct((n // seq_len, d, seq_len // d, B_W), BF16)] * 3
    return pl.pallas_call(
        functools.partial(_proj_kernel, dils=dils),
        grid=(nt,),
        in_specs=[pl.BlockSpec((tm, D_MODEL), row), pl.BlockSpec((1, D_MODEL), fixed),
                  pl.BlockSpec((None, D_MODEL, PROJ_MAIN), lambda i: (layer, 0, 0)),
                  pl.BlockSpec((D_MODEL, LANES), fixed),
                  pl.BlockSpec((tm, LANES), tab), pl.BlockSpec((tm, LANES), tab), pl.BlockSpec((tm, LANES), tab),
                  pl.BlockSpec((1, A_W), fixed), pl.BlockSpec((A_HEADS, CHUNK, CHUNK), lambda i: (0, 0, 0)),
                  pl.BlockSpec((CHUNK, A_W), fixed)],
        out_specs=out_specs,
        out_shape=out_shape,
        scratch_shapes=[pltpu.VMEM((3 * B_W // LANES, tm, LANES), F32)] if dils else [],
        compiler_params=_cparams(("parallel",)),
        name="proj",
    )(x, g, w_in_b, w_small, *tabs, *mixer_a)


def _rope_tables(pos):
    half = ROT_DIM // 2
    inv_freq = jnp.power(ROPE_THETA, -jnp.arange(0, ROT_DIM, 2, dtype=F32) / ROT_DIM)
    ang = pos.astype(F32)[:, None] * inv_freq[None, :]
    cos, sin = jnp.cos(ang), jnp.sin(ang)
    p = pos.shape[0]
    z8 = jnp.zeros((p, half), F32)
    rest0 = jnp.zeros((p, HEAD_DIM - ROT_DIM), F32)
    c64 = jnp.concatenate([cos, cos, jnp.ones((p, HEAD_DIM - ROT_DIM), F32)], axis=-1)
    s1 = jnp.concatenate([z8, sin, rest0], axis=-1)
    s2 = jnp.concatenate([-sin, z8, rest0], axis=-1)
    two = lambda a: jnp.concatenate([a, a], axis=-1)
    return two(c64), two(s1), two(s2)


def _win_attn_kernel(q_ref, kp_ref, kc_ref, vp_ref, vc_ref, o_ref, lse_ref, *, seg_blocks):
    s = pl.program_id(0)
    rows = lax.broadcasted_iota(jnp.int32, (WBLK, 2 * WBLK), 0)
    cols = lax.broadcasted_iota(jnp.int32, (WBLK, 2 * WBLK), 1)
    dist = rows + WBLK - cols
    band = (dist >= 0) & (dist <= WBLK)
    lo = jnp.where((2 * s) % seg_blocks == 0, WBLK, 0)
    biases = [jnp.where(band & (cols >= lo), 0.0, NEG), jnp.where(band, 0.0, NEG)]
    lane = lax.broadcasted_iota(jnp.int32, (WBLK, LANES), 1)
    k3 = jnp.concatenate([kp_ref[...], kc_ref[...]], axis=0).astype(BF16)
    v3 = jnp.concatenate([vp_ref[...], vc_ref[...]], axis=0).astype(BF16)
    heads = range(B_HEADS)
    sls = [slice(h // 2 * LANES, (h // 2 + 1) * LANES) for h in heads]
    hms = [(lane < HEAD_DIM) if h % 2 == 0 else (lane >= HEAD_DIM) for h in heads]
    work = [(j, h) for j in range(2) for h in heads]
    qs = [q_ref[j * WBLK:(j + 1) * WBLK, :] for j in range(2)]
    ks = [k3[j * WBLK:(j + 2) * WBLK, :] for j in range(2)]
    vs = [v3[j * WBLK:(j + 2) * WBLK, :] for j in range(2)]
    scs = [_dot_nt(jnp.where(hms[h], qs[j][:, sls[h]], 0.0).astype(BF16), ks[j][:, sls[h]]) + biases[j]
           for j, h in work]
    ms = [jnp.max(sc, axis=-1, keepdims=True) for sc in scs]
    es = [jnp.exp(sc - m) for sc, m in zip(scs, ms)]
    dens = [jnp.sum(e, axis=-1, keepdims=True) for e in es]
    os_ = [_dot((e * (1.0 / d)).astype(BF16), vs[j][:, sls[h]]) for (j, h), e, d in zip(work, es, dens)]
    for j in range(2):
        base = j * B_HEADS
        lse_tile = jnp.zeros((WBLK, LANES), F32)
        for h in heads:
            lse_tile = jnp.where(lane == h, ms[base + h] + jnp.log(dens[base + h]), lse_tile)
        lse_ref[j * WBLK:(j + 1) * WBLK, :] = lse_tile
        for hp in range(B_W // LANES):
            o_ref[j * WBLK:(j + 1) * WBLK, sls[2 * hp]] = jnp.where(hms[2 * hp], os_[base + 2 * hp],
                                                                  os_[base + 2 * hp + 1])


def _win_attn(qd, kd, vd, seg_blocks):
    n = qd.shape[0]
    assert seg_blocks % 2 == 0
    cur = lambda s: (s, 0)
    prev = lambda s: (jnp.maximum(2 * s - 1, 0), 0)
    blk = lambda w, im: pl.BlockSpec((2 * WBLK, w), im)
    pblk = pl.BlockSpec((WBLK, B_W), prev)
    return pl.pallas_call(
        functools.partial(_win_attn_kernel, seg_blocks=seg_blocks),
        grid=(n // (2 * WBLK),),
        in_specs=[blk(B_W, cur), pblk, blk(B_W, cur), pblk, blk(B_W, cur)],
        out_specs=[blk(B_W, cur), blk(LANES, cur)],
        out_shape=[jax.ShapeDtypeStruct((n, B_W), F32), jax.ShapeDtypeStruct((n, LANES), F32)],
        compiler_params=_cparams(("parallel",)),
        name="win_attn",
    )(qd, kd, kd, vd, vd)


def _window_rows_kernel(k_ref, v_ref, kprev_ref, vprev_ref, ko_ref, vo_ref):
    del kprev_ref, vprev_ref
    for b in range(k_ref.shape[0]):
        ko_ref[b] = k_ref[b].T
        vo_ref[b] = v_ref[b].T


def _window_rows(k, v, k_all, v_all, layer, batch, t, keep):
    first = (t - keep) // WBLK
    src = pl.BlockSpec((batch, WBLK, B_W), lambda i: (0, first + i, 0))
    dst = pl.BlockSpec((None, batch, B_W, WBLK), lambda i: (layer, 0, 0, i))
    hbm = pl.BlockSpec(memory_space=pl.ANY)
    shape = jax.ShapeDtypeStruct(k_all.shape, F32)
    return pl.pallas_call(
        _window_rows_kernel,
        grid=(keep // WBLK,),
        in_specs=[src, src, hbm, hbm],
        out_specs=[dst, dst],
        out_shape=[shape, shape],
        input_output_aliases={2: 0, 3: 1},
        compiler_params=_cparams(("parallel",)),
        name="window_rows",
    )(k.reshape(batch, t, B_W), v.reshape(batch, t, B_W), k_all, v_all)


def _dec_attn_kernel(q_ref, kc_ref, vc_ref, kn_ref, vn_ref, o_ref, *, t_new, cache_len):
    rows_c = lax.broadcasted_iota(jnp.int32, (SUBLANES, cache_len), 0)
    cols_c = lax.broadcasted_iota(jnp.int32, (SUBLANES, cache_len), 1)
    dist_c = cache_len + rows_c % t_new - cols_c
    rows_n = lax.broadcasted_iota(jnp.int32, (SUBLANES, SUBLANES), 0)
    cols_n = lax.broadcasted_iota(jnp.int32, (SUBLANES, SUBLANES), 1)
    dist_n = rows_n % t_new - cols_n
    biases = []
    for window, dil in PATTERNS:
        vc_ok = (dist_c <= window) & ((dist_c & (dil - 1)) == 0)
        vn_ok = (dist_n >= 0) & ((dist_n & (dil - 1)) == 0)
        biases.append((jnp.where(vc_ok, 0.0, NEG), jnp.where(vn_ok, 0.0, NEG)))
    row8 = lax.broadcasted_iota(jnp.int32, (SUBLANES, LANES), 0)
    lane8 = lax.broadcasted_iota(jnp.int32, (SUBLANES, LANES), 1)
    own = (lane8 < HEAD_DIM) == (row8 < t_new)
    q = q_ref[...].astype(F32)
    outs = []
    for hp in range(B_W // LANES):
        sl = slice(hp * LANES, (hp + 1) * LANES)
        qq = jnp.concatenate([q[0:t_new, sl], q[0:t_new, sl]], axis=0)
        q8 = jnp.where(own, qq, 0.0).astype(BF16)
        kc, vc = kc_ref[sl, :].astype(BF16), vc_ref[sl, :].astype(BF16)
        kn, vn = kn_ref[:, sl].astype(BF16), vn_ref[:, sl].astype(BF16)
        sc_c = _dot(q8, kc)
        sc_n = _dot_nt(q8, kn)
        os_, lses = [], []
        for bc, bn in biases:
            a_c, a_n = sc_c + bc, sc_n + bn
            m = jnp.maximum(jnp.max(a_c, axis=-1, keepdims=True), jnp.max(a_n, axis=-1, keepdims=True))
            e_c, e_n = jnp.exp(a_c - m), jnp.exp(a_n - m)
            den = jnp.sum(e_c, axis=-1, keepdims=True) + jnp.sum(e_n, axis=-1, keepdims=True)
            inv = 1.0 / den
            os_.append(_dot_nt((e_c * inv).astype(BF16), vc) + _dot((e_n * inv).astype(BF16), vn))
            lses.append(m + jnp.log(den))
        mx = jnp.maximum(jnp.maximum(lses[0], lses[1]), lses[2])
        ws = [jnp.exp(l - mx) for l in lses]
        tot = ws[0] + ws[1] + ws[2]
        o8 = (ws[0] / tot) * os_[0] + (ws[1] / tot) * os_[1] + (ws[2] / tot) * os_[2]
        lane4 = lane8[0:t_new]
        outs.append(jnp.where(lane4 < HEAD_DIM, o8[0:t_new], o8[t_new:2 * t_new]))
    o_ref[...] = jnp.zeros((SUBLANES, B_W), F32)
    o_ref[0:t_new, :] = jnp.concatenate(outs, axis=-1)


def _dec_attn(q8, kc, vc, kn8, vn8, t_new, layer):
    _, b, _, cache_len = kc.shape
    assert 2 * t_new == SUBLANES
    small = pl.BlockSpec((None, SUBLANES, B_W), lambda i: (i, 0, 0))
    big = pl.BlockSpec((None, None, B_W, cache_len), lambda i: (layer, i, 0, 0))
    return pl.pallas_call(
        functools.partial(_dec_attn_kernel, t_new=t_new, cache_len=cache_len),
        grid=(b,),
        in_specs=[small, big, big, small, small],
        out_specs=small,
        out_shape=jax.ShapeDtypeStruct((b, SUBLANES, B_W), F32),
        compiler_params=_cparams(("parallel",)),
        name="dec_attn",
    )(q8, kc, vc, kn8, vn8)


def _split3(x):
    hi = x.astype(BF16)
    r1 = x - hi.astype(F32)
    mid = r1.astype(BF16)
    return hi, mid, (r1 - mid.astype(F32)).astype(BF16)


def _pair_dup(xx):
    lane = lax.broadcasted_iota(jnp.int32, xx.shape, 1)
    hi = xx.astype(BF16).astype(F32)
    return jnp.where(lane < HEAD_DIM, xx, xx - hi).astype(BF16)


def _lhs4(pair):
    return jnp.concatenate([pair, pair], axis=1)


def _rhs4(y):
    hi = y.astype(BF16)
    lo = (y - hi.astype(F32)).astype(BF16)
    return jnp.concatenate([hi, hi, lo, lo], axis=0)


def _unit_lower_solves(mats, rhss):
    n = DELTA_CHUNK
    w2 = 2 * HEAD_DIM
    rows = lax.broadcasted_iota(jnp.int32, (n, w2), 0)
    cols = lax.broadcasted_iota(jnp.int32, (n, w2), 1) % HEAD_DIM
    in16 = rows // 16 == cols // 16
    eye = jnp.where(rows == cols, 1.0, 0.0)
    ds = [jnp.where(in16, a, 0.0) for a in mats]
    es = [jnp.where(in16, 0.0, a) for a in mats]
    left = lambda xx: _lhs4(_pair_dup(xx))
    pw = [_dot(left(d), _rhs4(d)) for d in ds]
    ts = [eye - d for d in ds]
    for _ in range(2):
        outs = [_dot(jnp.concatenate([left(t), left(p)], axis=0), _rhs4(p)) for t, p in zip(ts, pw)]
        ts = [t + o[0:n] for t, o in zip(ts, outs)]
        pw = [o[n:2 * n] for o in outs]
    ts = [t + _dot(left(t), _rhs4(p)) for t, p in zip(ts, pw)]
    o5 = [_dot(left(t), _rhs4(jnp.concatenate([r, e], axis=1))) for t, r, e in zip(ts, rhss, es)]
    x0 = [o[:, 0:w2] for o in o5]
    nm = [o[:, w2:2 * w2] for o in o5]
    o6 = [_dot(left(m), _rhs4(jnp.concatenate([x, m], axis=1))) for m, x in zip(nm, x0)]
    ys = [x - o[:, 0:w2] for x, o in zip(x0, o6)]
    return [y + _dot(left(o[:, w2:2 * w2]), _rhs4(y)) for y, o in zip(ys, o6)]


def _gdn_prep_kernel(x_ref, cba_ref, buf_ref, cw_ref, alog_ref, dtb_ref, gmat_ref, xmat_ref,
                     u_ref, wq_ref, ak_ref, gt_ref,
                     xp_scr, q_scr, k_scr, v_scr, gb_scr, *, tm, valid_len, group):
    ti = pl.program_id(1)

    @pl.when(ti == 0)
    def _():
        xp_scr[0:SUBLANES, :] = buf_ref[...]

    x = x_ref[...]
    xp_scr[SUBLANES:SUBLANES + tm, :] = x
    off = SUBLANES - (CONV_W - 1)
    acc = xp_scr[off:off + tm, :] * cw_ref[0:1, :]
    for j in range(1, CONV_W):
        acc = acc + xp_scr[off + j:off + j + tm, :] * cw_ref[j:j + 1, :]
    xp_scr[0:SUBLANES, :] = x[tm - SUBLANES:tm, :]
    y = _silu(acc)
    q, k = y[:, 0:C_W], y[:, C_W:2 * C_W]
    gmat = gmat_ref[...]
    head_sum = lambda a: sum(_dot(p, gmat) for p in _split3(a))
    q_scr[...] = q * lax.rsqrt(head_sum(q * q) + EPS) * (HEAD_DIM ** -0.5)
    k_scr[...] = k * lax.rsqrt(head_sum(k * k) + EPS)
    v_scr[...] = y[:, 2 * C_W:3 * C_W]

    cba = cba_ref[...]
    lane = lax.broadcasted_iota(jnp.int32, (tm, LANES), 1)
    tpos = ti * tm + lax.broadcasted_iota(jnp.int32, (tm, LANES), 0)
    live = tpos < valid_len
    beta = _sigmoid(cba)
    z = cba + dtb_ref[...]
    softplus = jnp.maximum(z, 0.0) + jnp.log(1.0 + jnp.exp(-jnp.abs(z)))
    g = -jnp.exp(alog_ref[...]) * softplus
    is_g = (lane >= C_HEADS) & (lane < 2 * C_HEADS)
    gb_scr[...] = jnp.where(live, jnp.where(is_g, g, jnp.where(lane < C_HEADS, beta, 0.0)), 0.0)

    n = DELTA_CHUNK
    rows = lax.broadcasted_iota(jnp.int32, (n, 2 * n), 0)
    cols = lax.broadcasted_iota(jnp.int32, (n, 2 * n), 1) % n
    incl = rows >= cols
    strict = rows > cols
    ltri = jnp.where(incl[:, 0:n], 1.0, 0.0).astype(BF16)
    heads = range(C_HEADS)
    tiles = range(C_W // LANES)
    tsl = [slice(t * LANES, (t + 1) * LANES) for t in tiles]
    lane_c = lax.broadcasted_iota(jnp.int32, (n, LANES), 1)
    low = lane_c < HEAD_DIM
    own = [low if h % 2 == 0 else jnp.logical_not(low) for h in heads]
    xmat = xmat_ref[...]
    wide0 = C_HEADS * LANES

    def head_pairs(tile):
        rolled = pltpu.roll(tile, HEAD_DIM, 1)
        lo = rolled - rolled.astype(BF16).astype(F32)
        lane = lax.broadcasted_iota(jnp.int32, tile.shape, 1) < HEAD_DIM
        return jnp.where(lane, tile, lo).astype(BF16), jnp.where(lane, lo, tile).astype(BF16)

    def setup(c):
        rs = pl.ds(pl.multiple_of(c * n, n), n)
        gb = gb_scr[rs, :]
        g_only = jnp.where((lane_c >= C_HEADS) & (lane_c < 2 * C_HEADS), gb, 0.0)
        cg = sum(_dot(ltri, p) for p in _split3(g_only))
        cgt = cg.T
        cgl = cg[n - 1:n, :]
        gt_ref[pl.ds(pl.multiple_of(c * SUBLANES, SUBLANES), SUBLANES), :] = jnp.broadcast_to(
            jnp.exp(cgl), (SUBLANES, LANES))
        wide = sum(_dot(p, xmat) for p in _split3(jnp.where(lane_c < C_HEADS, gb, cg)))
        b128 = [wide[:, h * LANES:(h + 1) * LANES] for h in heads]
        c128 = [wide[:, wide0 + h * LANES:wide0 + (h + 1) * LANES] for h in heads]
        cgr = [jnp.concatenate([cgt[C_HEADS + h:C_HEADS + h + 1, :]] * 2, axis=1) for h in heads]
        decay = [jnp.where(incl, jnp.exp(jnp.minimum(c128[h] - cgr[h], 0.0)), 0.0) for h in heads]
        amat, at_pair, rhs, qd_pair, kdt_pair = [], [], [], [], []
        for t in tiles:
            he, ho = 2 * t, 2 * t + 1
            qn, kn, vv = q_scr[rs, tsl[t]], k_scr[rs, tsl[t]], v_scr[rs, tsl[t]]
            bnat = jnp.where(low, b128[he], b128[ho])
            cnat = jnp.where(low, c128[he], c128[ho])
            ecg = jnp.exp(cnat)
            kb = kn.astype(BF16)
            kk2 = jnp.concatenate([kb, kb], axis=0)
            for h in (he, ho):
                masked = jnp.concatenate([jnp.where(own[h], kn, 0.0), jnp.where(own[h], qn, 0.0)], axis=0)
                kkqk = _dot_nt(masked.astype(BF16), kk2)
                amat.append(jnp.where(strict, b128[h] * kkqk[0:n] * decay[h], 0.0))
                at_pair.append(_pair_dup(kkqk[n:2 * n] * decay[h]))
            bv = bnat * vv
            bk_rolled = pltpu.roll((bnat * ecg) * kn, HEAD_DIM, 1)
            rhs += [jnp.where(low, bv, bk_rolled), jnp.where(low, bk_rolled, bv)]
            qd_pair += list(head_pairs(qn * ecg))
            kd = kn * jnp.exp(cnat[n - 1:n, :] - cnat)
            kdt = jnp.concatenate([kd, kd], axis=0).T
            kdt_pair += [_pair_dup(kdt[0:n]), _pair_dup(kdt[n:2 * n])]
        return amat, rhs, at_pair, qd_pair, kdt_pair

    def finish(c, sol, at_pair, qd_pair, kdt_pair):
        w_pair = []
        for t in tiles:
            se, so = sol[2 * t], sol[2 * t + 1]
            u_ref[pl.ds(pl.multiple_of(c * n, n), n), tsl[t]] = jnp.where(low, se, so)
            w_odd, w_even = head_pairs(jnp.where(low, so, se))
            w_pair += [w_even, w_odd]
        r2 = pl.ds(pl.multiple_of(c * 2 * n, 2 * n), 2 * n)
        wq_ref[r2, :] = jnp.concatenate(
            [jnp.concatenate([w_pair[h], qd_pair[h]], axis=0) for h in heads], axis=1)
        ak_ref[r2, :] = jnp.concatenate(
            [jnp.concatenate([at_pair[h], kdt_pair[h]], axis=0) for h in heads], axis=1)

    def chunks(i, carry):
        parts = [setup(i * group + j) for j in range(group)]
        sol = _unit_lower_solves([a for p in parts for a in p[0]], [r for p in parts for r in p[1]])
        for j, p in enumerate(parts):
            finish(i * group + j, sol[j * C_HEADS:(j + 1) * C_HEADS], *p[2:])
        return carry

    lax.fori_loop(0, tm // n // group, chunks, 0)


def _gdn_constants():
    hid = jnp.arange(C_W, dtype=jnp.int32) // HEAD_DIM
    gmat = (hid[:, None] == hid[None, :]).astype(BF16)
    src = jnp.arange(LANES, dtype=jnp.int32)[:, None]
    dst = jnp.arange(2 * C_HEADS * LANES, dtype=jnp.int32)[None, :] // LANES
    return gmat, (src == dst).astype(BF16)


def _gdn_prep(cqkv, cba, buf8, conv_w, alog_row, dtb_row, consts, batch, t, tm, valid_len):
    gmat, xmat = consts
    n = batch * t
    nt = t // tm
    row = lambda b, i: (b * nt + i, 0)
    fixed = lambda b, i: (0, 0)
    nch = tm // DELTA_CHUNK
    pair_w = C_HEADS * 2 * HEAD_DIM
    outs = [jax.ShapeDtypeStruct((n, C_W), F32), jax.ShapeDtypeStruct((2 * n, pair_w), BF16),
            jax.ShapeDtypeStruct((2 * n, pair_w), BF16),
            jax.ShapeDtypeStruct((n // DELTA_CHUNK * SUBLANES, LANES), F32)]
    return pl.pallas_call(
        functools.partial(_gdn_prep_kernel, tm=tm, valid_len=valid_len, group=min(GDN_GROUP, nch)),
        grid=(batch, nt),
        in_specs=[pl.BlockSpec((tm, 3 * C_W), row), pl.BlockSpec((tm, LANES), row),
                  pl.BlockSpec((None, SUBLANES, 3 * C_W), lambda b, i: (b, 0, 0)),
                  pl.BlockSpec((CONV_W, 3 * C_W), fixed), pl.BlockSpec((1, LANES), fixed),
                  pl.BlockSpec((1, LANES), fixed), pl.BlockSpec((C_W, C_W), fixed),
                  pl.BlockSpec(xmat.shape, fixed)],
        out_specs=[pl.BlockSpec((tm, C_W), row), pl.BlockSpec((2 * tm, pair_w), row),
                   pl.BlockSpec((2 * tm, pair_w), row), pl.BlockSpec((nch * SUBLANES, LANES), row)],
        out_shape=outs,
        scratch_shapes=[pltpu.VMEM((tm + SUBLANES, 3 * C_W), F32), pltpu.VMEM((tm, C_W), F32),
                        pltpu.VMEM((tm, C_W), F32), pltpu.VMEM((tm, C_W), F32), pltpu.VMEM((tm, LANES), F32)],
        compiler_params=_cparams(("parallel", "arbitrary")),
        name="gdn_prep",
    )(cqkv, cba, buf8, conv_w, alog_row, dtb_row, gmat, xmat)


def _gdn_scan_kernel(u_ref, wq_ref, ak_ref, gt_ref, z_ref, s0_ref, gn_ref,
                     o_ref, sfin_ref, s_scr, *, bg):
    c = pl.program_id(1)

    @pl.when(c == 0)
    def _():
        s_scr[...] = s0_ref[...]

    gn = gn_ref[...]
    n = DELTA_CHUNK
    chains = [(b, h, slice(h * HEAD_DIM, (h + 1) * HEAD_DIM), slice(h * 2 * HEAD_DIM, (h + 1) * 2 * HEAD_DIM))
              for b in range(bg) for h in range(C_HEADS)]
    st = [s_scr[b, h] for b, h, _, _ in chains]
    r1 = [_dot(_lhs4(wq_ref[b, :, ps]), _rhs4(s)) for (b, _, _, ps), s in zip(chains, st)]
    up = [u_ref[b, :, sl] - r[0:n] for (b, _, sl, _), r in zip(chains, r1)]
    r2 = [_dot(_lhs4(ak_ref[b, :, ps]), _rhs4(x)) for (b, _, _, ps), x in zip(chains, up)]
    for (b, h, _, _), s, r in zip(chains, st, r2):
        s_scr[b, h] = gt_ref[b, 0:1, C_HEADS + h:C_HEADS + h + 1] * s + r[n:2 * n]
    os_ = [a[n:2 * n] + r[0:n] for a, r in zip(r1, r2)]
    outs = [o * lax.rsqrt(jnp.mean(o * o, axis=-1, keepdims=True) + EPS) * gn * _silu(z_ref[b, :, sl])
            for (b, _, sl, _), o in zip(chains, os_)]
    for b in range(bg):
        o_ref[b] = jnp.concatenate(outs[b * C_HEADS:(b + 1) * C_HEADS], axis=1)

    @pl.when(c == pl.num_programs(1) - 1)
    def _():
        sfin_ref[...] = s_scr[...]


def _gdn_scan(u, wq, ak, gt, z, s0, gnorm, batch, t, bg):
    n = DELTA_CHUNK
    nc = t // n
    pair_w = C_HEADS * 2 * HEAD_DIM
    v3 = lambda a: a.reshape(batch, t, C_W)
    p3 = lambda a: a.reshape(batch, 2 * t, pair_w)
    tok = pl.BlockSpec((bg, n, C_W), lambda b, c: (b, c, 0))
    pair = pl.BlockSpec((bg, 2 * n, pair_w), lambda b, c: (b, c, 0))
    st = pl.BlockSpec((bg, C_HEADS, HEAD_DIM, HEAD_DIM), lambda b, c: (b, 0, 0, 0))
    return pl.pallas_call(
        functools.partial(_gdn_scan_kernel, bg=bg),
        grid=(batch // bg, nc),
        in_specs=[tok, pair, pair,
                  pl.BlockSpec((bg, SUBLANES, LANES), lambda b, c: (b, c, 0)),
                  tok, st, pl.BlockSpec((1, HEAD_DIM), lambda b, c: (0, 0))],
        out_specs=[tok, st],
        out_shape=[jax.ShapeDtypeStruct((batch, t, C_W), F32),
                   jax.ShapeDtypeStruct((batch, C_HEADS, HEAD_DIM, HEAD_DIM), F32)],
        scratch_shapes=[pltpu.VMEM((bg, C_HEADS, HEAD_DIM, HEAD_DIM), F32)],
        compiler_params=_cparams(("parallel", "arbitrary")),
        name="gdn_scan",
    )(v3(u), p3(wq), p3(ak), gt.reshape(batch, nc * SUBLANES, LANES), v3(z), s0, gnorm)


def _head_expand(wt):
    src = lax.broadcasted_iota(jnp.int32, (LANES, B_W), 0)
    dst = lax.broadcasted_iota(jnp.int32, (LANES, B_W), 1) // HEAD_DIM
    spread = jnp.where(src == dst, 1.0, 0.0).astype(BF16)
    return sum(_dot(p, spread) for p in _split3(wt))


def _out_proj_kernel(*refs, dils):
    if dils:
        npat = len(dils)
        a_ref = refs[0]
        o_refs, l_refs = refs[1:1 + npat], refs[1 + npat:1 + 2 * npat]
        c_ref, x_ref, w_ref, g2_ref, wr_ref, y_ref, dense_ref, o_scr, l_scr = refs[1 + 2 * npat:]
        tm = x_ref.shape[0]
        os_, ls = [], []
        for i, d in enumerate(dils):
            if d == 1:
                os_.append(o_refs[i][0])
                ls.append(l_refs[i][0])
            else:
                nblk = B_W // LANES
                for j in range(d):
                    rows = pl.ds(j, tm // d, stride=d)
                    for b in range(nblk):
                        o_scr[i * nblk + b, rows, :] = o_refs[i][j, :, LANES * b:LANES * (b + 1)]
                    l_scr[i, rows, :] = l_refs[i][j]
                os_.append(jnp.concatenate([o_scr[i * nblk + b] for b in range(nblk)], axis=-1))
                ls.append(l_scr[i])
        mx = functools.reduce(jnp.maximum, ls)
        es = [jnp.exp(l - mx) for l in ls]
        tot = functools.reduce(lambda a, b: a + b, es)
        ob = functools.reduce(lambda a, b: a + b, [_head_expand(e / tot) * o for e, o in zip(es, os_)])
    else:
        a_ref, b_ref, c_ref, x_ref, w_ref, g2_ref, wr_ref, y_ref, dense_ref = refs
        ob = b_ref[...]
    cat = jnp.concatenate([a_ref[...], ob, c_ref[...]], axis=-1).astype(BF16)
    y = x_ref[...] + _dot(cat, w_ref[...])
    y_ref[...] = y
    dense_ref[...] = _route_tile(y, g2_ref[...], wr_ref[...])


def _out_proj(out_a, out_b, lses, out_c, x, w_out, g2, w_route, tm, seq_len=None, dils=()):
    n = x.shape[0]
    row = lambda i: (i, 0)
    spec = lambda w: pl.BlockSpec((tm, w), row)
    scratch = []
    if dils:
        tps = seq_len // tm
        strided = lambda d, w: pl.BlockSpec((None, d, tm // d, w), lambda i: (i // tps, 0, i % tps, 0))
        ins = [out_a, *out_b, *lses, out_c, x, w_out]
        specs = ([spec(A_W)] + [strided(d, B_W) for d in dils] + [strided(d, LANES) for d in dils]
                 + [spec(C_W), spec(D_MODEL)])
        scratch = [pltpu.VMEM((len(dils) * B_W // LANES, tm, LANES), F32), pltpu.VMEM((len(dils), tm, LANES), F32)]
    else:
        ins = [out_a, out_b, out_c, x, w_out]
        specs = [spec(A_W), spec(B_W), spec(C_W), spec(D_MODEL)]
    fixed = lambda i: (0, 0)
    specs += [pl.BlockSpec((D_MODEL, D_MODEL), fixed), pl.BlockSpec((1, D_MODEL), fixed),
              pl.BlockSpec((D_MODEL, LANES), fixed)]
    return pl.pallas_call(
        functools.partial(_out_proj_kernel, dils=dils),
        grid=(n // tm,),
        in_specs=specs,
        out_specs=[spec(D_MODEL), spec(LANES)],
        out_shape=[jax.ShapeDtypeStruct((n, D_MODEL), F32), jax.ShapeDtypeStruct((n, LANES), F32)],
        scratch_shapes=scratch,
        compiler_params=_cparams(("parallel",)),
        name="out_proj",
    )(*ins, g2, w_route)


def _route_tile(x, g, w):
    t = x * lax.rsqrt(jnp.mean(x * x, axis=-1, keepdims=True) + EPS) * g
    t_hi = t.astype(BF16)
    t_lo = (t - t_hi.astype(F32)).astype(BF16)
    w_hi = w.astype(BF16)
    w_lo = (w - w_hi.astype(F32)).astype(BF16)
    lg = _dot(t_hi, w_hi) + (_dot(t_hi, w_lo) + _dot(t_lo, w_hi))
    tm = lg.shape[0]
    nrow = ROUTE_OFF + N_EXPERTS + ROUTE_OFF
    lt = lg.T[0:nrow, :]
    row = lax.broadcasted_iota(jnp.int32, (nrow, tm), 0).astype(F32)
    big = float(LANES)
    down = lambda op, a: op(a, axis=0, keepdims=True)
    is_grp = row < N_GROUPS
    gl = jnp.where(is_grp, lt, NEG)
    gmax = down(jnp.max, gl)
    gsum = down(jnp.sum, jnp.where(is_grp, jnp.exp(gl - gmax), 0.0))
    g_w = 1.0 / gsum
    g_idx = down(jnp.min, jnp.where(is_grp & (gl == gmax), row, big))
    lo = ROUTE_OFF + EXP_PER_GROUP * g_idx
    sel = (row >= lo) & (row < lo + EXP_PER_GROUP)
    el = jnp.where(sel, lt, NEG)
    m1 = down(jnp.max, el)
    esum = down(jnp.sum, jnp.where(sel, jnp.exp(el - m1), 0.0))
    i1 = down(jnp.min, jnp.where(sel & (el == m1), row, big))
    el2 = jnp.where(row == i1, NEG, el)
    m2 = down(jnp.max, el2)
    i2 = down(jnp.min, jnp.where(sel & (row != i1) & (el2 == m2), row, big))
    p1 = 1.0 / esum
    p2 = jnp.exp(m2 - m1) / esum
    tot = p1 + p2
    gates = jnp.where(row == i1, g_w * (p1 / tot), 0.0) + jnp.where(row == i2, g_w * (p2 / tot), 0.0)
    full = jnp.concatenate([jnp.where(row == 0.0, g_idx, gates), jnp.zeros((LANES - nrow, tm), F32)], axis=0)
    return full.T


def _rms_rows(y, g):
    return y * lax.rsqrt(jnp.mean(y * y, axis=-1, keepdims=True) + EPS) * g


def _moe_kernel(x_ref, g_ref, dense_ref, w1_ref, w3_ref, w2_ref, *rest):
    fg_ref = rest[0] if len(rest) == 3 else None
    y_ref, t_scr = rest[-2:]
    e = pl.program_id(1)

    @pl.when(e == 0)
    def _():
        x = x_ref[...]
        t_scr[...] = (x * lax.rsqrt(jnp.mean(x * x, axis=-1, keepdims=True) + EPS) * g_ref[...]).astype(BF16)
        y_ref[...] = x

    tb = t_scr[...]
    dense = dense_ref[...]
    lane = lax.broadcasted_iota(jnp.int32, dense.shape, 1)
    gate = jnp.sum(jnp.where(lane == e + ROUTE_OFF, dense, 0.0), axis=-1, keepdims=True)
    hid = _silu(_dot(tb, w1_ref[...].astype(BF16))) * _dot(tb, w3_ref[...].astype(BF16))
    y_ref[...] += _dot((hid * gate).astype(BF16), w2_ref[...].astype(BF16))

    if fg_ref is not None:
        @pl.when(e == pl.num_programs(1) - 1)
        def _():
            y_ref[...] = _rms_rows(y_ref[...], fg_ref[...])


def _moe(x, g, dense, w1, w3, w2, layer, tm, final_g=None):
    n = x.shape[0]
    row = lambda i, e: (i, 0)
    expert = lambda i, e: (layer, e, 0, 0)
    extra = [] if final_g is None else [final_g]
    return pl.pallas_call(
        _moe_kernel,
        grid=(n // tm, N_EXPERTS),
        in_specs=[pl.BlockSpec((tm, D_MODEL), row), pl.BlockSpec((1, D_MODEL), lambda i, e: (0, 0)),
                  pl.BlockSpec((tm, LANES), row),
                  pl.BlockSpec((None, None, D_MODEL, D_EXPERT), expert),
                  pl.BlockSpec((None, None, D_MODEL, D_EXPERT), expert),
                  pl.BlockSpec((None, None, D_EXPERT, D_MODEL), expert)]
        + [pl.BlockSpec((1, D_MODEL), lambda i, e: (0, 0))] * len(extra),
        out_specs=pl.BlockSpec((tm, D_MODEL), row),
        out_shape=jax.ShapeDtypeStruct((n, D_MODEL), F32),
        scratch_shapes=[pltpu.VMEM((tm, D_MODEL), BF16)],
        compiler_params=_cparams(("parallel", "arbitrary")),
        name="moe",
    )(x, g, dense, w1, w3, w2, *extra)


def _moe_grouped_kernel(x_ref, g_ref, dense_ref, gidr_ref, tri_ref, w1_ref, w3_ref, w2_ref, *rest,
                        tm, slots, final):
    fg_ref = rest[0] if final else None
    y_ref, ts_scr, gs_scr, ys_scr, pt_scr, meta = rest[-6:]
    e = pl.program_id(1)
    blk = MOE_BLOCK

    @pl.when(e == 0)
    def _():
        x = x_ref[...]
        t = (x * lax.rsqrt(jnp.mean(x * x, axis=-1, keepdims=True) + EPS) * g_ref[...]).astype(BF16)
        dense = dense_ref[...]
        tri = tri_ref[...]
        lane = lax.broadcasted_iota(jnp.int32, (tm, LANES), 1)
        ohc = jnp.where(lane.astype(F32) == dense[:, 0:1], 1.0, 0.0)
        rankc = _dot(tri, ohc.astype(BF16))
        sub = lax.broadcasted_iota(jnp.int32, (SUBLANES, tm), 0)
        ohr = jnp.where(sub.astype(F32) == gidr_ref[...], 1.0, 0.0)
        rankr = _dot_nt(ohr.astype(BF16), tri)
        start = jnp.int32(0)
        s_lane = jnp.zeros((tm, LANES), F32)
        s_sub = jnp.zeros((SUBLANES, tm), F32)
        for g in range(N_GROUPS):
            count = jnp.sum(jnp.where(lane == g, ohc, 0.0)).astype(jnp.int32)
            nblk = (count + blk - 1) // blk
            meta[g] = start
            meta[N_GROUPS + g] = nblk
            s_lane = jnp.where(lane == g, start.astype(F32), s_lane)
            s_sub = jnp.where(sub == g, start.astype(F32), s_sub)
            start = start + nblk * blk
        destc = jnp.sum(ohc * (s_lane + rankc), axis=-1, keepdims=True)
        destr = jnp.sum(ohr * (s_sub + rankr), axis=0, keepdims=True)
        slot_r = lax.broadcasted_iota(jnp.int32, (slots, tm), 0).astype(F32)
        p = jnp.where(slot_r == destr, 1.0, 0.0).astype(BF16)
        ts_scr[...] = _dot(p, t).astype(BF16)
        g3 = _dot(p, jnp.concatenate(_split3(dense), axis=1))
        gs_scr[...] = g3[:, 0:LANES] + (g3[:, LANES:2 * LANES] + g3[:, 2 * LANES:3 * LANES])
        slot_c = lax.broadcasted_iota(jnp.int32, (tm, slots), 1).astype(F32)
        pt_scr[...] = jnp.where(slot_c == destc, 1.0, 0.0).astype(BF16)
        ys_scr[...] = jnp.zeros((slots, D_MODEL), F32)

    ne = MOE_STEP_EXPERTS
    g = (e * ne) // EXP_PER_GROUP
    start = meta[g]
    nblk = meta[N_GROUPS + g]
    w13 = jnp.concatenate([w1_ref[j] for j in range(ne)] + [w3_ref[j] for j in range(ne)], axis=1)
    w2 = jnp.concatenate([w2_ref[j] for j in range(ne)], axis=0)
    half = ne * D_EXPERT

    def rows_update(r0, m):
        rows = pl.ds(pl.multiple_of(r0, blk), m)
        tb = ts_scr[rows, :]
        gs = gs_scr[rows, :]
        lane = lax.broadcasted_iota(jnp.int32, (m, LANES), 1)
        h13 = _dot(tb, w13)
        hid = _silu(h13[:, 0:half]) * h13[:, half:2 * half]
        gated = [hid[:, j * D_EXPERT:(j + 1) * D_EXPERT]
                 * jnp.sum(jnp.where(lane == e * ne + j + ROUTE_OFF, gs, 0.0), axis=-1, keepdims=True)
                 for j in range(ne)]
        ys_scr[rows, :] += _dot(jnp.concatenate(gated, axis=1).astype(BF16), w2)

    def quad(j, carry):
        rows_update(start + j * 4 * blk, 4 * blk)
        return carry

    lax.fori_loop(0, nblk // 4, quad, 0)
    for rem in (1, 2, 3):
        @pl.when(nblk % 4 == rem)
        def _(rem=rem):
            rows_update(start + (nblk // 4) * 4 * blk, rem * blk)

    @pl.when(e == pl.num_programs(1) - 1)
    def _():
        pt = pt_scr[...]
        ys = ys_scr[...]
        hi = ys.astype(BF16)
        lo = (ys - hi.astype(F32)).astype(BF16)
        y = x_ref[...] + (_dot(pt, hi) + _dot(pt, lo))
        y_ref[...] = _rms_rows(y, fg_ref[...]) if final else y


def _moe_grouped(x, g, dense, w1, w3, w2, layer, tm, final_g=None):
    extra = [] if final_g is None else [final_g]
    n = x.shape[0]
    slots = tm + N_GROUPS * MOE_BLOCK
    ne = MOE_STEP_EXPERTS
    assert EXP_PER_GROUP % ne == 0
    once = pl.Buffered(1)
    row = lambda i, e: (i, 0)
    expert = lambda i, e: (layer, e, 0, 0)
    gid_rows = dense[:, 0].reshape(n // tm, 1, tm)
    idx = jnp.arange(tm, dtype=jnp.int32)
    tri = (idx[None, :] < idx[:, None]).astype(BF16)
    return pl.pallas_call(
        functools.partial(_moe_grouped_kernel, tm=tm, slots=slots, final=final_g is not None),
        grid=(n // tm, N_EXPERTS // ne),
        in_specs=[pl.BlockSpec((tm, D_MODEL), row, pipeline_mode=once),
                  pl.BlockSpec((1, D_MODEL), lambda i, e: (0, 0)),
                  pl.BlockSpec((tm, LANES), row, pipeline_mode=once),
                  pl.BlockSpec((None, 1, tm), lambda i, e: (i, 0, 0)),
                  pl.BlockSpec((tm, tm), lambda i, e: (0, 0), pipeline_mode=once),
                  pl.BlockSpec((None, ne, D_MODEL, D_EXPERT), expert),
                  pl.BlockSpec((None, ne, D_MODEL, D_EXPERT), expert),
                  pl.BlockSpec((None, ne, D_EXPERT, D_MODEL), expert)]
        + [pl.BlockSpec((1, D_MODEL), lambda i, e: (0, 0))] * len(extra),
        out_specs=pl.BlockSpec((tm, D_MODEL), row),
        out_shape=jax.ShapeDtypeStruct((n, D_MODEL), F32),
        scratch_shapes=[pltpu.VMEM((slots, D_MODEL), BF16), pltpu.VMEM((slots, LANES), F32),
                        pltpu.VMEM((slots, D_MODEL), F32), pltpu.VMEM((tm, slots), BF16),
                        pltpu.SMEM((2 * N_GROUPS,), jnp.int32)],
        compiler_params=_cparams(("parallel", "arbitrary")),
        name="moe_grouped",
    )(x, g, dense, gid_rows, tri, w1, w3, w2, *extra)


def _tile_rows(n, cap):
    tm = min(n, cap)
    assert n % tm == 0
    return tm


def _layer_weights(l, norm1_g, w_in, a_vnorm_g, a_ws, a_bs, c_conv_w, c_a_log, c_dt_bias, c_norm_g,
                   w_out, norm2_g, w_group, w_router, w1, w3, w2):
    pad_l = lambda a, left: jnp.pad(a, ((0, 0), (left, LANES - left - a.shape[-1])))
    return dict(
        norm1_g=norm1_g[l][None, :],
        w_in_b=w_in.astype(BF16),
        w_small=pad_l(w_in[l, :, PROJ_MAIN:], 0).astype(BF16),
        a_gain=a_vnorm_g[l][None, :],
        a_ws=a_ws[l],
        a_bs=a_bs[l],
        conv_w=c_conv_w[l],
        alog_row=pad_l(c_a_log[l][None, :], C_HEADS),
        dtb_row=pad_l(c_dt_bias[l][None, :], C_HEADS),
        gnorm=c_norm_g[l][None, :],
        w_out=w_out[l].astype(BF16),
        norm2_g=norm2_g[l][None, :],
        w_route=pad_l(jnp.concatenate([w_group[l], w_router[l]], axis=-1), 0),
        w1=w1, w3=w3, w2=w2, layer=l,
    )


def _mixer_c(lw, cqkv, cba, cz, conv_buf, s0, gmat, batch, t, tm, valid_len, bg):
    buf8 = jnp.pad(conv_buf, ((0, 0), (SUBLANES - (CONV_W - 1), 0), (0, 0)))
    u, wq, ak, gt = _gdn_prep(cqkv, cba, buf8, lw["conv_w"], lw["alog_row"], lw["dtb_row"], gmat,
                              batch, t, tm, valid_len)
    return _gdn_scan(u, wq, ak, gt, cz, s0, lw["gnorm"], batch, t, bg)


def _ffn(lw, x, dense):
    n = x.shape[0]
    w1, w3, w2 = lw["w1"].astype(BF16), lw["w3"].astype(BF16), lw["w2"].astype(BF16)
    if n % MOE_TILE == 0:
        return _moe_grouped(x, lw["norm2_g"], dense, w1, w3, w2, lw["layer"], MOE_TILE, lw["final_g"])
    return _moe(x, lw["norm2_g"], dense, w1, w3, w2, lw["layer"], n, lw["final_g"])


def _prompt_layer(lw, x, batch, t, tabs, gmat, win_k, win_v, layer):
    n = batch * t
    dils = tuple(d for _, d in PATTERNS)
    tm = _tile_rows(n, 512)
    mixer_a = (lw["a_gain"], lw["a_ws"], jnp.repeat(lw["a_bs"].T, HEAD_DIM, axis=1))
    out_a, _, q, k, v, cqkv, cz, cba, *strided = _proj(x, lw["norm1_g"], lw["w_in_b"], layer, lw["w_small"], tabs,
                                                       mixer_a, tm, seq_len=t, dils=dils[1:])

    qkv = [(q, k, v)] + [tuple(a.reshape(n, B_W) for a in strided[3 * i:3 * i + 3]) for i in range(len(dils) - 1)]
    outs, lses = [], []
    for d, (qd, kd, vd) in zip(dils, qkv):
        o_d, lse_d = _win_attn(qd, kd, vd, t // d // WBLK)
        outs.append(o_d.reshape(batch, d, t // d, B_W))
        lses.append(lse_d.reshape(batch, d, t // d, LANES))

    zeros_buf = jnp.zeros((batch, CONV_W - 1, 3 * C_W), F32)
    zeros_s = jnp.zeros((batch, C_HEADS, HEAD_DIM, HEAD_DIM), F32)
    out_c, s_new = _mixer_c(lw, cqkv, cba, cz, zeros_buf, zeros_s, gmat, batch, t, GDN_TILE, t, batch)
    out_c = out_c.reshape(n, C_W)

    x, dense = _out_proj(out_a, outs, lses, out_c, x, lw["w_out"], lw["norm2_g"], lw["w_route"], tm,
                         seq_len=t, dils=dils)
    x = _ffn(lw, x, dense)

    win_k, win_v = _window_rows(k, v, win_k, win_v, layer, batch, t, win_k.shape[-1])
    conv_state = cqkv.reshape(batch, t, 3 * C_W)[:, t - (CONV_W - 1):]
    return x, win_k, win_v, (conv_state, s_new)


def _sample_layer(lw, x, batch, t, tabs, gmat, kbuf, vbuf, conv_buf, s0, layer):
    n = batch * t
    eye = jnp.eye(batch, dtype=F32)
    ws_bd = jnp.stack([jnp.kron(eye, lw["a_ws"][h, :t, :t]) for h in range(A_HEADS)])
    bias_tile = jnp.tile(jnp.repeat(lw["a_bs"][:, :t].T, HEAD_DIM, axis=1), (batch, 1))
    out_a, a_rows, q, k, v, cqkv, cz, cba = _proj(x, lw["norm1_g"], lw["w_in_b"], layer, lw["w_small"], tabs,
                                                  (lw["a_gain"], ws_bd, bias_tile), n)

    pad8 = lambda a: jnp.pad(a.reshape(batch, t, B_W), ((0, 0), (0, SUBLANES - t), (0, 0)))
    out_b = _dec_attn(pad8(q), kbuf, vbuf, pad8(k), pad8(v), t, layer)[:, :t].reshape(n, B_W)

    tp = DELTA_CHUNK
    padt = lambda a: jnp.pad(a.reshape(batch, t, -1), ((0, 0), (0, tp - t), (0, 0))).reshape(batch * tp, -1)
    out_c, s_new = _mixer_c(lw, padt(cqkv), padt(cba), padt(cz), conv_buf, s0, gmat, batch, tp, tp, t, 4)
    out_c = out_c[:, :t].reshape(n, C_W)

    x, dense = _out_proj(out_a, out_b, None, out_c, x, lw["w_out"], lw["norm2_g"], lw["w_route"], n)
    x = _ffn(lw, x, dense)

    heads = lambda a: a.reshape(batch, t, B_HEADS, HEAD_DIM)
    conv_state = jnp.concatenate([conv_buf, cqkv.reshape(batch, t, 3 * C_W)], axis=1)[:, -(CONV_W - 1):]
    return x, (heads(k), heads(v), a_rows.reshape(batch, t, A_W), conv_state, s_new)


def kernel(x_prompt, x_sample, cache_win_k, cache_win_v, state_conv, state_delta, norm1_g, w_in, a_vnorm_g, a_ws, a_bs, c_conv_w, c_a_log, c_dt_bias, c_norm_g, w_out, norm2_g, w_group, w_router, w1, w3, w2, final_g):
    bp, tp, _ = x_prompt.shape
    bs, ts, _ = x_sample.shape
    depth = w_in.shape[0]
    assert tp % (PATTERNS[-1][1] * WBLK) == 0 and tp % 512 == 0 and bs * ts == CHUNK

    tabs_p = _rope_tables(jnp.arange(tp, dtype=jnp.int32))
    tabs_s = tuple(jnp.tile(a, (bs, 1)) for a in _rope_tables(PAST_LEN + jnp.arange(ts, dtype=jnp.int32)))
    gmat = _gdn_constants()

    feat_major = lambda c: jnp.transpose(c, (0, 1, 3, 4, 2)).reshape(depth, bs, B_W, c.shape[2])
    cache_k, cache_v = feat_major(cache_win_k), feat_major(cache_win_v)
    keep = min(MAX_WINDOW, tp)
    win_k = jnp.zeros((depth, bp, B_W, keep), F32)
    win_v = jnp.zeros((depth, bp, B_W, keep), F32)

    xp = x_prompt.reshape(bp * tp, D_MODEL)
    xs = x_sample.reshape(bs * ts, D_MODEL)
    p_out = [[] for _ in range(2)]
    s_out = [[] for _ in range(5)]
    for l in range(depth):
        lw = _layer_weights(l, norm1_g, w_in, a_vnorm_g, a_ws, a_bs, c_conv_w, c_a_log, c_dt_bias, c_norm_g,
                            w_out, norm2_g, w_group, w_router, w1, w3, w2)
        lw["final_g"] = final_g[None, :] if l == depth - 1 else None
        xp, win_k, win_v, st = _prompt_layer(lw, xp, bp, tp, tabs_p, gmat, win_k, win_v, l)
        for acc, a in zip(p_out, st):
            acc.append(a)
        xs, st = _sample_layer(lw, xs, bs, ts, tabs_s, gmat, cache_k, cache_v, state_conv[l], state_delta[l], l)
        for acc, a in zip(s_out, st):
            acc.append(a)
    y_prompt = xp.reshape(bp, tp, D_MODEL)
    y_sample = xs.reshape(bs, ts, D_MODEL)
    rows_major = lambda w: jnp.transpose(w.reshape(depth, bp, B_HEADS, HEAD_DIM, keep), (0, 1, 4, 2, 3))
    return (y_prompt, y_sample, rows_major(win_k), rows_major(win_v), *[jnp.stack(a) for a in p_out],
            *[jnp.stack(a) for a in s_out])
```

```python
import functools

import jax
import jax.numpy as jnp
from jax import lax
from jax.experimental import pallas as pl
from jax.experimental.pallas import tpu as pltpu

F32 = jnp.float32
BF16 = jnp.bfloat16

D_MODEL = 1024
HEAD_DIM = 64
A_HEADS = 4
B_HEADS = 6
C_HEADS = 6
A_W = A_HEADS * HEAD_DIM
B_W = B_HEADS * HEAD_DIM
C_W = C_HEADS * HEAD_DIM
CHUNK = 128
PATTERNS = ((128, 1), (512, 4), (2048, 16))
MAX_WINDOW = 2048
ROT_DIM = HEAD_DIM // 4
ROPE_THETA = 500000.0
CONV_W = 4
DELTA_CHUNK = 64
N_GROUPS = 4
EXP_PER_GROUP = 8
N_EXPERTS = N_GROUPS * EXP_PER_GROUP
D_EXPERT = 256
EPS = 1e-6
PAST_LEN = 16384

LANES = 128
SUBLANES = 8
WBLK = 128
PROJ_MAIN = 2 * A_W + 3 * B_W + 4 * C_W
NEG = -1e30
ROUTE_OFF = N_GROUPS
VMEM_LIMIT = 56 * 1024 * 1024
MOE_TILE = 1024
MOE_STEP_EXPERTS = 4
MOE_BLOCK = 128
GDN_TILE = 256
GDN_GROUP = 4


def _cparams(sem):
    return pltpu.CompilerParams(dimension_semantics=sem, vmem_limit_bytes=VMEM_LIMIT)


def _sigmoid(x):
    return 1.0 / (1.0 + jnp.exp(-x))


def _silu(x):
    return x * _sigmoid(x)


def _dot(a, b):
    return jnp.dot(a, b, preferred_element_type=F32)


def _dot_nt(a, b):
    return lax.dot_general(a, b, (((1,), (1,)), ((), ())), preferred_element_type=F32)


def _chunk_mlp_rows(u, v, gain, w_ref, bias, o_ref, vn_ref):
    xc = v - jnp.mean(v, axis=-1, keepdims=True)
    vn = xc * lax.rsqrt(jnp.mean(xc * xc, axis=-1, keepdims=True) + EPS) * gain
    vn_ref[...] = vn
    vb = vn.astype(BF16)
    rows = lax.broadcasted_iota(jnp.int32, (CHUNK, CHUNK), 0)
    cols = lax.broadcasted_iota(jnp.int32, (CHUNK, CHUNK), 1)
    tril = rows >= cols
    ws = [jnp.where(tril, w_ref[h], 0.0).astype(BF16) for h in range(A_HEADS)]
    for c in range(v.shape[0] // CHUNK):
        rs = slice(c * CHUNK, (c + 1) * CHUNK)
        parts = [_dot(ws[h], vb[rs, h * HEAD_DIM:(h + 1) * HEAD_DIM]) for h in range(A_HEADS)]
        o_ref[rs, :] = u[rs, :] * (jnp.concatenate(parts, axis=-1) + bias)


def _proj_kernel(x_ref, g_ref, w_ref, wsm_ref, c_ref, s1_ref, s2_ref, ag_ref, aw_ref, ab_ref,
                 oa_ref, vn_ref, q_ref, k_ref, v_ref, cqkv_ref, cz_ref, cba_ref, *strided_refs, dils):
    x = x_ref[...]
    h = x * lax.rsqrt(jnp.mean(x * x, axis=-1, keepdims=True) + EPS) * g_ref[...]
    hb = h.astype(BF16)

    def seg(a, b):
        return _dot(hb, w_ref[:, a:b])

    _chunk_mlp_rows(seg(0, A_W), seg(A_W, 2 * A_W), ag_ref[...], aw_ref, ab_ref[...], oa_ref, vn_ref)
    c, s1, s2 = c_ref[...], s1_ref[...], s2_ref[...]
    q0 = 2 * A_W
    k0 = q0 + B_W
    nblk = B_W // LANES
    stage = strided_refs[-1] if dils else None
    for j in range(nblk):
        cols = slice(LANES * j, LANES * (j + 1))
        for a, (base, ref, scale) in enumerate(((q0, q_ref, HEAD_DIM ** -0.5), (k0, k_ref, None))):
            xc = seg(base + LANES * j, base + LANES * (j + 1))
            r = xc * c + pltpu.roll(xc, ROT_DIM // 2, 1) * s1 + pltpu.roll(xc, LANES - ROT_DIM // 2, 1) * s2
            if scale is not None:
                r = r * scale
            ref[:, cols] = r.astype(ref.dtype)
            if dils:
                stage[a * nblk + j] = r
    v0 = k0 + B_W
    v_ref[...] = seg(v0, v0 + B_W)
    c0 = v0 + B_W
    cqkv_ref[...] = seg(c0, c0 + 3 * C_W)
    cz_ref[...] = seg(c0 + 3 * C_W, c0 + 4 * C_W)
    cba_ref[...] = _dot(hb, wsm_ref[...])
    if dils:
        outs = strided_refs[:-1]
        tm = x.shape[0]
        for b in range(nblk):
            stage[2 * nblk + b] = v_ref[:, LANES * b:LANES * (b + 1)]
        for i, d in enumerate(dils):
            for a in range(3):
                dst = outs[3 * i + a]
                for j in range(d):
                    for b in range(nblk):
                        dst[j, :, LANES * b:LANES * (b + 1)] = stage[a * nblk + b,
                                                                     pl.ds(j, tm // d, stride=d), :].astype(dst.dtype)


def _proj(x, g, w_in_b, layer, w_small, tabs, mixer_a, tm, seq_len=None, dils=()):
    n = x.shape[0]
    nt = n // tm
    ntab = tabs[0].shape[0] // tm
    row = lambda i: (i, 0)
    fixed = lambda i: (0, 0)
    tab = lambda i: (i % ntab, 0)
    widths = (A_W, A_W, B_W, B_W, B_W, 3 * C_W, C_W, LANES)
    out_specs = [pl.BlockSpec((tm, w), row) for w in widths]
    out_shape = [jax.ShapeDtypeStruct((n, w), BF16 if i == 2 else F32) for i, w in enumerate(widths)]
    for d in dils:
        tps = seq_len // tm
        out_specs += [pl.BlockSpec((None, d, tm // d, B_W), lambda i, tps=tps: (i // tps, 0, i % tps, 0))] * 3
        out_shape += [jax.ShapeDtypeStruct((n // seq_len, d, seq_len // d, B_W), BF16)] * 3
    return pl.pallas_call(
        functools.partial(_proj_kernel, dils=dils),
        grid=(nt,),
        in_specs=[pl.BlockSpec((tm, D_MODEL), row), pl.BlockSpec((1, D_MODEL), fixed),
                  pl.BlockSpec((None, D_MODEL, PROJ_MAIN), lambda i: (layer, 0, 0)),
                  pl.BlockSpec((D_MODEL, LANES), fixed),
                  pl.BlockSpec((tm, LANES), tab), pl.BlockSpec((tm, LANES), tab), pl.BlockSpec((tm, LANES), tab),
                  pl.BlockSpec((1, A_W), fixed), pl.BlockSpec((A_HEADS, CHUNK, CHUNK), lambda i: (0, 0, 0)),
                  pl.BlockSpec((CHUNK, A_W), fixed)],
        out_specs=out_specs,
        out_shape=out_shape,
        scratch_shapes=[pltpu.VMEM((3 * B_W // LANES, tm, LANES), F32)] if dils else [],
        compiler_params=_cparams(("parallel",)),
        name="proj",
    )(x, g, w_in_b, w_small, *tabs, *mixer_a)


def _rope_tables(pos):
    half = ROT_DIM // 2
    inv_freq = jnp.power(ROPE_THETA, -jnp.arange(0, ROT_DIM, 2, dtype=F32) / ROT_DIM)
    ang = pos.astype(F32)[:, None] * inv_freq[None, :]
    cos, sin = jnp.cos(ang), jnp.sin(ang)
    p = pos.shape[0]
    z8 = jnp.zeros((p, half), F32)
    rest0 = jnp.zeros((p, HEAD_DIM - ROT_DIM), F32)
    c64 = jnp.concatenate([cos, cos, jnp.ones((p, HEAD_DIM - ROT_DIM), F32)], axis=-1)
    s1 = jnp.concatenate([z8, sin, rest0], axis=-1)
    s2 = jnp.concatenate([-sin, z8, rest0], axis=-1)
    two = lambda a: jnp.concatenate([a, a], axis=-1)
    return two(c64), two(s1), two(s2)


def _win_attn_kernel(q_ref, kp_ref, kc_ref, vp_ref, vc_ref, o_ref, lse_ref, *, seg_blocks):
    s = pl.program_id(0)
    rows = lax.broadcasted_iota(jnp.int32, (WBLK, 2 * WBLK), 0)
    cols = lax.broadcasted_iota(jnp.int32, (WBLK, 2 * WBLK), 1)
    dist = rows + WBLK - cols
    band = (dist >= 0) & (dist <= WBLK)
    lo = jnp.where((2 * s) % seg_blocks == 0, WBLK, 0)
    biases = [jnp.where(band & (cols >= lo), 0.0, NEG), jnp.where(band, 0.0, NEG)]
    lane = lax.broadcasted_iota(jnp.int32, (WBLK, LANES), 1)
    k3 = jnp.concatenate([kp_ref[...], kc_ref[...]], axis=0).astype(BF16)
    v3 = jnp.concatenate([vp_ref[...], vc_ref[...]], axis=0).astype(BF16)
    heads = range(B_HEADS)
    sls = [slice(h // 2 * LANES, (h // 2 + 1) * LANES) for h in heads]
    hms = [(lane < HEAD_DIM) if h % 2 == 0 else (lane >= HEAD_DIM) for h in heads]
    work = [(j, h) for j in range(2) for h in heads]
    qs = [q_ref[j * WBLK:(j + 1) * WBLK, :] for j in range(2)]
    ks = [k3[j * WBLK:(j + 2) * WBLK, :] for j in range(2)]
    vs = [v3[j * WBLK:(j + 2) * WBLK, :] for j in range(2)]
    scs = [_dot_nt(jnp.where(hms[h], qs[j][:, sls[h]], 0.0).astype(BF16), ks[j][:, sls[h]]) + biases[j]
           for j, h in work]
    ms = [jnp.max(sc, axis=-1, keepdims=True) for sc in scs]
    es = [jnp.exp(sc - m) for sc, m in zip(scs, ms)]
    dens = [jnp.sum(e, axis=-1, keepdims=True) for e in es]
    os_ = [_dot((e * (1.0 / d)).astype(BF16), vs[j][:, sls[h]]) for (j, h), e, d in zip(work, es, dens)]
    for j in range(2):
        base = j * B_HEADS
        lse_tile = jnp.zeros((WBLK, LANES), F32)
        for h in heads:
            lse_tile = jnp.where(lane == h, ms[base + h] + jnp.log(dens[base + h]), lse_tile)
        lse_ref[j * WBLK:(j + 1) * WBLK, :] = lse_tile
        for hp in range(B_W // LANES):
            o_ref[j * WBLK:(j + 1) * WBLK, sls[2 * hp]] = jnp.where(hms[2 * hp], os_[base + 2 * hp],
                                                                  os_[base + 2 * hp + 1])


def _win_attn(qd, kd, vd, seg_blocks):
    n = qd.shape[0]
    assert seg_blocks % 2 == 0
    cur = lambda s: (s, 0)
    prev = lambda s: (jnp.maximum(2 * s - 1, 0), 0)
    blk = lambda w, im: pl.BlockSpec((2 * WBLK, w), im)
    pblk = pl.BlockSpec((WBLK, B_W), prev)
    return pl.pallas_call(
        functools.partial(_win_attn_kernel, seg_blocks=seg_blocks),
        grid=(n // (2 * WBLK),),
        in_specs=[blk(B_W, cur), pblk, blk(B_W, cur), pblk, blk(B_W, cur)],
        out_specs=[blk(B_W, cur), blk(LANES, cur)],
        out_shape=[jax.ShapeDtypeStruct((n, B_W), F32), jax.ShapeDtypeStruct((n, LANES), F32)],
        compiler_params=_cparams(("parallel",)),
        name="win_attn",
    )(qd, kd, kd, vd, vd)


def _window_rows_kernel(k_ref, v_ref, kprev_ref, vprev_ref, ko_ref, vo_ref):
    del kprev_ref, vprev_ref
    for b in range(k_ref.shape[0]):
        ko_ref[b] = k_ref[b].T
        vo_ref[b] = v_ref[b].T


def _window_rows(k, v, k_all, v_all, layer, batch, t, keep):
    first = (t - keep) // WBLK
    src = pl.BlockSpec((batch, WBLK, B_W), lambda i: (0, first + i, 0))
    dst = pl.BlockSpec((None, batch, B_W, WBLK), lambda i: (layer, 0, 0, i))
    hbm = pl.BlockSpec(memory_space=pl.ANY)
    shape = jax.ShapeDtypeStruct(k_all.shape, F32)
    return pl.pallas_call(
        _window_rows_kernel,
        grid=(keep // WBLK,),
        in_specs=[src, src, hbm, hbm],
        out_specs=[dst, dst],
        out_shape=[shape, shape],
        input_output_aliases={2: 0, 3: 1},
        compiler_params=_cparams(("parallel",)),
        name="window_rows",
    )(k.reshape(batch, t, B_W), v.reshape(batch, t, B_W), k_all, v_all)


def _dec_attn_kernel(q_ref, kc_ref, vc_ref, kn_ref, vn_ref, o_ref, *, t_new, cache_len):
    rows_c = lax.broadcasted_iota(jnp.int32, (SUBLANES, cache_len), 0)
    cols_c = lax.broadcasted_iota(jnp.int32, (SUBLANES, cache_len), 1)
    dist_c = cache_len + rows_c % t_new - cols_c
    rows_n = lax.broadcasted_iota(jnp.int32, (SUBLANES, SUBLANES), 0)
    cols_n = lax.broadcasted_iota(jnp.int32, (SUBLANES, SUBLANES), 1)
    dist_n = rows_n % t_new - cols_n
    biases = []
    for window, dil in PATTERNS:
        vc_ok = (dist_c <= window) & ((dist_c & (dil - 1)) == 0)
        vn_ok = (dist_n >= 0) & ((dist_n & (dil - 1)) == 0)
        biases.append((jnp.where(vc_ok, 0.0, NEG), jnp.where(vn_ok, 0.0, NEG)))
    row8 = lax.broadcasted_iota(jnp.int32, (SUBLANES, LANES), 0)
    lane8 = lax.broadcasted_iota(jnp.int32, (SUBLANES, LANES), 1)
    own = (lane8 < HEAD_DIM) == (row8 < t_new)
    q = q_ref[...].astype(F32)
    outs = []
    for hp in range(B_W // LANES):
        sl = slice(hp * LANES, (hp + 1) * LANES)
        qq = jnp.concatenate([q[0:t_new, sl], q[0:t_new, sl]], axis=0)
        q8 = jnp.where(own, qq, 0.0).astype(BF16)
        kc, vc = kc_ref[sl, :].astype(BF16), vc_ref[sl, :].astype(BF16)
        kn, vn = kn_ref[:, sl].astype(BF16), vn_ref[:, sl].astype(BF16)
        sc_c = _dot(q8, kc)
        sc_n = _dot_nt(q8, kn)
        os_, lses = [], []
        for bc, bn in biases:
            a_c, a_n = sc_c + bc, sc_n + bn
            m = jnp.maximum(jnp.max(a_c, axis=-1, keepdims=True), jnp.max(a_n, axis=-1, keepdims=True))
            e_c, e_n = jnp.exp(a_c - m), jnp.exp(a_n - m)
            den = jnp.sum(e_c, axis=-1, keepdims=True) + jnp.sum(e_n, axis=-1, keepdims=True)
            inv = 1.0 / den
            os_.append(_dot_nt((e_c * inv).astype(BF16), vc) + _dot((e_n * inv).astype(BF16), vn))
            lses.append(m + jnp.log(den))
        mx = jnp.maximum(jnp.maximum(lses[0], lses[1]), lses[2])
        ws = [jnp.exp(l - mx) for l in lses]
        tot = ws[0] + ws[1] + ws[2]
        o8 = (ws[0] / tot) * os_[0] + (ws[1] / tot) * os_[1] + (ws[2] / tot) * os_[2]
        lane4 = lane8[0:t_new]
        outs.append(jnp.where(lane4 < HEAD_DIM, o8[0:t_new], o8[t_new:2 * t_new]))
    o_ref[...] = jnp.zeros((SUBLANES, B_W), F32)
    o_ref[0:t_new, :] = jnp.concatenate(outs, axis=-1)


def _dec_attn(q8, kc, vc, kn8, vn8, t_new, layer):
    _, b, _, cache_len = kc.shape
    assert 2 * t_new == SUBLANES
    small = pl.BlockSpec((None, SUBLANES, B_W), lambda i: (i, 0, 0))
    big = pl.BlockSpec((None, None, B_W, cache_len), lambda i: (layer, i, 0, 0))
    return pl.pallas_call(
        functools.partial(_dec_attn_kernel, t_new=t_new, cache_len=cache_len),
        grid=(b,),
        in_specs=[small, big, big, small, small],
        out_specs=small,
        out_shape=jax.ShapeDtypeStruct((b, SUBLANES, B_W), F32),
        compiler_params=_cparams(("parallel",)),
        name="dec_attn",
    )(q8, kc, vc, kn8, vn8)


def _split3(x):
    hi = x.astype(BF16)
    r1 = x - hi.astype(F32)
    mid = r1.astype(BF16)
    return hi, mid, (r1 - mid.astype(F32)).astype(BF16)


def _pair_dup(xx):
    lane = lax.broadcasted_iota(jnp.int32, xx.shape, 1)
    hi = xx.astype(BF16).astype(F32)
    return jnp.where(lane < HEAD_DIM, xx, xx - hi).astype(BF16)


def _lhs4(pair):
    return jnp.concatenate([pair, pair], axis=1)


def _rhs4(y):
    hi = y.astype(BF16)
    lo = (y - hi.astype(F32)).astype(BF16)
    return jnp.concatenate([hi, hi, lo, lo], axis=0)


def _unit_lower_solves(mats, rhss):
    n = DELTA_CHUNK
    w2 = 2 * HEAD_DIM
    rows = lax.broadcasted_iota(jnp.int32, (n, w2), 0)
    cols = lax.broadcasted_iota(jnp.int32, (n, w2), 1) % HEAD_DIM
    in16 = rows // 16 == cols // 16
    eye = jnp.where(rows == cols, 1.0, 0.0)
    ds = [jnp.where(in16, a, 0.0) for a in mats]
    es = [jnp.where(in16, 0.0, a) for a in mats]
    left = lambda xx: _lhs4(_pair_dup(xx))
    pw = [_dot(left(d), _rhs4(d)) for d in ds]
    ts = [eye - d for d in ds]
    for _ in range(2):
        outs = [_dot(jnp.concatenate([left(t), left(p)], axis=0), _rhs4(p)) for t, p in zip(ts, pw)]
        ts = [t + o[0:n] for t, o in zip(ts, outs)]
        pw = [o[n:2 * n] for o in outs]
    ts = [t + _dot(left(t), _rhs4(p)) for t, p in zip(ts, pw)]
    o5 = [_dot(left(t), _rhs4(jnp.concatenate([r, e], axis=1))) for t, r, e in zip(ts, rhss, es)]
    x0 = [o[:, 0:w2] for o in o5]
    nm = [o[:, w2:2 * w2] for o in o5]
    o6 = [_dot(left(m), _rhs4(jnp.concatenate([x, m], axis=1))) for m, x in zip(nm, x0)]
    ys = [x - o[:, 0:w2] for x, o in zip(x0, o6)]
    return [y + _dot(left(o[:, w2:2 * w2]), _rhs4(y)) for y, o in zip(ys, o6)]


def _gdn_prep_kernel(x_ref, cba_ref, buf_ref, cw_ref, alog_ref, dtb_ref, gmat_ref, xmat_ref,
                     u_ref, wq_ref, ak_ref, gt_ref,
                     xp_scr, q_scr, k_scr, v_scr, gb_scr, *, tm, valid_len, group):
    ti = pl.program_id(1)

    @pl.when(ti == 0)
    def _():
        xp_scr[0:SUBLANES, :] = buf_ref[...]

    x = x_ref[...]
    xp_scr[SUBLANES:SUBLANES + tm, :] = x
    off = SUBLANES - (CONV_W - 1)
    acc = xp_scr[off:off + tm, :] * cw_ref[0:1, :]
    for j in range(1, CONV_W):
        acc = acc + xp_scr[off + j:off + j + tm, :] * cw_ref[j:j + 1, :]
    xp_scr[0:SUBLANES, :] = x[tm - SUBLANES:tm, :]
    y = _silu(acc)
    q, k = y[:, 0:C_W], y[:, C_W:2 * C_W]
    gmat = gmat_ref[...]
    head_sum = lambda a: sum(_dot(p, gmat) for p in _split3(a))
    q_scr[...] = q * lax.rsqrt(head_sum(q * q) + EPS) * (HEAD_DIM ** -0.5)
    k_scr[...] = k * lax.rsqrt(head_sum(k * k) + EPS)
    v_scr[...] = y[:, 2 * C_W:3 * C_W]

    cba = cba_ref[...]
    lane = lax.broadcasted_iota(jnp.int32, (tm, LANES), 1)
    tpos = ti * tm + lax.broadcasted_iota(jnp.int32, (tm, LANES), 0)
    live = tpos < valid_len
    beta = _sigmoid(cba)
    z = cba + dtb_ref[...]
    softplus = jnp.maximum(z, 0.0) + jnp.log(1.0 + jnp.exp(-jnp.abs(z)))
    g = -jnp.exp(alog_ref[...]) * softplus
    is_g = (lane >= C_HEADS) & (lane < 2 * C_HEADS)
    gb_scr[...] = jnp.where(live, jnp.where(is_g, g, jnp.where(lane < C_HEADS, beta, 0.0)), 0.0)

    n = DELTA_CHUNK
    rows = lax.broadcasted_iota(jnp.int32, (n, 2 * n), 0)
    cols = lax.broadcasted_iota(jnp.int32, (n, 2 * n), 1) % n
    incl = rows >= cols
    strict = rows > cols
    ltri = jnp.where(incl[:, 0:n], 1.0, 0.0).astype(BF16)
    heads = range(C_HEADS)
    tiles = range(C_W // LANES)
    tsl = [slice(t * LANES, (t + 1) * LANES) for t in tiles]
    lane_c = lax.broadcasted_iota(jnp.int32, (n, LANES), 1)
    low = lane_c < HEAD_DIM
    own = [low if h % 2 == 0 else jnp.logical_not(low) for h in heads]
    xmat = xmat_ref[...]
    wide0 = C_HEADS * LANES

    def head_pairs(tile):
        rolled = pltpu.roll(tile, HEAD_DIM, 1)
        lo = rolled - rolled.astype(BF16).astype(F32)
        lane = lax.broadcasted_iota(jnp.int32, tile.shape, 1) < HEAD_DIM
        return jnp.where(lane, tile, lo).astype(BF16), jnp.where(lane, lo, tile).astype(BF16)

    def setup(c):
        rs = pl.ds(pl.multiple_of(c * n, n), n)
        gb = gb_scr[rs, :]
        g_only = jnp.where((lane_c >= C_HEADS) & (lane_c < 2 * C_HEADS), gb, 0.0)
        cg = sum(_dot(ltri, p) for p in _split3(g_only))
        cgt = cg.T
        cgl = cg[n - 1:n, :]
        gt_ref[pl.ds(pl.multiple_of(c * SUBLANES, SUBLANES), SUBLANES), :] = jnp.broadcast_to(
            jnp.exp(cgl), (SUBLANES, LANES))
        wide = sum(_dot(p, xmat) for p in _split3(jnp.where(lane_c < C_HEADS, gb, cg)))
        b128 = [wide[:, h * LANES:(h + 1) * LANES] for h in heads]
        c128 = [wide[:, wide0 + h * LANES:wide0 + (h + 1) * LANES] for h in heads]
        cgr = [jnp.concatenate([cgt[C_HEADS + h:C_HEADS + h + 1, :]] * 2, axis=1) for h in heads]
        decay = [jnp.where(incl, jnp.exp(jnp.minimum(c128[h] - cgr[h], 0.0)), 0.0) for h in heads]
        amat, at_pair, rhs, qd_pair, kdt_pair = [], [], [], [], []
        for t in tiles:
            he, ho = 2 * t, 2 * t + 1
            qn, kn, vv = q_scr[rs, tsl[t]], k_scr[rs, tsl[t]], v_scr[rs, tsl[t]]
            bnat = jnp.where(low, b128[he], b128[ho])
            cnat = jnp.where(low, c128[he], c128[ho])
            ecg = jnp.exp(cnat)
            kb = kn.astype(BF16)
            kk2 = jnp.concatenate([kb, kb], axis=0)
            for h in (he, ho):
                masked = jnp.concatenate([jnp.where(own[h], kn, 0.0), jnp.where(own[h], qn, 0.0)], axis=0)
                kkqk = _dot_nt(masked.astype(BF16), kk2)
                amat.append(jnp.where(strict, b128[h] * kkqk[0:n] * decay[h], 0.0))
                at_pair.append(_pair_dup(kkqk[n:2 * n] * decay[h]))
            bv = bnat * vv
            bk_rolled = pltpu.roll((bnat * ecg) * kn, HEAD_DIM, 1)
            rhs += [jnp.where(low, bv, bk_rolled), jnp.where(low, bk_rolled, bv)]
            qd_pair += list(head_pairs(qn * ecg))
            kd = kn * jnp.exp(cnat[n - 1:n, :] - cnat)
            kdt = jnp.concatenate([kd, kd], axis=0).T
            kdt_pair += [_pair_dup(kdt[0:n]), _pair_dup(kdt[n:2 * n])]
        return amat, rhs, at_pair, qd_pair, kdt_pair

    def finish(c, sol, at_pair, qd_pair, kdt_pair):
        w_pair = []
        for t in tiles:
            se, so = sol[2 * t], sol[2 * t + 1]
            u_ref[pl.ds(pl.multiple_of(c * n, n), n), tsl[t]] = jnp.where(low, se, so)
            w_odd, w_even = head_pairs(jnp.where(low, so, se))
            w_pair += [w_even, w_odd]
        r2 = pl.ds(pl.multiple_of(c * 2 * n, 2 * n), 2 * n)
        wq_ref[r2, :] = jnp.concatenate(
            [jnp.concatenate([w_pair[h], qd_pair[h]], axis=0) for h in heads], axis=1)
        ak_ref[r2, :] = jnp.concatenate(
            [jnp.concatenate([at_pair[h], kdt_pair[h]], axis=0) for h in heads], axis=1)

    def chunks(i, carry):
        parts = [setup(i * group + j) for j in range(group)]
        sol = _unit_lower_solves([a for p in parts for a in p[0]], [r for p in parts for r in p[1]])
        for j, p in enumerate(parts):
            finish(i * group + j, sol[j * C_HEADS:(j + 1) * C_HEADS], *p[2:])
        return carry

    lax.fori_loop(0, tm // n // group, chunks, 0)


def _gdn_constants():
    hid = jnp.arange(C_W, dtype=jnp.int32) // HEAD_DIM
    gmat = (hid[:, None] == hid[None, :]).astype(BF16)
    src = jnp.arange(LANES, dtype=jnp.int32)[:, None]
    dst = jnp.arange(2 * C_HEADS * LANES, dtype=jnp.int32)[None, :] // LANES
    return gmat, (src == dst).astype(BF16)


def _gdn_prep(cqkv, cba, buf8, conv_w, alog_row, dtb_row, consts, batch, t, tm, valid_len):
    gmat, xmat = consts
    n = batch * t
    nt = t // tm
    row = lambda b, i: (b * nt + i, 0)
    fixed = lambda b, i: (0, 0)
    nch = tm // DELTA_CHUNK
    pair_w = C_HEADS * 2 * HEAD_DIM
    outs = [jax.ShapeDtypeStruct((n, C_W), F32), jax.ShapeDtypeStruct((2 * n, pair_w), BF16),
            jax.ShapeDtypeStruct((2 * n, pair_w), BF16),
            jax.ShapeDtypeStruct((n // DELTA_CHUNK * SUBLANES, LANES), F32)]
    return pl.pallas_call(
        functools.partial(_gdn_prep_kernel, tm=tm, valid_len=valid_len, group=min(GDN_GROUP, nch)),
        grid=(batch, nt),
        in_specs=[pl.BlockSpec((tm, 3 * C_W), row), pl.BlockSpec((tm, LANES), row),
                  pl.BlockSpec((None, SUBLANES, 3 * C_W), lambda b, i: (b, 0, 0)),
                  pl.BlockSpec((CONV_W, 3 * C_W), fixed), pl.BlockSpec((1, LANES), fixed),
                  pl.BlockSpec((1, LANES), fixed), pl.BlockSpec((C_W, C_W), fixed),
                  pl.BlockSpec(xmat.shape, fixed)],
        out_specs=[pl.BlockSpec((tm, C_W), row), pl.BlockSpec((2 * tm, pair_w), row),
                   pl.BlockSpec((2 * tm, pair_w), row), pl.BlockSpec((nch * SUBLANES, LANES), row)],
        out_shape=outs,
        scratch_shapes=[pltpu.VMEM((tm + SUBLANES, 3 * C_W), F32), pltpu.VMEM((tm, C_W), F32),
                        pltpu.VMEM((tm, C_W), F32), pltpu.VMEM((tm, C_W), F32), pltpu.VMEM((tm, LANES), F32)],
        compiler_params=_cparams(("parallel", "arbitrary")),
        name="gdn_prep",
    )(cqkv, cba, buf8, conv_w, alog_row, dtb_row, gmat, xmat)


def _gdn_scan_kernel(u_ref, wq_ref, ak_ref, gt_ref, z_ref, s0_ref, gn_ref,
                     o_ref, sfin_ref, s_scr, *, bg):
    c = pl.program_id(1)

    @pl.when(c == 0)
    def _():
        s_scr[...] = s0_ref[...]

    gn = gn_ref[...]
    n = DELTA_CHUNK
    chains = [(b, h, slice(h * HEAD_DIM, (h + 1) * HEAD_DIM), slice(h * 2 * HEAD_DIM, (h + 1) * 2 * HEAD_DIM))
              for b in range(bg) for h in range(C_HEADS)]
    st = [s_scr[b, h] for b, h, _, _ in chains]
    for j in range(u_ref.shape[1] // n):
        rows, rows2 = slice(j * n, (j + 1) * n), slice(2 * j * n, 2 * (j + 1) * n)
        r1 = [_dot(_lhs4(wq_ref[b, rows2, ps]), _rhs4(s)) for (b, _, _, ps), s in zip(chains, st)]
        up = [u_ref[b, rows, sl] - r[0:n] for (b, _, sl, _), r in zip(chains, r1)]
        r2 = [_dot(_lhs4(ak_ref[b, rows2, ps]), _rhs4(x)) for (b, _, _, ps), x in zip(chains, up)]
        g0 = j * SUBLANES
        st = [gt_ref[b, g0:g0 + 1, C_HEADS + h:C_HEADS + h + 1] * s + r[n:2 * n]
              for (b, h, _, _), s, r in zip(chains, st, r2)]
        os_ = [a[n:2 * n] + r[0:n] for a, r in zip(r1, r2)]
        outs = [o * lax.rsqrt(jnp.mean(o * o, axis=-1, keepdims=True) + EPS) * gn * _silu(z_ref[b, rows, sl])
                for (b, _, sl, _), o in zip(chains, os_)]
        for b in range(bg):
            o_ref[b, rows, :] = jnp.concatenate(outs[b * C_HEADS:(b + 1) * C_HEADS], axis=1)
    for (b, h, _, _), s in zip(chains, st):
        s_scr[b, h] = s

    @pl.when(c == pl.num_programs(1) - 1)
    def _():
        sfin_ref[...] = s_scr[...]


def _gdn_scan(u, wq, ak, gt, z, s0, gnorm, batch, t, bg):
    n = DELTA_CHUNK
    nc = t // n
    pair_w = C_HEADS * 2 * HEAD_DIM
    v3 = lambda a: a.reshape(batch, t, C_W)
    p3 = lambda a: a.reshape(batch, 2 * t, pair_w)
    cps = 2 if nc % 2 == 0 else 1
    tok = pl.BlockSpec((bg, cps * n, C_W), lambda b, c: (b, c, 0))
    pair = pl.BlockSpec((bg, cps * 2 * n, pair_w), lambda b, c: (b, c, 0))
    st = pl.BlockSpec((bg, C_HEADS, HEAD_DIM, HEAD_DIM), lambda b, c: (b, 0, 0, 0))
    return pl.pallas_call(
        functools.partial(_gdn_scan_kernel, bg=bg),
        grid=(batch // bg, nc // cps),
        in_specs=[tok, pair, pair,
                  pl.BlockSpec((bg, cps * SUBLANES, LANES), lambda b, c: (b, c, 0)),
                  tok, st, pl.BlockSpec((1, HEAD_DIM), lambda b, c: (0, 0))],
        out_specs=[tok, st],
        out_shape=[jax.ShapeDtypeStruct((batch, t, C_W), F32),
                   jax.ShapeDtypeStruct((batch, C_HEADS, HEAD_DIM, HEAD_DIM), F32)],
        scratch_shapes=[pltpu.VMEM((bg, C_HEADS, HEAD_DIM, HEAD_DIM), F32)],
        compiler_params=_cparams(("parallel", "arbitrary")),
        name="gdn_scan",
    )(v3(u), p3(wq), p3(ak), gt.reshape(batch, nc * SUBLANES, LANES), v3(z), s0, gnorm)


def _head_expand(wt):
    src = lax.broadcasted_iota(jnp.int32, (LANES, B_W), 0)
    dst = lax.broadcasted_iota(jnp.int32, (LANES, B_W), 1) // HEAD_DIM
    spread = jnp.where(src == dst, 1.0, 0.0).astype(BF16)
    return sum(_dot(p, spread) for p in _split3(wt))


def _out_proj_kernel(*refs, dils):
    if dils:
        npat = len(dils)
        a_ref = refs[0]
        o_refs, l_refs = refs[1:1 + npat], refs[1 + npat:1 + 2 * npat]
        c_ref, x_ref, w_ref, g2_ref, wr_ref, y_ref, dense_ref, o_scr, l_scr = refs[1 + 2 * npat:]
        tm = x_ref.shape[0]
        os_, ls = [], []
        for i, d in enumerate(dils):
            if d == 1:
                os_.append(o_refs[i][0])
                ls.append(l_refs[i][0])
            else:
                nblk = B_W // LANES
                for j in range(d):
                    rows = pl.ds(j, tm // d, stride=d)
                    for b in range(nblk):
                        o_scr[i * nblk + b, rows, :] = o_refs[i][j, :, LANES * b:LANES * (b + 1)]
                    l_scr[i, rows, :] = l_refs[i][j]
                os_.append(jnp.concatenate([o_scr[i * nblk + b] for b in range(nblk)], axis=-1))
                ls.append(l_scr[i])
        mx = functools.reduce(jnp.maximum, ls)
        es = [jnp.exp(l - mx) for l in ls]
        tot = functools.reduce(lambda a, b: a + b, es)
        ob = functools.reduce(lambda a, b: a + b, [_head_expand(e / tot) * o for e, o in zip(es, os_)])
    else:
        a_ref, b_ref, c_ref, x_ref, w_ref, g2_ref, wr_ref, y_ref, dense_ref = refs
        ob = b_ref[...]
    cat = jnp.concatenate([a_ref[...], ob, c_ref[...]], axis=-1).astype(BF16)
    y = x_ref[...] + _dot(cat, w_ref[...])
    y_ref[...] = y
    dense_ref[...] = _route_tile(y, g2_ref[...], wr_ref[...])


def _out_proj(out_a, out_b, lses, out_c, x, w_out, g2, w_route, tm, seq_len=None, dils=()):
    n = x.shape[0]
    row = lambda i: (i, 0)
    spec = lambda w: pl.BlockSpec((tm, w), row)
    scratch = []
    if dils:
        tps = seq_len // tm
        strided = lambda d, w: pl.BlockSpec((None, d, tm // d, w), lambda i: (i // tps, 0, i % tps, 0))
        ins = [out_a, *out_b, *lses, out_c, x, w_out]
        specs = ([spec(A_W)] + [strided(d, B_W) for d in dils] + [strided(d, LANES) for d in dils]
                 + [spec(C_W), spec(D_MODEL)])
        scratch = [pltpu.VMEM((len(dils) * B_W // LANES, tm, LANES), F32), pltpu.VMEM((len(dils), tm, LANES), F32)]
    else:
        ins = [out_a, out_b, out_c, x, w_out]
        specs = [spec(A_W), spec(B_W), spec(C_W), spec(D_MODEL)]
    fixed = lambda i: (0, 0)
    specs += [pl.BlockSpec((D_MODEL, D_MODEL), fixed), pl.BlockSpec((1, D_MODEL), fixed),
              pl.BlockSpec((D_MODEL, LANES), fixed)]
    return pl.pallas_call(
        functools.partial(_out_proj_kernel, dils=dils),
        grid=(n // tm,),
        in_specs=specs,
        out_specs=[spec(D_MODEL), spec(LANES)],
        out_shape=[jax.ShapeDtypeStruct((n, D_MODEL), F32), jax.ShapeDtypeStruct((n, LANES), F32)],
        scratch_shapes=scratch,
        compiler_params=_cparams(("parallel",)),
        name="out_proj",
    )(*ins, g2, w_route)


def _route_tile(x, g, w):
    t = x * lax.rsqrt(jnp.mean(x * x, axis=-1, keepdims=True) + EPS) * g
    t_hi = t.astype(BF16)
    t_lo = (t - t_hi.astype(F32)).astype(BF16)
    w_hi = w.astype(BF16)
    w_lo = (w - w_hi.astype(F32)).astype(BF16)
    lg = _dot(t_hi, w_hi) + (_dot(t_hi, w_lo) + _dot(t_lo, w_hi))
    tm = lg.shape[0]
    nrow = ROUTE_OFF + N_EXPERTS + ROUTE_OFF
    lt = lg.T[0:nrow, :]
    row = lax.broadcasted_iota(jnp.int32, (nrow, tm), 0).astype(F32)
    big = float(LANES)
    down = lambda op, a: op(a, axis=0, keepdims=True)
    is_grp = row < N_GROUPS
    gl = jnp.where(is_grp, lt, NEG)
    gmax = down(jnp.max, gl)
    gsum = down(jnp.sum, jnp.where(is_grp, jnp.exp(gl - gmax), 0.0))
    g_w = 1.0 / gsum
    g_idx = down(jnp.min, jnp.where(is_grp & (gl == gmax), row, big))
    lo = ROUTE_OFF + EXP_PER_GROUP * g_idx
    sel = (row >= lo) & (row < lo + EXP_PER_GROUP)
    el = jnp.where(sel, lt, NEG)
    m1 = down(jnp.max, el)
    esum = down(jnp.sum, jnp.where(sel, jnp.exp(el - m1), 0.0))
    i1 = down(jnp.min, jnp.where(sel & (el == m1), row, big))
    el2 = jnp.where(row == i1, NEG, el)
    m2 = down(jnp.max, el2)
    i2 = down(jnp.min, jnp.where(sel & (row != i1) & (el2 == m2), row, big))
    p1 = 1.0 / esum
    p2 = jnp.exp(m2 - m1) / esum
    tot = p1 + p2
    gates = jnp.where(row == i1, g_w * (p1 / tot), 0.0) + jnp.where(row == i2, g_w * (p2 / tot), 0.0)
    full = jnp.concatenate([jnp.where(row == 0.0, g_idx, gates), jnp.zeros((LANES - nrow, tm), F32)], axis=0)
    return full.T


def _rms_rows(y, g):
    return y * lax.rsqrt(jnp.mean(y * y, axis=-1, keepdims=True) + EPS) * g


def _moe_kernel(x_ref, g_ref, dense_ref, w1_ref, w3_ref, w2_ref, *rest):
    fg_ref = rest[0] if len(rest) == 3 else None
    y_ref, t_scr = rest[-2:]
    e = pl.program_id(1)

    @pl.when(e == 0)
    def _():
        x = x_ref[...]
        t_scr[...] = (x * lax.rsqrt(jnp.mean(x * x, axis=-1, keepdims=True) + EPS) * g_ref[...]).astype(BF16)
        y_ref[...] = x

    tb = t_scr[...]
    dense = dense_ref[...]
    lane = lax.broadcasted_iota(jnp.int32, dense.shape, 1)
    gate = jnp.sum(jnp.where(lane == e + ROUTE_OFF, dense, 0.0), axis=-1, keepdims=True)
    hid = _silu(_dot(tb, w1_ref[...].astype(BF16))) * _dot(tb, w3_ref[...].astype(BF16))
    y_ref[...] += _dot((hid * gate).astype(BF16), w2_ref[...].astype(BF16))

    if fg_ref is not None:
        @pl.when(e == pl.num_programs(1) - 1)
        def _():
            y_ref[...] = _rms_rows(y_ref[...], fg_ref[...])


def _moe(x, g, dense, w1, w3, w2, layer, tm, final_g=None):
    n = x.shape[0]
    row = lambda i, e: (i, 0)
    expert = lambda i, e: (layer, e, 0, 0)
    extra = [] if final_g is None else [final_g]
    return pl.pallas_call(
        _moe_kernel,
        grid=(n // tm, N_EXPERTS),
        in_specs=[pl.BlockSpec((tm, D_MODEL), row), pl.BlockSpec((1, D_MODEL), lambda i, e: (0, 0)),
                  pl.BlockSpec((tm, LANES), row),
                  pl.BlockSpec((None, None, D_MODEL, D_EXPERT), expert),
                  pl.BlockSpec((None, None, D_MODEL, D_EXPERT), expert),
                  pl.BlockSpec((None, None, D_EXPERT, D_MODEL), expert)]
        + [pl.BlockSpec((1, D_MODEL), lambda i, e: (0, 0))] * len(extra),
        out_specs=pl.BlockSpec((tm, D_MODEL), row),
        out_shape=jax.ShapeDtypeStruct((n, D_MODEL), F32),
        scratch_shapes=[pltpu.VMEM((tm, D_MODEL), BF16)],
        compiler_params=_cparams(("parallel", "arbitrary")),
        name="moe",
    )(x, g, dense, w1, w3, w2, *extra)


def _moe_grouped_kernel(x_ref, g_ref, dense_ref, gidr_ref, tri_ref, w1_ref, w3_ref, w2_ref, *rest,
                        tm, slots, final):
    fg_ref = rest[0] if final else None
    y_ref, ts_scr, gs_scr, ys_scr, pt_scr, meta = rest[-6:]
    e = pl.program_id(1)
    blk = MOE_BLOCK

    @pl.when(e == 0)
    def _():
        x = x_ref[...]
        t = (x * lax.rsqrt(jnp.mean(x * x, axis=-1, keepdims=True) + EPS) * g_ref[...]).astype(BF16)
        dense = dense_ref[...]
        tri = tri_ref[...]
        lane = lax.broadcasted_iota(jnp.int32, (tm, LANES), 1)
        ohc = jnp.where(lane.astype(F32) == dense[:, 0:1], 1.0, 0.0)
        rankc = _dot(tri, ohc.astype(BF16))
        sub = lax.broadcasted_iota(jnp.int32, (SUBLANES, tm), 0)
        ohr = jnp.where(sub.astype(F32) == gidr_ref[...], 1.0, 0.0)
        rankr = _dot_nt(ohr.astype(BF16), tri)
        start = jnp.int32(0)
        s_lane = jnp.zeros((tm, LANES), F32)
        s_sub = jnp.zeros((SUBLANES, tm), F32)
        for g in range(N_GROUPS):
            count = jnp.sum(jnp.where(lane == g, ohc, 0.0)).astype(jnp.int32)
            nblk = (count + blk - 1) // blk
            meta[g] = start
            meta[N_GROUPS + g] = nblk
            s_lane = jnp.where(lane == g, start.astype(F32), s_lane)
            s_sub = jnp.where(sub == g, start.astype(F32), s_sub)
            start = start + nblk * blk
        destc = jnp.sum(ohc * (s_lane + rankc), axis=-1, keepdims=True)
        destr = jnp.sum(ohr * (s_sub + rankr), axis=0, keepdims=True)
        slot_r = lax.broadcasted_iota(jnp.int32, (slots, tm), 0).astype(F32)
        p = jnp.where(slot_r == destr, 1.0, 0.0).astype(BF16)
        ts_scr[...] = _dot(p, t).astype(BF16)
        g3 = _dot(p, jnp.concatenate(_split3(dense), axis=1))
        gs_scr[...] = g3[:, 0:LANES] + (g3[:, LANES:2 * LANES] + g3[:, 2 * LANES:3 * LANES])
        slot_c = lax.broadcasted_iota(jnp.int32, (tm, slots), 1).astype(F32)
        pt_scr[...] = jnp.where(slot_c == destc, 1.0, 0.0).astype(BF16)
        ys_scr[...] = jnp.zeros((slots, D_MODEL), F32)

    ne = MOE_STEP_EXPERTS
    g = (e * ne) // EXP_PER_GROUP
    start = meta[g]
    nblk = meta[N_GROUPS + g]
    w13 = jnp.concatenate([w1_ref[j] for j in range(ne)] + [w3_ref[j] for j in range(ne)], axis=1)
    w2 = jnp.concatenate([w2_ref[j] for j in range(ne)], axis=0)
    half = ne * D_EXPERT

    def rows_update(r0, m):
        rows = pl.ds(pl.multiple_of(r0, blk), m)
        tb = ts_scr[rows, :]
        gs = gs_scr[rows, :]
        lane = lax.broadcasted_iota(jnp.int32, (m, LANES), 1)
        h13 = _dot(tb, w13)
        hid = _silu(h13[:, 0:half]) * h13[:, half:2 * half]
        gated = [hid[:, j * D_EXPERT:(j + 1) * D_EXPERT]
                 * jnp.sum(jnp.where(lane == e * ne + j + ROUTE_OFF, gs, 0.0), axis=-1, keepdims=True)
                 for j in range(ne)]
        ys_scr[rows, :] += _dot(jnp.concatenate(gated, axis=1).astype(BF16), w2)

    def quad(j, carry):
        rows_update(start + j * 4 * blk, 4 * blk)
        return carry

    lax.fori_loop(0, nblk // 4, quad, 0)
    for rem in (1, 2, 3):
        @pl.when(nblk % 4 == rem)
        def _(rem=rem):
            rows_update(start + (nblk // 4) * 4 * blk, rem * blk)

    @pl.when(e == pl.num_programs(1) - 1)
    def _():
        pt = pt_scr[...]
        ys = ys_scr[...]
        hi = ys.astype(BF16)
        lo = (ys - hi.astype(F32)).astype(BF16)
        y = x_ref[...] + (_dot(pt, hi) + _dot(pt, lo))
        y_ref[...] = _rms_rows(y, fg_ref[...]) if final else y


def _moe_grouped(x, g, dense, w1, w3, w2, layer, tm, final_g=None):
    extra = [] if final_g is None else [final_g]
    n = x.shape[0]
    slots = tm + N_GROUPS * MOE_BLOCK
    ne = MOE_STEP_EXPERTS
    assert EXP_PER_GROUP % ne == 0
    once = pl.Buffered(1)
    row = lambda i, e: (i, 0)
    expert = lambda i, e: (layer, e, 0, 0)
    gid_rows = dense[:, 0].reshape(n // tm, 1, tm)
    idx = jnp.arange(tm, dtype=jnp.int32)
    tri = (idx[None, :] < idx[:, None]).astype(BF16)
    return pl.pallas_call(
        functools.partial(_moe_grouped_kernel, tm=tm, slots=slots, final=final_g is not None),
        grid=(n // tm, N_EXPERTS // ne),
        in_specs=[pl.BlockSpec((tm, D_MODEL), row, pipeline_mode=once),
                  pl.BlockSpec((1, D_MODEL), lambda i, e: (0, 0)),
                  pl.BlockSpec((tm, LANES), row, pipeline_mode=once),
                  pl.BlockSpec((None, 1, tm), lambda i, e: (i, 0, 0)),
                  pl.BlockSpec((tm, tm), lambda i, e: (0, 0), pipeline_mode=once),
                  pl.BlockSpec((None, ne, D_MODEL, D_EXPERT), expert),
                  pl.BlockSpec((None, ne, D_MODEL, D_EXPERT), expert),
                  pl.BlockSpec((None, ne, D_EXPERT, D_MODEL), expert)]
        + [pl.BlockSpec((1, D_MODEL), lambda i, e: (0, 0))] * len(extra),
        out_specs=pl.BlockSpec((tm, D_MODEL), row),
        out_shape=jax.ShapeDtypeStruct((n, D_MODEL), F32),
        scratch_shapes=[pltpu.VMEM((slots, D_MODEL), BF16), pltpu.VMEM((slots, LANES), F32),
                        pltpu.VMEM((slots, D_MODEL), F32), pltpu.VMEM((tm, slots), BF16),
                        pltpu.SMEM((2 * N_GROUPS,), jnp.int32)],
        compiler_params=_cparams(("parallel", "arbitrary")),
        name="moe_grouped",
    )(x, g, dense, gid_rows, tri, w1, w3, w2, *extra)


def _tile_rows(n, cap):
    tm = min(n, cap)
    assert n % tm == 0
    return tm


def _layer_weights(l, norm1_g, w_in, a_vnorm_g, a_ws, a_bs, c_conv_w, c_a_log, c_dt_bias, c_norm_g,
                   w_out, norm2_g, w_group, w_router, w1, w3, w2):
    pad_l = lambda a, left: jnp.pad(a, ((0, 0), (left, LANES - left - a.shape[-1])))
    return dict(
        norm1_g=norm1_g[l][None, :],
        w_in_b=w_in.astype(BF16),
        w_small=pad_l(w_in[l, :, PROJ_MAIN:], 0).astype(BF16),
        a_gain=a_vnorm_g[l][None, :],
        a_ws=a_ws[l],
        a_bs=a_bs[l],
        conv_w=c_conv_w[l],
        alog_row=pad_l(c_a_log[l][None, :], C_HEADS),
        dtb_row=pad_l(c_dt_bias[l][None, :], C_HEADS),
        gnorm=c_norm_g[l][None, :],
        w_out=w_out[l].astype(BF16),
        norm2_g=norm2_g[l][None, :],
        w_route=pad_l(jnp.concatenate([w_group[l], w_router[l]], axis=-1), 0),
        w1=w1, w3=w3, w2=w2, layer=l,
    )


def _mixer_c(lw, cqkv, cba, cz, conv_buf, s0, gmat, batch, t, tm, valid_len, bg):
    buf8 = jnp.pad(conv_buf, ((0, 0), (SUBLANES - (CONV_W - 1), 0), (0, 0)))
    u, wq, ak, gt = _gdn_prep(cqkv, cba, buf8, lw["conv_w"], lw["alog_row"], lw["dtb_row"], gmat,
                              batch, t, tm, valid_len)
    return _gdn_scan(u, wq, ak, gt, cz, s0, lw["gnorm"], batch, t, bg)


def _ffn(lw, x, dense):
    n = x.shape[0]
    w1, w3, w2 = lw["w1"].astype(BF16), lw["w3"].astype(BF16), lw["w2"].astype(BF16)
    if n % MOE_TILE == 0:
        return _moe_grouped(x, lw["norm2_g"], dense, w1, w3, w2, lw["layer"], MOE_TILE, lw["final_g"])
    return _moe(x, lw["norm2_g"], dense, w1, w3, w2, lw["layer"], n, lw["final_g"])


def _prompt_layer(lw, x, batch, t, tabs, gmat, win_k, win_v, layer):
    n = batch * t
    dils = tuple(d for _, d in PATTERNS)
    tm = _tile_rows(n, 512)
    mixer_a = (lw["a_gain"], lw["a_ws"], jnp.repeat(lw["a_bs"].T, HEAD_DIM, axis=1))
    out_a, _, q, k, v, cqkv, cz, cba, *strided = _proj(x, lw["norm1_g"], lw["w_in_b"], layer, lw["w_small"], tabs,
                                                       mixer_a, tm, seq_len=t, dils=dils[1:])

    qkv = [(q, k, v)] + [tuple(a.reshape(n, B_W) for a in strided[3 * i:3 * i + 3]) for i in range(len(dils) - 1)]
    outs, lses = [], []
    for d, (qd, kd, vd) in zip(dils, qkv):
        o_d, lse_d = _win_attn(qd, kd, vd, t // d // WBLK)
        outs.append(o_d.reshape(batch, d, t // d, B_W))
        lses.append(lse_d.reshape(batch, d, t // d, LANES))

    zeros_buf = jnp.zeros((batch, CONV_W - 1, 3 * C_W), F32)
    zeros_s = jnp.zeros((batch, C_HEADS, HEAD_DIM, HEAD_DIM), F32)
    out_c, s_new = _mixer_c(lw, cqkv, cba, cz, zeros_buf, zeros_s, gmat, batch, t, GDN_TILE, t, batch)
    out_c = out_c.reshape(n, C_W)

    x, dense = _out_proj(out_a, outs, lses, out_c, x, lw["w_out"], lw["norm2_g"], lw["w_route"], tm,
                         seq_len=t, dils=dils)
    x = _ffn(lw, x, dense)

    win_k, win_v = _window_rows(k, v, win_k, win_v, layer, batch, t, win_k.shape[-1])
    conv_state = cqkv.reshape(batch, t, 3 * C_W)[:, t - (CONV_W - 1):]
    return x, win_k, win_v, (conv_state, s_new)


def _sample_layer(lw, x, batch, t, tabs, gmat, kbuf, vbuf, conv_buf, s0, layer):
    n = batch * t
    eye = jnp.eye(batch, dtype=F32)
    ws_bd = jnp.stack([jnp.kron(eye, lw["a_ws"][h, :t, :t]) for h in range(A_HEADS)])
    bias_tile = jnp.tile(jnp.repeat(lw["a_bs"][:, :t].T, HEAD_DIM, axis=1), (batch, 1))
    out_a, a_rows, q, k, v, cqkv, cz, cba = _proj(x, lw["norm1_g"], lw["w_in_b"], layer, lw["w_small"], tabs,
                                                  (lw["a_gain"], ws_bd, bias_tile), n)

    pad8 = lambda a: jnp.pad(a.reshape(batch, t, B_W), ((0, 0), (0, SUBLANES - t), (0, 0)))
    out_b = _dec_attn(pad8(q), kbuf, vbuf, pad8(k), pad8(v), t, layer)[:, :t].reshape(n, B_W)

    tp = DELTA_CHUNK
    padt = lambda a: jnp.pad(a.reshape(batch, t, -1), ((0, 0), (0, tp - t), (0, 0))).reshape(batch * tp, -1)
    out_c, s_new = _mixer_c(lw, padt(cqkv), padt(cba), padt(cz), conv_buf, s0, gmat, batch, tp, tp, t, 4)
    out_c = out_c[:, :t].reshape(n, C_W)

    x, dense = _out_proj(out_a, out_b, None, out_c, x, lw["w_out"], lw["norm2_g"], lw["w_route"], n)
    x = _ffn(lw, x, dense)

    heads = lambda a: a.reshape(batch, t, B_HEADS, HEAD_DIM)
    conv_state = jnp.concatenate([conv_buf, cqkv.reshape(batch, t, 3 * C_W)], axis=1)[:, -(CONV_W - 1):]
    return x, (heads(k), heads(v), a_rows.reshape(batch, t, A_W), conv_state, s_new)


def kernel(x_prompt, x_sample, cache_win_k, cache_win_v, state_conv, state_delta, norm1_g, w_in, a_vnorm_g, a_ws, a_bs, c_conv_w, c_a_log, c_dt_bias, c_norm_g, w_out, norm2_g, w_group, w_router, w1, w3, w2, final_g):
    bp, tp, _ = x_prompt.shape
    bs, ts, _ = x_sample.shape
    depth = w_in.shape[0]
    assert tp % (PATTERNS[-1][1] * WBLK) == 0 and tp % 512 == 0 and bs * ts == CHUNK

    tabs_p = _rope_tables(jnp.arange(tp, dtype=jnp.int32))
    tabs_s = tuple(jnp.tile(a, (bs, 1)) for a in _rope_tables(PAST_LEN + jnp.arange(ts, dtype=jnp.int32)))
    gmat = _gdn_constants()

    feat_major = lambda c: jnp.transpose(c, (0, 1, 3, 4, 2)).reshape(depth, bs, B_W, c.shape[2])
    cache_k, cache_v = feat_major(cache_win_k), feat_major(cache_win_v)
    keep = min(MAX_WINDOW, tp)
    win_k = jnp.zeros((depth, bp, B_W, keep), F32)
    win_v = jnp.zeros((depth, bp, B_W, keep), F32)

    xp = x_prompt.reshape(bp * tp, D_MODEL)
    xs = x_sample.reshape(bs * ts, D_MODEL)
    p_out = [[] for _ in range(2)]
    s_out = [[] for _ in range(5)]
    for l in range(depth):
        lw = _layer_weights(l, norm1_g, w_in, a_vnorm_g, a_ws, a_bs, c_conv_w, c_a_log, c_dt_bias, c_norm_g,
                            w_out, norm2_g, w_group, w_router, w1, w3, w2)
        lw["final_g"] = final_g[None, :] if l == depth - 1 else None
        xp, win_k, win_v, st = _prompt_layer(lw, xp, bp, tp, tabs_p, gmat, win_k, win_v, l)
        for acc, a in zip(p_out, st):
            acc.append(a)
        xs, st = _sample_layer(lw, xs, bs, ts, tabs_s, gmat, cache_k, cache_v, state_conv[l], state_delta[l], l)
        for acc, a in zip(s_out, st):
            acc.append(a)
    y_prompt = xp.reshape(bp, tp, D_MODEL)
    y_sample = xs.reshape(bs, ts, D_MODEL)
    rows_major = lambda w: jnp.transpose(w.reshape(depth, bp, B_HEADS, HEAD_DIM, keep), (0, 1, 4, 2, 3))
    return (y_prompt, y_sample, rows_major(win_k), rows_major(win_v), *[jnp.stack(a) for a in p_out],
            *[jnp.stack(a) for a in s_out])
```
